```python
import math
import numpy as np
import jax
import jax.numpy as jnp
from jax import lax


D_MODEL = 1024
BATCH = 8
SEQ = 4096
DEPTH = 4

GRID_W = 64
CTX_LEN = 256
ROPE_THETA = 10000.0
Q_BLOCK = 128
EPS = 1e-6
ADA_SCALE = 0.2

MIX_WIDTH = D_MODEL
GROUP_WIDTH = MIX_WIDTH // 4

RET_HEADS = 4
RET_DK = GROUP_WIDTH // RET_HEADS
RET_DV = GROUP_WIDTH // RET_HEADS
RET_CHUNK = 128
DIFF_HEADS = 4
DIFF_DV = GROUP_WIDTH // DIFF_HEADS
DIFF_D = DIFF_DV // 2
GQA_HEADS = 4
GQA_KV_HEADS = 2
GQA_D = GROUP_WIDTH // GQA_HEADS
MLA_HEADS = 4
MLA_Q_RANK = 192
MLA_KV_RANK = 128
MLA_NOPE = 64
MLA_ROPE = 32
MLA_DV = GROUP_WIDTH // MLA_HEADS

IN_SPLITS = (
    RET_HEADS * RET_DK, RET_HEADS * RET_DK, RET_HEADS * RET_DV, RET_HEADS * RET_DV,
    DIFF_HEADS * 2 * DIFF_D, DIFF_HEADS * 2 * DIFF_D, DIFF_HEADS * DIFF_DV,
    GQA_HEADS * GQA_D, GQA_KV_HEADS * GQA_D, GQA_KV_HEADS * GQA_D,
    MLA_Q_RANK, MLA_KV_RANK, MLA_ROPE,
)
IN_WIDTH = sum(IN_SPLITS)

N_EXPERTS = 32
TOP_K = 4
D_FF = D_MODEL
SWIGLU_LIMIT = 7.0
SWIGLU_ALPHA = 1.702
MOE_BLOCK = 256

kernel_name = 'hybrid_parallel_heads_retention_diff_gqa_mla_moe'


def rms_norm(x, g):
    xf = x.astype(jnp.float32)
    y = xf * lax.rsqrt(jnp.mean(xf * xf, axis=-1, keepdims=True) + EPS)
    return y.astype(x.dtype) * g


def modulate(x, g, shift, scale):
    return rms_norm(x, g) * (1 + scale) + shift


def grid_rope(rows, rot_dim):
    row = jnp.repeat(jnp.arange(rows), GRID_W)
    col = jnp.tile(jnp.arange(GRID_W), rows)
    n_freq = rot_dim // 4
    inv = ROPE_THETA ** (-jnp.arange(n_freq, dtype=jnp.float32) / n_freq)
    ang = jnp.concatenate([row[:, None] * inv, col[:, None] * inv], axis=-1)
    return jnp.cos(ang), jnp.sin(ang)


def apply_rope(x, rope):
    cos, sin = rope
    half = x.shape[-1] // 2
    x1 = x[..., :half].astype(jnp.float32)
    x2 = x[..., half:].astype(jnp.float32)
    return jnp.concatenate([x1 * cos - x2 * sin, x1 * sin + x2 * cos], axis=-1).astype(x.dtype)


def split_heads(a, n_heads):
    b, s, w = a.shape
    return a.reshape(b, s, n_heads, w // n_heads).transpose(0, 2, 1, 3)


def merge_heads(o):
    b, h, s, d = o.shape
    return o.transpose(0, 2, 1, 3).reshape(b, s, h * d)


def softmax_f32(s):
    return jax.nn.softmax(s.astype(jnp.float32), axis=-1)


def sweep_query_blocks(block_fn, *qs):
    n = qs[0].shape[-2]
    nb = n // Q_BLOCK

    def to_blocks(a):
        return jnp.moveaxis(a.reshape(a.shape[:-2] + (nb, Q_BLOCK, a.shape[-1])), -3, 0)

    out = lax.map(lambda qb: block_fn(*qb), tuple(to_blocks(a) for a in qs))
    out = jnp.moveaxis(out, 0, -3)
    return out.reshape(out.shape[:-3] + (n, out.shape[-1]))


def retention_chunks(q, k, v, log_g, s0):
    b, h, s, _ = q.shape
    nc = s // RET_CHUNK
    pos = jnp.arange(RET_CHUNK, dtype=jnp.float32)
    dist = pos[:, None] - pos[None, :]
    decay = jnp.where(dist >= 0, jnp.exp(log_g[:, None, None] * jnp.maximum(dist, 0.0)), 0.0)
    xi = jnp.exp(log_g[:, None] * (pos + 1.0))[:, :, None]
    zeta = jnp.exp(log_g[:, None] * (RET_CHUNK - 1.0 - pos))[:, :, None]
    g_chunk = jnp.exp(log_g * RET_CHUNK)[:, None, None]

    def to_chunks(a):
        return jnp.moveaxis(a.reshape(b, h, nc, RET_CHUNK, a.shape[-1]), 2, 0)

    def step(state, qkv):
        qc, kc, vc = qkv
        inner = jnp.einsum('bhnd,bhmd->bhnm', qc, kc) * decay
        o = jnp.einsum('bhnm,bhme->bhne', inner, vc) + jnp.einsum('bhnd,bhde->bhne', qc, state) * xi
        state = g_chunk * state + jnp.einsum('bhmd,bhme->bhde', kc * zeta, vc)
        return state, o

    state, o = lax.scan(step, s0, (to_chunks(q), to_chunks(k), to_chunks(v)))
    o = jnp.moveaxis(o, 0, 2).reshape(b, h, s, v.shape[-1])
    return o, state


def head_group_norm(o, w, bias):
    of = o.astype(jnp.float32)
    mu = jnp.mean(of, axis=-1, keepdims=True)
    var = jnp.mean(jnp.square(of - mu), axis=-1, keepdims=True)
    return merge_heads((of - mu) * lax.rsqrt(var + EPS)) * w + bias


def retention_group(q_l, k_l, v_l, g_l, q_c, k_c, v_c, g_c, log_decay, gn_w, gn_b, rope, ctx_out):
    scale = RET_DK ** -0.5
    ql = apply_rope(split_heads(q_l, RET_HEADS), rope)
    kl = apply_rope(split_heads(k_l, RET_HEADS), rope) * scale
    vl = split_heads(v_l, RET_HEADS)
    qc = split_heads(q_c, RET_HEADS)
    kc = split_heads(k_c, RET_HEADS) * scale
    vc = split_heads(v_c, RET_HEADS)
    log_g = -jnp.exp(log_decay.astype(jnp.float32))
    zero = jnp.zeros((qc.shape[0], RET_HEADS, RET_DK, RET_DV), jnp.float32)
    flip = lambda a: jnp.flip(a, axis=2)
    oc_f, sc_f = retention_chunks(qc, kc, vc, log_g[0], zero)
    oc_b, sc_b = retention_chunks(flip(qc), flip(kc), flip(vc), log_g[1], zero)
    ol_f, _ = retention_chunks(ql, kl, vl, log_g[0], sc_f)
    ol_b, _ = retention_chunks(flip(ql), flip(kl), flip(vl), log_g[1], sc_b)

    def readout(o, g):
        return (head_group_norm(o, gn_w, gn_b) * jax.nn.silu(g)).astype(g.dtype)

    y_l = readout(ol_f + flip(ol_b), g_l)
    y_c = readout(oc_f + flip(oc_b), g_c) if ctx_out else None
    return y_l, y_c


def diff_group(q_l, k_l, v_l, q_c, k_c, v_c, lam_params, subln_w, lam_init, rope, ctx_out):
    def prep(q, k, v):
        q = split_heads(q, DIFF_HEADS)
        k = split_heads(k, DIFF_HEADS)
        return q[..., :DIFF_D], q[..., DIFF_D:], k[..., :DIFF_D], k[..., DIFF_D:], split_heads(v, DIFF_HEADS)

    q1l, q2l, k1l, k2l, vl = prep(q_l, k_l, v_l)
    q1l, q2l, k1l, k2l = (apply_rope(a, rope) for a in (q1l, q2l, k1l, k2l))
    q1c, q2c, k1c, k2c, vc = prep(q_c, k_c, v_c)
    lp = lam_params.astype(jnp.float32)
    lam = jnp.exp(jnp.sum(lp[0] * lp[1])) - jnp.exp(jnp.sum(lp[2] * lp[3])) + lam_init
    scale = DIFF_D ** -0.5

    def attend(q1, q2, k1, k2, v):
        p1 = softmax_f32(jnp.einsum('bhqd,bhkd->bhqk', q1, k1) * scale)
        p2 = softmax_f32(jnp.einsum('bhqd,bhkd->bhqk', q2, k2) * scale)
        return jnp.einsum('bhqk,bhkd->bhqd', (p1 - lam * p2).astype(v.dtype), v)

    k1a = jnp.concatenate([k1c, k1l], axis=2)
    k2a = jnp.concatenate([k2c, k2l], axis=2)
    va = jnp.concatenate([vc, vl], axis=2)

    def readout(o):
        return merge_heads(rms_norm(o, subln_w) * (1 - lam_init))

    y_l = readout(sweep_query_blocks(lambda a, b: attend(a, b, k1a, k2a, va), q1l, q2l))
    y_c = readout(attend(q1c, q2c, k1c, k2c, vc)) if ctx_out else None
    return y_l, y_c


def gqa_group(q_l, k_l, v_l, q_c, k_c, v_c, qk_norm, rope, ctx_out):
    n_rep = GQA_HEADS // GQA_KV_HEADS

    def prep(q, k, v):
        b, s, _ = q.shape
        q = q.reshape(b, s, GQA_KV_HEADS, n_rep, GQA_D).transpose(0, 2, 3, 1, 4)
        k = split_heads(k, GQA_KV_HEADS)
        return rms_norm(q, qk_norm[0]), rms_norm(k, qk_norm[1]), split_heads(v, GQA_KV_HEADS)

    ql, kl, vl = prep(q_l, k_l, v_l)
    ql, kl = apply_rope(ql, rope), apply_rope(kl, rope)
    qc, kc, vc = prep(q_c, k_c, v_c)
    scale = GQA_D ** -0.5

    def attend(q, k, v):
        p = softmax_f32(jnp.einsum('bngqd,bnkd->bngqk', q, k) * scale)
        return jnp.einsum('bngqk,bnkd->bngqd', p.astype(v.dtype), v)

    ka = jnp.concatenate([kc, kl], axis=2)
    va = jnp.concatenate([vc, vl], axis=2)

    def readout(o):
        b, _, _, s, _ = o.shape
        return o.transpose(0, 3, 1, 2, 4).reshape(b, s, GROUP_WIDTH)

    y_l = readout(sweep_query_blocks(lambda q: attend(q, ka, va), ql))
    y_c = readout(attend(qc, kc, vc)) if ctx_out else None
    return y_l, y_c


def mla_group(cq_l, ckv_l, kr_l, cq_c, ckv_c, kr_c, q_norm, kv_norm, w_uq, w_ukv, rope, ctx_out):
    def prep(cq, ckv, kr):
        q = split_heads(rms_norm(cq, q_norm) @ w_uq, MLA_HEADS)
        kv = split_heads(rms_norm(ckv, kv_norm) @ w_ukv, MLA_HEADS)
        return q[..., :MLA_NOPE], q[..., MLA_NOPE:], kv[..., :MLA_NOPE], kv[..., MLA_NOPE:], kr

    qnl, qrl, knl, vl, krl = prep(cq_l, ckv_l, kr_l)
    qrl, krl = apply_rope(qrl, rope), apply_rope(krl, rope)
    qnc, qrc, knc, vc, krc = prep(cq_c, ckv_c, kr_c)
    scale = (MLA_NOPE + MLA_ROPE) ** -0.5

    def attend(qn, qr, kn, kr, v):
        s = jnp.einsum('bhqd,bhkd->bhqk', qn, kn) + jnp.einsum('bhqr,bkr->bhqk', qr, kr)
        p = softmax_f32(s * scale)
        return jnp.einsum('bhqk,bhkd->bhqd', p.astype(v.dtype), v)

    kna = jnp.concatenate([knc, knl], axis=2)
    kra = jnp.concatenate([krc, krl], axis=1)
    va = jnp.concatenate([vc, vl], axis=2)
    y_l = merge_heads(sweep_query_blocks(lambda a, b: attend(a, b, kna, kra, va), qnl, qrl))
    y_c = merge_heads(attend(qnc, qrc, knc, krc, vc)) if ctx_out else None
    return y_l, y_c


def mixing_sublayer(h_lat, h_ctx, w_in, w_out, ret_log_decay, ret_gn_w, ret_gn_b, diff_lambda,
                    diff_subln, gqa_qk_norm, mla_q_norm, mla_kv_norm, mla_w_uq, mla_w_ukv,
                    rope64, rope32, lam_init, ctx_out):
    points = [int(p) for p in np.cumsum(IN_SPLITS)[:-1]]
    pl = jnp.split(h_lat @ w_in, points, axis=-1)
    pc = jnp.split(h_ctx @ w_in, points, axis=-1)
    ret_l, ret_c = retention_group(*pl[0:4], *pc[0:4], ret_log_decay, ret_gn_w, ret_gn_b, rope64, ctx_out)
    dif_l, dif_c = diff_group(*pl[4:7], *pc[4:7], diff_lambda, diff_subln, lam_init, rope32, ctx_out)
    gqa_l, gqa_c = gqa_group(*pl[7:10], *pc[7:10], gqa_qk_norm, rope64, ctx_out)
    mla_l, mla_c = mla_group(*pl[10:13], *pc[10:13], mla_q_norm, mla_kv_norm, mla_w_uq, mla_w_ukv,
                             rope32, ctx_out)
    y_l = jnp.concatenate([ret_l, dif_l, gqa_l, mla_l], axis=-1) @ w_out
    y_c = jnp.concatenate([ret_c, dif_c, gqa_c, mla_c], axis=-1) @ w_out if ctx_out else None
    return y_l, y_c


def moe_ffn(h, w_r, b_r, w1, b1, w2, b2):
    n_tok, d = h.shape
    logits = (h @ w_r + b_r).astype(jnp.float32)
    top_logit, top_idx = lax.top_k(logits, TOP_K)
    gates = jax.nn.softmax(top_logit, axis=-1)
    n_assign = n_tok * TOP_K
    n_blocks = -(-n_assign // MOE_BLOCK) + N_EXPERTS
    e_flat = top_idx.reshape(-1)
    order = jnp.argsort(e_flat)
    e_sorted = e_flat[order]
    counts = jnp.bincount(e_flat, length=N_EXPERTS)
    starts = jnp.cumsum(counts) - counts
    padded = (counts + MOE_BLOCK - 1) // MOE_BLOCK * MOE_BLOCK
    pad_ends = jnp.cumsum(padded)
    dest = pad_ends[e_sorted] - padded[e_sorted] + jnp.arange(n_assign) - starts[e_sorted]
    src_tok = jnp.full((n_blocks * MOE_BLOCK,), n_tok, jnp.int32).at[dest].set((order // TOP_K).astype(jnp.int32))
    row_gate = jnp.zeros((n_blocks * MOE_BLOCK,), jnp.float32).at[dest].set(gates.reshape(-1)[order])
    block_expert = jnp.minimum(
        jnp.searchsorted(pad_ends, jnp.arange(n_blocks) * MOE_BLOCK, side='right'), N_EXPERTS - 1)
    x_rows = jnp.take(h, src_tok, axis=0, mode='fill', fill_value=0).reshape(n_blocks, MOE_BLOCK, d)

    def expert_block(args):
        xb, e = args
        z = xb @ w1[e] + b1[e]
        glu = jnp.minimum(z[:, ::2], SWIGLU_LIMIT)
        lin = jnp.clip(z[:, 1::2], -SWIGLU_LIMIT, SWIGLU_LIMIT)
        return (glu * jax.nn.sigmoid(SWIGLU_ALPHA * glu) * (lin + 1)) @ w2[e] + b2[e]

    y_rows = lax.map(expert_block, (x_rows, block_expert)).reshape(n_blocks * MOE_BLOCK, d)
    out = jnp.zeros((n_tok, d), jnp.float32).at[src_tok].add(y_rows * row_gate[:, None], mode='drop')
    return out.astype(h.dtype)


def setup_inputs(seed: int = 0) -> dict:
    key = jax.random.key(seed)
    ks = jax.random.split(key, 25)
    f32 = jnp.float32

    def normal(k, shape, scale):
        return jax.random.normal(k, shape, f32) * scale

    def gain(k, shape):
        return 1.0 + 0.02 * jax.random.normal(k, shape, f32)

    base_decay = jnp.log(-jnp.log(1.0 - 2.0 ** (-5.0 - jnp.arange(RET_HEADS, dtype=f32))))
    return {
        'x': normal(ks[0], (BATCH, SEQ, D_MODEL), 1.0),
        'c': normal(ks[1], (BATCH, D_MODEL), 1.0),
        'ctx': normal(ks[2], (BATCH, CTX_LEN, D_MODEL), 1.0),
        'c_ctx': normal(ks[3], (D_MODEL,), 1.0),
        'ada_w': normal(ks[4], (DEPTH, D_MODEL, 6 * D_MODEL), ADA_SCALE * D_MODEL ** -0.5),
        'ada_b': normal(ks[5], (DEPTH, 6 * D_MODEL), 0.01),
        'norm_g': gain(ks[6], (DEPTH, 4, D_MODEL)),
        'w_in': normal(ks[7], (DEPTH, D_MODEL, IN_WIDTH), D_MODEL ** -0.5),
        'w_out': normal(ks[8], (DEPTH, MIX_WIDTH, D_MODEL), MIX_WIDTH ** -0.5),
        'ret_log_decay': base_decay + normal(ks[9], (DEPTH, 2, RET_HEADS), 0.05),
        'ret_gn_w': gain(ks[10], (DEPTH, GROUP_WIDTH)),
        'ret_gn_b': normal(ks[11], (DEPTH, GROUP_WIDTH), 0.01),
        'diff_lambda': normal(ks[12], (DEPTH, 4, DIFF_D), 0.1),
        'diff_subln': gain(ks[13], (DEPTH, DIFF_DV)),
        'gqa_qk_norm': gain(ks[14], (DEPTH, 2, GQA_D)),
        'mla_q_norm': gain(ks[15], (DEPTH, MLA_Q_RANK)),
        'mla_kv_norm': gain(ks[16], (DEPTH, MLA_KV_RANK)),
        'mla_w_uq': normal(ks[17], (DEPTH, MLA_Q_RANK, MLA_HEADS * (MLA_NOPE + MLA_ROPE)), MLA_Q_RANK ** -0.5),
        'mla_w_ukv': normal(ks[18], (DEPTH, MLA_KV_RANK, MLA_HEADS * (MLA_NOPE + MLA_DV)), MLA_KV_RANK ** -0.5),
        'router_w': normal(ks[19], (DEPTH, D_MODEL, N_EXPERTS), D_MODEL ** -0.5),
        'router_b': normal(ks[20], (DEPTH, N_EXPERTS), 0.01),
        'exp_w1': normal(ks[21], (DEPTH, N_EXPERTS, D_MODEL, 2 * D_FF), D_MODEL ** -0.5),
        'exp_b1': normal(ks[22], (DEPTH, N_EXPERTS, 2 * D_FF), 0.01),
        'exp_w2': normal(ks[23], (DEPTH, N_EXPERTS, D_FF, D_MODEL), D_FF ** -0.5),
        'exp_b2': normal(ks[24], (DEPTH, N_EXPERTS, D_MODEL), 0.01),
    }


def reference(x, c, ctx, c_ctx, ada_w, ada_b, norm_g, w_in, w_out, ret_log_decay, ret_gn_w, ret_gn_b,
              diff_lambda, diff_subln, gqa_qk_norm, mla_q_norm, mla_kv_norm, mla_w_uq, mla_w_ukv,
              router_w, router_b, exp_w1, exp_b1, exp_w2, exp_b2):
    b, s, d = x.shape
    n_ctx = ctx.shape[1]
    ROWS = s // GRID_W
    rope64 = grid_rope(ROWS, RET_DK)
    rope32 = grid_rope(ROWS, DIFF_D)
    silu_c = jax.nn.silu(c)
    silu_cc = jax.nn.silu(c_ctx)
    for l in range(DEPTH):
        last = l == DEPTH - 1
        sh_a, sc_a, g_a, sh_f, sc_f, g_f = jnp.split((silu_c @ ada_w[l] + ada_b[l])[:, None, :], 6, axis=-1)
        csh_a, csc_a, cg_a, csh_f, csc_f, cg_f = jnp.split(silu_cc @ ada_w[l] + ada_b[l], 6, axis=-1)
        lam_init = 0.8 - 0.6 * math.exp(-0.3 * l)
        h_lat = modulate(x, norm_g[l, 0], sh_a, sc_a)
        h_ctx = modulate(ctx, norm_g[l, 0], csh_a, csc_a)
        y_lat, y_ctx = mixing_sublayer(h_lat, h_ctx, w_in[l], w_out[l], ret_log_decay[l], ret_gn_w[l],
                                       ret_gn_b[l], diff_lambda[l], diff_subln[l], gqa_qk_norm[l],
                                       mla_q_norm[l], mla_kv_norm[l], mla_w_uq[l], mla_w_ukv[l],
                                       rope64, rope32, lam_init, not last)
        x = x + g_a * rms_norm(y_lat, norm_g[l, 1])
        moe_params = (router_w[l], router_b[l], exp_w1[l], exp_b1[l], exp_w2[l], exp_b2[l])
        if last:
            h = modulate(x, norm_g[l, 2], sh_f, sc_f)
            y = moe_ffn(h.reshape(b * s, d), *moe_params).reshape(b, s, d)
            x = x + g_f * rms_norm(y, norm_g[l, 3])
        else:
            ctx = ctx + cg_a * rms_norm(y_ctx, norm_g[l, 1])
            h_lat = modulate(x, norm_g[l, 2], sh_f, sc_f).reshape(b * s, d)
            h_ctx = modulate(ctx, norm_g[l, 2], csh_f, csc_f).reshape(b * n_ctx, d)
            y = moe_ffn(jnp.concatenate([h_lat, h_ctx], axis=0), *moe_params)
            x = x + g_f * rms_norm(y[:b * s].reshape(b, s, d), norm_g[l, 3])
            ctx = ctx + cg_f * rms_norm(y[b * s:].reshape(b, n_ctx, d), norm_g[l, 3])
    return x
```

```python
import functools
import math

import jax
import jax.numpy as jnp
import numpy as np
from jax import lax
from jax.experimental import pallas as pl
from jax.experimental.pallas import tpu as pltpu

F32 = jnp.float32
BF16 = jnp.bfloat16

GRID_W = 64
ROPE_THETA = 10000.0
EPS = 1e-6
GROUP_WIDTH = 256
HEAD_DIM = 64
N_HEADS = 4
RET_CHUNK = 128
DIFF_D = 32
GQA_KV_HEADS = 2
MLA_Q_RANK = 192
MLA_KV_RANK = 128
MLA_NOPE = 64
MLA_ROPE = 32
N_EXPERTS = 32
TOP_K = 4
SWIGLU_LIMIT = 7.0
SWIGLU_ALPHA = 1.702
MOE_BLOCK = 256

LANES = 128
TOKEN_TILE = 256
KV_TILE = 256
VMEM_LIMIT = 48 * 1024 * 1024

IN_PIECES = (
    ("ret_q", 256, 256), ("ret_k", 256, 256), ("ret_v", 256, 256), ("ret_g", 256, 256),
    ("dif_q", 256, 256), ("dif_k", 256, 256), ("dif_v", 256, 256),
    ("gqa_q", 256, 256), ("gqa_k", 128, 128), ("gqa_v", 128, 128),
    ("mla_cq", MLA_Q_RANK, 256), ("mla_ckv", MLA_KV_RANK, 128), ("mla_kr", MLA_ROPE, 128),
)
IN_OFFSETS = {}
_off = 0
for _name, _w, _pw in IN_PIECES:
    IN_OFFSETS[_name] = _off
    _off += _pw
IN_PACKED_WIDTH = _off


def _params(sem):
    return pltpu.CompilerParams(dimension_semantics=sem, vmem_limit_bytes=VMEM_LIMIT)


def _rms(x):
    return x * lax.rsqrt(jnp.mean(x * x, axis=-1, keepdims=True) + EPS)


def _split_bf16(a):
    hi = a.astype(BF16)
    lo = (a - hi.astype(F32)).astype(BF16)
    return hi, lo


def _dot_nt(a, b):
    return lax.dot_general(a, b, (((1,), (1,)), ((), ())), preferred_element_type=F32)


def _dot(a, b):
    return jnp.dot(a, b, preferred_element_type=F32)


def _sigmoid(a):
    return 1.0 / (1.0 + jnp.exp(-a))


def _ada_kernel(c_ref, w_ref, b_ref, o_ref):
    s = c_ref[...]
    s = s * _sigmoid(s)
    s_hi, s_lo = _split_bf16(s)
    w_hi, w_lo = _split_bf16(w_ref[0])
    o_ref[0] = _dot(s_hi, w_hi) + _dot(s_hi, w_lo) + _dot(s_lo, w_hi) + b_ref[0]


def _ada_call(c_rows, ada_w, ada_b):
    depth, d, n6 = ada_w.shape
    rows = c_rows.shape[0]
    tn = 1536
    return pl.pallas_call(
        _ada_kernel,
        grid=(depth, n6 // tn),
        in_specs=[
            pl.BlockSpec((rows, d), lambda l, j: (0, 0)),
            pl.BlockSpec((1, d, tn), lambda l, j: (l, 0, j)),
            pl.BlockSpec((1, 1, tn), lambda l, j: (l, 0, j)),
        ],
        out_specs=pl.BlockSpec((1, rows, tn), lambda l, j: (l, 0, j)),
        out_shape=jax.ShapeDtypeStruct((depth, rows, n6), F32),
        compiler_params=_params(("parallel", "parallel")),
    )(c_rows, ada_w, ada_b.reshape(depth, 1, n6))


def _rope(x, cos, sin_signed, half):
    outs = []
    for c in range(x.shape[1] // LANES):
        sl = slice(c * LANES, (c + 1) * LANES)
        xc = x[:, sl]
        lane = lax.broadcasted_iota(jnp.int32, xc.shape, 1)
        first_half = (lane % (2 * half)) < half
        partner = jnp.where(first_half, pltpu.roll(xc, LANES - half, 1), pltpu.roll(xc, half, 1))
        outs.append(xc * cos[:, sl] + partner * sin_signed[:, sl])
    return outs[0] if len(outs) == 1 else jnp.concatenate(outs, axis=1)


def _group_mean_sq(x, gsize):
    w = x.shape[1]
    r = lax.broadcasted_iota(jnp.int32, (w, w), 0) // gsize
    c = lax.broadcasted_iota(jnp.int32, (w, w), 1) // gsize
    ones = jnp.where(r == c, 1.0, 0.0).astype(BF16)
    hi, lo = _split_bf16(x * x)
    return (_dot(hi, ones) + _dot(lo, ones)) * (1.0 / gsize)


def _in_proj_kernel(x_ref, mod_ref, g_ref, w_ref, cos64_ref, sin64_ref, cos32_ref, sin32_ref,
                    gqn_ref, gkn_ref, mqn_ref, mkvn_ref, wuq_ref, wukv_ref,
                    rq_ref, rk_ref, rv_ref, rg_ref, dq_ref, dk_ref, dv_ref, gq_ref, gk_ref, gv_ref,
                    mqn_o, mqr_o, mkn_o, mv_o, mkr_o):
    mod = mod_ref[0, 0]
    h = _rms(x_ref[0]) * g_ref[...] * (1.0 + mod[1:2]) + mod[0:1]
    hb = h.astype(BF16)
    cos64, sin64 = cos64_ref[...], sin64_ref[...]
    cos32, sin32 = cos32_ref[...], sin32_ref[...]

    def proj(name, width):
        o = IN_OFFSETS[name]
        return _dot(hb, w_ref[:, o:o + width])

    rq_ref[0] = _rope(proj("ret_q", 256), cos64, sin64, 32).astype(BF16)
    rk_ref[0] = (_rope(proj("ret_k", 256), cos64, sin64, 32) * (HEAD_DIM ** -0.5)).astype(BF16)
    rv_ref[0] = proj("ret_v", 256).astype(BF16)
    rg_ref[0] = proj("ret_g", 256)
    dq_ref[0] = (_rope(proj("dif_q", 256), cos32, sin32, 16) * (DIFF_D ** -0.5)).astype(BF16)
    dk_ref[0] = _rope(proj("dif_k", 256), cos32, sin32, 16).astype(BF16)
    dv_ref[0] = proj("dif_v", 256).astype(BF16)
    gq = proj("gqa_q", 256)
    gq = gq * lax.rsqrt(_group_mean_sq(gq, HEAD_DIM) + EPS) * gqn_ref[...]
    gq_ref[0] = (_rope(gq, cos64, sin64, 32) * (HEAD_DIM ** -0.5)).astype(BF16)
    gk = proj("gqa_k", 128)
    gk = gk * lax.rsqrt(_group_mean_sq(gk, HEAD_DIM) + EPS) * gkn_ref[...]
    gk_ref[0] = _rope(gk, cos64[:, :LANES], sin64[:, :LANES], 32).astype(BF16)
    gv_ref[0] = proj("gqa_v", 128).astype(BF16)
    cq = proj("mla_cq", 256)
    cq = cq * lax.rsqrt(jnp.sum(cq * cq, axis=-1, keepdims=True) * (1.0 / MLA_Q_RANK) + EPS) * mqn_ref[...]
    q_up = _dot(cq.astype(BF16), wuq_ref[...])
    mla_scale = (MLA_NOPE + MLA_ROPE) ** -0.5
    mqn_o[0] = (q_up[:, :256] * mla_scale).astype(BF16)
    mqr_o[0] = (_rope(q_up[:, 256:], cos32[:, :LANES], sin32[:, :LANES], 16) * mla_scale).astype(BF16)
    ckv = proj("mla_ckv", 128)
    ckv = ckv * lax.rsqrt(jnp.mean(ckv * ckv, axis=-1, keepdims=True) + EPS) * mkvn_ref[...]
    kv_up = _dot(ckv.astype(BF16), wukv_ref[...])
    mkn_o[0] = kv_up[:, :256].astype(BF16)
    mv_o[0] = kv_up[:, 256:].astype(BF16)
    mkr_o[0] = _rope(proj("mla_kr", 128), cos32[:, :LANES], sin32[:, :LANES], 16).astype(BF16)


def _in_proj_call(xc, modl, g0, w_in_p, tables, gqn, gkn, mqn, mkvn, wuq_p, wukv_p, n_ctx):
    b, n, d = xc.shape
    tm = TOKEN_TILE
    nct = n_ctx // tm
    tok = lambda w: pl.BlockSpec((1, tm, w), lambda i, t: (i, t, 0))
    const2 = lambda a: pl.BlockSpec(a.shape, lambda i, t: (0, 0))
    tab = pl.BlockSpec((tm, 256), lambda i, t: (t, 0))
    out_widths = (256, 256, 256, 256, 256, 256, 256, 256, 128, 128, 256, 128, 256, 256, 128)
    out_dtypes = (BF16, BF16, BF16, F32, BF16, BF16, BF16, BF16, BF16, BF16, BF16, BF16, BF16, BF16, BF16)
    return pl.pallas_call(
        _in_proj_kernel,
        grid=(b, n // tm),
        in_specs=[
            tok(d),
            pl.BlockSpec((1, 1, 6, d), lambda i, t: (i, jnp.where(t >= nct, 1, 0), 0, 0)),
            const2(g0), const2(w_in_p), tab, tab, tab, tab,
            const2(gqn), const2(gkn), const2(mqn), const2(mkvn), const2(wuq_p), const2(wukv_p),
        ],
        out_specs=[tok(w) for w in out_widths],
        out_shape=[jax.ShapeDtypeStruct((b, n, w), dt) for w, dt in zip(out_widths, out_dtypes)],
        compiler_params=_params(("parallel", "parallel")),
    )(xc, modl, g0, w_in_p, *tables, gqn, gkn, mqn, mkvn, wuq_p, wukv_p)


def _ret_kernel(*refs, readout):
    if readout:
        (q_ref, k_ref, v_ref, dec_ref, xi_ref, zeta_ref, gc_ref, of_ref, g_ref, gnw_ref, gnb_ref,
         o_ref, state_ref) = refs
    else:
        q_ref, k_ref, v_ref, dec_ref, xi_ref, zeta_ref, gc_ref, o_ref, state_ref = refs

    @pl.when(pl.program_id(1) == 0)
    def _():
        state_ref[...] = jnp.zeros_like(state_ref)

    q, k, v = q_ref[0], k_ref[0], v_ref[0]
    outs = []
    for h in range(N_HEADS):
        sl = slice(h * HEAD_DIM, (h + 1) * HEAD_DIM)
        qh, kh, vh = q[:, sl], k[:, sl], v[:, sl]
        st = state_ref[h]
        inner = _dot_nt(qh, kh) * dec_ref[h]
        o = _dot(inner.astype(BF16), vh) + _dot(qh, st.astype(BF16)) * xi_ref[h]
        kz = (kh.astype(F32) * zeta_ref[h]).astype(BF16)
        kv = lax.dot_general(kz, vh, (((0,), (0,)), ((), ())), preferred_element_type=F32)
        state_ref[h] = gc_ref[h] * st + kv
        if readout:
            o = o + of_ref[0][:, sl]
            mu = jnp.mean(o, axis=-1, keepdims=True)
            var = jnp.mean(jnp.square(o - mu), axis=-1, keepdims=True)
            o = (o - mu) * lax.rsqrt(var + EPS)
        outs.append(o)
    o = jnp.concatenate(outs, axis=1)
    if readout:
        g = g_ref[0]
        o_ref[0] = ((o * gnw_ref[...] + gnb_ref[...]) * (g * _sigmoid(g))).astype(o_ref.dtype)
    else:
        o_ref[0] = o


def _ret_call(q, k, v, tabs, n_ctx, backward, readout_args=None):
    b, n, w = q.shape
    c = RET_CHUNK
    nch = n // c
    ncc = n_ctx // c
    if backward:
        chunk = lambda t: jnp.where(t < ncc, ncc - 1 - t, nch + ncc - 1 - t)
    else:
        chunk = lambda t: t
    tok = lambda dt_w: pl.BlockSpec((1, c, dt_w), lambda i, t: (i, chunk(t), 0))
    const = lambda a: pl.BlockSpec(a.shape, lambda i, t: (0,) * a.ndim)
    in_specs = [tok(w), tok(w), tok(w)] + [const(a) for a in tabs]
    args = [q, k, v, *tabs]
    if readout_args is not None:
        o_f, g, gnw, gnb = readout_args
        in_specs += [tok(w), tok(w), const(gnw), const(gnb)]
        args += [o_f, g, gnw, gnb]
    out_dtype = BF16 if readout_args is not None else F32
    return pl.pallas_call(
        functools.partial(_ret_kernel, readout=readout_args is not None),
        grid=(b, nch),
        in_specs=in_specs,
        out_specs=tok(w),
        out_shape=jax.ShapeDtypeStruct((b, n, w), out_dtype),
        scratch_shapes=[pltpu.VMEM((N_HEADS, HEAD_DIM, HEAD_DIM), F32)],
        compiler_params=_params(("parallel", "arbitrary")),
    )(*args)


def _ret_tables(log_g, backward):
    c = RET_CHUNK
    pos = jnp.arange(c, dtype=F32)
    dist = (pos[None, :] - pos[:, None]) if backward else (pos[:, None] - pos[None, :])
    lg = log_g[:, None, None]
    decay = jnp.where(dist >= 0, jnp.exp(lg * jnp.maximum(dist, 0.0)), 0.0)
    to_state = (c - pos) if backward else (pos + 1.0)
    to_end = pos if backward else (c - 1.0 - pos)
    xi = jnp.exp(log_g[:, None] * to_state)[:, :, None]
    zeta = jnp.exp(log_g[:, None] * to_end)[:, :, None]
    ones = jnp.ones((1, 1, HEAD_DIM), F32)
    gc = jnp.exp(log_g * c)[:, None, None] * jnp.ones((1, HEAD_DIM, HEAD_DIM), F32)
    return decay, xi * ones, zeta * ones, gc


def _flash_kernel(q_ref, k_ref, v_ref, o_ref, *, n_ctx):
    t = pl.program_id(2)
    q = q_ref[0, 0]
    tq = q.shape[0]
    n = k_ref.shape[2]
    dv = v_ref.shape[3]
    n_steps = jnp.where(t * tq < n_ctx, n_ctx // KV_TILE, n // KV_TILE)

    def body(j, carry):
        m, l, acc = carry
        off = pl.multiple_of(j * KV_TILE, KV_TILE)
        kc = k_ref[0, 0, pl.ds(off, KV_TILE), :]
        vc = v_ref[0, 0, pl.ds(off, KV_TILE), :]
        s = _dot_nt(q, kc)
        m_new = jnp.maximum(m, jnp.max(s, axis=-1, keepdims=True))
        alpha = jnp.exp(m - m_new)
        p = jnp.exp(s - m_new)
        l = alpha * l + jnp.sum(p, axis=-1, keepdims=True)
        acc = alpha * acc + _dot(p.astype(BF16), vc)
        return m_new, l, acc

    init = (jnp.full((tq, 1), -1e30, F32), jnp.zeros((tq, 1), F32), jnp.zeros((tq, dv), F32))
    _, l, acc = lax.fori_loop(0, n_steps, body, init)
    o_ref[0, 0] = acc / l


def _flash_call(q, k, v, n_ctx):
    b, hq, n, d = q.shape
    hk, hv, dv = k.shape[1], v.shape[1], v.shape[3]
    tq = TOKEN_TILE
    return pl.pallas_call(
        functools.partial(_flash_kernel, n_ctx=n_ctx),
        grid=(b, hq, n // tq),
        in_specs=[
            pl.BlockSpec((1, 1, tq, d), lambda i, h, t: (i, h, t, 0)),
            pl.BlockSpec((1, 1, n, d), lambda i, h, t: (i, h // (hq // hk), 0, 0)),
            pl.BlockSpec((1, 1, n, dv), lambda i, h, t: (i, h // (hq // hv), 0, 0)),
        ],
        out_specs=pl.BlockSpec((1, 1, tq, dv), lambda i, h, t: (i, h, t, 0)),
        out_shape=jax.ShapeDtypeStruct((b, hq, n, dv), F32),
        compiler_params=_params(("parallel", "parallel", "parallel")),
    )(q, k, v)


def _out_proj_kernel(y_ref, w_ref, x_ref, mod_ref, g_ref, wr_ref, br_ref, xo_ref, h_ref, idx_ref, gate_ref):
    mod = mod_ref[0, 0]
    g = g_ref[...]
    y = _dot(y_ref[0], w_ref[...])
    xn = x_ref[0] + mod[2:3] * (_rms(y) * g[1:2])
    xo_ref[0] = xn
    h = _rms(xn) * g[2:3] * (1.0 + mod[4:5]) + mod[3:4]
    h_ref[0] = h.astype(BF16)
    w_hi, w_lo = _split_bf16(wr_ref[...])
    h_hi, h_lo = _split_bf16(h)
    logits = _dot_nt(w_hi, h_hi) + _dot_nt(w_hi, h_lo) + _dot_nt(w_lo, h_hi) + br_ref[...]
    row = lax.broadcasted_iota(jnp.int32, logits.shape, 0)
    vals = logits
    tops, idxs = [], []
    for _ in range(TOP_K):
        m = jnp.max(vals, axis=0, keepdims=True)
        idx = jnp.min(jnp.where(vals == m, row, N_EXPERTS), axis=0, keepdims=True)
        tops.append(m)
        idxs.append(idx)
        vals = jnp.where(row == idx, -jnp.inf, vals)
    ex = [jnp.exp(tv - tops[0]) for tv in tops]
    den = ex[0] + ex[1] + ex[2] + ex[3]
    gate_ref[0] = jnp.concatenate([e / den for e in ex], axis=0)
    idx_ref[0] = jnp.concatenate(idxs, axis=0)


def _out_proj_call(ymix, w_out, xc, modl, g, wr_t, br, n_ctx):
    b, n, d = xc.shape
    tm = TOKEN_TILE
    nct = n_ctx // tm
    tok = lambda w: pl.BlockSpec((1, tm, w), lambda i, t: (i, t, 0))
    const2 = lambda a: pl.BlockSpec(a.shape, lambda i, t: (0, 0))
    sel = pl.BlockSpec((1, TOP_K, tm), lambda i, t: (i, 0, t))
    return pl.pallas_call(
        _out_proj_kernel,
        grid=(b, n // tm),
        in_specs=[
            tok(ymix.shape[2]), const2(w_out), tok(d),
            pl.BlockSpec((1, 1, 6, d), lambda i, t: (i, jnp.where(t >= nct, 1, 0), 0, 0)),
            const2(g), const2(wr_t), const2(br),
        ],
        out_specs=[tok(d), tok(d), sel, sel],
        out_shape=[
            jax.ShapeDtypeStruct((b, n, d), F32),
            jax.ShapeDtypeStruct((b, n, d), BF16),
            jax.ShapeDtypeStruct((b, TOP_K, n), jnp.int32),
            jax.ShapeDtypeStruct((b, TOP_K, n), F32),
        ],
        compiler_params=_params(("parallel", "parallel")),
    )(ymix, w_out, xc, modl, g, wr_t, br)


def _expert_kernel(be_ref, nu_ref, x_ref, w1g_ref, w1l_ref, b1g_ref, b1l_ref, w2_ref, b2_ref, o_ref):
    del be_ref
    i = pl.program_id(0)

    @pl.when(i < nu_ref[0])
    def _():
        x = x_ref[...]
        glu = jnp.minimum(_dot(x, w1g_ref[0]) + b1g_ref[0], SWIGLU_LIMIT)
        lin = jnp.clip(_dot(x, w1l_ref[0]) + b1l_ref[0], -SWIGLU_LIMIT, SWIGLU_LIMIT)
        act = glu * _sigmoid(SWIGLU_ALPHA * glu) * (lin + 1.0)
        o_ref[...] = _dot(act.astype(BF16), w2_ref[0]) + b2_ref[0]

    @pl.when(i >= nu_ref[0])
    def _():
        o_ref[...] = jnp.zeros_like(o_ref)


def _expert_call(block_expert, n_used, x_rows, w1g, w1l, b1g, b1l, w2, b2):
    r, d = x_rows.shape
    f = w1g.shape[2]
    tm = MOE_BLOCK
    wspec = lambda s: pl.BlockSpec((1,) + s, lambda i, be, nu: (be[i], 0, 0))
    return pl.pallas_call(
        _expert_kernel,
        grid_spec=pltpu.PrefetchScalarGridSpec(
            num_scalar_prefetch=2,
            grid=(r // tm,),
            in_specs=[
                pl.BlockSpec((tm, d), lambda i, be, nu: (i, 0)),
                wspec((d, f)), wspec((d, f)), wspec((1, f)), wspec((1, f)), wspec((f, d)), wspec((1, d)),
            ],
            out_specs=pl.BlockSpec((tm, d), lambda i, be, nu: (i, 0)),
        ),
        out_shape=jax.ShapeDtypeStruct((r, d), F32),
        compiler_params=_params(("arbitrary",)),
    )(block_expert, n_used, x_rows, w1g, w1l, b1g, b1l, w2, b2)


def _ffn_residual_kernel(x_ref, y_ref, mod_ref, g_ref, o_ref):
    mod = mod_ref[0, 0]
    o_ref[0] = x_ref[0] + mod[5:6] * (_rms(y_ref[0]) * g_ref[...][3:4])


def _ffn_residual_call(xc, y, modl, g, n_ctx):
    b, n, d = xc.shape
    tm = TOKEN_TILE
    nct = n_ctx // tm
    tok = pl.BlockSpec((1, tm, d), lambda i, t: (i, t, 0))
    return pl.pallas_call(
        _ffn_residual_kernel,
        grid=(b, n // tm),
        in_specs=[
            tok, tok,
            pl.BlockSpec((1, 1, 6, d), lambda i, t: (i, jnp.where(t >= nct, 1, 0), 0, 0)),
            pl.BlockSpec(g.shape, lambda i, t: (0, 0)),
        ],
        out_specs=tok,
        out_shape=jax.ShapeDtypeStruct((b, n, d), F32),
        compiler_params=_params(("parallel", "parallel")),
    )(xc, y, modl, g)


def _rope_tables(rows, n_ctx, rot_dim):
    row = jnp.repeat(jnp.arange(rows), GRID_W)
    col = jnp.tile(jnp.arange(GRID_W), rows)
    n_freq = rot_dim // 4
    inv = ROPE_THETA ** (-jnp.arange(n_freq, dtype=F32) / n_freq)
    ang = jnp.concatenate([row[:, None] * inv, col[:, None] * inv], axis=-1)
    cos, sin = jnp.cos(ang), jnp.sin(ang)
    reps = 256 // rot_dim
    cos_t = jnp.tile(jnp.concatenate([cos, cos], axis=-1), (1, reps))
    sin_t = jnp.tile(jnp.concatenate([-sin, sin], axis=-1), (1, reps))
    cos_t = jnp.concatenate([jnp.ones((n_ctx, 256), F32), cos_t], axis=0)
    sin_t = jnp.concatenate([jnp.zeros((n_ctx, 256), F32), sin_t], axis=0)
    return cos_t, sin_t


def _pack_w_in(w):
    parts, src = [], 0
    for _, width, padded in IN_PIECES:
        parts.append(w[:, src:src + width])
        if padded > width:
            parts.append(jnp.zeros((w.shape[0], padded - width), w.dtype))
        src += width
    return jnp.concatenate(parts, axis=1).astype(BF16)


def _pack_w_uq(w):
    wh = w.reshape(MLA_Q_RANK, N_HEADS, MLA_NOPE + MLA_ROPE)
    nope = wh[:, :, :MLA_NOPE].reshape(MLA_Q_RANK, N_HEADS * MLA_NOPE)
    rope = wh[:, :, MLA_NOPE:].reshape(MLA_Q_RANK, N_HEADS * MLA_ROPE)
    packed = jnp.concatenate([nope, rope], axis=1)
    return jnp.pad(packed, ((0, 256 - MLA_Q_RANK), (0, 0))).astype(BF16)


def _pack_w_ukv(w):
    wh = w.reshape(MLA_KV_RANK, N_HEADS, MLA_NOPE + HEAD_DIM)
    kn = wh[:, :, :MLA_NOPE].reshape(MLA_KV_RANK, N_HEADS * MLA_NOPE)
    vv = wh[:, :, MLA_NOPE:].reshape(MLA_KV_RANK, N_HEADS * HEAD_DIM)
    return jnp.concatenate([kn, vv], axis=1).astype(BF16)


def _heads(a, n_heads):
    b, n, w = a.shape
    return a.reshape(b, n, n_heads, w // n_heads).transpose(0, 2, 1, 3)


def _merge(o):
    b, h, n, d = o.shape
    return o.transpose(0, 2, 1, 3).reshape(b, n, h * d)


def _moe(h2, idx, gates, w1, b1, w2, b2):
    t, d = h2.shape
    e_flat = idx.reshape(-1)
    n_assign = t * TOP_K
    n_blocks = -(-n_assign // MOE_BLOCK) + N_EXPERTS
    onehot = (e_flat[:, None] == jnp.arange(N_EXPERTS, dtype=jnp.int32)[None, :]).astype(jnp.int32)
    csum = jnp.cumsum(onehot, axis=0)
    rank = jnp.take_along_axis(csum, e_flat[:, None], axis=1)[:, 0] - 1
    counts = csum[-1]
    padded = (counts + MOE_BLOCK - 1) // MOE_BLOCK * MOE_BLOCK
    pad_ends = jnp.cumsum(padded)
    dest = (pad_ends - padded)[e_flat] + rank
    src_tok = jnp.zeros((n_blocks * MOE_BLOCK,), jnp.int32).at[dest].set(
        jnp.arange(n_assign, dtype=jnp.int32) // TOP_K)
    block_expert = jnp.minimum(
        jnp.searchsorted(pad_ends, jnp.arange(n_blocks, dtype=jnp.int32) * MOE_BLOCK, side="right"),
        N_EXPERTS - 1).astype(jnp.int32)
    n_used = (pad_ends[-1:] // MOE_BLOCK).astype(jnp.int32)
    x_rows = jnp.take(h2, src_tok, axis=0)
    f = w2.shape[1]
    y_rows = _expert_call(
        block_expert, n_used, x_rows,
        w1[:, :, 0::2].astype(BF16), w1[:, :, 1::2].astype(BF16),
        b1[:, 0::2].reshape(N_EXPERTS, 1, f), b1[:, 1::2].reshape(N_EXPERTS, 1, f),
        w2.astype(BF16), b2.reshape(N_EXPERTS, 1, d))
    picked = jnp.take(y_rows, dest, axis=0).reshape(t, TOP_K, d)
    return jnp.sum(picked * gates[:, :, None], axis=1)


def kernel(x, c, ctx, c_ctx, ada_w, ada_b, norm_g, w_in, w_out, ret_log_decay, ret_gn_w, ret_gn_b,
           diff_lambda, diff_subln, gqa_qk_norm, mla_q_norm, mla_kv_norm, mla_w_uq, mla_w_ukv,
           router_w, router_b, exp_w1, exp_b1, exp_w2, exp_b2):
    b, s, d = x.shape
    n_ctx = ctx.shape[1]
    n = n_ctx + s
    depth = ada_w.shape[0]
    assert n_ctx % TOKEN_TILE == 0 and s % TOKEN_TILE == 0 and s % GRID_W == 0
    assert n_ctx % RET_CHUNK == 0 and n % KV_TILE == 0 and n_ctx % KV_TILE == 0

    tables = _rope_tables(s // GRID_W, n_ctx, HEAD_DIM) + _rope_tables(s // GRID_W, n_ctx, DIFF_D)
    c_rows = jnp.zeros((16, d), F32).at[:b].set(c).at[b].set(c_ctx)
    mods = _ada_call(c_rows, ada_w, ada_b)
    xc = jnp.concatenate([ctx, x], axis=1)

    for l in range(depth):
        lam_init = 0.8 - 0.6 * math.exp(-0.3 * l)
        mod_lat = mods[l, :b].reshape(b, 1, 6, d)
        mod_ctx = jnp.broadcast_to(mods[l, b].reshape(1, 1, 6, d), (b, 1, 6, d))
        modl = jnp.concatenate([mod_ctx, mod_lat], axis=1)

        (rq, rk, rv, rg, dq, dk, dv, gq, gk, gv, mqn, mqr, mkn, mv, mkr) = _in_proj_call(
            xc, modl, norm_g[l, 0:1], _pack_w_in(w_in[l]), tables,
            jnp.tile(gqa_qk_norm[l, 0], N_HEADS)[None, :], jnp.tile(gqa_qk_norm[l, 1], GQA_KV_HEADS)[None, :],
            jnp.pad(mla_q_norm[l], (0, 256 - MLA_Q_RANK))[None, :], mla_kv_norm[l][None, :],
            _pack_w_uq(mla_w_uq[l]), _pack_w_ukv(mla_w_ukv[l]), n_ctx)

        log_g = -jnp.exp(ret_log_decay[l].astype(F32))
        o_f = _ret_call(rq, rk, rv, _ret_tables(log_g[0], False), n_ctx, False)
        ret_y = _ret_call(rq, rk, rv, _ret_tables(log_g[1], True), n_ctx, True,
                          (o_f, rg, ret_gn_w[l][None, :], ret_gn_b[l][None, :]))

        lp = diff_lambda[l].astype(F32)
        lam = jnp.exp(jnp.sum(lp[0] * lp[1])) - jnp.exp(jnp.sum(lp[2] * lp[3])) + lam_init
        o8 = _flash_call(_heads(dq, 2 * N_HEADS), _heads(dk, 2 * N_HEADS), _heads(dv, N_HEADS), n_ctx)
        od = o8[:, 0::2] - lam * o8[:, 1::2]
        od = od * lax.rsqrt(jnp.mean(od * od, axis=-1, keepdims=True) + EPS) * diff_subln[l]
        dif_y = _merge(od * (1.0 - lam_init))

        gqa_y = _merge(_flash_call(_heads(gq, N_HEADS), _heads(gk, GQA_KV_HEADS), _heads(gv, GQA_KV_HEADS), n_ctx))

        q_m = jnp.concatenate([_heads(mqn, N_HEADS), _heads(mqr, N_HEADS)], axis=-1)
        kr = jnp.broadcast_to(mkr[:, None, :, :MLA_ROPE], (b, N_HEADS, n, MLA_ROPE))
        k_m = jnp.concatenate([_heads(mkn, N_HEADS), kr], axis=-1)
        mla_y = _merge(_flash_call(q_m, k_m, _heads(mv, N_HEADS), n_ctx))

        ymix = jnp.concatenate([ret_y, dif_y.astype(BF16), gqa_y.astype(BF16), mla_y.astype(BF16)], axis=-1)
        xc, h2, idx, gates = _out_proj_call(
            ymix, w_out[l].astype(BF16), xc, modl, norm_g[l], router_w[l].T, router_b[l][:, None], n_ctx)

        idx_t = idx.transpose(0, 2, 1).reshape(b * n, TOP_K)
        gates_t = gates.transpose(0, 2, 1).reshape(b * n, TOP_K)
        y = _moe(h2.reshape(b * n, d), idx_t, gates_t, exp_w1[l], exp_b1[l], exp_w2[l], exp_b2[l])
        xc = _ffn_residual_call(xc, y.reshape(b, n, d), modl, norm_g[l], n_ctx)

    return xc[:, n_ctx:]
```

```python
import functools
import math

import jax
import jax.numpy as jnp
from jax import lax
from jax.experimental import pallas as pl
from jax.experimental.pallas import tpu as pltpu

F32 = jnp.float32
BF16 = jnp.bfloat16

GRID_W = 64
ROPE_THETA = 10000.0
EPS = 1e-6
GROUP_WIDTH = 256
HEAD_DIM = 64
N_HEADS = 4
RET_CHUNK = 128
DIFF_D = 32
GQA_KV_HEADS = 2
MLA_Q_RANK = 192
MLA_KV_RANK = 128
MLA_NOPE = 64
MLA_ROPE = 32
N_EXPERTS = 32
TOP_K = 4
SWIGLU_LIMIT = 7.0
SWIGLU_ALPHA = 1.702
MOE_BLOCK = 256

LANES = 128
MXU_DIM = 256
TOKEN_TILE = 256
Q_TILE = 1024
KV_TILE = 512
VMEM_LIMIT = 48 * 1024 * 1024
LOG2E = math.log2(math.e)

IN_PIECES = (
    ("ret_q", 256, 256), ("ret_k", 256, 256), ("ret_v", 256, 256), ("ret_g", 256, 256),
    ("dif_q", 256, 256), ("dif_k", 256, 256), ("dif_v", 256, 256),
    ("gqa_q", 256, 256), ("gqa_k", 128, 128), ("gqa_v", 128, 128),
    ("mla_cq", MLA_Q_RANK, 256), ("mla_ckv", MLA_KV_RANK, 128), ("mla_kr", MLA_ROPE, 128),
)
IN_OFFSETS = {}
_off = 0
for _name, _w, _pw in IN_PIECES:
    IN_OFFSETS[_name] = _off
    _off += _pw
IN_PACKED_WIDTH = _off


def _params(sem):
    return pltpu.CompilerParams(dimension_semantics=sem, vmem_limit_bytes=VMEM_LIMIT)


def _rms(x):
    return x * lax.rsqrt(jnp.mean(x * x, axis=-1, keepdims=True) + EPS)


def _split_bf16(a):
    hi = a.astype(BF16)
    lo = (a - hi.astype(F32)).astype(BF16)
    return hi, lo


def _dot_nt(a, b):
    return lax.dot_general(a, b, (((1,), (1,)), ((), ())), preferred_element_type=F32)


def _dot(a, b):
    return jnp.dot(a, b, preferred_element_type=F32)


def _sigmoid(a):
    return 1.0 / (1.0 + jnp.exp(-a))


def _mod_spec(d, n_lat_tiles):
    return pl.BlockSpec((1, 1, 6, d), lambda i, t: (i, jnp.where(t < n_lat_tiles, 1, 0), 0, 0))


def _ada_kernel(c_ref, w_ref, b_ref, o_ref):
    s = c_ref[...]
    s = s * _sigmoid(s)
    s_hi, s_lo = _split_bf16(s)
    w_hi, w_lo = _split_bf16(w_ref[0])
    o_ref[0] = _dot(s_hi, w_hi) + _dot(s_hi, w_lo) + _dot(s_lo, w_hi) + b_ref[0]


def _ada_call(c_rows, ada_w, ada_b):
    depth, d, n6 = ada_w.shape
    rows = c_rows.shape[0]
    tn = 1536
    return pl.pallas_call(
        _ada_kernel,
        grid=(depth, n6 // tn),
        in_specs=[
            pl.BlockSpec((rows, d), lambda l, j: (0, 0)),
            pl.BlockSpec((1, d, tn), lambda l, j: (l, 0, j)),
            pl.BlockSpec((1, 1, tn), lambda l, j: (l, 0, j)),
        ],
        out_specs=pl.BlockSpec((1, rows, tn), lambda l, j: (l, 0, j)),
        out_shape=jax.ShapeDtypeStruct((depth, rows, n6), F32),
        compiler_params=_params(("parallel", "parallel")),
    )(c_rows, ada_w, ada_b.reshape(depth, 1, n6))


def _rope(x, cos, sin_signed, half):
    outs = []
    for c in range(x.shape[1] // LANES):
        sl = slice(c * LANES, (c + 1) * LANES)
        xc = x[:, sl]
        lane = lax.broadcasted_iota(jnp.int32, xc.shape, 1)
        first_half = (lane % (2 * half)) < half
        partner = jnp.where(first_half, pltpu.roll(xc, LANES - half, 1), pltpu.roll(xc, half, 1))
        outs.append(xc * cos[:, sl] + partner * sin_signed[:, sl])
    return outs[0] if len(outs) == 1 else jnp.concatenate(outs, axis=1)


def _group_mean_sq(x, gsize):
    w = x.shape[1]
    r = lax.broadcasted_iota(jnp.int32, (w, w), 0) // gsize
    c = lax.broadcasted_iota(jnp.int32, (w, w), 1) // gsize
    ones = jnp.where(r == c, 1.0, 0.0).astype(BF16)
    hi, lo = _split_bf16(x * x)
    return (_dot(hi, ones) + _dot(lo, ones)) * (1.0 / gsize)


def _in_proj_kernel(x_ref, mod_ref, g_ref, w_ref, cos64_ref, sin64_ref, cos32_ref, sin32_ref,
                    gqn_ref, gkn_ref, mqn_ref, mkvn_ref, wuq_ref, wukv_ref,
                    rq_ref, rk_ref, rv_ref, rg_ref, dq_ref, dk_ref, dv_ref, gq_ref, gk_ref, gv_ref,
                    mqn_o, mqr_o, mkn_o, mv_o, mkr_o):
    mod = mod_ref[0, 0]
    h = _rms(x_ref[0]) * g_ref[...] * (1.0 + mod[1:2]) + mod[0:1]
    hb = h.astype(BF16)
    cos64, sin64 = cos64_ref[...], sin64_ref[...]
    cos32, sin32 = cos32_ref[...], sin32_ref[...]

    def proj(name, width):
        o = IN_OFFSETS[name]
        return _dot(hb, w_ref[:, o:o + width])

    rq_ref[0] = _rope(proj("ret_q", 256), cos64, sin64, 32).astype(BF16)
    rk_ref[0] = (_rope(proj("ret_k", 256), cos64, sin64, 32) * (HEAD_DIM ** -0.5)).astype(BF16)
    rv_ref[0] = proj("ret_v", 256).astype(BF16)
    rg_ref[0] = proj("ret_g", 256)
    dq_ref[0] = (_rope(proj("dif_q", 256), cos32, sin32, 16) * (DIFF_D ** -0.5 * LOG2E)).astype(BF16)
    dk_ref[0] = _rope(proj("dif_k", 256), cos32, sin32, 16).astype(BF16)
    dv_ref[0] = proj("dif_v", 256).astype(BF16)
    gq = proj("gqa_q", 256)
    gq = gq * lax.rsqrt(_group_mean_sq(gq, HEAD_DIM) + EPS) * gqn_ref[...]
    gq_ref[0] = (_rope(gq, cos64, sin64, 32) * (HEAD_DIM ** -0.5 * LOG2E)).astype(BF16)
    gk = proj("gqa_k", 128)
    gk = gk * lax.rsqrt(_group_mean_sq(gk, HEAD_DIM) + EPS) * gkn_ref[...]
    gk_ref[0] = _rope(gk, cos64[:, :LANES], sin64[:, :LANES], 32).astype(BF16)
    gv_ref[0] = proj("gqa_v", 128).astype(BF16)
    cq = proj("mla_cq", 256)
    cq = cq * lax.rsqrt(jnp.sum(cq * cq, axis=-1, keepdims=True) * (1.0 / MLA_Q_RANK) + EPS) * mqn_ref[...]
    q_up = _dot(cq.astype(BF16), wuq_ref[...])
    mla_scale = (MLA_NOPE + MLA_ROPE) ** -0.5 * LOG2E
    mqn_o[0] = (q_up[:, :256] * mla_scale).astype(BF16)
    mqr_o[0] = (_rope(q_up[:, 256:], cos32[:, :LANES], sin32[:, :LANES], 16) * mla_scale).astype(BF16)
    ckv = proj("mla_ckv", 128)
    ckv = ckv * lax.rsqrt(jnp.mean(ckv * ckv, axis=-1, keepdims=True) + EPS) * mkvn_ref[...]
    kv_up = _dot(ckv.astype(BF16), wukv_ref[...])
    mkn_o[0] = kv_up[:, :256].astype(BF16)
    mv_o[0] = kv_up[:, 256:].astype(BF16)
    mkr_o[0] = _rope(proj("mla_kr", 128), cos32[:, :LANES], sin32[:, :LANES], 16).astype(BF16)


def _in_proj_call(xc, modl, g0, w_in_p, tables, gqn, gkn, mqn, mkvn, wuq_p, wukv_p, n_lat):
    b, n, d = xc.shape
    tm = TOKEN_TILE
    tok = lambda w: pl.BlockSpec((1, tm, w), lambda i, t: (i, t, 0))
    const2 = lambda a: pl.BlockSpec(a.shape, lambda i, t: (0, 0))
    tab = pl.BlockSpec((tm, 256), lambda i, t: (t, 0))
    out_widths = (256, 256, 256, 256, 256, 256, 256, 256, 128, 128, 256, 128, 256, 256, 128)
    out_dtypes = (BF16, BF16, BF16, F32, BF16, BF16, BF16, BF16, BF16, BF16, BF16, BF16, BF16, BF16, BF16)
    return pl.pallas_call(
        _in_proj_kernel,
        grid=(b, n // tm),
        in_specs=[
            tok(d), _mod_spec(d, n_lat // tm),
            const2(g0), const2(w_in_p), tab, tab, tab, tab,
            const2(gqn), const2(gkn), const2(mqn), const2(mkvn), const2(wuq_p), const2(wukv_p),
        ],
        out_specs=[tok(w) for w in out_widths],
        out_shape=[jax.ShapeDtypeStruct((b, n, w), dt) for w, dt in zip(out_widths, out_dtypes)],
        compiler_params=_params(("parallel", "parallel")),
    )(xc, modl, g0, w_in_p, *tables, gqn, gkn, mqn, mkvn, wuq_p, wukv_p)


def _ret_kernel(*refs, readout):
    if readout:
        (q_ref, k_ref, v_ref, dec_ref, xi_ref, zeta_ref, gc_ref, of_ref, g_ref, gnw_ref, gnb_ref,
         o_ref, state_ref) = refs
    else:
        q_ref, k_ref, v_ref, dec_ref, xi_ref, zeta_ref, gc_ref, o_ref, state_ref = refs

    @pl.when(pl.program_id(1) == 0)
    def _():
        state_ref[...] = jnp.zeros_like(state_ref)

    q, k, v = q_ref[0], k_ref[0], v_ref[0]
    outs = []
    for h in range(N_HEADS):
        sl = slice(h * HEAD_DIM, (h + 1) * HEAD_DIM)
        qh, kh, vh = q[:, sl], k[:, sl], v[:, sl]
        st = state_ref[h]
        inner = _dot_nt(qh, kh) * dec_ref[h]
        o = _dot(inner.astype(BF16), vh) + _dot(qh, st.astype(BF16)) * xi_ref[h]
        kz = (kh.astype(F32) * zeta_ref[h]).astype(BF16)
        kv = lax.dot_general(kz, vh, (((0,), (0,)), ((), ())), preferred_element_type=F32)
        state_ref[h] = gc_ref[h] * st + kv
        if readout:
            o = o + of_ref[0][:, sl]
            mu = jnp.mean(o, axis=-1, keepdims=True)
            var = jnp.mean(jnp.square(o - mu), axis=-1, keepdims=True)
            o = (o - mu) * lax.rsqrt(var + EPS)
        outs.append(o)
    o = jnp.concatenate(outs, axis=1)
    if readout:
        g = g_ref[0]
        o_ref[0] = ((o * gnw_ref[...] + gnb_ref[...]) * (g * _sigmoid(g))).astype(o_ref.dtype)
    else:
        o_ref[0] = o


def _ret_call(q, k, v, tabs, n_lat, backward, readout_args=None):
    b, n, w = q.shape
    c = RET_CHUNK
    nch = n // c
    nlc = n_lat // c
    if backward:
        chunk = lambda t: nch - 1 - t
    else:
        chunk = lambda t: jnp.where(t < nch - nlc, nlc + t, t - (nch - nlc))
    tok = lambda dt_w: pl.BlockSpec((1, c, dt_w), lambda i, t: (i, chunk(t), 0))
    const = lambda a: pl.BlockSpec(a.shape, lambda i, t: (0,) * a.ndim)
    in_specs = [tok(w), tok(w), tok(w)] + [const(a) for a in tabs]
    args = [q, k, v, *tabs]
    if readout_args is not None:
        o_f, g, gnw, gnb = readout_args
        in_specs += [tok(w), tok(w), const(gnw), const(gnb)]
        args += [o_f, g, gnw, gnb]
    out_dtype = BF16 if readout_args is not None else F32
    return pl.pallas_call(
        functools.partial(_ret_kernel, readout=readout_args is not None),
        grid=(b, nch),
        in_specs=in_specs,
        out_specs=tok(w),
        out_shape=jax.ShapeDtypeStruct((b, n, w), out_dtype),
        scratch_shapes=[pltpu.VMEM((N_HEADS, HEAD_DIM, HEAD_DIM), F32)],
        compiler_params=_params(("parallel", "arbitrary")),
    )(*args)


def _ret_tables(log_g, backward):
    c = RET_CHUNK
    pos = jnp.arange(c, dtype=F32)
    dist = (pos[None, :] - pos[:, None]) if backward else (pos[:, None] - pos[None, :])
    lg = log_g[:, None, None]
    decay = jnp.where(dist >= 0, jnp.exp(lg * jnp.maximum(dist, 0.0)), 0.0)
    to_state = (c - pos) if backward else (pos + 1.0)
    to_end = pos if backward else (c - 1.0 - pos)
    xi = jnp.exp(log_g[:, None] * to_state)[:, :, None]
    zeta = jnp.exp(log_g[:, None] * to_end)[:, :, None]
    ones = jnp.ones((1, 1, HEAD_DIM), F32)
    gc = jnp.exp(log_g * c)[:, None, None] * jnp.ones((1, HEAD_DIM, HEAD_DIM), F32)
    return decay, xi * ones, zeta * ones, gc


def _attn_kernel(*refs, n_maps, readout_scale):
    if n_maps == 2:
        lam_ref, q_ref, kt_ref, v_ref, sub_ref, o_ref, first_ref = refs
    else:
        q_ref, kt_ref, v_ref, o_ref = refs
    hm = pl.program_id(2)
    q = q_ref[0, 0]
    tq = q.shape[0]
    n_keys = kt_ref.shape[3]
    dv = v_ref.shape[3]
    tk = min(KV_TILE, n_keys)
    n_full, tail = divmod(n_keys, tk)

    def step(off, size, carry):
        m, l, acc = carry
        s = _dot(q, kt_ref[0, 0, :, pl.ds(off, size)])
        m_new = jnp.maximum(m, jnp.max(s, axis=-1, keepdims=True))
        alpha = jnp.exp2(m - m_new)
        p = jnp.exp2(s - m_new)
        l = alpha * l + jnp.sum(p, axis=-1, keepdims=True)
        acc = alpha * acc + _dot(p.astype(BF16), v_ref[0, 0, pl.ds(off, size), :])
        return m_new, l, acc

    carry = (jnp.full((tq, 1), -1e30, F32), jnp.zeros((tq, 1), F32), jnp.zeros((tq, dv), F32))
    carry = lax.fori_loop(0, n_full, lambda j, cr: step(pl.multiple_of(j * tk, tk), tk, cr), carry)
    if tail:
        carry = step(n_full * tk, tail, carry)
    _, l, acc = carry
    o = acc / l

    def store(head_of_step, val):
        for hh in range(N_HEADS):
            @pl.when(head_of_step == hh)
            def _(hh=hh):
                o_ref[0, :, hh * dv:(hh + 1) * dv] = val.astype(o_ref.dtype)

    if n_maps == 1:
        store(hm, o)
    else:
        @pl.when(hm % 2 == 0)
        def _():
            first_ref[...] = o

        @pl.when(hm % 2 == 1)
        def _():
            od = first_ref[...] - lam_ref[0] * o
            od = od * lax.rsqrt(jnp.mean(od * od, axis=-1, keepdims=True) + EPS) * sub_ref[...]
            store(hm // 2, od * readout_scale)


def _attn_call(q, kt, v, rows, keys, tq, diff=None):
    b, hq, n, d = q.shape
    hk, hv, dv = kt.shape[1], v.shape[1], v.shape[3]
    (q0, qn), (k0, kn) = rows, keys
    assert q0 % tq == 0 and qn % tq == 0 and k0 % kn == 0
    n_maps = 2 if diff is not None else 1
    in_specs = [
        pl.BlockSpec((1, 1, tq, d), lambda i, t, h: (i, h, q0 // tq + t, 0)),
        pl.BlockSpec((1, 1, d, kn), lambda i, t, h: (i, h // (hq // hk), 0, k0 // kn)),
        pl.BlockSpec((1, 1, kn, dv), lambda i, t, h: (i, h // (hq // hv), k0 // kn, 0)),
    ]
    args = [q, kt, v]
    scratch = []
    scale = 1.0
    if diff is not None:
        lam, subln, scale = diff
        in_specs = [pl.BlockSpec(memory_space=pltpu.SMEM)] + in_specs + [pl.BlockSpec((1, dv), lambda i, t, h: (0, 0))]
        args = [lam.reshape(1)] + args + [subln.reshape(1, dv)]
        scratch = [pltpu.VMEM((tq, dv), F32)]
    return pl.pallas_call(
        functools.partial(_attn_kernel, n_maps=n_maps, readout_scale=scale),
        grid=(b, qn // tq, hq),
        in_specs=in_specs,
        out_specs=pl.BlockSpec((1, tq, (hq // n_maps) * dv), lambda i, t, h: (i, t, 0)),
        out_shape=jax.ShapeDtypeStruct((b, qn, (hq // n_maps) * dv), BF16),
        scratch_shapes=scratch,
        compiler_params=_params(("parallel", "parallel", "arbitrary")),
    )(*args)


def _out_proj_kernel(y_ref, w_ref, x_ref, mod_ref, g_ref, wr_ref, br_ref, xo_ref, h_ref, idx_ref, gate_ref):
    mod = mod_ref[0, 0]
    g = g_ref[...]
    y = _dot(y_ref[0], w_ref[...])
    xn = x_ref[0] + mod[2:3] * (_rms(y) * g[1:2])
    xo_ref[0] = xn
    h = _rms(xn) * g[2:3] * (1.0 + mod[4:5]) + mod[3:4]
    h_ref[0] = h.astype(BF16)
    w_hi, w_lo = _split_bf16(wr_ref[...])
    h_hi, h_lo = _split_bf16(h)
    logits = _dot_nt(w_hi, h_hi) + _dot_nt(w_hi, h_lo) + _dot_nt(w_lo, h_hi) + br_ref[...]
    row = lax.broadcasted_iota(jnp.int32, logits.shape, 0)
    vals = logits
    tops, idxs = [], []
    for _ in range(TOP_K):
        m = jnp.max(vals, axis=0, keepdims=True)
        idx = jnp.min(jnp.where(vals == m, row, N_EXPERTS), axis=0, keepdims=True)
        tops.append(m)
        idxs.append(idx)
        vals = jnp.where(row == idx, -jnp.inf, vals)
    ex = [jnp.exp(tv - tops[0]) for tv in tops]
    den = ex[0] + ex[1] + ex[2] + ex[3]
    gate_ref[0] = jnp.concatenate([e / den for e in ex], axis=0)
    idx_ref[0] = jnp.concatenate(idxs, axis=0)


def _out_proj_call(ymix, w_out, xc, modl, g, wr_t, br, n_lat, n_rows):
    b, n, d = xc.shape
    tm = TOKEN_TILE
    tok = lambda w: pl.BlockSpec((1, tm, w), lambda i, t: (i, t, 0))
    const2 = lambda a: pl.BlockSpec(a.shape, lambda i, t: (0, 0))
    sel = pl.BlockSpec((1, TOP_K, tm), lambda i, t: (i, 0, t))
    return pl.pallas_call(
        _out_proj_kernel,
        grid=(b, n_rows // tm),
        in_specs=[
            tok(ymix.shape[2]), const2(w_out), tok(d), _mod_spec(d, n_lat // tm),
            const2(g), const2(wr_t), const2(br),
        ],
        out_specs=[tok(d), tok(d), sel, sel],
        out_shape=[
            jax.ShapeDtypeStruct((b, n_rows, d), F32),
            jax.ShapeDtypeStruct((b, n_rows, d), BF16),
            jax.ShapeDtypeStruct((b, TOP_K, n_rows), jnp.int32),
            jax.ShapeDtypeStruct((b, TOP_K, n_rows), F32),
        ],
        compiler_params=_params(("parallel", "parallel")),
    )(ymix, w_out, xc, modl, g, wr_t, br)


def _w1_prep_kernel(w_ref, g_ref, l_ref):
    r = lax.broadcasted_iota(jnp.int32, (MXU_DIM, MXU_DIM), 0)
    c = lax.broadcasted_iota(jnp.int32, (MXU_DIM, MXU_DIM), 1)
    src = jnp.where(c < LANES, 2 * c, 2 * (c - LANES) + 1)
    perm = jnp.where(r == src, 1.0, 0.0).astype(BF16)
    out = _dot(w_ref[0].astype(BF16), perm)
    g_ref[0] = out[:, :LANES].astype(BF16)
    l_ref[0] = out[:, LANES:].astype(BF16)


def _w1_prep_call(w1):
    e, d, f2 = w1.shape
    f = f2 // 2
    out_spec = pl.BlockSpec((1, d, LANES), lambda i, j: (i, 0, j))
    return pl.pallas_call(
        _w1_prep_kernel,
        grid=(e, f2 // MXU_DIM),
        in_specs=[pl.BlockSpec((1, d, MXU_DIM), lambda i, j: (i, 0, j))],
        out_specs=[out_spec, out_spec],
        out_shape=[jax.ShapeDtypeStruct((e, d, f), BF16)] * 2,
        compiler_params=_params(("parallel", "parallel")),
    )(w1)


def _expert_kernel(be_ref, nu_ref, x_ref, w1g_ref, w1l_ref, b1g_ref, b1l_ref, w2_ref, b2_ref, o_ref):
    del be_ref
    i = pl.program_id(0)

    @pl.when(i < nu_ref[0])
    def _():
        x = x_ref[...]
        glu = jnp.minimum(_dot(x, w1g_ref[0]) + b1g_ref[0], SWIGLU_LIMIT)
        lin = jnp.clip(_dot(x, w1l_ref[0]) + b1l_ref[0], -SWIGLU_LIMIT, SWIGLU_LIMIT)
        act = glu * _sigmoid(SWIGLU_ALPHA * glu) * (lin + 1.0)
        o_ref[...] = _dot(act.astype(BF16), w2_ref[0]) + b2_ref[0]

    @pl.when(i >= nu_ref[0])
    def _():
        o_ref[...] = jnp.zeros_like(o_ref)


def _expert_call(block_expert, n_used, x_rows, w1g, w1l, b1g, b1l, w2, b2):
    r, d = x_rows.shape
    f = w1g.shape[2]
    tm = MOE_BLOCK
    wspec = lambda s: pl.BlockSpec((1,) + s, lambda i, be, nu: (be[i], 0, 0))
    return pl.pallas_call(
        _expert_kernel,
        grid_spec=pltpu.PrefetchScalarGridSpec(
            num_scalar_prefetch=2,
            grid=(r // tm,),
            in_specs=[
                pl.BlockSpec((tm, d), lambda i, be, nu: (i, 0)),
                wspec((d, f)), wspec((d, f)), wspec((1, f)), wspec((1, f)), wspec((f, d)), wspec((1, d)),
            ],
            out_specs=pl.BlockSpec((tm, d), lambda i, be, nu: (i, 0)),
        ),
        out_shape=jax.ShapeDtypeStruct((r, d), F32),
        compiler_params=_params(("arbitrary",)),
    )(block_expert, n_used, x_rows, w1g, w1l, b1g, b1l, w2, b2)


def _ffn_residual_kernel(x_ref, y_ref, mod_ref, g_ref, o_ref):
    mod = mod_ref[0, 0]
    o_ref[0] = x_ref[0] + mod[5:6] * (_rms(y_ref[0]) * g_ref[...][3:4])


def _ffn_residual_call(xc, y, modl, g, n_lat):
    b, n, d = xc.shape
    tm = TOKEN_TILE
    tok = pl.BlockSpec((1, tm, d), lambda i, t: (i, t, 0))
    return pl.pallas_call(
        _ffn_residual_kernel,
        grid=(b, n // tm),
        in_specs=[tok, tok, _mod_spec(d, n_lat // tm), pl.BlockSpec(g.shape, lambda i, t: (0, 0))],
        out_specs=tok,
        out_shape=jax.ShapeDtypeStruct((b, n, d), F32),
        compiler_params=_params(("parallel", "parallel")),
    )(xc, y, modl, g)


def _rope_tables(rows, n_ctx, rot_dim):
    row = jnp.repeat(jnp.arange(rows), GRID_W)
    col = jnp.tile(jnp.arange(GRID_W), rows)
    n_freq = rot_dim // 4
    inv = ROPE_THETA ** (-jnp.arange(n_freq, dtype=F32) / n_freq)
    ang = jnp.concatenate([row[:, None] * inv, col[:, None] * inv], axis=-1)
    cos, sin = jnp.cos(ang), jnp.sin(ang)
    reps = 256 // rot_dim
    cos_t = jnp.tile(jnp.concatenate([cos, cos], axis=-1), (1, reps))
    sin_t = jnp.tile(jnp.concatenate([-sin, sin], axis=-1), (1, reps))
    cos_t = jnp.concatenate([cos_t, jnp.ones((n_ctx, 256), F32)], axis=0)
    sin_t = jnp.concatenate([sin_t, jnp.zeros((n_ctx, 256), F32)], axis=0)
    return cos_t, sin_t


def _pack_w_in(w):
    parts, src = [], 0
    for _, width, padded in IN_PIECES:
        parts.append(w[:, src:src + width])
        if padded > width:
            parts.append(jnp.zeros((w.shape[0], padded - width), w.dtype))
        src += width
    return jnp.concatenate(parts, axis=1).astype(BF16)


def _pack_w_uq(w):
    wh = w.reshape(MLA_Q_RANK, N_HEADS, MLA_NOPE + MLA_ROPE)
    nope = wh[:, :, :MLA_NOPE].reshape(MLA_Q_RANK, N_HEADS * MLA_NOPE)
    rope = wh[:, :, MLA_NOPE:].reshape(MLA_Q_RANK, N_HEADS * MLA_ROPE)
    packed = jnp.concatenate([nope, rope], axis=1)
    return jnp.pad(packed, ((0, 256 - MLA_Q_RANK), (0, 0))).astype(BF16)


def _pack_w_ukv(w):
    wh = w.reshape(MLA_KV_RANK, N_HEADS, MLA_NOPE + HEAD_DIM)
    kn = wh[:, :, :MLA_NOPE].reshape(MLA_KV_RANK, N_HEADS * MLA_NOPE)
    vv = wh[:, :, MLA_NOPE:].reshape(MLA_KV_RANK, N_HEADS * HEAD_DIM)
    return jnp.concatenate([kn, vv], axis=1).astype(BF16)


def _heads(a, n_heads):
    b, n, w = a.shape
    return a.reshape(b, n, n_heads, w // n_heads).transpose(0, 2, 1, 3)


def _heads_t(a, n_heads):
    b, n, w = a.shape
    return a.reshape(b, n, n_heads, w // n_heads).transpose(0, 2, 3, 1)


def _attention(q, kt, v, n_lat, n_ctx, ctx_out, diff=None):
    n = n_lat + n_ctx
    y = _attn_call(q, kt, v, (0, n_lat), (0, n), min(Q_TILE, n_lat), diff)
    if ctx_out:
        y_c = _attn_call(q, kt, v, (n_lat, n_ctx), (n_lat, n_ctx), n_ctx, diff)
    else:
        y_c = jnp.zeros((y.shape[0], n_ctx, y.shape[2]), y.dtype)
    return jnp.concatenate([y, y_c], axis=1)


def _moe(h2, idx, gates, w1, b1, w2, b2):
    t, d = h2.shape
    e_flat = idx.reshape(-1)
    n_assign = t * TOP_K
    n_blocks = -(-n_assign // MOE_BLOCK) + N_EXPERTS
    onehot = (e_flat[:, None] == jnp.arange(N_EXPERTS, dtype=jnp.int32)[None, :]).astype(jnp.int32)
    csum = jnp.cumsum(onehot, axis=0)
    rank = jnp.take_along_axis(csum, e_flat[:, None], axis=1)[:, 0] - 1
    counts = csum[-1]
    padded = (counts + MOE_BLOCK - 1) // MOE_BLOCK * MOE_BLOCK
    pad_ends = jnp.cumsum(padded)
    dest = (pad_ends - padded)[e_flat] + rank
    src_tok = jnp.zeros((n_blocks * MOE_BLOCK,), jnp.int32).at[dest].set(
        jnp.arange(n_assign, dtype=jnp.int32) // TOP_K)
    block_expert = jnp.minimum(
        jnp.searchsorted(pad_ends, jnp.arange(n_blocks, dtype=jnp.int32) * MOE_BLOCK, side="right"),
        N_EXPERTS - 1).astype(jnp.int32)
    n_used = (pad_ends[-1:] // MOE_BLOCK).astype(jnp.int32)
    x_rows = jnp.take(h2, src_tok, axis=0)
    f = w2.shape[1]
    w1g, w1l = _w1_prep_call(w1)
    y_rows = _expert_call(
        block_expert, n_used, x_rows, w1g, w1l,
        b1[:, 0::2].reshape(N_EXPERTS, 1, f), b1[:, 1::2].reshape(N_EXPERTS, 1, f),
        w2.astype(BF16), b2.reshape(N_EXPERTS, 1, d))
    picked = jnp.take(y_rows, dest, axis=0).reshape(t, TOP_K, d)
    return jnp.sum(picked * gates[:, :, None], axis=1)


def kernel(x, c, ctx, c_ctx, ada_w, ada_b, norm_g, w_in, w_out, ret_log_decay, ret_gn_w, ret_gn_b,
           diff_lambda, diff_subln, gqa_qk_norm, mla_q_norm, mla_kv_norm, mla_w_uq, mla_w_ukv,
           router_w, router_b, exp_w1, exp_b1, exp_w2, exp_b2):
    b, s, d = x.shape
    n_ctx = ctx.shape[1]
    n = s + n_ctx
    depth = ada_w.shape[0]
    assert n_ctx % TOKEN_TILE == 0 and s % TOKEN_TILE == 0 and s % GRID_W == 0
    assert n_ctx % RET_CHUNK == 0 and s % min(Q_TILE, s) == 0 and s % n_ctx == 0

    tables = _rope_tables(s // GRID_W, n_ctx, HEAD_DIM) + _rope_tables(s // GRID_W, n_ctx, DIFF_D)
    c_rows = jnp.zeros((16, d), F32).at[:b].set(c).at[b].set(c_ctx)
    mods = _ada_call(c_rows, ada_w, ada_b)
    xc = jnp.concatenate([x, ctx], axis=1)

    for l in range(depth):
        last = l == depth - 1
        lam_init = 0.8 - 0.6 * math.exp(-0.3 * l)
        mod_lat = mods[l, :b].reshape(b, 1, 6, d)
        mod_ctx = jnp.broadcast_to(mods[l, b].reshape(1, 1, 6, d), (b, 1, 6, d))
        modl = jnp.concatenate([mod_ctx, mod_lat], axis=1)

        (rq, rk, rv, rg, dq, dk, dv, gq, gk, gv, mqn, mqr, mkn, mv, mkr) = _in_proj_call(
            xc, modl, norm_g[l, 0:1], _pack_w_in(w_in[l]), tables,
            jnp.tile(gqa_qk_norm[l, 0], N_HEADS)[None, :], jnp.tile(gqa_qk_norm[l, 1], GQA_KV_HEADS)[None, :],
            jnp.pad(mla_q_norm[l], (0, 256 - MLA_Q_RANK))[None, :], mla_kv_norm[l][None, :],
            _pack_w_uq(mla_w_uq[l]), _pack_w_ukv(mla_w_ukv[l]), s)

        log_g = -jnp.exp(ret_log_decay[l].astype(F32))
        o_f = _ret_call(rq, rk, rv, _ret_tables(log_g[0], False), s, False)
        ret_y = _ret_call(rq, rk, rv, _ret_tables(log_g[1], True), s, True,
                          (o_f, rg, ret_gn_w[l][None, :], ret_gn_b[l][None, :]))

        lp = diff_lambda[l].astype(F32)
        lam = jnp.exp(jnp.sum(lp[0] * lp[1])) - jnp.exp(jnp.sum(lp[2] * lp[3])) + lam_init
        dif_y = _attention(_heads(dq, 2 * N_HEADS), _heads_t(dk, 2 * N_HEADS), _heads(dv, N_HEADS), s, n_ctx,
                           not last, (lam, diff_subln[l], 1.0 - lam_init))

        gqa_y = _attention(_heads(gq, N_HEADS), _heads_t(gk, GQA_KV_HEADS), _heads(gv, GQA_KV_HEADS), s, n_ctx,
                           not last)

        q_m = jnp.concatenate([_heads(mqn, N_HEADS), _heads(mqr, N_HEADS)], axis=-1)
        kr_t = jnp.broadcast_to(mkr[:, None, :, :MLA_ROPE].transpose(0, 1, 3, 2), (b, N_HEADS, MLA_ROPE, n))
        kt_m = jnp.concatenate([_heads_t(mkn, N_HEADS), kr_t], axis=2)
        mla_y = _attention(q_m, kt_m, _heads(mv, N_HEADS), s, n_ctx, not last)

        ymix = jnp.concatenate([ret_y, dif_y, gqa_y, mla_y], axis=-1)
        n_rows = s if last else n
        x_new, h2, idx, gates = _out_proj_call(
            ymix, w_out[l].astype(BF16), xc, modl, norm_g[l], router_w[l].T, router_b[l][:, None], s, n_rows)

        idx_t = idx.transpose(0, 2, 1).reshape(b * n_rows, TOP_K)
        gates_t = gates.transpose(0, 2, 1).reshape(b * n_rows, TOP_K)
        y = _moe(h2.reshape(b * n_rows, d), idx_t, gates_t, exp_w1[l], exp_b1[l], exp_w2[l], exp_b2[l])
        xc = _ffn_residual_call(x_new, y.reshape(b, n_rows, d), modl, norm_g[l], s)

    return xc[:, :s]
```

```python
import functools
import math

import jax
import jax.numpy as jnp
from jax import lax
from jax.experimental import pallas as pl
from jax.experimental.pallas import tpu as pltpu

F32 = jnp.float32
BF16 = jnp.bfloat16

GRID_W = 64
ROPE_THETA = 10000.0
EPS = 1e-6
GROUP_WIDTH = 256
HEAD_DIM = 64
N_HEADS = 4
RET_CHUNK = 128
DIFF_D = 32
GQA_KV_HEADS = 2
MLA_Q_RANK = 192
MLA_KV_RANK = 128
MLA_NOPE = 64
MLA_ROPE = 32
N_EXPERTS = 32
TOP_K = 4
SWIGLU_LIMIT = 7.0
SWIGLU_ALPHA = 1.702
MOE_BLOCK = 256

LANES = 128
MXU_DIM = 256
TOKEN_TILE = 256
Q_TILE = 1024
KV_TILE = 512
VMEM_LIMIT = 48 * 1024 * 1024
LOG2E = math.log2(math.e)

IN_PIECES = (
    ("ret_q", 256, 256), ("ret_k", 256, 256), ("ret_v", 256, 256), ("ret_g", 256, 256),
    ("dif_q", 256, 256), ("dif_k", 256, 256), ("dif_v", 256, 256),
    ("gqa_q", 256, 256), ("gqa_k", 128, 128), ("gqa_v", 128, 128),
    ("mla_cq", MLA_Q_RANK, 256), ("mla_ckv", MLA_KV_RANK, 128), ("mla_kr", MLA_ROPE, 128),
)
IN_OFFSETS = {}
_off = 0
for _name, _w, _pw in IN_PIECES:
    IN_OFFSETS[_name] = _off
    _off += _pw
IN_PACKED_WIDTH = _off


def _params(sem):
    return pltpu.CompilerParams(dimension_semantics=sem, vmem_limit_bytes=VMEM_LIMIT)


def _rms(x):
    return x * lax.rsqrt(jnp.mean(x * x, axis=-1, keepdims=True) + EPS)


def _split_bf16(a):
    hi = a.astype(BF16)
    lo = (a - hi.astype(F32)).astype(BF16)
    return hi, lo


def _dot_nt(a, b):
    return lax.dot_general(a, b, (((1,), (1,)), ((), ())), preferred_element_type=F32)


def _dot(a, b):
    return jnp.dot(a, b, preferred_element_type=F32)


def _sigmoid(a):
    return 1.0 / (1.0 + jnp.exp(-a))


def _mod_spec(d, n_lat_tiles):
    return pl.BlockSpec((1, 1, 6, d), lambda i, t: (i, jnp.where(t < n_lat_tiles, 1, 0), 0, 0))


def _ada_kernel(c_ref, w_ref, b_ref, o_ref):
    s = c_ref[...]
    s = s * _sigmoid(s)
    s_hi, s_lo = _split_bf16(s)
    w_hi, w_lo = _split_bf16(w_ref[0])
    o_ref[0] = _dot(s_hi, w_hi) + _dot(s_hi, w_lo) + _dot(s_lo, w_hi) + b_ref[0]


def _ada_call(c_rows, ada_w, ada_b):
    depth, d, n6 = ada_w.shape
    rows = c_rows.shape[0]
    tn = 1536
    return pl.pallas_call(
        _ada_kernel,
        grid=(depth, n6 // tn),
        in_specs=[
            pl.BlockSpec((rows, d), lambda l, j: (0, 0)),
            pl.BlockSpec((1, d, tn), lambda l, j: (l, 0, j)),
            pl.BlockSpec((1, 1, tn), lambda l, j: (l, 0, j)),
        ],
        out_specs=pl.BlockSpec((1, rows, tn), lambda l, j: (l, 0, j)),
        out_shape=jax.ShapeDtypeStruct((depth, rows, n6), F32),
        compiler_params=_params(("parallel", "parallel")),
    )(c_rows, ada_w, ada_b.reshape(depth, 1, n6))


def _rope(x, cos, sin_signed, half):
    outs = []
    for c in range(x.shape[1] // LANES):
        sl = slice(c * LANES, (c + 1) * LANES)
        xc = x[:, sl]
        lane = lax.broadcasted_iota(jnp.int32, xc.shape, 1)
        first_half = (lane % (2 * half)) < half
        partner = jnp.where(first_half, pltpu.roll(xc, LANES - half, 1), pltpu.roll(xc, half, 1))
        outs.append(xc * cos[:, sl] + partner * sin_signed[:, sl])
    return outs[0] if len(outs) == 1 else jnp.concatenate(outs, axis=1)


def _group_mean_sq(x, gsize):
    w = x.shape[1]
    r = lax.broadcasted_iota(jnp.int32, (w, w), 0) // gsize
    c = lax.broadcasted_iota(jnp.int32, (w, w), 1) // gsize
    ones = jnp.where(r == c, 1.0, 0.0).astype(BF16)
    hi, lo = _split_bf16(x * x)
    return (_dot(hi, ones) + _dot(lo, ones)) * (1.0 / gsize)


def _in_proj_kernel(x_ref, mod_ref, g_ref, w_ref, cos64_ref, sin64_ref, cos32_ref, sin32_ref,
                    gqn_ref, gkn_ref, mqn_ref, mkvn_ref, wuq_ref, wukv_ref,
                    rq_ref, rk_ref, rv_ref, rg_ref, dq_ref, dk_ref, dv_ref, gq_ref, gk_ref, gv_ref,
                    mqn_o, mqr_o, mkn_o, mv_o, mkr_o):
    mod = mod_ref[0, 0]
    h = _rms(x_ref[0]) * g_ref[...] * (1.0 + mod[1:2]) + mod[0:1]
    hb = h.astype(BF16)
    cos64, sin64 = cos64_ref[...], sin64_ref[...]
    cos32, sin32 = cos32_ref[...], sin32_ref[...]

    def proj(name, width):
        o = IN_OFFSETS[name]
        return _dot(hb, w_ref[:, o:o + width])

    rq_ref[0] = _rope(proj("ret_q", 256), cos64, sin64, 32).astype(BF16)
    rk_ref[0] = (_rope(proj("ret_k", 256), cos64, sin64, 32) * (HEAD_DIM ** -0.5)).astype(BF16)
    rv_ref[0] = proj("ret_v", 256).astype(BF16)
    rg_ref[0] = proj("ret_g", 256)
    dq_ref[0] = (_rope(proj("dif_q", 256), cos32, sin32, 16) * (DIFF_D ** -0.5 * LOG2E)).astype(BF16)
    dk_ref[0] = _rope(proj("dif_k", 256), cos32, sin32, 16).astype(BF16)
    dv_ref[0] = proj("dif_v", 256).astype(BF16)
    gq = proj("gqa_q", 256)
    gq = gq * lax.rsqrt(_group_mean_sq(gq, HEAD_DIM) + EPS) * gqn_ref[...]
    gq_ref[0] = (_rope(gq, cos64, sin64, 32) * (HEAD_DIM ** -0.5 * LOG2E)).astype(BF16)
    gk = proj("gqa_k", 128)
    gk = gk * lax.rsqrt(_group_mean_sq(gk, HEAD_DIM) + EPS) * gkn_ref[...]
    gk_ref[0] = _rope(gk, cos64[:, :LANES], sin64[:, :LANES], 32).astype(BF16)
    gv_ref[0] = proj("gqa_v", 128).astype(BF16)
    cq = proj("mla_cq", 256)
    cq = cq * lax.rsqrt(jnp.sum(cq * cq, axis=-1, keepdims=True) * (1.0 / MLA_Q_RANK) + EPS) * mqn_ref[...]
    q_up = _dot(cq.astype(BF16), wuq_ref[...])
    mla_scale = (MLA_NOPE + MLA_ROPE) ** -0.5 * LOG2E
    mqn_o[0] = (q_up[:, :256] * mla_scale).astype(BF16)
    mqr_o[0] = (_rope(q_up[:, 256:], cos32[:, :LANES], sin32[:, :LANES], 16) * mla_scale).astype(BF16)
    ckv = proj("mla_ckv", 128)
    ckv = ckv * lax.rsqrt(jnp.mean(ckv * ckv, axis=-1, keepdims=True) + EPS) * mkvn_ref[...]
    kv_up = _dot(ckv.astype(BF16), wukv_ref[...])
    mkn_o[0] = kv_up[:, :256].astype(BF16)
    mv_o[0] = kv_up[:, 256:].astype(BF16)
    mkr_o[0] = _rope(proj("mla_kr", 128), cos32[:, :LANES], sin32[:, :LANES], 16).astype(BF16)


def _in_proj_call(xc, modl, g0, w_in_p, tables, gqn, gkn, mqn, mkvn, wuq_p, wukv_p, n_lat):
    b, n, d = xc.shape
    tm = TOKEN_TILE
    tok = lambda w: pl.BlockSpec((1, tm, w), lambda i, t: (i, t, 0))
    const2 = lambda a: pl.BlockSpec(a.shape, lambda i, t: (0, 0))
    tab = pl.BlockSpec((tm, 256), lambda i, t: (t, 0))
    out_widths = (256, 256, 256, 256, 256, 256, 256, 256, 128, 128, 256, 128, 256, 256, 128)
    out_dtypes = (BF16, BF16, BF16, F32, BF16, BF16, BF16, BF16, BF16, BF16, BF16, BF16, BF16, BF16, BF16)
    return pl.pallas_call(
        _in_proj_kernel,
        grid=(b, n // tm),
        in_specs=[
            tok(d), _mod_spec(d, n_lat // tm),
            const2(g0), const2(w_in_p), tab, tab, tab, tab,
            const2(gqn), const2(gkn), const2(mqn), const2(mkvn), const2(wuq_p), const2(wukv_p),
        ],
        out_specs=[tok(w) for w in out_widths],
        out_shape=[jax.ShapeDtypeStruct((b, n, w), dt) for w, dt in zip(out_widths, out_dtypes)],
        compiler_params=_params(("parallel", "parallel")),
    )(xc, modl, g0, w_in_p, *tables, gqn, gkn, mqn, mkvn, wuq_p, wukv_p)


def _ret_kernel(*refs, readout):
    if readout:
        (q_ref, k_ref, v_ref, dec_ref, xi_ref, zeta_ref, gc_ref, of_ref, g_ref, gnw_ref, gnb_ref,
         o_ref, state_ref) = refs
    else:
        q_ref, k_ref, v_ref, dec_ref, xi_ref, zeta_ref, gc_ref, o_ref, state_ref = refs

    @pl.when(pl.program_id(1) == 0)
    def _():
        state_ref[...] = jnp.zeros_like(state_ref)

    q, k, v = q_ref[0], k_ref[0], v_ref[0]
    outs = []
    for h in range(N_HEADS):
        sl = slice(h * HEAD_DIM, (h + 1) * HEAD_DIM)
        qh, kh, vh = q[:, sl], k[:, sl], v[:, sl]
        st = state_ref[h]
        inner = _dot_nt(qh, kh) * dec_ref[h]
        o = _dot(inner.astype(BF16), vh) + _dot(qh, st.astype(BF16)) * xi_ref[h]
        kz = (kh.astype(F32) * zeta_ref[h]).astype(BF16)
        kv = lax.dot_general(kz, vh, (((0,), (0,)), ((), ())), preferred_element_type=F32)
        state_ref[h] = gc_ref[h] * st + kv
        if readout:
            o = o + of_ref[0][:, sl]
            mu = jnp.mean(o, axis=-1, keepdims=True)
            var = jnp.mean(jnp.square(o - mu), axis=-1, keepdims=True)
            o = (o - mu) * lax.rsqrt(var + EPS)
        outs.append(o)
    o = jnp.concatenate(outs, axis=1)
    if readout:
        g = g_ref[0]
        o_ref[0] = ((o * gnw_ref[...] + gnb_ref[...]) * (g * _sigmoid(g))).astype(o_ref.dtype)
    else:
        o_ref[0] = o


def _ret_call(q, k, v, tabs, n_lat, backward, readout_args=None):
    b, n, w = q.shape
    c = RET_CHUNK
    nch = n // c
    nlc = n_lat // c
    if backward:
        chunk = lambda t: nch - 1 - t
    else:
        chunk = lambda t: jnp.where(t < nch - nlc, nlc + t, t - (nch - nlc))
    tok = lambda dt_w: pl.BlockSpec((1, c, dt_w), lambda i, t: (i, chunk(t), 0))
    const = lambda a: pl.BlockSpec(a.shape, lambda i, t: (0,) * a.ndim)
    in_specs = [tok(w), tok(w), tok(w)] + [const(a) for a in tabs]
    args = [q, k, v, *tabs]
    if readout_args is not None:
        o_f, g, gnw, gnb = readout_args
        in_specs += [tok(w), tok(w), const(gnw), const(gnb)]
        args += [o_f, g, gnw, gnb]
    out_dtype = BF16 if readout_args is not None else F32
    return pl.pallas_call(
        functools.partial(_ret_kernel, readout=readout_args is not None),
        grid=(b, nch),
        in_specs=in_specs,
        out_specs=tok(w),
        out_shape=jax.ShapeDtypeStruct((b, n, w), out_dtype),
        scratch_shapes=[pltpu.VMEM((N_HEADS, HEAD_DIM, HEAD_DIM), F32)],
        compiler_params=_params(("parallel", "arbitrary")),
    )(*args)


def _ret_tables(log_g, backward):
    c = RET_CHUNK
    pos = jnp.arange(c, dtype=F32)
    dist = (pos[None, :] - pos[:, None]) if backward else (pos[:, None] - pos[None, :])
    lg = log_g[:, None, None]
    decay = jnp.where(dist >= 0, jnp.exp(lg * jnp.maximum(dist, 0.0)), 0.0)
    to_state = (c - pos) if backward else (pos + 1.0)
    to_end = pos if backward else (c - 1.0 - pos)
    xi = jnp.exp(log_g[:, None] * to_state)[:, :, None]
    zeta = jnp.exp(log_g[:, None] * to_end)[:, :, None]
    ones = jnp.ones((1, 1, HEAD_DIM), F32)
    gc = jnp.exp(log_g * c)[:, None, None] * jnp.ones((1, HEAD_DIM, HEAD_DIM), F32)
    return decay, xi * ones, zeta * ones, gc


def _attn_kernel(*refs, n_maps, readout_scale):
    if n_maps == 2:
        lam_ref, q_ref, kt_ref, v_ref, sub_ref, o_ref, first_ref = refs
    else:
        q_ref, kt_ref, v_ref, o_ref = refs
    hm = pl.program_id(2)
    q = q_ref[0, 0]
    tq = q.shape[0]
    n_keys = kt_ref.shape[3]
    dv = v_ref.shape[3] // 2
    tk = min(KV_TILE, n_keys)
    n_full, tail = divmod(n_keys, tk)

    def step(off, size, carry):
        m, acc = carry
        s = _dot(q, kt_ref[0, 0, :, pl.ds(off, size)])
        m_new = jnp.maximum(m, jnp.max(s, axis=-1, keepdims=True))
        p = jnp.exp2(s - m_new)
        acc = jnp.exp2(m - m_new) * acc + _dot(p.astype(BF16), v_ref[0, 0, pl.ds(off, size), :])
        return m_new, acc

    carry = (jnp.full((tq, 1), -1e30, F32), jnp.zeros((tq, 2 * dv), F32))
    carry = lax.fori_loop(0, n_full, lambda j, cr: step(pl.multiple_of(j * tk, tk), tk, cr), carry)
    if tail:
        carry = step(n_full * tk, tail, carry)
    acc = carry[1]
    o = acc[:, :dv] / acc[:, dv:]

    def store(head_of_step, val):
        for hh in range(N_HEADS):
            @pl.when(head_of_step == hh)
            def _(hh=hh):
                o_ref[0, :, hh * dv:(hh + 1) * dv] = val.astype(o_ref.dtype)

    if n_maps == 1:
        store(hm, o)
    else:
        @pl.when(hm % 2 == 0)
        def _():
            first_ref[...] = o

        @pl.when(hm % 2 == 1)
        def _():
            od = first_ref[...] - lam_ref[0] * o
            od = od * lax.rsqrt(jnp.mean(od * od, axis=-1, keepdims=True) + EPS) * sub_ref[...]
            store(hm // 2, od * readout_scale)


def _attn_call(q, kt, v, rows, keys, tq, diff=None):
    b, hq, n, d = q.shape
    hk, hv, dv = kt.shape[1], v.shape[1], v.shape[3] // 2
    (q0, qn), (k0, kn) = rows, keys
    assert q0 % tq == 0 and qn % tq == 0 and k0 % kn == 0
    n_maps = 2 if diff is not None else 1
    in_specs = [
        pl.BlockSpec((1, 1, tq, d), lambda i, t, h: (i, h, q0 // tq + t, 0)),
        pl.BlockSpec((1, 1, d, kn), lambda i, t, h: (i, h // (hq // hk), 0, k0 // kn)),
        pl.BlockSpec((1, 1, kn, 2 * dv), lambda i, t, h: (i, h // (hq // hv), k0 // kn, 0)),
    ]
    args = [q, kt, v]
    scratch = []
    scale = 1.0
    if diff is not None:
        lam, subln, scale = diff
        in_specs = [pl.BlockSpec(memory_space=pltpu.SMEM)] + in_specs + [pl.BlockSpec((1, dv), lambda i, t, h: (0, 0))]
        args = [lam.reshape(1)] + args + [subln.reshape(1, dv)]
        scratch = [pltpu.VMEM((tq, dv), F32)]
    return pl.pallas_call(
        functools.partial(_attn_kernel, n_maps=n_maps, readout_scale=scale),
        grid=(b, qn // tq, hq),
        in_specs=in_specs,
        out_specs=pl.BlockSpec((1, tq, (hq // n_maps) * dv), lambda i, t, h: (i, t, 0)),
        out_shape=jax.ShapeDtypeStruct((b, qn, (hq // n_maps) * dv), BF16),
        scratch_shapes=scratch,
        compiler_params=_params(("parallel", "parallel", "arbitrary")),
    )(*args)


def _out_proj_kernel(y_ref, w_ref, x_ref, mod_ref, g_ref, wr_ref, br_ref,
                     xo_ref, h_ref, idx_ref, gate_ref, rank_ref, cnt_ref):
    mod = mod_ref[0, 0]
    g = g_ref[...]
    y = _dot(y_ref[0], w_ref[...])
    xn = x_ref[0] + mod[2:3] * (_rms(y) * g[1:2])
    xo_ref[0] = xn
    h = _rms(xn) * g[2:3] * (1.0 + mod[4:5]) + mod[3:4]
    h_ref[0] = h.astype(BF16)
    w_hi, w_lo = _split_bf16(wr_ref[...])
    h_hi, h_lo = _split_bf16(h)
    logits = _dot_nt(w_hi, h_hi) + _dot_nt(w_hi, h_lo) + _dot_nt(w_lo, h_hi) + br_ref[...]
    row = lax.broadcasted_iota(jnp.int32, logits.shape, 0)
    vals = logits
    tops, idxs = [], []
    for _ in range(TOP_K):
        m = jnp.max(vals, axis=0, keepdims=True)
        idx = jnp.min(jnp.where(vals == m, row, N_EXPERTS), axis=0, keepdims=True)
        tops.append(m)
        idxs.append(idx)
        vals = jnp.where(row == idx, -jnp.inf, vals)
    ex = [jnp.exp(tv - tops[0]) for tv in tops]
    den = ex[0] + ex[1] + ex[2] + ex[3]
    gate_ref[0] = jnp.concatenate([e / den for e in ex], axis=0)
    idx_ref[0] = jnp.concatenate(idxs, axis=0)
    tm = logits.shape[1]
    before = jnp.where(lax.broadcasted_iota(jnp.int32, (tm, tm), 0) < lax.broadcasted_iota(jnp.int32, (tm, tm), 1),
                       1.0, 0.0).astype(BF16)
    base = jnp.zeros((N_EXPERTS, 1), F32)
    ranks = []
    for idx in idxs:
        onehot = jnp.where(row == idx, 1.0, 0.0)
        prefix = _dot(onehot.astype(BF16), before)
        ranks.append(jnp.sum(onehot * (prefix + base), axis=0, keepdims=True))
        base = base + jnp.sum(onehot, axis=1, keepdims=True)
    rank_ref[0] = jnp.concatenate(ranks, axis=0).astype(jnp.int32)
    cnt_ref[0, 0] = jnp.broadcast_to(base, (N_EXPERTS, LANES)).astype(jnp.int32)


def _out_proj_call(ymix, w_out, xc, modl, g, wr_t, br, n_lat, n_rows):
    b, n, d = xc.shape
    tm = TOKEN_TILE
    tok = lambda w: pl.BlockSpec((1, tm, w), lambda i, t: (i, t, 0))
    const2 = lambda a: pl.BlockSpec(a.shape, lambda i, t: (0, 0))
    sel = pl.BlockSpec((1, TOP_K, tm), lambda i, t: (i, 0, t))
    return pl.pallas_call(
        _out_proj_kernel,
        grid=(b, n_rows // tm),
        in_specs=[
            tok(ymix.shape[2]), const2(w_out), tok(d), _mod_spec(d, n_lat // tm),
            const2(g), const2(wr_t), const2(br),
        ],
        out_specs=[tok(d), tok(d), sel, sel, sel,
                   pl.BlockSpec((1, 1, N_EXPERTS, LANES), lambda i, t: (i, t, 0, 0))],
        out_shape=[
            jax.ShapeDtypeStruct((b, n_rows, d), F32),
            jax.ShapeDtypeStruct((b, n_rows, d), BF16),
            jax.ShapeDtypeStruct((b, TOP_K, n_rows), jnp.int32),
            jax.ShapeDtypeStruct((b, TOP_K, n_rows), F32),
            jax.ShapeDtypeStruct((b, TOP_K, n_rows), jnp.int32),
            jax.ShapeDtypeStruct((b, n_rows // tm, N_EXPERTS, LANES), jnp.int32),
        ],
        compiler_params=_params(("parallel", "parallel")),
    )(ymix, w_out, xc, modl, g, wr_t, br)


def _w1_prep_kernel(w_ref, g_ref, l_ref):
    r = lax.broadcasted_iota(jnp.int32, (MXU_DIM, MXU_DIM), 0)
    c = lax.broadcasted_iota(jnp.int32, (MXU_DIM, MXU_DIM), 1)
    src = jnp.where(c < LANES, 2 * c, 2 * (c - LANES) + 1)
    perm = jnp.where(r == src, 1.0, 0.0).astype(BF16)
    for j in range(w_ref.shape[3] // MXU_DIM):
        out = _dot(w_ref[0, 0, :, j * MXU_DIM:(j + 1) * MXU_DIM].astype(BF16), perm)
        g_ref[0, :, j * LANES:(j + 1) * LANES] = out[:, :LANES].astype(BF16)
        l_ref[0, :, j * LANES:(j + 1) * LANES] = out[:, LANES:].astype(BF16)


def _w1_prep_call(w1_all, layer):
    _, e, d, f2 = w1_all.shape
    f = f2 // 2
    rows = 256
    out_spec = pl.BlockSpec((1, rows, f), lambda i, j: (i, j, 0))
    return pl.pallas_call(
        _w1_prep_kernel,
        grid=(e, d // rows),
        in_specs=[pl.BlockSpec((1, 1, rows, f2), lambda i, j: (layer, i, j, 0))],
        out_specs=[out_spec, out_spec],
        out_shape=[jax.ShapeDtypeStruct((e, d, f), BF16)] * 2,
        compiler_params=_params(("parallel", "parallel")),
    )(w1_all)


def _expert_kernel(be_ref, nu_ref, x_ref, w1g_ref, w1l_ref, b1g_ref, b1l_ref, w2_ref, b2_ref, o_ref):
    del be_ref
    i = pl.program_id(0)

    @pl.when(i < nu_ref[0])
    def _():
        x = x_ref[...]
        glu = jnp.minimum(_dot(x, w1g_ref[0]) + b1g_ref[0], SWIGLU_LIMIT)
        lin = jnp.clip(_dot(x, w1l_ref[0]) + b1l_ref[0], -SWIGLU_LIMIT, SWIGLU_LIMIT)
        act = glu * _sigmoid(SWIGLU_ALPHA * glu) * (lin + 1.0)
        o_ref[...] = _dot(act.astype(BF16), w2_ref[0, 0]) + b2_ref[0]

    @pl.when(i >= nu_ref[0])
    def _():
        o_ref[...] = jnp.zeros_like(o_ref)


def _expert_call(block_expert, n_used, x_rows, w1g, w1l, b1g, b1l, w2_all, layer, b2):
    r, d = x_rows.shape
    f = w1g.shape[2]
    tm = MOE_BLOCK
    wspec = lambda s: pl.BlockSpec((1,) + s, lambda i, be, nu: (be[i], 0, 0))
    return pl.pallas_call(
        _expert_kernel,
        grid_spec=pltpu.PrefetchScalarGridSpec(
            num_scalar_prefetch=2,
            grid=(r // tm,),
            in_specs=[
                pl.BlockSpec((tm, d), lambda i, be, nu: (i, 0)),
                wspec((d, f)), wspec((d, f)), wspec((1, f)), wspec((1, f)),
                pl.BlockSpec((1, 1, f, d), lambda i, be, nu: (layer, be[i], 0, 0)),
                wspec((1, d)),
            ],
            out_specs=pl.BlockSpec((tm, d), lambda i, be, nu: (i, 0)),
        ),
        out_shape=jax.ShapeDtypeStruct((r, d), F32),
        compiler_params=_params(("arbitrary",)),
    )(block_expert, n_used, x_rows, w1g, w1l, b1g, b1l, w2_all, b2)


def _combine_kernel(x_ref, y_ref, gate_ref, mod_ref, g_ref, o_ref):
    gates = gate_ref[0]
    y = gates[:, 0:1] * y_ref[0, 0]
    for k in range(1, TOP_K):
        y = y + gates[:, k:k + 1] * y_ref[k, 0]
    mod = mod_ref[0, 0]
    o_ref[0] = x_ref[0] + mod[5:6] * (_rms(y) * g_ref[...][3:4])


def _combine_call(xc, picked, gates, modl, g, n_lat):
    b, n, d = xc.shape
    tm = TOKEN_TILE
    tok = pl.BlockSpec((1, tm, d), lambda i, t: (i, t, 0))
    return pl.pallas_call(
        _combine_kernel,
        grid=(b, n // tm),
        in_specs=[
            tok,
            pl.BlockSpec((TOP_K, 1, tm, d), lambda i, t: (0, i, t, 0)),
            pl.BlockSpec((1, tm, TOP_K), lambda i, t: (i, t, 0)),
            _mod_spec(d, n_lat // tm),
            pl.BlockSpec(g.shape, lambda i, t: (0, 0)),
        ],
        out_specs=tok,
        out_shape=jax.ShapeDtypeStruct((b, n, d), F32),
        compiler_params=_params(("parallel", "parallel")),
    )(xc, picked, gates, modl, g)


def _rope_tables(rows, n_ctx, rot_dim):
    row = jnp.repeat(jnp.arange(rows), GRID_W)
    col = jnp.tile(jnp.arange(GRID_W), rows)
    n_freq = rot_dim // 4
    inv = ROPE_THETA ** (-jnp.arange(n_freq, dtype=F32) / n_freq)
    ang = jnp.concatenate([row[:, None] * inv, col[:, None] * inv], axis=-1)
    cos, sin = jnp.cos(ang), jnp.sin(ang)
    reps = 256 // rot_dim
    cos_t = jnp.tile(jnp.concatenate([cos, cos], axis=-1), (1, reps))
    sin_t = jnp.tile(jnp.concatenate([-sin, sin], axis=-1), (1, reps))
    cos_t = jnp.concatenate([cos_t, jnp.ones((n_ctx, 256), F32)], axis=0)
    sin_t = jnp.concatenate([sin_t, jnp.zeros((n_ctx, 256), F32)], axis=0)
    return cos_t, sin_t


def _pack_w_in(w):
    parts, src = [], 0
    for _, width, padded in IN_PIECES:
        parts.append(w[:, src:src + width])
        if padded > width:
            parts.append(jnp.zeros((w.shape[0], padded - width), w.dtype))
        src += width
    return jnp.concatenate(parts, axis=1).astype(BF16)


def _pack_w_uq(w):
    wh = w.reshape(MLA_Q_RANK, N_HEADS, MLA_NOPE + MLA_ROPE)
    nope = wh[:, :, :MLA_NOPE].reshape(MLA_Q_RANK, N_HEADS * MLA_NOPE)
    rope = wh[:, :, MLA_NOPE:].reshape(MLA_Q_RANK, N_HEADS * MLA_ROPE)
    packed = jnp.concatenate([nope, rope], axis=1)
    return jnp.pad(packed, ((0, 256 - MLA_Q_RANK), (0, 0))).astype(BF16)


def _pack_w_ukv(w):
    wh = w.reshape(MLA_KV_RANK, N_HEADS, MLA_NOPE + HEAD_DIM)
    kn = wh[:, :, :MLA_NOPE].reshape(MLA_KV_RANK, N_HEADS * MLA_NOPE)
    vv = wh[:, :, MLA_NOPE:].reshape(MLA_KV_RANK, N_HEADS * HEAD_DIM)
    return jnp.concatenate([kn, vv], axis=1).astype(BF16)


def _heads(a, n_heads):
    b, n, w = a.shape
    return a.reshape(b, n, n_heads, w // n_heads).transpose(0, 2, 1, 3)


def _heads_t(a, n_heads):
    b, n, w = a.shape
    return a.reshape(b, n, n_heads, w // n_heads).transpose(0, 2, 3, 1)


def _attention(q, kt, v, n_lat, n_ctx, ctx_out, diff=None):
    n = n_lat + n_ctx
    y = _attn_call(q, kt, v, (0, n_lat), (0, n), min(Q_TILE, n_lat), diff)
    if ctx_out:
        y_c = _attn_call(q, kt, v, (n_lat, n_ctx), (n_lat, n_ctx), n_ctx, diff)
    else:
        y_c = jnp.zeros((y.shape[0], n_ctx, y.shape[2]), y.dtype)
    return jnp.concatenate([y, y_c], axis=1)


def _heads_v(a, n_heads):
    h = _heads(a, n_heads)
    return jnp.concatenate([h, jnp.ones_like(h)], axis=-1)


def _moe(x_new, h2, idx, gates, rank, counts, modl, g, n_lat, w1_all, b1, w2_all, b2, layer):
    b, n, d = h2.shape
    t = b * n
    n_assign = t * TOP_K
    n_blocks = -(-n_assign // MOE_BLOCK) + N_EXPERTS
    experts = jnp.arange(N_EXPERTS, dtype=jnp.int32)
    tile_counts = counts.reshape(-1, N_EXPERTS)
    tile_base = jnp.cumsum(tile_counts, axis=0) - tile_counts
    totals = jnp.sum(tile_counts, axis=0)
    padded = (totals + MOE_BLOCK - 1) // MOE_BLOCK * MOE_BLOCK
    pad_ends = jnp.cumsum(padded)
    table = (tile_base + (pad_ends - padded)[None, :]).reshape(b, 1, n // TOKEN_TILE, 1, N_EXPERTS)
    idx5 = idx.reshape(b, TOP_K, n // TOKEN_TILE, TOKEN_TILE, 1)
    dest = rank + jnp.sum(jnp.where(idx5 == experts, table, 0), axis=-1).reshape(b, TOP_K, n)
    tok = jnp.broadcast_to((jnp.arange(b, dtype=jnp.int32)[:, None, None] * n
                            + jnp.arange(n, dtype=jnp.int32)[None, None, :]), (b, TOP_K, n))
    src_tok = jnp.zeros((n_blocks * MOE_BLOCK,), jnp.int32).at[dest.reshape(-1)].set(
        tok.reshape(-1), unique_indices=True)
    block_start = jnp.arange(n_blocks, dtype=jnp.int32) * MOE_BLOCK
    block_expert = jnp.minimum(jnp.sum((pad_ends[None, :] <= block_start[:, None]).astype(jnp.int32), axis=1),
                               N_EXPERTS - 1)
    n_used = (pad_ends[-1:] // MOE_BLOCK).astype(jnp.int32)
    x_rows = h2.reshape(t, d).at[src_tok].get(mode="promise_in_bounds")
    f = w2_all.shape[2]
    w1g, w1l = _w1_prep_call(w1_all, layer)
    y_rows = _expert_call(
        block_expert, n_used, x_rows, w1g, w1l,
        b1[:, 0::2].reshape(N_EXPERTS, 1, f), b1[:, 1::2].reshape(N_EXPERTS, 1, f),
        w2_all, layer, b2.reshape(N_EXPERTS, 1, d))
    picked = y_rows.at[dest.transpose(1, 0, 2).reshape(-1)].get(mode="promise_in_bounds")
    return _combine_call(x_new, picked.reshape(TOP_K, b, n, d), gates.transpose(0, 2, 1), modl, g, n_lat)


def kernel(x, c, ctx, c_ctx, ada_w, ada_b, norm_g, w_in, w_out, ret_log_decay, ret_gn_w, ret_gn_b,
           diff_lambda, diff_subln, gqa_qk_norm, mla_q_norm, mla_kv_norm, mla_w_uq, mla_w_ukv,
           router_w, router_b, exp_w1, exp_b1, exp_w2, exp_b2):
    b, s, d = x.shape
    n_ctx = ctx.shape[1]
    n = s + n_ctx
    depth = ada_w.shape[0]
    assert n_ctx % TOKEN_TILE == 0 and s % TOKEN_TILE == 0 and s % GRID_W == 0
    assert n_ctx % RET_CHUNK == 0 and s % min(Q_TILE, s) == 0 and s % n_ctx == 0

    tables = _rope_tables(s // GRID_W, n_ctx, HEAD_DIM) + _rope_tables(s // GRID_W, n_ctx, DIFF_D)
    c_rows = jnp.zeros((16, d), F32).at[:b].set(c).at[b].set(c_ctx)
    mods = _ada_call(c_rows, ada_w, ada_b)
    w2_all = exp_w2.astype(BF16)
    xc = jnp.concatenate([x, ctx], axis=1)

    for l in range(depth):
        last = l == depth - 1
        lam_init = 0.8 - 0.6 * math.exp(-0.3 * l)
        mod_lat = mods[l, :b].reshape(b, 1, 6, d)
        mod_ctx = jnp.broadcast_to(mods[l, b].reshape(1, 1, 6, d), (b, 1, 6, d))
        modl = jnp.concatenate([mod_ctx, mod_lat], axis=1)

        (rq, rk, rv, rg, dq, dk, dv, gq, gk, gv, mqn, mqr, mkn, mv, mkr) = _in_proj_call(
            xc, modl, norm_g[l, 0:1], _pack_w_in(w_in[l]), tables,
            jnp.tile(gqa_qk_norm[l, 0], N_HEADS)[None, :], jnp.tile(gqa_qk_norm[l, 1], GQA_KV_HEADS)[None, :],
            jnp.pad(mla_q_norm[l], (0, 256 - MLA_Q_RANK))[None, :], mla_kv_norm[l][None, :],
            _pack_w_uq(mla_w_uq[l]), _pack_w_ukv(mla_w_ukv[l]), s)

        log_g = -jnp.exp(ret_log_decay[l].astype(F32))
        o_f = _ret_call(rq, rk, rv, _ret_tables(log_g[0], False), s, False)
        ret_y = _ret_call(rq, rk, rv, _ret_tables(log_g[1], True), s, True,
                          (o_f, rg, ret_gn_w[l][None, :], ret_gn_b[l][None, :]))

        lp = diff_lambda[l].astype(F32)
        lam = jnp.exp(jnp.sum(lp[0] * lp[1])) - jnp.exp(jnp.sum(lp[2] * lp[3])) + lam_init
        dif_y = _attention(_heads(dq, 2 * N_HEADS), _heads_t(dk, 2 * N_HEADS), _heads_v(dv, N_HEADS), s, n_ctx,
                           not last, (lam, diff_subln[l], 1.0 - lam_init))

        gqa_y = _attention(_heads(gq, N_HEADS), _heads_t(gk, GQA_KV_HEADS), _heads_v(gv, GQA_KV_HEADS), s, n_ctx,
                           not last)

        q_m = jnp.concatenate([_heads(mqn, N_HEADS), _heads(mqr, N_HEADS)], axis=-1)
        kr_t = jnp.broadcast_to(mkr[:, None, :, :MLA_ROPE].transpose(0, 1, 3, 2), (b, N_HEADS, MLA_ROPE, n))
        kt_m = jnp.concatenate([_heads_t(mkn, N_HEADS), kr_t], axis=2)
        mla_y = _attention(q_m, kt_m, _heads_v(mv, N_HEADS), s, n_ctx, not last)

        ymix = jnp.concatenate([ret_y, dif_y, gqa_y, mla_y], axis=-1)
        n_rows = s if last else n
        x_new, h2, idx, gates, rank, counts = _out_proj_call(
            ymix, w_out[l].astype(BF16), xc, modl, norm_g[l], router_w[l].T, router_b[l][:, None], s, n_rows)
        xc = _moe(x_new, h2, idx, gates, rank, counts[..., 0], modl, norm_g[l], s,
                  exp_w1, exp_b1[l], w2_all, exp_b2[l], l)

    return xc[:, :s]
```

```python
import functools
import math

import jax
import jax.numpy as jnp
from jax import lax
from jax.experimental import pallas as pl
from jax.experimental.pallas import tpu as pltpu

F32 = jnp.float32
BF16 = jnp.bfloat16

GRID_W = 64
ROPE_THETA = 10000.0
EPS = 1e-6
GROUP_WIDTH = 256
HEAD_DIM = 64
N_HEADS = 4
RET_CHUNK = 128
DIFF_D = 32
GQA_KV_HEADS = 2
MLA_Q_RANK = 192
MLA_KV_RANK = 128
MLA_NOPE = 64
MLA_ROPE = 32
N_EXPERTS = 32
TOP_K = 4
SWIGLU_LIMIT = 7.0
SWIGLU_ALPHA = 1.702
MOE_BLOCK = 256

LANES = 128
MXU_DIM = 256
TOKEN_TILE = 256
Q_TILE = 1024
KV_TILE = 512
VMEM_LIMIT = 48 * 1024 * 1024
LOG2E = math.log2(math.e)

IN_PIECES = (
    ("ret_q", 256, 256), ("ret_k", 256, 256), ("ret_v", 256, 256), ("ret_g", 256, 256),
    ("dif_q", 256, 256), ("dif_k", 256, 256), ("dif_v", 256, 256),
    ("gqa_q", 256, 256), ("gqa_k", 128, 128), ("gqa_v", 128, 128),
    ("mla_cq", MLA_Q_RANK, 256), ("mla_ckv", MLA_KV_RANK, 128), ("mla_kr", MLA_ROPE, 128),
)
IN_OFFSETS = {}
_off = 0
for _name, _w, _pw in IN_PIECES:
    IN_OFFSETS[_name] = _off
    _off += _pw
IN_PACKED_WIDTH = _off


def _params(sem):
    return pltpu.CompilerParams(dimension_semantics=sem, vmem_limit_bytes=VMEM_LIMIT)


def _rms(x):
    return x * lax.rsqrt(jnp.mean(x * x, axis=-1, keepdims=True) + EPS)


def _split_bf16(a):
    hi = a.astype(BF16)
    lo = (a - hi.astype(F32)).astype(BF16)
    return hi, lo


def _dot_nt(a, b):
    return lax.dot_general(a, b, (((1,), (1,)), ((), ())), preferred_element_type=F32)


def _dot(a, b):
    return jnp.dot(a, b, preferred_element_type=F32)


def _sigmoid(a):
    return 1.0 / (1.0 + jnp.exp(-a))


def _mod_spec(d, n_lat_tiles):
    return pl.BlockSpec((1, 1, 6, d), lambda i, t: (i, jnp.where(t < n_lat_tiles, 1, 0), 0, 0))


def _ada_kernel(c_ref, w_ref, b_ref, o_ref):
    s = c_ref[...]
    s = s * _sigmoid(s)
    s_hi, s_lo = _split_bf16(s)
    w_hi, w_lo = _split_bf16(w_ref[0])
    o_ref[0] = _dot(s_hi, w_hi) + _dot(s_hi, w_lo) + _dot(s_lo, w_hi) + b_ref[0]


def _ada_call(c_rows, ada_w, ada_b):
    depth, d, n6 = ada_w.shape
    rows = c_rows.shape[0]
    tn = 1536
    return pl.pallas_call(
        _ada_kernel,
        grid=(depth, n6 // tn),
        in_specs=[
            pl.BlockSpec((rows, d), lambda l, j: (0, 0)),
            pl.BlockSpec((1, d, tn), lambda l, j: (l, 0, j)),
            pl.BlockSpec((1, 1, tn), lambda l, j: (l, 0, j)),
        ],
        out_specs=pl.BlockSpec((1, rows, tn), lambda l, j: (l, 0, j)),
        out_shape=jax.ShapeDtypeStruct((depth, rows, n6), F32),
        compiler_params=_params(("parallel", "parallel")),
    )(c_rows, ada_w, ada_b.reshape(depth, 1, n6))


def _rope(x, cos, sin_signed, half):
    outs = []
    for c in range(x.shape[1] // LANES):
        sl = slice(c * LANES, (c + 1) * LANES)
        xc = x[:, sl]
        lane = lax.broadcasted_iota(jnp.int32, xc.shape, 1)
        first_half = (lane % (2 * half)) < half
        partner = jnp.where(first_half, pltpu.roll(xc, LANES - half, 1), pltpu.roll(xc, half, 1))
        outs.append(xc * cos[:, sl] + partner * sin_signed[:, sl])
    return outs[0] if len(outs) == 1 else jnp.concatenate(outs, axis=1)


def _group_mean_sq(x, gsize):
    w = x.shape[1]
    r = lax.broadcasted_iota(jnp.int32, (w, w), 0) // gsize
    c = lax.broadcasted_iota(jnp.int32, (w, w), 1) // gsize
    ones = jnp.where(r == c, 1.0, 0.0).astype(BF16)
    hi, lo = _split_bf16(x * x)
    return (_dot(hi, ones) + _dot(lo, ones)) * (1.0 / gsize)


def _in_proj_kernel(x_ref, mod_ref, g_ref, w_ref, cos64_ref, sin64_ref, cos32_ref, sin32_ref,
                    gqn_ref, gkn_ref, mqn_ref, mkvn_ref, wuq_ref, wukv_ref,
                    rq_ref, rk_ref, rv_ref, rg_ref, dq_ref, dk_ref, dv_ref, gq_ref, gk_ref, gv_ref,
                    mqn_o, mqr_o, mkn_o, mv_o, mkr_o):
    mod = mod_ref[0, 0]
    h = _rms(x_ref[0]) * g_ref[...] * (1.0 + mod[1:2]) + mod[0:1]
    hb = h.astype(BF16)
    cos64, sin64 = cos64_ref[...], sin64_ref[...]
    cos32, sin32 = cos32_ref[...], sin32_ref[...]

    def proj(name, width):
        o = IN_OFFSETS[name]
        return _dot(hb, w_ref[:, o:o + width])

    rq_ref[0] = _rope(proj("ret_q", 256), cos64, sin64, 32).astype(BF16)
    rk_ref[0] = (_rope(proj("ret_k", 256), cos64, sin64, 32) * (HEAD_DIM ** -0.5)).astype(BF16)
    rv_ref[0] = proj("ret_v", 256).astype(BF16)
    rg_ref[0] = proj("ret_g", 256)
    dq_ref[0] = (_rope(proj("dif_q", 256), cos32, sin32, 16) * (DIFF_D ** -0.5 * LOG2E)).astype(BF16)
    dk_ref[0] = _rope(proj("dif_k", 256), cos32, sin32, 16).astype(BF16)
    dv_ref[0] = proj("dif_v", 256).astype(BF16)
    gq = proj("gqa_q", 256)
    gq = gq * lax.rsqrt(_group_mean_sq(gq, HEAD_DIM) + EPS) * gqn_ref[...]
    gq_ref[0] = (_rope(gq, cos64, sin64, 32) * (HEAD_DIM ** -0.5 * LOG2E)).astype(BF16)
    gk = proj("gqa_k", 128)
    gk = gk * lax.rsqrt(_group_mean_sq(gk, HEAD_DIM) + EPS) * gkn_ref[...]
    gk_ref[0] = _rope(gk, cos64[:, :LANES], sin64[:, :LANES], 32).astype(BF16)
    gv_ref[0] = proj("gqa_v", 128).astype(BF16)
    cq = proj("mla_cq", 256)
    cq = cq * lax.rsqrt(jnp.sum(cq * cq, axis=-1, keepdims=True) * (1.0 / MLA_Q_RANK) + EPS) * mqn_ref[...]
    q_up = _dot(cq.astype(BF16), wuq_ref[...])
    mla_scale = (MLA_NOPE + MLA_ROPE) ** -0.5 * LOG2E
    mqn_o[0] = (q_up[:, :256] * mla_scale).astype(BF16)
    mqr_o[0] = (_rope(q_up[:, 256:], cos32[:, :LANES], sin32[:, :LANES], 16) * mla_scale).astype(BF16)
    ckv = proj("mla_ckv", 128)
    ckv = ckv * lax.rsqrt(jnp.mean(ckv * ckv, axis=-1, keepdims=True) + EPS) * mkvn_ref[...]
    kv_up = _dot(ckv.astype(BF16), wukv_ref[...])
    mkn_o[0] = kv_up[:, :256].astype(BF16)
    mv_o[0] = kv_up[:, 256:].astype(BF16)
    mkr_o[0] = _rope(proj("mla_kr", 128), cos32[:, :LANES], sin32[:, :LANES], 16).astype(BF16)


def _in_proj_call(xc, modl, g0, w_in_p, tables, gqn, gkn, mqn, mkvn, wuq_p, wukv_p, n_lat):
    b, n, d = xc.shape
    tm = TOKEN_TILE
    tok = lambda w: pl.BlockSpec((1, tm, w), lambda i, t: (i, t, 0))
    const2 = lambda a: pl.BlockSpec(a.shape, lambda i, t: (0, 0))
    tab = pl.BlockSpec((tm, 256), lambda i, t: (t, 0))
    out_widths = (256, 256, 256, 256, 256, 256, 256, 256, 128, 128, 256, 128, 256, 256, 128)
    out_dtypes = (BF16, BF16, BF16, F32, BF16, BF16, BF16, BF16, BF16, BF16, BF16, BF16, BF16, BF16, BF16)
    return pl.pallas_call(
        _in_proj_kernel,
        grid=(b, n // tm),
        in_specs=[
            tok(d), _mod_spec(d, n_lat // tm),
            const2(g0), const2(w_in_p), tab, tab, tab, tab,
            const2(gqn), const2(gkn), const2(mqn), const2(mkvn), const2(wuq_p), const2(wukv_p),
        ],
        out_specs=[tok(w) for w in out_widths],
        out_shape=[jax.ShapeDtypeStruct((b, n, w), dt) for w, dt in zip(out_widths, out_dtypes)],
        compiler_params=_params(("parallel", "parallel")),
    )(xc, modl, g0, w_in_p, *tables, gqn, gkn, mqn, mkvn, wuq_p, wukv_p)


def _ret_kernel(*refs, readout):
    if readout:
        (q_ref, k_ref, v_ref, dec_ref, xi_ref, zeta_ref, gc_ref, of_ref, g_ref, gnw_ref, gnb_ref,
         o_ref, state_ref) = refs
    else:
        q_ref, k_ref, v_ref, dec_ref, xi_ref, zeta_ref, gc_ref, o_ref, state_ref = refs

    @pl.when(pl.program_id(1) == 0)
    def _():
        state_ref[...] = jnp.zeros_like(state_ref)

    q, k, v = q_ref[0], k_ref[0], v_ref[0]
    outs = []
    for h in range(N_HEADS):
        sl = slice(h * HEAD_DIM, (h + 1) * HEAD_DIM)
        qh, kh, vh = q[:, sl], k[:, sl], v[:, sl]
        st = state_ref[h]
        inner = _dot_nt(qh, kh) * dec_ref[h]
        o = _dot(inner.astype(BF16), vh) + _dot(qh, st.astype(BF16)) * xi_ref[h]
        kz = (kh.astype(F32) * zeta_ref[h]).astype(BF16)
        kv = lax.dot_general(kz, vh, (((0,), (0,)), ((), ())), preferred_element_type=F32)
        state_ref[h] = gc_ref[h] * st + kv
        if readout:
            o = o + of_ref[0][:, sl]
            mu = jnp.mean(o, axis=-1, keepdims=True)
            var = jnp.mean(jnp.square(o - mu), axis=-1, keepdims=True)
            o = (o - mu) * lax.rsqrt(var + EPS)
        outs.append(o)
    o = jnp.concatenate(outs, axis=1)
    if readout:
        g = g_ref[0]
        o_ref[0] = ((o * gnw_ref[...] + gnb_ref[...]) * (g * _sigmoid(g))).astype(o_ref.dtype)
    else:
        o_ref[0] = o


def _ret_call(q, k, v, tabs, n_lat, backward, readout_args=None):
    b, n, w = q.shape
    c = RET_CHUNK
    nch = n // c
    nlc = n_lat // c
    if backward:
        chunk = lambda t: nch - 1 - t
    else:
        chunk = lambda t: jnp.where(t < nch - nlc, nlc + t, t - (nch - nlc))
    tok = lambda dt_w: pl.BlockSpec((1, c, dt_w), lambda i, t: (i, chunk(t), 0))
    const = lambda a: pl.BlockSpec(a.shape, lambda i, t: (0,) * a.ndim)
    in_specs = [tok(w), tok(w), tok(w)] + [const(a) for a in tabs]
    args = [q, k, v, *tabs]
    if readout_args is not None:
        o_f, g, gnw, gnb = readout_args
        in_specs += [tok(w), tok(w), const(gnw), const(gnb)]
        args += [o_f, g, gnw, gnb]
    out_dtype = BF16 if readout_args is not None else F32
    return pl.pallas_call(
        functools.partial(_ret_kernel, readout=readout_args is not None),
        grid=(b, nch),
        in_specs=in_specs,
        out_specs=tok(w),
        out_shape=jax.ShapeDtypeStruct((b, n, w), out_dtype),
        scratch_shapes=[pltpu.VMEM((N_HEADS, HEAD_DIM, HEAD_DIM), F32)],
        compiler_params=_params(("parallel", "arbitrary")),
    )(*args)


def _ret_tables(log_g, backward):
    c = RET_CHUNK
    pos = jnp.arange(c, dtype=F32)
    dist = (pos[None, :] - pos[:, None]) if backward else (pos[:, None] - pos[None, :])
    lg = log_g[:, None, None]
    decay = jnp.where(dist >= 0, jnp.exp(lg * jnp.maximum(dist, 0.0)), 0.0)
    to_state = (c - pos) if backward else (pos + 1.0)
    to_end = pos if backward else (c - 1.0 - pos)
    xi = jnp.exp(log_g[:, None] * to_state)[:, :, None]
    zeta = jnp.exp(log_g[:, None] * to_end)[:, :, None]
    ones = jnp.ones((1, 1, HEAD_DIM), F32)
    gc = jnp.exp(log_g * c)[:, None, None] * jnp.ones((1, HEAD_DIM, HEAD_DIM), F32)
    return decay, xi * ones, zeta * ones, gc


def _attn_kernel(*refs, n_maps, readout_scale):
    if n_maps == 2:
        lam_ref, q_ref, kt_ref, v_ref, sub_ref, o_ref, first_ref = refs
    else:
        q_ref, kt_ref, v_ref, o_ref = refs
    hm = pl.program_id(2)
    q = q_ref[0, 0]
    tq = q.shape[0]
    n_keys = kt_ref.shape[3]
    dv = v_ref.shape[3] // 2
    tk = min(KV_TILE, n_keys)
    n_full, tail = divmod(n_keys, tk)

    def step(off, size, carry):
        m, acc = carry
        s = _dot(q, kt_ref[0, 0, :, pl.ds(off, size)])
        m_new = jnp.maximum(m, jnp.max(s, axis=-1, keepdims=True))
        p = jnp.exp2(s - m_new)
        acc = jnp.exp2(m - m_new) * acc + _dot(p.astype(BF16), v_ref[0, 0, pl.ds(off, size), :])
        return m_new, acc

    carry = (jnp.full((tq, 1), -1e30, F32), jnp.zeros((tq, 2 * dv), F32))
    for j in range(n_full):
        carry = step(j * tk, tk, carry)
    if tail:
        carry = step(n_full * tk, tail, carry)
    acc = carry[1]
    o = acc[:, :dv] / acc[:, dv:]

    def store(head_of_step, val):
        for hh in range(N_HEADS):
            @pl.when(head_of_step == hh)
            def _(hh=hh):
                o_ref[0, :, hh * dv:(hh + 1) * dv] = val.astype(o_ref.dtype)

    if n_maps == 1:
        store(hm, o)
    else:
        @pl.when(hm % 2 == 0)
        def _():
            first_ref[...] = o

        @pl.when(hm % 2 == 1)
        def _():
            od = first_ref[...] - lam_ref[0] * o
            od = od * lax.rsqrt(jnp.mean(od * od, axis=-1, keepdims=True) + EPS) * sub_ref[...]
            store(hm // 2, od * readout_scale)


def _attn_call(q, kt, v, rows, keys, tq, diff=None):
    b, hq, n, d = q.shape
    hk, hv, dv = kt.shape[1], v.shape[1], v.shape[3] // 2
    (q0, qn), (k0, kn) = rows, keys
    assert q0 % tq == 0 and qn % tq == 0 and k0 % kn == 0
    n_maps = 2 if diff is not None else 1
    in_specs = [
        pl.BlockSpec((1, 1, tq, d), lambda i, t, h: (i, h, q0 // tq + t, 0)),
        pl.BlockSpec((1, 1, d, kn), lambda i, t, h: (i, h // (hq // hk), 0, k0 // kn)),
        pl.BlockSpec((1, 1, kn, 2 * dv), lambda i, t, h: (i, h // (hq // hv), k0 // kn, 0)),
    ]
    args = [q, kt, v]
    scratch = []
    scale = 1.0
    if diff is not None:
        lam, subln, scale = diff
        in_specs = [pl.BlockSpec(memory_space=pltpu.SMEM)] + in_specs + [pl.BlockSpec((1, dv), lambda i, t, h: (0, 0))]
        args = [lam.reshape(1)] + args + [subln.reshape(1, dv)]
        scratch = [pltpu.VMEM((tq, dv), F32)]
    return pl.pallas_call(
        functools.partial(_attn_kernel, n_maps=n_maps, readout_scale=scale),
        grid=(b, qn // tq, hq),
        in_specs=in_specs,
        out_specs=pl.BlockSpec((1, tq, (hq // n_maps) * dv), lambda i, t, h: (i, t, 0)),
        out_shape=jax.ShapeDtypeStruct((b, qn, (hq // n_maps) * dv), BF16),
        scratch_shapes=scratch,
        compiler_params=_params(("parallel", "parallel", "arbitrary")),
    )(*args)


def _out_proj_kernel(y_ref, w_ref, x_ref, mod_ref, g_ref, wr_ref, br_ref,
                     xo_ref, h_ref, idx_ref, gate_ref, rank_ref, cnt_ref):
    mod = mod_ref[0, 0]
    g = g_ref[...]
    y = _dot(y_ref[0], w_ref[...])
    xn = x_ref[0] + mod[2:3] * (_rms(y) * g[1:2])
    xo_ref[0] = xn
    h = _rms(xn) * g[2:3] * (1.0 + mod[4:5]) + mod[3:4]
    h_ref[0] = h.astype(BF16)
    w_hi, w_lo = _split_bf16(wr_ref[...])
    h_hi, h_lo = _split_bf16(h)
    logits = _dot_nt(w_hi, h_hi) + _dot_nt(w_hi, h_lo) + _dot_nt(w_lo, h_hi) + br_ref[...]
    row = lax.broadcasted_iota(jnp.int32, logits.shape, 0)
    vals = logits
    tops, idxs = [], []
    for _ in range(TOP_K):
        m = jnp.max(vals, axis=0, keepdims=True)
        idx = jnp.min(jnp.where(vals == m, row, N_EXPERTS), axis=0, keepdims=True)
        tops.append(m)
        idxs.append(idx)
        vals = jnp.where(row == idx, -jnp.inf, vals)
    ex = [jnp.exp(tv - tops[0]) for tv in tops]
    den = ex[0] + ex[1] + ex[2] + ex[3]
    gate_ref[0] = jnp.concatenate([e / den for e in ex], axis=0)
    idx_ref[0] = jnp.concatenate(idxs, axis=0)
    tm = logits.shape[1]
    before = jnp.where(lax.broadcasted_iota(jnp.int32, (tm, tm), 0) < lax.broadcasted_iota(jnp.int32, (tm, tm), 1),
                       1.0, 0.0).astype(BF16)
    base = jnp.zeros((N_EXPERTS, 1), F32)
    ranks = []
    for idx in idxs:
        onehot = jnp.where(row == idx, 1.0, 0.0)
        prefix = _dot(onehot.astype(BF16), before)
        ranks.append(jnp.sum(onehot * (prefix + base), axis=0, keepdims=True))
        base = base + jnp.sum(onehot, axis=1, keepdims=True)
    rank_ref[0] = jnp.concatenate(ranks, axis=0).astype(jnp.int32)
    cnt_ref[0, 0] = jnp.broadcast_to(base, (N_EXPERTS, LANES)).astype(jnp.int32)


def _out_proj_call(ymix, w_out, xc, modl, g, wr_t, br, n_lat, n_rows):
    b, n, d = xc.shape
    tm = TOKEN_TILE
    tok = lambda w: pl.BlockSpec((1, tm, w), lambda i, t: (i, t, 0))
    const2 = lambda a: pl.BlockSpec(a.shape, lambda i, t: (0, 0))
    sel = pl.BlockSpec((1, TOP_K, tm), lambda i, t: (i, 0, t))
    return pl.pallas_call(
        _out_proj_kernel,
        grid=(b, n_rows // tm),
        in_specs=[
            tok(ymix.shape[2]), const2(w_out), tok(d), _mod_spec(d, n_lat // tm),
            const2(g), const2(wr_t), const2(br),
        ],
        out_specs=[tok(d), tok(d), sel, sel, sel,
                   pl.BlockSpec((1, 1, N_EXPERTS, LANES), lambda i, t: (i, t, 0, 0))],
        out_shape=[
            jax.ShapeDtypeStruct((b, n_rows, d), F32),
            jax.ShapeDtypeStruct((b, n_rows, d), BF16),
            jax.ShapeDtypeStruct((b, TOP_K, n_rows), jnp.int32),
            jax.ShapeDtypeStruct((b, TOP_K, n_rows), F32),
            jax.ShapeDtypeStruct((b, TOP_K, n_rows), jnp.int32),
            jax.ShapeDtypeStruct((b, n_rows // tm, N_EXPERTS, LANES), jnp.int32),
        ],
        compiler_params=_params(("parallel", "parallel")),
    )(ymix, w_out, xc, modl, g, wr_t, br)


def _w1_prep_kernel(w_ref, g_ref, l_ref):
    r = lax.broadcasted_iota(jnp.int32, (MXU_DIM, MXU_DIM), 0)
    c = lax.broadcasted_iota(jnp.int32, (MXU_DIM, MXU_DIM), 1)
    src = jnp.where(c < LANES, 2 * c, 2 * (c - LANES) + 1)
    perm = jnp.where(r == src, 1.0, 0.0).astype(BF16)
    for j in range(w_ref.shape[3] // MXU_DIM):
        out = _dot(w_ref[0, 0, :, j * MXU_DIM:(j + 1) * MXU_DIM].astype(BF16), perm)
        g_ref[0, :, j * LANES:(j + 1) * LANES] = out[:, :LANES].astype(BF16)
        l_ref[0, :, j * LANES:(j + 1) * LANES] = out[:, LANES:].astype(BF16)


def _w1_prep_call(w1_all, layer):
    _, e, d, f2 = w1_all.shape
    f = f2 // 2
    rows = 256
    out_spec = pl.BlockSpec((1, rows, f), lambda i, j: (i, j, 0))
    return pl.pallas_call(
        _w1_prep_kernel,
        grid=(e, d // rows),
        in_specs=[pl.BlockSpec((1, 1, rows, f2), lambda i, j: (layer, i, j, 0))],
        out_specs=[out_spec, out_spec],
        out_shape=[jax.ShapeDtypeStruct((e, d, f), BF16)] * 2,
        compiler_params=_params(("parallel", "parallel")),
    )(w1_all)


def _expert_kernel(be_ref, nu_ref, x_ref, w1g_ref, w1l_ref, b1g_ref, b1l_ref, w2_ref, b2_ref, o_ref):
    del be_ref
    i = pl.program_id(0)

    @pl.when(i < nu_ref[0])
    def _():
        x = x_ref[...]
        glu = jnp.minimum(_dot(x, w1g_ref[0]) + b1g_ref[0], SWIGLU_LIMIT)
        lin = jnp.clip(_dot(x, w1l_ref[0]) + b1l_ref[0], -SWIGLU_LIMIT, SWIGLU_LIMIT)
        act = glu * _sigmoid(SWIGLU_ALPHA * glu) * (lin + 1.0)
        o_ref[...] = _dot(act.astype(BF16), w2_ref[0, 0]) + b2_ref[0]

    @pl.when(i >= nu_ref[0])
    def _():
        o_ref[...] = jnp.zeros_like(o_ref)


def _expert_call(block_expert, n_used, x_rows, w1g, w1l, b1g, b1l, w2_all, layer, b2):
    r, d = x_rows.shape
    f = w1g.shape[2]
    tm = MOE_BLOCK
    wspec = lambda s: pl.BlockSpec((1,) + s, lambda i, be, nu: (be[i], 0, 0))
    return pl.pallas_call(
        _expert_kernel,
        grid_spec=pltpu.PrefetchScalarGridSpec(
            num_scalar_prefetch=2,
            grid=(r // tm,),
            in_specs=[
                pl.BlockSpec((tm, d), lambda i, be, nu: (i, 0)),
                wspec((d, f)), wspec((d, f)), wspec((1, f)), wspec((1, f)),
                pl.BlockSpec((1, 1, f, d), lambda i, be, nu: (layer, be[i], 0, 0)),
                wspec((1, d)),
            ],
            out_specs=pl.BlockSpec((tm, d), lambda i, be, nu: (i, 0)),
        ),
        out_shape=jax.ShapeDtypeStruct((r, d), F32),
        compiler_params=_params(("arbitrary",)),
    )(block_expert, n_used, x_rows, w1g, w1l, b1g, b1l, w2_all, b2)


def _combine_kernel(x_ref, y_ref, gate_ref, mod_ref, g_ref, o_ref):
    gates = gate_ref[0]
    y = gates[:, 0:1] * y_ref[0, 0]
    for k in range(1, TOP_K):
        y = y + gates[:, k:k + 1] * y_ref[k, 0]
    mod = mod_ref[0, 0]
    o_ref[0] = x_ref[0] + mod[5:6] * (_rms(y) * g_ref[...][3:4])


def _combine_call(xc, picked, gates, modl, g, n_lat):
    b, n, d = xc.shape
    tm = TOKEN_TILE
    tok = pl.BlockSpec((1, tm, d), lambda i, t: (i, t, 0))
    return pl.pallas_call(
        _combine_kernel,
        grid=(b, n // tm),
        in_specs=[
            tok,
            pl.BlockSpec((TOP_K, 1, tm, d), lambda i, t: (0, i, t, 0)),
            pl.BlockSpec((1, tm, TOP_K), lambda i, t: (i, t, 0)),
            _mod_spec(d, n_lat // tm),
            pl.BlockSpec(g.shape, lambda i, t: (0, 0)),
        ],
        out_specs=tok,
        out_shape=jax.ShapeDtypeStruct((b, n, d), F32),
        compiler_params=_params(("parallel", "parallel")),
    )(xc, picked, gates, modl, g)


def _rope_tables(rows, n_ctx, rot_dim):
    row = jnp.repeat(jnp.arange(rows), GRID_W)
    col = jnp.tile(jnp.arange(GRID_W), rows)
    n_freq = rot_dim // 4
    inv = ROPE_THETA ** (-jnp.arange(n_freq, dtype=F32) / n_freq)
    ang = jnp.concatenate([row[:, None] * inv, col[:, None] * inv], axis=-1)
    cos, sin = jnp.cos(ang), jnp.sin(ang)
    reps = 256 // rot_dim
    cos_t = jnp.tile(jnp.concatenate([cos, cos], axis=-1), (1, reps))
    sin_t = jnp.tile(jnp.concatenate([-sin, sin], axis=-1), (1, reps))
    cos_t = jnp.concatenate([cos_t, jnp.ones((n_ctx, 256), F32)], axis=0)
    sin_t = jnp.concatenate([sin_t, jnp.zeros((n_ctx, 256), F32)], axis=0)
    return cos_t, sin_t


def _pack_w_in(w):
    parts, src = [], 0
    for _, width, padded in IN_PIECES:
        parts.append(w[:, src:src + width])
        if padded > width:
            parts.append(jnp.zeros((w.shape[0], padded - width), w.dtype))
        src += width
    return jnp.concatenate(parts, axis=1).astype(BF16)


def _pack_w_uq(w):
    wh = w.reshape(MLA_Q_RANK, N_HEADS, MLA_NOPE + MLA_ROPE)
    nope = wh[:, :, :MLA_NOPE].reshape(MLA_Q_RANK, N_HEADS * MLA_NOPE)
    rope = wh[:, :, MLA_NOPE:].reshape(MLA_Q_RANK, N_HEADS * MLA_ROPE)
    packed = jnp.concatenate([nope, rope], axis=1)
    return jnp.pad(packed, ((0, 256 - MLA_Q_RANK), (0, 0))).astype(BF16)


def _pack_w_ukv(w):
    wh = w.reshape(MLA_KV_RANK, N_HEADS, MLA_NOPE + HEAD_DIM)
    kn = wh[:, :, :MLA_NOPE].reshape(MLA_KV_RANK, N_HEADS * MLA_NOPE)
    vv = wh[:, :, MLA_NOPE:].reshape(MLA_KV_RANK, N_HEADS * HEAD_DIM)
    return jnp.concatenate([kn, vv], axis=1).astype(BF16)


def _heads(a, n_heads):
    b, n, w = a.shape
    return a.reshape(b, n, n_heads, w // n_heads).transpose(0, 2, 1, 3)


def _heads_t(a, n_heads):
    b, n, w = a.shape
    return a.reshape(b, n, n_heads, w // n_heads).transpose(0, 2, 3, 1)


def _attention(q, kt, v, n_lat, n_ctx, ctx_out, diff=None):
    n = n_lat + n_ctx
    y = _attn_call(q, kt, v, (0, n_lat), (0, n), min(Q_TILE, n_lat), diff)
    if ctx_out:
        y_c = _attn_call(q, kt, v, (n_lat, n_ctx), (n_lat, n_ctx), n_ctx, diff)
    else:
        y_c = jnp.zeros((y.shape[0], n_ctx, y.shape[2]), y.dtype)
    return jnp.concatenate([y, y_c], axis=1)


def _heads_v(a, n_heads):
    h = _heads(a, n_heads)
    return jnp.concatenate([h, jnp.ones_like(h)], axis=-1)


def _moe(x_new, h2, idx, gates, rank, counts, modl, g, n_lat, w1_all, b1, w2_all, b2, layer):
    b, n, d = h2.shape
    t = b * n
    n_assign = t * TOP_K
    n_blocks = -(-n_assign // MOE_BLOCK) + N_EXPERTS
    experts = jnp.arange(N_EXPERTS, dtype=jnp.int32)
    tile_counts = counts.reshape(-1, N_EXPERTS)
    tile_base = jnp.cumsum(tile_counts, axis=0) - tile_counts
    totals = jnp.sum(tile_counts, axis=0)
    padded = (totals + MOE_BLOCK - 1) // MOE_BLOCK * MOE_BLOCK
    pad_ends = jnp.cumsum(padded)
    table = (tile_base + (pad_ends - padded)[None, :]).reshape(b, 1, n // TOKEN_TILE, 1, N_EXPERTS)
    idx5 = idx.reshape(b, TOP_K, n // TOKEN_TILE, TOKEN_TILE, 1)
    dest = rank + jnp.sum(jnp.where(idx5 == experts, table, 0), axis=-1).reshape(b, TOP_K, n)
    tok = jnp.broadcast_to((jnp.arange(b, dtype=jnp.int32)[:, None, None] * n
                            + jnp.arange(n, dtype=jnp.int32)[None, None, :]), (b, TOP_K, n))
    src_tok = jnp.zeros((n_blocks * MOE_BLOCK,), jnp.int32).at[dest.reshape(-1)].set(
        tok.reshape(-1), unique_indices=True)
    block_start = jnp.arange(n_blocks, dtype=jnp.int32) * MOE_BLOCK
    block_expert = jnp.minimum(jnp.sum((pad_ends[None, :] <= block_start[:, None]).astype(jnp.int32), axis=1),
                               N_EXPERTS - 1)
    n_used = (pad_ends[-1:] // MOE_BLOCK).astype(jnp.int32)
    x_rows = h2.reshape(t, d).at[src_tok].get(mode="promise_in_bounds")
    f = w2_all.shape[2]
    w1g, w1l = _w1_prep_call(w1_all, layer)
    y_rows = _expert_call(
        block_expert, n_used, x_rows, w1g, w1l,
        b1[:, 0::2].reshape(N_EXPERTS, 1, f), b1[:, 1::2].reshape(N_EXPERTS, 1, f),
        w2_all, layer, b2.reshape(N_EXPERTS, 1, d))
    picked = y_rows.at[dest.transpose(1, 0, 2).reshape(-1)].get(mode="promise_in_bounds")
    return _combine_call(x_new, picked.reshape(TOP_K, b, n, d), gates.transpose(0, 2, 1), modl, g, n_lat)


def kernel(x, c, ctx, c_ctx, ada_w, ada_b, norm_g, w_in, w_out, ret_log_decay, ret_gn_w, ret_gn_b,
           diff_lambda, diff_subln, gqa_qk_norm, mla_q_norm, mla_kv_norm, mla_w_uq, mla_w_ukv,
           router_w, router_b, exp_w1, exp_b1, exp_w2, exp_b2):
    b, s, d = x.shape
    n_ctx = ctx.shape[1]
    n = s + n_ctx
    depth = ada_w.shape[0]
    assert n_ctx % TOKEN_TILE == 0 and s % TOKEN_TILE == 0 and s % GRID_W == 0
    assert n_ctx % RET_CHUNK == 0 and s % min(Q_TILE, s) == 0 and s % n_ctx == 0

    tables = _rope_tables(s // GRID_W, n_ctx, HEAD_DIM) + _rope_tables(s // GRID_W, n_ctx, DIFF_D)
    c_rows = jnp.zeros((16, d), F32).at[:b].set(c).at[b].set(c_ctx)
    mods = _ada_call(c_rows, ada_w, ada_b)
    w2_all = exp_w2.astype(BF16)
    xc = jnp.concatenate([x, ctx], axis=1)

    for l in range(depth):
        last = l == depth - 1
        lam_init = 0.8 - 0.6 * math.exp(-0.3 * l)
        mod_lat = mods[l, :b].reshape(b, 1, 6, d)
        mod_ctx = jnp.broadcast_to(mods[l, b].reshape(1, 1, 6, d), (b, 1, 6, d))
        modl = jnp.concatenate([mod_ctx, mod_lat], axis=1)

        (rq, rk, rv, rg, dq, dk, dv, gq, gk, gv, mqn, mqr, mkn, mv, mkr) = _in_proj_call(
            xc, modl, norm_g[l, 0:1], _pack_w_in(w_in[l]), tables,
            jnp.tile(gqa_qk_norm[l, 0], N_HEADS)[None, :], jnp.tile(gqa_qk_norm[l, 1], GQA_KV_HEADS)[None, :],
            jnp.pad(mla_q_norm[l], (0, 256 - MLA_Q_RANK))[None, :], mla_kv_norm[l][None, :],
            _pack_w_uq(mla_w_uq[l]), _pack_w_ukv(mla_w_ukv[l]), s)

        log_g = -jnp.exp(ret_log_decay[l].astype(F32))
        o_f = _ret_call(rq, rk, rv, _ret_tables(log_g[0], False), s, False)
        ret_y = _ret_call(rq, rk, rv, _ret_tables(log_g[1], True), s, True,
                          (o_f, rg, ret_gn_w[l][None, :], ret_gn_b[l][None, :]))

        lp = diff_lambda[l].astype(F32)
        lam = jnp.exp(jnp.sum(lp[0] * lp[1])) - jnp.exp(jnp.sum(lp[2] * lp[3])) + lam_init
        dif_y = _attention(_heads(dq, 2 * N_HEADS), _heads_t(dk, 2 * N_HEADS), _heads_v(dv, N_HEADS), s, n_ctx,
                           not last, (lam, diff_subln[l], 1.0 - lam_init))

        gqa_y = _attention(_heads(gq, N_HEADS), _heads_t(gk, GQA_KV_HEADS), _heads_v(gv, GQA_KV_HEADS), s, n_ctx,
                           not last)

        q_m = jnp.concatenate([_heads(mqn, N_HEADS), _heads(mqr, N_HEADS)], axis=-1)
        kr_t = jnp.broadcast_to(mkr[:, None, :, :MLA_ROPE].transpose(0, 1, 3, 2), (b, N_HEADS, MLA_ROPE, n))
        kt_m = jnp.concatenate([_heads_t(mkn, N_HEADS), kr_t], axis=2)
        mla_y = _attention(q_m, kt_m, _heads_v(mv, N_HEADS), s, n_ctx, not last)

        ymix = jnp.concatenate([ret_y, dif_y, gqa_y, mla_y], axis=-1)
        n_rows = s if last else n
        x_new, h2, idx, gates, rank, counts = _out_proj_call(
            ymix, w_out[l].astype(BF16), xc, modl, norm_g[l], router_w[l].T, router_b[l][:, None], s, n_rows)
        xc = _moe(x_new, h2, idx, gates, rank, counts[..., 0], modl, norm_g[l], s,
                  exp_w1, exp_b1[l], w2_all, exp_b2[l], l)

    return xc[:, :s]
```

```python
import functools
import math

import jax
import jax.numpy as jnp
from jax import lax
from jax.experimental import pallas as pl
from jax.experimental.pallas import tpu as pltpu

F32 = jnp.float32
BF16 = jnp.bfloat16

GRID_W = 64
ROPE_THETA = 10000.0
EPS = 1e-6
GROUP_WIDTH = 256
HEAD_DIM = 64
N_HEADS = 4
RET_CHUNK = 256
DIFF_D = 32
GQA_KV_HEADS = 2
MLA_Q_RANK = 192
MLA_KV_RANK = 128
MLA_NOPE = 64
MLA_ROPE = 32
N_EXPERTS = 32
TOP_K = 4
SWIGLU_LIMIT = 7.0
SWIGLU_ALPHA = 1.702
MOE_BLOCK = 512

LANES = 128
MXU_DIM = 256
TOKEN_TILE = 256
Q_TILE = 1024
KV_TILE = 512
VMEM_LIMIT = 48 * 1024 * 1024
LOG2E = math.log2(math.e)

IN_PIECES = (
    ("ret_q", 256, 256), ("ret_k", 256, 256), ("ret_v", 256, 256), ("ret_g", 256, 256),
    ("dif_q", 256, 256), ("dif_k", 256, 256), ("dif_v", 256, 256),
    ("gqa_q", 256, 256), ("gqa_k", 128, 128), ("gqa_v", 128, 128),
    ("mla_cq", MLA_Q_RANK, 256), ("mla_ckv", MLA_KV_RANK, 128), ("mla_kr", MLA_ROPE, 128),
)
IN_OFFSETS = {}
_off = 0
for _name, _w, _pw in IN_PIECES:
    IN_OFFSETS[_name] = _off
    _off += _pw
IN_PACKED_WIDTH = _off


def _params(sem):
    return pltpu.CompilerParams(dimension_semantics=sem, vmem_limit_bytes=VMEM_LIMIT)


def _rms(x):
    return x * lax.rsqrt(jnp.mean(x * x, axis=-1, keepdims=True) + EPS)


def _split_bf16(a):
    hi = a.astype(BF16)
    lo = (a - hi.astype(F32)).astype(BF16)
    return hi, lo


def _dot_nt(a, b):
    return lax.dot_general(a, b, (((1,), (1,)), ((), ())), preferred_element_type=F32)


def _dot(a, b):
    return jnp.dot(a, b, preferred_element_type=F32)


def _sigmoid(a):
    return 1.0 / (1.0 + jnp.exp(-a))


def _mod_spec(d, n_lat_tiles):
    return pl.BlockSpec((1, 1, 6, d), lambda i, t: (i, jnp.where(t < n_lat_tiles, 1, 0), 0, 0))


def _ada_kernel(c_ref, w_ref, b_ref, o_ref):
    s = c_ref[...]
    s = s * _sigmoid(s)
    s_hi, s_lo = _split_bf16(s)
    w_hi, w_lo = _split_bf16(w_ref[0])
    o_ref[0] = _dot(s_hi, w_hi) + _dot(s_hi, w_lo) + _dot(s_lo, w_hi) + b_ref[0]


def _ada_call(c_rows, ada_w, ada_b):
    depth, d, n6 = ada_w.shape
    rows = c_rows.shape[0]
    tn = 1536
    return pl.pallas_call(
        _ada_kernel,
        grid=(depth, n6 // tn),
        in_specs=[
            pl.BlockSpec((rows, d), lambda l, j: (0, 0)),
            pl.BlockSpec((1, d, tn), lambda l, j: (l, 0, j)),
            pl.BlockSpec((1, 1, tn), lambda l, j: (l, 0, j)),
        ],
        out_specs=pl.BlockSpec((1, rows, tn), lambda l, j: (l, 0, j)),
        out_shape=jax.ShapeDtypeStruct((depth, rows, n6), F32),
        compiler_params=_params(("parallel", "parallel")),
    )(c_rows, ada_w, ada_b.reshape(depth, 1, n6))


def _rope(x, cos, sin_signed, half):
    outs = []
    for c in range(x.shape[1] // LANES):
        sl = slice(c * LANES, (c + 1) * LANES)
        xc = x[:, sl]
        lane = lax.broadcasted_iota(jnp.int32, xc.shape, 1)
        first_half = (lane % (2 * half)) < half
        partner = jnp.where(first_half, pltpu.roll(xc, LANES - half, 1), pltpu.roll(xc, half, 1))
        outs.append(xc * cos[:, sl] + partner * sin_signed[:, sl])
    return outs[0] if len(outs) == 1 else jnp.concatenate(outs, axis=1)


def _group_mean_sq(x, gsize):
    w = x.shape[1]
    r = lax.broadcasted_iota(jnp.int32, (w, w), 0) // gsize
    c = lax.broadcasted_iota(jnp.int32, (w, w), 1) // gsize
    ones = jnp.where(r == c, 1.0, 0.0).astype(BF16)
    hi, lo = _split_bf16(x * x)
    return (_dot(hi, ones) + _dot(lo, ones)) * (1.0 / gsize)


def _in_proj_kernel(x_ref, mod_ref, g_ref, w_ref, cos64_ref, sin64_ref, cos32_ref, sin32_ref,
                    gqn_ref, gkn_ref, mqn_ref, mkvn_ref, wuq_ref, wukv_ref,
                    rq_ref, rk_ref, rv_ref, rg_ref, dq_ref, dk_ref, dv_ref, gq_ref, gk_ref, gv_ref,
                    mqn_o, mqr_o, mkn_o, mv_o, mkr_o):
    mod = mod_ref[0, 0]
    h = _rms(x_ref[0]) * g_ref[...] * (1.0 + mod[1:2]) + mod[0:1]
    hb = h.astype(BF16)
    cos64, sin64 = cos64_ref[...], sin64_ref[...]
    cos32, sin32 = cos32_ref[...], sin32_ref[...]

    def proj(name, width):
        o = IN_OFFSETS[name]
        return _dot(hb, w_ref[:, o:o + width])

    rq_ref[0] = _rope(proj("ret_q", 256), cos64, sin64, 32).astype(BF16)
    rk_ref[0] = (_rope(proj("ret_k", 256), cos64, sin64, 32) * (HEAD_DIM ** -0.5)).astype(BF16)
    rv_ref[0] = proj("ret_v", 256).astype(BF16)
    rg_ref[0] = proj("ret_g", 256)
    dq_ref[0] = (_rope(proj("dif_q", 256), cos32, sin32, 16) * (DIFF_D ** -0.5 * LOG2E)).astype(BF16)
    dk_ref[0] = _rope(proj("dif_k", 256), cos32, sin32, 16).astype(BF16)
    dv_ref[0] = proj("dif_v", 256).astype(BF16)
    gq = proj("gqa_q", 256)
    gq = gq * lax.rsqrt(_group_mean_sq(gq, HEAD_DIM) + EPS) * gqn_ref[...]
    gq_ref[0] = (_rope(gq, cos64, sin64, 32) * (HEAD_DIM ** -0.5 * LOG2E)).astype(BF16)
    gk = proj("gqa_k", 128)
    gk = gk * lax.rsqrt(_group_mean_sq(gk, HEAD_DIM) + EPS) * gkn_ref[...]
    gk_ref[0] = _rope(gk, cos64[:, :LANES], sin64[:, :LANES], 32).astype(BF16)
    gv_ref[0] = proj("gqa_v", 128).astype(BF16)
    cq = proj("mla_cq", 256)
    cq = cq * lax.rsqrt(jnp.sum(cq * cq, axis=-1, keepdims=True) * (1.0 / MLA_Q_RANK) + EPS) * mqn_ref[...]
    q_up = _dot(cq.astype(BF16), wuq_ref[...])
    mla_scale = (MLA_NOPE + MLA_ROPE) ** -0.5 * LOG2E
    mqn_o[0] = (q_up[:, :256] * mla_scale).astype(BF16)
    mqr_o[0] = (_rope(q_up[:, 256:], cos32[:, :LANES], sin32[:, :LANES], 16) * mla_scale).astype(BF16)
    ckv = proj("mla_ckv", 128)
    ckv = ckv * lax.rsqrt(jnp.mean(ckv * ckv, axis=-1, keepdims=True) + EPS) * mkvn_ref[...]
    kv_up = _dot(ckv.astype(BF16), wukv_ref[...])
    mkn_o[0] = kv_up[:, :256].astype(BF16)
    mv_o[0] = kv_up[:, 256:].astype(BF16)
    mkr_o[0] = _rope(proj("mla_kr", 128), cos32[:, :LANES], sin32[:, :LANES], 16).astype(BF16)


def _in_proj_call(xc, modl, g0, w_in_p, tables, gqn, gkn, mqn, mkvn, wuq_p, wukv_p, n_lat):
    b, n, d = xc.shape
    tm = TOKEN_TILE
    tok = lambda w: pl.BlockSpec((1, tm, w), lambda i, t: (i, t, 0))
    const2 = lambda a: pl.BlockSpec(a.shape, lambda i, t: (0, 0))
    tab = pl.BlockSpec((tm, 256), lambda i, t: (t, 0))
    out_widths = (256, 256, 256, 256, 256, 256, 256, 256, 128, 128, 256, 128, 256, 256, 128)
    out_dtypes = (BF16, BF16, BF16, F32, BF16, BF16, BF16, BF16, BF16, BF16, BF16, BF16, BF16, BF16, BF16)
    return pl.pallas_call(
        _in_proj_kernel,
        grid=(b, n // tm),
        in_specs=[
            tok(d), _mod_spec(d, n_lat // tm),
            const2(g0), const2(w_in_p), tab, tab, tab, tab,
            const2(gqn), const2(gkn), const2(mqn), const2(mkvn), const2(wuq_p), const2(wukv_p),
        ],
        out_specs=[tok(w) for w in out_widths],
        out_shape=[jax.ShapeDtypeStruct((b, n, w), dt) for w, dt in zip(out_widths, out_dtypes)],
        compiler_params=_params(("parallel", "parallel")),
    )(xc, modl, g0, w_in_p, *tables, gqn, gkn, mqn, mkvn, wuq_p, wukv_p)


def _ret_kernel(*refs, readout):
    if readout:
        (q_ref, k_ref, v_ref, dec_ref, xi_ref, zeta_ref, gc_ref, of_ref, g_ref, gnw_ref, gnb_ref,
         o_ref, state_ref) = refs
    else:
        q_ref, k_ref, v_ref, dec_ref, xi_ref, zeta_ref, gc_ref, o_ref, state_ref = refs

    @pl.when(pl.program_id(1) == 0)
    def _():
        state_ref[...] = jnp.zeros_like(state_ref)

    q, k, v = q_ref[0], k_ref[0], v_ref[0]
    outs = []
    for h in range(N_HEADS):
        sl = slice(h * HEAD_DIM, (h + 1) * HEAD_DIM)
        qh, kh, vh = q[:, sl], k[:, sl], v[:, sl]
        st = state_ref[h]
        inner = _dot_nt(qh, kh) * dec_ref[h]
        o = _dot(inner.astype(BF16), vh) + _dot(qh, st.astype(BF16)) * xi_ref[h]
        kz = (kh.astype(F32) * zeta_ref[h]).astype(BF16)
        kv = lax.dot_general(kz, vh, (((0,), (0,)), ((), ())), preferred_element_type=F32)
        state_ref[h] = gc_ref[h] * st + kv
        if readout:
            o = o + of_ref[0][:, sl]
            mu = jnp.mean(o, axis=-1, keepdims=True)
            var = jnp.mean(jnp.square(o - mu), axis=-1, keepdims=True)
            o = (o - mu) * lax.rsqrt(var + EPS)
        outs.append(o)
    o = jnp.concatenate(outs, axis=1)
    if readout:
        g = g_ref[0]
        o_ref[0] = ((o * gnw_ref[...] + gnb_ref[...]) * (g * _sigmoid(g))).astype(o_ref.dtype)
    else:
        o_ref[0] = o


def _ret_call(q, k, v, tabs, n_lat, backward, readout_args=None):
    b, n, w = q.shape
    c = RET_CHUNK
    nch = n // c
    nlc = n_lat // c
    if backward:
        chunk = lambda t: nch - 1 - t
    else:
        chunk = lambda t: jnp.where(t < nch - nlc, nlc + t, t - (nch - nlc))
    tok = lambda dt_w: pl.BlockSpec((1, c, dt_w), lambda i, t: (i, chunk(t), 0))
    const = lambda a: pl.BlockSpec(a.shape, lambda i, t: (0,) * a.ndim)
    in_specs = [tok(w), tok(w), tok(w)] + [const(a) for a in tabs]
    args = [q, k, v, *tabs]
    if readout_args is not None:
        o_f, g, gnw, gnb = readout_args
        in_specs += [tok(w), tok(w), const(gnw), const(gnb)]
        args += [o_f, g, gnw, gnb]
    out_dtype = BF16 if readout_args is not None else F32
    return pl.pallas_call(
        functools.partial(_ret_kernel, readout=readout_args is not None),
        grid=(b, nch),
        in_specs=in_specs,
        out_specs=tok(w),
        out_shape=jax.ShapeDtypeStruct((b, n, w), out_dtype),
        scratch_shapes=[pltpu.VMEM((N_HEADS, HEAD_DIM, HEAD_DIM), F32)],
        compiler_params=_params(("parallel", "arbitrary")),
    )(*args)


def _ret_tables(log_g, backward):
    c = RET_CHUNK
    pos = jnp.arange(c, dtype=F32)
    dist = (pos[None, :] - pos[:, None]) if backward else (pos[:, None] - pos[None, :])
    lg = log_g[:, None, None]
    decay = jnp.where(dist >= 0, jnp.exp(lg * jnp.maximum(dist, 0.0)), 0.0)
    to_state = (c - pos) if backward else (pos + 1.0)
    to_end = pos if backward else (c - 1.0 - pos)
    xi = jnp.exp(log_g[:, None] * to_state)[:, :, None]
    zeta = jnp.exp(log_g[:, None] * to_end)[:, :, None]
    ones = jnp.ones((1, 1, HEAD_DIM), F32)
    gc = jnp.exp(log_g * c)[:, None, None] * jnp.ones((1, HEAD_DIM, HEAD_DIM), F32)
    return decay, xi * ones, zeta * ones, gc


def _attn_kernel(*refs, n_maps, readout_scale):
    if n_maps == 2:
        lam_ref, q_ref, kt_ref, v_ref, sub_ref, o_ref, first_ref = refs
    else:
        q_ref, kt_ref, v_ref, o_ref = refs
    hm = pl.program_id(2)
    q = q_ref[0, 0]
    tq = q.shape[0]
    n_keys = kt_ref.shape[3]
    dv = v_ref.shape[3] // 2
    tk = min(KV_TILE, n_keys)
    n_full, tail = divmod(n_keys, tk)

    def step(off, size, carry):
        m, acc = carry
        s = _dot(q, kt_ref[0, 0, :, pl.ds(off, size)])
        m_new = jnp.maximum(m, jnp.max(s, axis=-1, keepdims=True))
        p = jnp.exp2(s - m_new)
        acc = jnp.exp2(m - m_new) * acc + _dot(p.astype(BF16), v_ref[0, 0, pl.ds(off, size), :])
        return m_new, acc

    carry = (jnp.full((tq, 1), -1e30, F32), jnp.zeros((tq, 2 * dv), F32))
    for j in range(n_full):
        carry = step(j * tk, tk, carry)
    if tail:
        carry = step(n_full * tk, tail, carry)
    acc = carry[1]
    o = acc[:, :dv] / acc[:, dv:]

    def store(head_of_step, val):
        for hh in range(N_HEADS):
            @pl.when(head_of_step == hh)
            def _(hh=hh):
                o_ref[0, :, hh * dv:(hh + 1) * dv] = val.astype(o_ref.dtype)

    if n_maps == 1:
        store(hm, o)
    else:
        @pl.when(hm % 2 == 0)
        def _():
            first_ref[...] = o

        @pl.when(hm % 2 == 1)
        def _():
            od = first_ref[...] - lam_ref[0] * o
            od = od * lax.rsqrt(jnp.mean(od * od, axis=-1, keepdims=True) + EPS) * sub_ref[...]
            store(hm // 2, od * readout_scale)


def _attn_call(q, kt, v, rows, keys, tq, diff=None):
    b, hq, n, d = q.shape
    hk, hv, dv = kt.shape[1], v.shape[1], v.shape[3] // 2
    (q0, qn), (k0, kn) = rows, keys
    assert q0 % tq == 0 and qn % tq == 0 and k0 % kn == 0
    n_maps = 2 if diff is not None else 1
    in_specs = [
        pl.BlockSpec((1, 1, tq, d), lambda i, t, h: (i, h, q0 // tq + t, 0)),
        pl.BlockSpec((1, 1, d, kn), lambda i, t, h: (i, h // (hq // hk), 0, k0 // kn)),
        pl.BlockSpec((1, 1, kn, 2 * dv), lambda i, t, h: (i, h // (hq // hv), k0 // kn, 0)),
    ]
    args = [q, kt, v]
    scratch = []
    scale = 1.0
    if diff is not None:
        lam, subln, scale = diff
        in_specs = [pl.BlockSpec(memory_space=pltpu.SMEM)] + in_specs + [pl.BlockSpec((1, dv), lambda i, t, h: (0, 0))]
        args = [lam.reshape(1)] + args + [subln.reshape(1, dv)]
        scratch = [pltpu.VMEM((tq, dv), F32)]
    return pl.pallas_call(
        functools.partial(_attn_kernel, n_maps=n_maps, readout_scale=scale),
        grid=(b, qn // tq, hq),
        in_specs=in_specs,
        out_specs=pl.BlockSpec((1, tq, (hq // n_maps) * dv), lambda i, t, h: (i, t, 0)),
        out_shape=jax.ShapeDtypeStruct((b, qn, (hq // n_maps) * dv), BF16),
        scratch_shapes=scratch,
        compiler_params=_params(("parallel", "parallel", "arbitrary")),
    )(*args)


def _out_proj_kernel(y_ref, w_ref, x_ref, mod_ref, g_ref, wr_ref, br_ref,
                     xo_ref, h_ref, idx_ref, gate_ref, rank_ref, cnt_ref):
    mod = mod_ref[0, 0]
    g = g_ref[...]
    y = _dot(y_ref[0], w_ref[...])
    xn = x_ref[0] + mod[2:3] * (_rms(y) * g[1:2])
    xo_ref[0] = xn
    h = _rms(xn) * g[2:3] * (1.0 + mod[4:5]) + mod[3:4]
    h_ref[0] = h.astype(BF16)
    w_hi, w_lo = _split_bf16(wr_ref[...])
    h_hi, h_lo = _split_bf16(h)
    logits = _dot_nt(w_hi, h_hi) + _dot_nt(w_hi, h_lo) + _dot_nt(w_lo, h_hi) + br_ref[...]
    row = lax.broadcasted_iota(jnp.int32, logits.shape, 0)
    vals = logits
    tops, idxs = [], []
    for _ in range(TOP_K):
        m = jnp.max(vals, axis=0, keepdims=True)
        idx = jnp.min(jnp.where(vals == m, row, N_EXPERTS), axis=0, keepdims=True)
        tops.append(m)
        idxs.append(idx)
        vals = jnp.where(row == idx, -jnp.inf, vals)
    ex = [jnp.exp(tv - tops[0]) for tv in tops]
    den = ex[0] + ex[1] + ex[2] + ex[3]
    gate_ref[0] = jnp.concatenate([e / den for e in ex], axis=0)
    idx_ref[0] = jnp.concatenate(idxs, axis=0)
    tm = logits.shape[1]
    before = jnp.where(lax.broadcasted_iota(jnp.int32, (tm, tm), 0) < lax.broadcasted_iota(jnp.int32, (tm, tm), 1),
                       1.0, 0.0).astype(BF16)
    base = jnp.zeros((N_EXPERTS, 1), F32)
    ranks = []
    for idx in idxs:
        onehot = jnp.where(row == idx, 1.0, 0.0)
        prefix = _dot(onehot.astype(BF16), before)
        ranks.append(jnp.sum(onehot * (prefix + base), axis=0, keepdims=True))
        base = base + jnp.sum(onehot, axis=1, keepdims=True)
    rank_ref[0] = jnp.concatenate(ranks, axis=0).astype(jnp.int32)
    cnt_ref[0, 0] = jnp.broadcast_to(base, (N_EXPERTS, LANES)).astype(jnp.int32)


def _out_proj_call(ymix, w_out, xc, modl, g, wr_t, br, n_lat, n_rows):
    b, n, d = xc.shape
    tm = TOKEN_TILE
    tok = lambda w: pl.BlockSpec((1, tm, w), lambda i, t: (i, t, 0))
    const2 = lambda a: pl.BlockSpec(a.shape, lambda i, t: (0, 0))
    sel = pl.BlockSpec((1, TOP_K, tm), lambda i, t: (i, 0, t))
    return pl.pallas_call(
        _out_proj_kernel,
        grid=(b, n_rows // tm),
        in_specs=[
            tok(ymix.shape[2]), const2(w_out), tok(d), _mod_spec(d, n_lat // tm),
            const2(g), const2(wr_t), const2(br),
        ],
        out_specs=[tok(d), tok(d), sel, sel, sel,
                   pl.BlockSpec((1, 1, N_EXPERTS, LANES), lambda i, t: (i, t, 0, 0))],
        out_shape=[
            jax.ShapeDtypeStruct((b, n_rows, d), F32),
            jax.ShapeDtypeStruct((b, n_rows, d), BF16),
            jax.ShapeDtypeStruct((b, TOP_K, n_rows), jnp.int32),
            jax.ShapeDtypeStruct((b, TOP_K, n_rows), F32),
            jax.ShapeDtypeStruct((b, TOP_K, n_rows), jnp.int32),
            jax.ShapeDtypeStruct((b, n_rows // tm, N_EXPERTS, LANES), jnp.int32),
        ],
        compiler_params=_params(("parallel", "parallel")),
    )(ymix, w_out, xc, modl, g, wr_t, br)


def _w1_prep_kernel(w_ref, g_ref, l_ref):
    r = lax.broadcasted_iota(jnp.int32, (MXU_DIM, MXU_DIM), 0)
    c = lax.broadcasted_iota(jnp.int32, (MXU_DIM, MXU_DIM), 1)
    src = jnp.where(c < LANES, 2 * c, 2 * (c - LANES) + 1)
    perm = jnp.where(r == src, 1.0, 0.0).astype(BF16)
    for j in range(w_ref.shape[3] // MXU_DIM):
        out = _dot(w_ref[0, 0, :, j * MXU_DIM:(j + 1) * MXU_DIM].astype(BF16), perm)
        g_ref[0, :, j * LANES:(j + 1) * LANES] = out[:, :LANES].astype(BF16)
        l_ref[0, :, j * LANES:(j + 1) * LANES] = out[:, LANES:].astype(BF16)


def _w1_prep_call(w1_all, layer):
    _, e, d, f2 = w1_all.shape
    f = f2 // 2
    rows = 256
    out_spec = pl.BlockSpec((1, rows, f), lambda i, j: (i, j, 0))
    return pl.pallas_call(
        _w1_prep_kernel,
        grid=(e, d // rows),
        in_specs=[pl.BlockSpec((1, 1, rows, f2), lambda i, j: (layer, i, j, 0))],
        out_specs=[out_spec, out_spec],
        out_shape=[jax.ShapeDtypeStruct((e, d, f), BF16)] * 2,
        compiler_params=_params(("parallel", "parallel")),
    )(w1_all)


def _expert_kernel(be_ref, nu_ref, x_ref, w1g_ref, w1l_ref, b1g_ref, b1l_ref, w2_ref, b2_ref, o_ref):
    del be_ref
    i = pl.program_id(0)

    @pl.when(i < nu_ref[0])
    def _():
        x = x_ref[...]
        glu = jnp.minimum(_dot(x, w1g_ref[0]) + b1g_ref[0], SWIGLU_LIMIT)
        lin = jnp.clip(_dot(x, w1l_ref[0]) + b1l_ref[0], -SWIGLU_LIMIT, SWIGLU_LIMIT)
        act = glu * _sigmoid(SWIGLU_ALPHA * glu) * (lin + 1.0)
        o_ref[...] = (_dot(act.astype(BF16), w2_ref[0, 0]) + b2_ref[0]).astype(o_ref.dtype)

    @pl.when(i >= nu_ref[0])
    def _():
        o_ref[...] = jnp.zeros_like(o_ref)


def _expert_call(block_expert, n_used, x_rows, w1g, w1l, b1g, b1l, w2_all, layer, b2):
    r, d = x_rows.shape
    f = w1g.shape[2]
    tm = MOE_BLOCK
    wspec = lambda s: pl.BlockSpec((1,) + s, lambda i, be, nu: (be[i], 0, 0))
    return pl.pallas_call(
        _expert_kernel,
        grid_spec=pltpu.PrefetchScalarGridSpec(
            num_scalar_prefetch=2,
            grid=(r // tm,),
            in_specs=[
                pl.BlockSpec((tm, d), lambda i, be, nu: (i, 0)),
                wspec((d, f)), wspec((d, f)), wspec((1, f)), wspec((1, f)),
                pl.BlockSpec((1, 1, f, d), lambda i, be, nu: (layer, be[i], 0, 0)),
                wspec((1, d)),
            ],
            out_specs=pl.BlockSpec((tm, d), lambda i, be, nu: (i, 0)),
        ),
        out_shape=jax.ShapeDtypeStruct((r, d), BF16),
        compiler_params=_params(("arbitrary",)),
    )(block_expert, n_used, x_rows, w1g, w1l, b1g, b1l, w2_all, b2)


def _combine_kernel(x_ref, y_ref, gate_ref, mod_ref, g_ref, o_ref):
    gates = gate_ref[0]
    y = gates[:, 0:1] * y_ref[0, 0].astype(F32)
    for k in range(1, TOP_K):
        y = y + gates[:, k:k + 1] * y_ref[k, 0].astype(F32)
    mod = mod_ref[0, 0]
    o_ref[0] = x_ref[0] + mod[5:6] * (_rms(y) * g_ref[...][3:4])


def _combine_call(xc, picked, gates, modl, g, n_lat):
    b, n, d = xc.shape
    tm = TOKEN_TILE
    tok = pl.BlockSpec((1, tm, d), lambda i, t: (i, t, 0))
    return pl.pallas_call(
        _combine_kernel,
        grid=(b, n // tm),
        in_specs=[
            tok,
            pl.BlockSpec((TOP_K, 1, tm, d), lambda i, t: (0, i, t, 0)),
            pl.BlockSpec((1, tm, TOP_K), lambda i, t: (i, t, 0)),
            _mod_spec(d, n_lat // tm),
            pl.BlockSpec(g.shape, lambda i, t: (0, 0)),
        ],
        out_specs=tok,
        out_shape=jax.ShapeDtypeStruct((b, n, d), F32),
        compiler_params=_params(("parallel", "parallel")),
    )(xc, picked, gates, modl, g)


def _rope_tables(rows, n_ctx, rot_dim):
    row = jnp.repeat(jnp.arange(rows), GRID_W)
    col = jnp.tile(jnp.arange(GRID_W), rows)
    n_freq = rot_dim // 4
    inv = ROPE_THETA ** (-jnp.arange(n_freq, dtype=F32) / n_freq)
    ang = jnp.concatenate([row[:, None] * inv, col[:, None] * inv], axis=-1)
    cos, sin = jnp.cos(ang), jnp.sin(ang)
    reps = 256 // rot_dim
    cos_t = jnp.tile(jnp.concatenate([cos, cos], axis=-1), (1, reps))
    sin_t = jnp.tile(jnp.concatenate([-sin, sin], axis=-1), (1, reps))
    cos_t = jnp.concatenate([cos_t, jnp.ones((n_ctx, 256), F32)], axis=0)
    sin_t = jnp.concatenate([sin_t, jnp.zeros((n_ctx, 256), F32)], axis=0)
    return cos_t, sin_t


def _pack_w_in(w):
    parts, src = [], 0
    for _, width, padded in IN_PIECES:
        parts.append(w[:, src:src + width])
        if padded > width:
            parts.append(jnp.zeros((w.shape[0], padded - width), w.dtype))
        src += width
    return jnp.concatenate(parts, axis=1).astype(BF16)


def _pack_w_uq(w):
    wh = w.reshape(MLA_Q_RANK, N_HEADS, MLA_NOPE + MLA_ROPE)
    nope = wh[:, :, :MLA_NOPE].reshape(MLA_Q_RANK, N_HEADS * MLA_NOPE)
    rope = wh[:, :, MLA_NOPE:].reshape(MLA_Q_RANK, N_HEADS * MLA_ROPE)
    packed = jnp.concatenate([nope, rope], axis=1)
    return jnp.pad(packed, ((0, 256 - MLA_Q_RANK), (0, 0))).astype(BF16)


def _pack_w_ukv(w):
    wh = w.reshape(MLA_KV_RANK, N_HEADS, MLA_NOPE + HEAD_DIM)
    kn = wh[:, :, :MLA_NOPE].reshape(MLA_KV_RANK, N_HEADS * MLA_NOPE)
    vv = wh[:, :, MLA_NOPE:].reshape(MLA_KV_RANK, N_HEADS * HEAD_DIM)
    return jnp.concatenate([kn, vv], axis=1).astype(BF16)


def _heads(a, n_heads):
    b, n, w = a.shape
    return a.reshape(b, n, n_heads, w // n_heads).transpose(0, 2, 1, 3)


def _heads_t(a, n_heads):
    b, n, w = a.shape
    return a.reshape(b, n, n_heads, w // n_heads).transpose(0, 2, 3, 1)


def _attention(q, kt, v, n_lat, n_ctx, ctx_out, diff=None):
    n = n_lat + n_ctx
    y = _attn_call(q, kt, v, (0, n_lat), (0, n), min(Q_TILE, n_lat), diff)
    if ctx_out:
        y_c = _attn_call(q, kt, v, (n_lat, n_ctx), (n_lat, n_ctx), n_ctx, diff)
    else:
        y_c = jnp.zeros((y.shape[0], n_ctx, y.shape[2]), y.dtype)
    return jnp.concatenate([y, y_c], axis=1)


def _heads_v(a, n_heads):
    h = _heads(a, n_heads)
    return jnp.concatenate([h, jnp.ones_like(h)], axis=-1)


def _moe(x_new, h2, idx, gates, rank, counts, modl, g, n_lat, w1_all, b1, w2_all, b2, layer):
    b, n, d = h2.shape
    t = b * n
    n_assign = t * TOP_K
    n_blocks = -(-n_assign // MOE_BLOCK) + N_EXPERTS
    experts = jnp.arange(N_EXPERTS, dtype=jnp.int32)
    tile_counts = counts.reshape(-1, N_EXPERTS)
    tile_base = jnp.cumsum(tile_counts, axis=0) - tile_counts
    totals = jnp.sum(tile_counts, axis=0)
    padded = (totals + MOE_BLOCK - 1) // MOE_BLOCK * MOE_BLOCK
    pad_ends = jnp.cumsum(padded)
    table = (tile_base + (pad_ends - padded)[None, :]).reshape(b, 1, n // TOKEN_TILE, 1, N_EXPERTS)
    idx5 = idx.reshape(b, TOP_K, n // TOKEN_TILE, TOKEN_TILE, 1)
    dest = rank + jnp.sum(jnp.where(idx5 == experts, table, 0), axis=-1).reshape(b, TOP_K, n)
    tok = jnp.broadcast_to((jnp.arange(b, dtype=jnp.int32)[:, None, None] * n
                            + jnp.arange(n, dtype=jnp.int32)[None, None, :]), (b, TOP_K, n))
    src_tok = jnp.zeros((n_blocks * MOE_BLOCK,), jnp.int32).at[dest.reshape(-1)].set(
        tok.reshape(-1), unique_indices=True)
    block_start = jnp.arange(n_blocks, dtype=jnp.int32) * MOE_BLOCK
    block_expert = jnp.minimum(jnp.sum((pad_ends[None, :] <= block_start[:, None]).astype(jnp.int32), axis=1),
                               N_EXPERTS - 1)
    n_used = (pad_ends[-1:] // MOE_BLOCK).astype(jnp.int32)
    x_rows = h2.reshape(t, d).at[src_tok].get(mode="promise_in_bounds")
    f = w2_all.shape[2]
    w1g, w1l = _w1_prep_call(w1_all, layer)
    y_rows = _expert_call(
        block_expert, n_used, x_rows, w1g, w1l,
        b1[:, 0::2].reshape(N_EXPERTS, 1, f), b1[:, 1::2].reshape(N_EXPERTS, 1, f),
        w2_all, layer, b2.reshape(N_EXPERTS, 1, d))
    picked = y_rows.at[dest.transpose(1, 0, 2).reshape(-1)].get(mode="promise_in_bounds")
    return _combine_call(x_new, picked.reshape(TOP_K, b, n, d), gates.transpose(0, 2, 1), modl, g, n_lat)


def kernel(x, c, ctx, c_ctx, ada_w, ada_b, norm_g, w_in, w_out, ret_log_decay, ret_gn_w, ret_gn_b,
           diff_lambda, diff_subln, gqa_qk_norm, mla_q_norm, mla_kv_norm, mla_w_uq, mla_w_ukv,
           router_w, router_b, exp_w1, exp_b1, exp_w2, exp_b2):
    b, s, d = x.shape
    n_ctx = ctx.shape[1]
    n = s + n_ctx
    depth = ada_w.shape[0]
    assert n_ctx % TOKEN_TILE == 0 and s % TOKEN_TILE == 0 and s % GRID_W == 0
    assert n_ctx % RET_CHUNK == 0 and s % min(Q_TILE, s) == 0 and s % n_ctx == 0

    tables = _rope_tables(s // GRID_W, n_ctx, HEAD_DIM) + _rope_tables(s // GRID_W, n_ctx, DIFF_D)
    c_rows = jnp.zeros((16, d), F32).at[:b].set(c).at[b].set(c_ctx)
    mods = _ada_call(c_rows, ada_w, ada_b)
    w2_all = exp_w2.astype(BF16)
    xc = jnp.concatenate([x, ctx], axis=1)

    for l in range(depth):
        last = l == depth - 1
        lam_init = 0.8 - 0.6 * math.exp(-0.3 * l)
        mod_lat = mods[l, :b].reshape(b, 1, 6, d)
        mod_ctx = jnp.broadcast_to(mods[l, b].reshape(1, 1, 6, d), (b, 1, 6, d))
        modl = jnp.concatenate([mod_ctx, mod_lat], axis=1)

        (rq, rk, rv, rg, dq, dk, dv, gq, gk, gv, mqn, mqr, mkn, mv, mkr) = _in_proj_call(
            xc, modl, norm_g[l, 0:1], _pack_w_in(w_in[l]), tables,
            jnp.tile(gqa_qk_norm[l, 0], N_HEADS)[None, :], jnp.tile(gqa_qk_norm[l, 1], GQA_KV_HEADS)[None, :],
            jnp.pad(mla_q_norm[l], (0, 256 - MLA_Q_RANK))[None, :], mla_kv_norm[l][None, :],
            _pack_w_uq(mla_w_uq[l]), _pack_w_ukv(mla_w_ukv[l]), s)

        log_g = -jnp.exp(ret_log_decay[l].astype(F32))
        o_f = _ret_call(rq, rk, rv, _ret_tables(log_g[0], False), s, False)
        ret_y = _ret_call(rq, rk, rv, _ret_tables(log_g[1], True), s, True,
                          (o_f, rg, ret_gn_w[l][None, :], ret_gn_b[l][None, :]))

        lp = diff_lambda[l].astype(F32)
        lam = jnp.exp(jnp.sum(lp[0] * lp[1])) - jnp.exp(jnp.sum(lp[2] * lp[3])) + lam_init
        dif_y = _attention(_heads(dq, 2 * N_HEADS), _heads_t(dk, 2 * N_HEADS), _heads_v(dv, N_HEADS), s, n_ctx,
                           not last, (lam, diff_subln[l], 1.0 - lam_init))

        gqa_y = _attention(_heads(gq, N_HEADS), _heads_t(gk, GQA_KV_HEADS), _heads_v(gv, GQA_KV_HEADS), s, n_ctx,
                           not last)

        q_m = jnp.concatenate([_heads(mqn, N_HEADS), _heads(mqr, N_HEADS)], axis=-1)
        kr_t = jnp.broadcast_to(mkr[:, None, :, :MLA_ROPE].transpose(0, 1, 3, 2), (b, N_HEADS, MLA_ROPE, n))
        kt_m = jnp.concatenate([_heads_t(mkn, N_HEADS), kr_t], axis=2)
        mla_y = _attention(q_m, kt_m, _heads_v(mv, N_HEADS), s, n_ctx, not last)

        ymix = jnp.concatenate([ret_y, dif_y, gqa_y, mla_y], axis=-1)
        n_rows = s if last else n
        x_new, h2, idx, gates, rank, counts = _out_proj_call(
            ymix, w_out[l].astype(BF16), xc, modl, norm_g[l], router_w[l].T, router_b[l][:, None], s, n_rows)
        xc = _moe(x_new, h2, idx, gates, rank, counts[..., 0], modl, norm_g[l], s,
                  exp_w1, exp_b1[l], w2_all, exp_b2[l], l)

    return xc[:, :s]
```

```python
import functools
import math

import jax
import jax.numpy as jnp
from jax import lax
from jax.experimental import pallas as pl
from jax.experimental.pallas import tpu as pltpu

F32 = jnp.float32
BF16 = jnp.bfloat16

GRID_W = 64
ROPE_THETA = 10000.0
EPS = 1e-6
GROUP_WIDTH = 256
HEAD_DIM = 64
N_HEADS = 4
RET_CHUNK = 256
DIFF_D = 32
GQA_KV_HEADS = 2
MLA_Q_RANK = 192
MLA_KV_RANK = 128
MLA_NOPE = 64
MLA_ROPE = 32
N_EXPERTS = 32
TOP_K = 4
SWIGLU_LIMIT = 7.0
SWIGLU_ALPHA = 1.702
MOE_BLOCK = 512

LANES = 128
MXU_DIM = 256
TOKEN_TILE = 256
Q_TILE = 1024
KV_TILE = 512
VMEM_LIMIT = 48 * 1024 * 1024
LOG2E = math.log2(math.e)

IN_PIECES = (
    ("ret_q", 256, 256), ("ret_k", 256, 256), ("ret_v", 256, 256), ("ret_g", 256, 256),
    ("dif_q", 256, 256), ("dif_k", 256, 256), ("dif_v", 256, 256),
    ("gqa_q", 256, 256), ("gqa_k", 128, 128), ("gqa_v", 128, 128),
    ("mla_cq", MLA_Q_RANK, 256), ("mla_ckv", MLA_KV_RANK, 128), ("mla_kr", MLA_ROPE, 128),
)
IN_OFFSETS = {}
_off = 0
for _name, _w, _pw in IN_PIECES:
    IN_OFFSETS[_name] = _off
    _off += _pw
IN_PACKED_WIDTH = _off


def _params(sem):
    return pltpu.CompilerParams(dimension_semantics=sem, vmem_limit_bytes=VMEM_LIMIT)


def _rms(x):
    return x * lax.rsqrt(jnp.mean(x * x, axis=-1, keepdims=True) + EPS)


def _split_bf16(a):
    hi = a.astype(BF16)
    lo = (a - hi.astype(F32)).astype(BF16)
    return hi, lo


def _dot_nt(a, b):
    return lax.dot_general(a, b, (((1,), (1,)), ((), ())), preferred_element_type=F32)


def _dot(a, b):
    return jnp.dot(a, b, preferred_element_type=F32)


def _sigmoid(a):
    return 1.0 / (1.0 + jnp.exp(-a))


def _pack_halves(a):
    w = a.shape[1] // 2
    bits = lax.bitcast_convert_type(a.astype(BF16).astype(F32), jnp.uint32)
    return bits[:, :w] | (bits[:, w:] >> 16)


def _unpack_halves(u):
    hi = lax.bitcast_convert_type(u & jnp.uint32(0xFFFF0000), F32)
    lo = lax.bitcast_convert_type(u << 16, F32)
    return jnp.concatenate([hi, lo], axis=1)


def _mod_spec(d, n_lat_tiles):
    return pl.BlockSpec((1, 1, 6, d), lambda i, t: (i, jnp.where(t < n_lat_tiles, 1, 0), 0, 0))


def _ada_kernel(c_ref, w_ref, b_ref, o_ref):
    s = c_ref[...]
    s = s * _sigmoid(s)
    s_hi, s_lo = _split_bf16(s)
    w_hi, w_lo = _split_bf16(w_ref[0])
    o_ref[0] = _dot(s_hi, w_hi) + _dot(s_hi, w_lo) + _dot(s_lo, w_hi) + b_ref[0]


def _ada_call(c_rows, ada_w, ada_b):
    depth, d, n6 = ada_w.shape
    rows = c_rows.shape[0]
    tn = 1536
    return pl.pallas_call(
        _ada_kernel,
        grid=(depth, n6 // tn),
        in_specs=[
            pl.BlockSpec((rows, d), lambda l, j: (0, 0)),
            pl.BlockSpec((1, d, tn), lambda l, j: (l, 0, j)),
            pl.BlockSpec((1, 1, tn), lambda l, j: (l, 0, j)),
        ],
        out_specs=pl.BlockSpec((1, rows, tn), lambda l, j: (l, 0, j)),
        out_shape=jax.ShapeDtypeStruct((depth, rows, n6), F32),
        compiler_params=_params(("parallel", "parallel")),
    )(c_rows, ada_w, ada_b.reshape(depth, 1, n6))


def _rope(x, cos, sin_signed, half):
    outs = []
    for c in range(x.shape[1] // LANES):
        sl = slice(c * LANES, (c + 1) * LANES)
        xc = x[:, sl]
        lane = lax.broadcasted_iota(jnp.int32, xc.shape, 1)
        first_half = (lane % (2 * half)) < half
        partner = jnp.where(first_half, pltpu.roll(xc, LANES - half, 1), pltpu.roll(xc, half, 1))
        outs.append(xc * cos[:, sl] + partner * sin_signed[:, sl])
    return outs[0] if len(outs) == 1 else jnp.concatenate(outs, axis=1)


def _group_mean_sq(x, gsize):
    w = x.shape[1]
    r = lax.broadcasted_iota(jnp.int32, (w, w), 0) // gsize
    c = lax.broadcasted_iota(jnp.int32, (w, w), 1) // gsize
    ones = jnp.where(r == c, 1.0, 0.0).astype(BF16)
    hi, lo = _split_bf16(x * x)
    return (_dot(hi, ones) + _dot(lo, ones)) * (1.0 / gsize)


def _in_proj_kernel(x_ref, mod_ref, g_ref, w_ref, cos64_ref, sin64_ref, cos32_ref, sin32_ref,
                    gqn_ref, gkn_ref, mqn_ref, mkvn_ref, wuq_ref, wukv_ref,
                    rq_ref, rk_ref, rv_ref, rg_ref, dq_ref, dk_ref, dv_ref, gq_ref, gk_ref, gv_ref,
                    mqn_o, mqr_o, mkn_o, mv_o, mkr_o):
    mod = mod_ref[0, 0]
    h = _rms(x_ref[0]) * g_ref[...] * (1.0 + mod[1:2]) + mod[0:1]
    hb = h.astype(BF16)
    cos64, sin64 = cos64_ref[...], sin64_ref[...]
    cos32, sin32 = cos32_ref[...], sin32_ref[...]

    def proj(name, width):
        o = IN_OFFSETS[name]
        return _dot(hb, w_ref[:, o:o + width])

    rq_ref[0] = _rope(proj("ret_q", 256), cos64, sin64, 32).astype(BF16)
    rk_ref[0] = (_rope(proj("ret_k", 256), cos64, sin64, 32) * (HEAD_DIM ** -0.5)).astype(BF16)
    rv_ref[0] = proj("ret_v", 256).astype(BF16)
    rg_ref[0] = proj("ret_g", 256)
    dq_ref[0] = (_rope(proj("dif_q", 256), cos32, sin32, 16) * (DIFF_D ** -0.5 * LOG2E)).astype(BF16)
    dk_ref[0] = _rope(proj("dif_k", 256), cos32, sin32, 16).astype(BF16)
    dv_ref[0] = proj("dif_v", 256).astype(BF16)
    gq = proj("gqa_q", 256)
    gq = gq * lax.rsqrt(_group_mean_sq(gq, HEAD_DIM) + EPS) * gqn_ref[...]
    gq_ref[0] = (_rope(gq, cos64, sin64, 32) * (HEAD_DIM ** -0.5 * LOG2E)).astype(BF16)
    gk = proj("gqa_k", 128)
    gk = gk * lax.rsqrt(_group_mean_sq(gk, HEAD_DIM) + EPS) * gkn_ref[...]
    gk_ref[0] = _rope(gk, cos64[:, :LANES], sin64[:, :LANES], 32).astype(BF16)
    gv_ref[0] = proj("gqa_v", 128).astype(BF16)
    cq = proj("mla_cq", 256)
    cq = cq * lax.rsqrt(jnp.sum(cq * cq, axis=-1, keepdims=True) * (1.0 / MLA_Q_RANK) + EPS) * mqn_ref[...]
    q_up = _dot(cq.astype(BF16), wuq_ref[...])
    mla_scale = (MLA_NOPE + MLA_ROPE) ** -0.5 * LOG2E
    mqn_o[0] = (q_up[:, :256] * mla_scale).astype(BF16)
    mqr_o[0] = (_rope(q_up[:, 256:], cos32[:, :LANES], sin32[:, :LANES], 16) * mla_scale).astype(BF16)
    ckv = proj("mla_ckv", 128)
    ckv = ckv * lax.rsqrt(jnp.mean(ckv * ckv, axis=-1, keepdims=True) + EPS) * mkvn_ref[...]
    kv_up = _dot(ckv.astype(BF16), wukv_ref[...])
    mkn_o[0] = kv_up[:, :256].astype(BF16)
    mv_o[0] = kv_up[:, 256:].astype(BF16)
    mkr_o[0] = _rope(proj("mla_kr", 128), cos32[:, :LANES], sin32[:, :LANES], 16).astype(BF16)


def _in_proj_call(xc, modl, g0, w_in_p, tables, gqn, gkn, mqn, mkvn, wuq_p, wukv_p, n_lat):
    b, n, d = xc.shape
    tm = TOKEN_TILE
    tok = lambda w: pl.BlockSpec((1, tm, w), lambda i, t: (i, t, 0))
    const2 = lambda a: pl.BlockSpec(a.shape, lambda i, t: (0, 0))
    tab = pl.BlockSpec((tm, 256), lambda i, t: (t, 0))
    out_widths = (256, 256, 256, 256, 256, 256, 256, 256, 128, 128, 256, 128, 256, 256, 128)
    out_dtypes = (BF16, BF16, BF16, F32, BF16, BF16, BF16, BF16, BF16, BF16, BF16, BF16, BF16, BF16, BF16)
    return pl.pallas_call(
        _in_proj_kernel,
        grid=(b, n // tm),
        in_specs=[
            tok(d), _mod_spec(d, n_lat // tm),
            const2(g0), const2(w_in_p), tab, tab, tab, tab,
            const2(gqn), const2(gkn), const2(mqn), const2(mkvn), const2(wuq_p), const2(wukv_p),
        ],
        out_specs=[tok(w) for w in out_widths],
        out_shape=[jax.ShapeDtypeStruct((b, n, w), dt) for w, dt in zip(out_widths, out_dtypes)],
        compiler_params=_params(("parallel", "parallel")),
    )(xc, modl, g0, w_in_p, *tables, gqn, gkn, mqn, mkvn, wuq_p, wukv_p)


def _ret_kernel(*refs, readout):
    if readout:
        (q_ref, k_ref, v_ref, dec_ref, xi_ref, zeta_ref, gc_ref, of_ref, g_ref, gnw_ref, gnb_ref,
         o_ref, state_ref) = refs
    else:
        q_ref, k_ref, v_ref, dec_ref, xi_ref, zeta_ref, gc_ref, o_ref, state_ref = refs

    @pl.when(pl.program_id(1) == 0)
    def _():
        state_ref[...] = jnp.zeros_like(state_ref)

    q, k, v = q_ref[0], k_ref[0], v_ref[0]
    outs = []
    for h in range(N_HEADS):
        sl = slice(h * HEAD_DIM, (h + 1) * HEAD_DIM)
        qh, kh, vh = q[:, sl], k[:, sl], v[:, sl]
        st = state_ref[h]
        inner = _dot_nt(qh, kh) * dec_ref[h]
        o = _dot(inner.astype(BF16), vh) + _dot(qh, st.astype(BF16)) * xi_ref[h]
        kz = (kh.astype(F32) * zeta_ref[h]).astype(BF16)
        kv = lax.dot_general(kz, vh, (((0,), (0,)), ((), ())), preferred_element_type=F32)
        state_ref[h] = gc_ref[h] * st + kv
        if readout:
            o = o + of_ref[0][:, sl]
            mu = jnp.mean(o, axis=-1, keepdims=True)
            var = jnp.mean(jnp.square(o - mu), axis=-1, keepdims=True)
            o = (o - mu) * lax.rsqrt(var + EPS)
        outs.append(o)
    o = jnp.concatenate(outs, axis=1)
    if readout:
        g = g_ref[0]
        o_ref[0] = ((o * gnw_ref[...] + gnb_ref[...]) * (g * _sigmoid(g))).astype(o_ref.dtype)
    else:
        o_ref[0] = o


def _ret_call(q, k, v, tabs, n_lat, backward, readout_args=None):
    b, n, w = q.shape
    c = RET_CHUNK
    nch = n // c
    nlc = n_lat // c
    if backward:
        chunk = lambda t: nch - 1 - t
    else:
        chunk = lambda t: jnp.where(t < nch - nlc, nlc + t, t - (nch - nlc))
    tok = lambda dt_w: pl.BlockSpec((1, c, dt_w), lambda i, t: (i, chunk(t), 0))
    const = lambda a: pl.BlockSpec(a.shape, lambda i, t: (0,) * a.ndim)
    in_specs = [tok(w), tok(w), tok(w)] + [const(a) for a in tabs]
    args = [q, k, v, *tabs]
    if readout_args is not None:
        o_f, g, gnw, gnb = readout_args
        in_specs += [tok(w), tok(w), const(gnw), const(gnb)]
        args += [o_f, g, gnw, gnb]
    out_dtype = BF16 if readout_args is not None else F32
    return pl.pallas_call(
        functools.partial(_ret_kernel, readout=readout_args is not None),
        grid=(b, nch),
        in_specs=in_specs,
        out_specs=tok(w),
        out_shape=jax.ShapeDtypeStruct((b, n, w), out_dtype),
        scratch_shapes=[pltpu.VMEM((N_HEADS, HEAD_DIM, HEAD_DIM), F32)],
        compiler_params=_params(("parallel", "arbitrary")),
    )(*args)


def _ret_tables(log_g, backward):
    c = RET_CHUNK
    pos = jnp.arange(c, dtype=F32)
    dist = (pos[None, :] - pos[:, None]) if backward else (pos[:, None] - pos[None, :])
    lg = log_g[:, None, None]
    decay = jnp.where(dist >= 0, jnp.exp(lg * jnp.maximum(dist, 0.0)), 0.0)
    to_state = (c - pos) if backward else (pos + 1.0)
    to_end = pos if backward else (c - 1.0 - pos)
    xi = jnp.exp(log_g[:, None] * to_state)[:, :, None]
    zeta = jnp.exp(log_g[:, None] * to_end)[:, :, None]
    ones = jnp.ones((1, 1, HEAD_DIM), F32)
    gc = jnp.exp(log_g * c)[:, None, None] * jnp.ones((1, HEAD_DIM, HEAD_DIM), F32)
    return decay, xi * ones, zeta * ones, gc


def _attn_kernel(*refs, n_maps, readout_scale):
    if n_maps == 2:
        lam_ref, q_ref, kt_ref, v_ref, sub_ref, o_ref, first_ref = refs
    else:
        q_ref, kt_ref, v_ref, o_ref = refs
    hm = pl.program_id(2)
    q = q_ref[0, 0]
    tq = q.shape[0]
    n_keys = kt_ref.shape[3]
    dv = v_ref.shape[3] // 2
    tk = min(KV_TILE, n_keys)
    n_full, tail = divmod(n_keys, tk)

    def step(off, size, carry):
        m, acc = carry
        s = _dot(q, kt_ref[0, 0, :, pl.ds(off, size)])
        m_new = jnp.maximum(m, jnp.max(s, axis=-1, keepdims=True))
        p = jnp.exp2(s - m_new)
        acc = jnp.exp2(m - m_new) * acc + _dot(p.astype(BF16), v_ref[0, 0, pl.ds(off, size), :])
        return m_new, acc

    carry = (jnp.full((tq, 1), -1e30, F32), jnp.zeros((tq, 2 * dv), F32))
    for j in range(n_full):
        carry = step(j * tk, tk, carry)
    if tail:
        carry = step(n_full * tk, tail, carry)
    acc = carry[1]
    o = acc[:, :dv] / acc[:, dv:]

    def store(head_of_step, val):
        for hh in range(N_HEADS):
            @pl.when(head_of_step == hh)
            def _(hh=hh):
                o_ref[0, :, hh * dv:(hh + 1) * dv] = val.astype(o_ref.dtype)

    if n_maps == 1:
        store(hm, o)
    else:
        @pl.when(hm % 2 == 0)
        def _():
            first_ref[...] = o

        @pl.when(hm % 2 == 1)
        def _():
            od = first_ref[...] - lam_ref[0] * o
            od = od * lax.rsqrt(jnp.mean(od * od, axis=-1, keepdims=True) + EPS) * sub_ref[...]
            store(hm // 2, od * readout_scale)


def _attn_call(q, kt, v, rows, keys, tq, diff=None):
    b, hq, n, d = q.shape
    hk, hv, dv = kt.shape[1], v.shape[1], v.shape[3] // 2
    (q0, qn), (k0, kn) = rows, keys
    assert q0 % tq == 0 and qn % tq == 0 and k0 % kn == 0
    n_maps = 2 if diff is not None else 1
    in_specs = [
        pl.BlockSpec((1, 1, tq, d), lambda i, t, h: (i, h, q0 // tq + t, 0)),
        pl.BlockSpec((1, 1, d, kn), lambda i, t, h: (i, h // (hq // hk), 0, k0 // kn)),
        pl.BlockSpec((1, 1, kn, 2 * dv), lambda i, t, h: (i, h // (hq // hv), k0 // kn, 0)),
    ]
    args = [q, kt, v]
    scratch = []
    scale = 1.0
    if diff is not None:
        lam, subln, scale = diff
        in_specs = [pl.BlockSpec(memory_space=pltpu.SMEM)] + in_specs + [pl.BlockSpec((1, dv), lambda i, t, h: (0, 0))]
        args = [lam.reshape(1)] + args + [subln.reshape(1, dv)]
        scratch = [pltpu.VMEM((tq, dv), F32)]
    return pl.pallas_call(
        functools.partial(_attn_kernel, n_maps=n_maps, readout_scale=scale),
        grid=(b, qn // tq, hq),
        in_specs=in_specs,
        out_specs=pl.BlockSpec((1, tq, (hq // n_maps) * dv), lambda i, t, h: (i, t, 0)),
        out_shape=jax.ShapeDtypeStruct((b, qn, (hq // n_maps) * dv), BF16),
        scratch_shapes=scratch,
        compiler_params=_params(("parallel", "parallel", "arbitrary")),
    )(*args)


def _out_proj_kernel(y_ref, w_ref, x_ref, mod_ref, g_ref, wr_ref, br_ref,
                     xo_ref, h_ref, idx_ref, gate_ref, rank_ref, cnt_ref):
    mod = mod_ref[0, 0]
    g = g_ref[...]
    y = _dot(y_ref[0], w_ref[...])
    xn = x_ref[0] + mod[2:3] * (_rms(y) * g[1:2])
    xo_ref[0] = xn
    h = _rms(xn) * g[2:3] * (1.0 + mod[4:5]) + mod[3:4]
    h_ref[0] = _pack_halves(h)
    w_hi, w_lo = _split_bf16(wr_ref[...])
    h_hi, h_lo = _split_bf16(h)
    logits = _dot_nt(w_hi, h_hi) + _dot_nt(w_hi, h_lo) + _dot_nt(w_lo, h_hi) + br_ref[...]
    row = lax.broadcasted_iota(jnp.int32, logits.shape, 0)
    vals = logits
    tops, idxs = [], []
    for _ in range(TOP_K):
        m = jnp.max(vals, axis=0, keepdims=True)
        idx = jnp.min(jnp.where(vals == m, row, N_EXPERTS), axis=0, keepdims=True)
        tops.append(m)
        idxs.append(idx)
        vals = jnp.where(row == idx, -jnp.inf, vals)
    ex = [jnp.exp(tv - tops[0]) for tv in tops]
    den = ex[0] + ex[1] + ex[2] + ex[3]
    gate_ref[0] = jnp.concatenate([e / den for e in ex], axis=0)
    idx_ref[0] = jnp.concatenate(idxs, axis=0)
    tm = logits.shape[1]
    before = jnp.where(lax.broadcasted_iota(jnp.int32, (tm, tm), 0) < lax.broadcasted_iota(jnp.int32, (tm, tm), 1),
                       1.0, 0.0).astype(BF16)
    base = jnp.zeros((N_EXPERTS, 1), F32)
    ranks = []
    for idx in idxs:
        onehot = jnp.where(row == idx, 1.0, 0.0)
        prefix = _dot(onehot.astype(BF16), before)
        ranks.append(jnp.sum(onehot * (prefix + base), axis=0, keepdims=True))
        base = base + jnp.sum(onehot, axis=1, keepdims=True)
    rank_ref[0] = jnp.concatenate(ranks, axis=0).astype(jnp.int32)
    cnt_ref[0, 0] = jnp.broadcast_to(base, (N_EXPERTS, LANES)).astype(jnp.int32)


def _out_proj_call(ymix, w_out, xc, modl, g, wr_t, br, n_lat, n_rows):
    b, n, d = xc.shape
    tm = TOKEN_TILE
    tok = lambda w: pl.BlockSpec((1, tm, w), lambda i, t: (i, t, 0))
    const2 = lambda a: pl.BlockSpec(a.shape, lambda i, t: (0, 0))
    sel = pl.BlockSpec((1, TOP_K, tm), lambda i, t: (i, 0, t))
    return pl.pallas_call(
        _out_proj_kernel,
        grid=(b, n_rows // tm),
        in_specs=[
            tok(ymix.shape[2]), const2(w_out), tok(d), _mod_spec(d, n_lat // tm),
            const2(g), const2(wr_t), const2(br),
        ],
        out_specs=[tok(d), tok(d // 2), sel, sel, sel,
                   pl.BlockSpec((1, 1, N_EXPERTS, LANES), lambda i, t: (i, t, 0, 0))],
        out_shape=[
            jax.ShapeDtypeStruct((b, n_rows, d), F32),
            jax.ShapeDtypeStruct((b, n_rows, d // 2), jnp.uint32),
            jax.ShapeDtypeStruct((b, TOP_K, n_rows), jnp.int32),
            jax.ShapeDtypeStruct((b, TOP_K, n_rows), F32),
            jax.ShapeDtypeStruct((b, TOP_K, n_rows), jnp.int32),
            jax.ShapeDtypeStruct((b, n_rows // tm, N_EXPERTS, LANES), jnp.int32),
        ],
        compiler_params=_params(("parallel", "parallel")),
    )(ymix, w_out, xc, modl, g, wr_t, br)


def _w1_prep_kernel(w_ref, g_ref, l_ref):
    r = lax.broadcasted_iota(jnp.int32, (MXU_DIM, MXU_DIM), 0)
    c = lax.broadcasted_iota(jnp.int32, (MXU_DIM, MXU_DIM), 1)
    src = jnp.where(c < LANES, 2 * c, 2 * (c - LANES) + 1)
    perm = jnp.where(r == src, 1.0, 0.0).astype(BF16)
    for j in range(w_ref.shape[3] // MXU_DIM):
        out = _dot(w_ref[0, 0, :, j * MXU_DIM:(j + 1) * MXU_DIM].astype(BF16), perm)
        g_ref[0, :, j * LANES:(j + 1) * LANES] = out[:, :LANES].astype(BF16)
        l_ref[0, :, j * LANES:(j + 1) * LANES] = out[:, LANES:].astype(BF16)


def _w1_prep_call(w1_all, layer):
    _, e, d, f2 = w1_all.shape
    f = f2 // 2
    rows = 256
    out_spec = pl.BlockSpec((1, rows, f), lambda i, j: (i, j, 0))
    return pl.pallas_call(
        _w1_prep_kernel,
        grid=(e, d // rows),
        in_specs=[pl.BlockSpec((1, 1, rows, f2), lambda i, j: (layer, i, j, 0))],
        out_specs=[out_spec, out_spec],
        out_shape=[jax.ShapeDtypeStruct((e, d, f), BF16)] * 2,
        compiler_params=_params(("parallel", "parallel")),
    )(w1_all)


def _expert_kernel(be_ref, nu_ref, x_ref, w1g_ref, w1l_ref, b1g_ref, b1l_ref, w2_ref, b2_ref, o_ref):
    del be_ref
    i = pl.program_id(0)

    @pl.when(i < nu_ref[0])
    def _():
        x = _unpack_halves(x_ref[...]).astype(BF16)
        glu = jnp.minimum(_dot(x, w1g_ref[0]) + b1g_ref[0], SWIGLU_LIMIT)
        lin = jnp.clip(_dot(x, w1l_ref[0]) + b1l_ref[0], -SWIGLU_LIMIT, SWIGLU_LIMIT)
        act = glu * _sigmoid(SWIGLU_ALPHA * glu) * (lin + 1.0)
        o_ref[...] = _pack_halves(_dot(act.astype(BF16), w2_ref[0, 0]) + b2_ref[0])

    @pl.when(i >= nu_ref[0])
    def _():
        o_ref[...] = jnp.zeros_like(o_ref)


def _expert_call(block_expert, n_used, x_rows, w1g, w1l, b1g, b1l, w2_all, layer, b2):
    r = x_rows.shape[0]
    d, f = w1g.shape[1:]
    tm = MOE_BLOCK
    wspec = lambda s: pl.BlockSpec((1,) + s, lambda i, be, nu: (be[i], 0, 0))
    return pl.pallas_call(
        _expert_kernel,
        grid_spec=pltpu.PrefetchScalarGridSpec(
            num_scalar_prefetch=2,
            grid=(r // tm,),
            in_specs=[
                pl.BlockSpec((tm, d // 2), lambda i, be, nu: (i, 0)),
                wspec((d, f)), wspec((d, f)), wspec((1, f)), wspec((1, f)),
                pl.BlockSpec((1, 1, f, d), lambda i, be, nu: (layer, be[i], 0, 0)),
                wspec((1, d)),
            ],
            out_specs=pl.BlockSpec((tm, d // 2), lambda i, be, nu: (i, 0)),
        ),
        out_shape=jax.ShapeDtypeStruct((r, d // 2), jnp.uint32),
        compiler_params=_params(("arbitrary",)),
    )(block_expert, n_used, x_rows, w1g, w1l, b1g, b1l, w2_all, b2)


def _combine_kernel(x_ref, y_ref, gate_ref, mod_ref, g_ref, o_ref):
    gates = gate_ref[0]
    y = gates[:, 0:1] * _unpack_halves(y_ref[0, 0])
    for k in range(1, TOP_K):
        y = y + gates[:, k:k + 1] * _unpack_halves(y_ref[k, 0])
    mod = mod_ref[0, 0]
    o_ref[0] = x_ref[0] + mod[5:6] * (_rms(y) * g_ref[...][3:4])


def _combine_call(xc, picked, gates, modl, g, n_lat):
    b, n, d = xc.shape
    tm = TOKEN_TILE
    tok = pl.BlockSpec((1, tm, d), lambda i, t: (i, t, 0))
    return pl.pallas_call(
        _combine_kernel,
        grid=(b, n // tm),
        in_specs=[
            tok,
            pl.BlockSpec((TOP_K, 1, tm, d // 2), lambda i, t: (0, i, t, 0)),
            pl.BlockSpec((1, tm, TOP_K), lambda i, t: (i, t, 0)),
            _mod_spec(d, n_lat // tm),
            pl.BlockSpec(g.shape, lambda i, t: (0, 0)),
        ],
        out_specs=tok,
        out_shape=jax.ShapeDtypeStruct((b, n, d), F32),
        compiler_params=_params(("parallel", "parallel")),
    )(xc, picked, gates, modl, g)


def _rope_tables(rows, n_ctx, rot_dim):
    row = jnp.repeat(jnp.arange(rows), GRID_W)
    col = jnp.tile(jnp.arange(GRID_W), rows)
    n_freq = rot_dim // 4
    inv = ROPE_THETA ** (-jnp.arange(n_freq, dtype=F32) / n_freq)
    ang = jnp.concatenate([row[:, None] * inv, col[:, None] * inv], axis=-1)
    cos, sin = jnp.cos(ang), jnp.sin(ang)
    reps = 256 // rot_dim
    cos_t = jnp.tile(jnp.concatenate([cos, cos], axis=-1), (1, reps))
    sin_t = jnp.tile(jnp.concatenate([-sin, sin], axis=-1), (1, reps))
    cos_t = jnp.concatenate([cos_t, jnp.ones((n_ctx, 256), F32)], axis=0)
    sin_t = jnp.concatenate([sin_t, jnp.zeros((n_ctx, 256), F32)], axis=0)
    return cos_t, sin_t


def _pack_w_in(w):
    parts, src = [], 0
    for _, width, padded in IN_PIECES:
        parts.append(w[:, src:src + width])
        if padded > width:
            parts.append(jnp.zeros((w.shape[0], padded - width), w.dtype))
        src += width
    return jnp.concatenate(parts, axis=1).astype(BF16)


def _pack_w_uq(w):
    wh = w.reshape(MLA_Q_RANK, N_HEADS, MLA_NOPE + MLA_ROPE)
    nope = wh[:, :, :MLA_NOPE].reshape(MLA_Q_RANK, N_HEADS * MLA_NOPE)
    rope = wh[:, :, MLA_NOPE:].reshape(MLA_Q_RANK, N_HEADS * MLA_ROPE)
    packed = jnp.concatenate([nope, rope], axis=1)
    return jnp.pad(packed, ((0, 256 - MLA_Q_RANK), (0, 0))).astype(BF16)


def _pack_w_ukv(w):
    wh = w.reshape(MLA_KV_RANK, N_HEADS, MLA_NOPE + HEAD_DIM)
    kn = wh[:, :, :MLA_NOPE].reshape(MLA_KV_RANK, N_HEADS * MLA_NOPE)
    vv = wh[:, :, MLA_NOPE:].reshape(MLA_KV_RANK, N_HEADS * HEAD_DIM)
    return jnp.concatenate([kn, vv], axis=1).astype(BF16)


def _heads(a, n_heads):
    b, n, w = a.shape
    return a.reshape(b, n, n_heads, w // n_heads).transpose(0, 2, 1, 3)


def _heads_t(a, n_heads):
    b, n, w = a.shape
    return a.reshape(b, n, n_heads, w // n_heads).transpose(0, 2, 3, 1)


def _attention(q, kt, v, n_lat, n_ctx, ctx_out, diff=None):
    n = n_lat + n_ctx
    y = _attn_call(q, kt, v, (0, n_lat), (0, n), min(Q_TILE, n_lat), diff)
    if ctx_out:
        y_c = _attn_call(q, kt, v, (n_lat, n_ctx), (n_lat, n_ctx), n_ctx, diff)
    else:
        y_c = jnp.zeros((y.shape[0], n_ctx, y.shape[2]), y.dtype)
    return jnp.concatenate([y, y_c], axis=1)


def _heads_v(a, n_heads):
    h = _heads(a, n_heads)
    return jnp.concatenate([h, jnp.ones_like(h)], axis=-1)


def _moe(x_new, h2, idx, gates, rank, counts, modl, g, n_lat, w1_all, b1, w2_all, b2, layer):
    b, n, d = x_new.shape
    t = b * n
    n_assign = t * TOP_K
    n_blocks = -(-n_assign // MOE_BLOCK) + N_EXPERTS
    experts = jnp.arange(N_EXPERTS, dtype=jnp.int32)
    tile_counts = counts.reshape(-1, N_EXPERTS)
    tile_base = jnp.cumsum(tile_counts, axis=0) - tile_counts
    totals = jnp.sum(tile_counts, axis=0)
    padded = (totals + MOE_BLOCK - 1) // MOE_BLOCK * MOE_BLOCK
    pad_ends = jnp.cumsum(padded)
    table = (tile_base + (pad_ends - padded)[None, :]).reshape(b, 1, n // TOKEN_TILE, 1, N_EXPERTS)
    idx5 = idx.reshape(b, TOP_K, n // TOKEN_TILE, TOKEN_TILE, 1)
    dest = rank + jnp.sum(jnp.where(idx5 == experts, table, 0), axis=-1).reshape(b, TOP_K, n)
    tok = jnp.broadcast_to((jnp.arange(b, dtype=jnp.int32)[:, None, None] * n
                            + jnp.arange(n, dtype=jnp.int32)[None, None, :]), (b, TOP_K, n))
    src_tok = jnp.zeros((n_blocks * MOE_BLOCK,), jnp.int32).at[dest.reshape(-1)].set(
        tok.reshape(-1), unique_indices=True)
    block_start = jnp.arange(n_blocks, dtype=jnp.int32) * MOE_BLOCK
    block_expert = jnp.minimum(jnp.sum((pad_ends[None, :] <= block_start[:, None]).astype(jnp.int32), axis=1),
                               N_EXPERTS - 1)
    n_used = (pad_ends[-1:] // MOE_BLOCK).astype(jnp.int32)
    x_rows = h2.reshape(t, d // 2).at[src_tok].get(mode="promise_in_bounds")
    f = w2_all.shape[2]
    w1g, w1l = _w1_prep_call(w1_all, layer)
    y_rows = _expert_call(
        block_expert, n_used, x_rows, w1g, w1l,
        b1[:, 0::2].reshape(N_EXPERTS, 1, f), b1[:, 1::2].reshape(N_EXPERTS, 1, f),
        w2_all, layer, b2.reshape(N_EXPERTS, 1, d))
    picked = y_rows.at[dest.transpose(1, 0, 2).reshape(-1)].get(mode="promise_in_bounds")
    return _combine_call(x_new, picked.reshape(TOP_K, b, n, d // 2), gates.transpose(0, 2, 1), modl, g, n_lat)


def kernel(x, c, ctx, c_ctx, ada_w, ada_b, norm_g, w_in, w_out, ret_log_decay, ret_gn_w, ret_gn_b,
           diff_lambda, diff_subln, gqa_qk_norm, mla_q_norm, mla_kv_norm, mla_w_uq, mla_w_ukv,
           router_w, router_b, exp_w1, exp_b1, exp_w2, exp_b2):
    b, s, d = x.shape
    n_ctx = ctx.shape[1]
    n = s + n_ctx
    depth = ada_w.shape[0]
    assert n_ctx % TOKEN_TILE == 0 and s % TOKEN_TILE == 0 and s % GRID_W == 0
    assert n_ctx % RET_CHUNK == 0 and s % min(Q_TILE, s) == 0 and s % n_ctx == 0

    tables = _rope_tables(s // GRID_W, n_ctx, HEAD_DIM) + _rope_tables(s // GRID_W, n_ctx, DIFF_D)
    c_rows = jnp.zeros((16, d), F32).at[:b].set(c).at[b].set(c_ctx)
    mods = _ada_call(c_rows, ada_w, ada_b)
    w2_all = exp_w2.astype(BF16)
    xc = jnp.concatenate([x, ctx], axis=1)

    for l in range(depth):
        last = l == depth - 1
        lam_init = 0.8 - 0.6 * math.exp(-0.3 * l)
        mod_lat = mods[l, :b].reshape(b, 1, 6, d)
        mod_ctx = jnp.broadcast_to(mods[l, b].reshape(1, 1, 6, d), (b, 1, 6, d))
        modl = jnp.concatenate([mod_ctx, mod_lat], axis=1)

        (rq, rk, rv, rg, dq, dk, dv, gq, gk, gv, mqn, mqr, mkn, mv, mkr) = _in_proj_call(
            xc, modl, norm_g[l, 0:1], _pack_w_in(w_in[l]), tables,
            jnp.tile(gqa_qk_norm[l, 0], N_HEADS)[None, :], jnp.tile(gqa_qk_norm[l, 1], GQA_KV_HEADS)[None, :],
            jnp.pad(mla_q_norm[l], (0, 256 - MLA_Q_RANK))[None, :], mla_kv_norm[l][None, :],
            _pack_w_uq(mla_w_uq[l]), _pack_w_ukv(mla_w_ukv[l]), s)

        log_g = -jnp.exp(ret_log_decay[l].astype(F32))
        o_f = _ret_call(rq, rk, rv, _ret_tables(log_g[0], False), s, False)
        ret_y = _ret_call(rq, rk, rv, _ret_tables(log_g[1], True), s, True,
                          (o_f, rg, ret_gn_w[l][None, :], ret_gn_b[l][None, :]))

        lp = diff_lambda[l].astype(F32)
        lam = jnp.exp(jnp.sum(lp[0] * lp[1])) - jnp.exp(jnp.sum(lp[2] * lp[3])) + lam_init
        dif_y = _attention(_heads(dq, 2 * N_HEADS), _heads_t(dk, 2 * N_HEADS), _heads_v(dv, N_HEADS), s, n_ctx,
                           not last, (lam, diff_subln[l], 1.0 - lam_init))

        gqa_y = _attention(_heads(gq, N_HEADS), _heads_t(gk, GQA_KV_HEADS), _heads_v(gv, GQA_KV_HEADS), s, n_ctx,
                           not last)

        q_m = jnp.concatenate([_heads(mqn, N_HEADS), _heads(mqr, N_HEADS)], axis=-1)
        kr_t = jnp.broadcast_to(mkr[:, None, :, :MLA_ROPE].transpose(0, 1, 3, 2), (b, N_HEADS, MLA_ROPE, n))
        kt_m = jnp.concatenate([_heads_t(mkn, N_HEADS), kr_t], axis=2)
        mla_y = _attention(q_m, kt_m, _heads_v(mv, N_HEADS), s, n_ctx, not last)

        ymix = jnp.concatenate([ret_y, dif_y, gqa_y, mla_y], axis=-1)
        n_rows = s if last else n
        x_new, h2, idx, gates, rank, counts = _out_proj_call(
            ymix, w_out[l].astype(BF16), xc, modl, norm_g[l], router_w[l].T, router_b[l][:, None], s, n_rows)
        xc = _moe(x_new, h2, idx, gates, rank, counts[..., 0], modl, norm_g[l], s,
                  exp_w1, exp_b1[l], w2_all, exp_b2[l], l)

    return xc[:, :s]
```

```python
import functools
import math

import jax
import jax.numpy as jnp
from jax import lax
from jax.experimental import pallas as pl
from jax.experimental.pallas import tpu as pltpu

F32 = jnp.float32
BF16 = jnp.bfloat16

GRID_W = 64
ROPE_THETA = 10000.0
EPS = 1e-6
GROUP_WIDTH = 256
HEAD_DIM = 64
N_HEADS = 4
RET_CHUNK = 256
DIFF_D = 32
GQA_KV_HEADS = 2
MLA_Q_RANK = 192
MLA_KV_RANK = 128
MLA_NOPE = 64
MLA_ROPE = 32
N_EXPERTS = 32
TOP_K = 4
SWIGLU_LIMIT = 7.0
SWIGLU_ALPHA = 1.702
MOE_BLOCK = 512

LANES = 128
MXU_DIM = 256
TOKEN_TILE = 256
Q_TILE = 1024
KV_TILE = 512
VMEM_LIMIT = 48 * 1024 * 1024
LOG2E = math.log2(math.e)

IN_PIECES = (
    ("ret_q", 256, 256), ("ret_k", 256, 256), ("ret_v", 256, 256), ("ret_g", 256, 256),
    ("dif_q", 256, 256), ("dif_k", 256, 256), ("dif_v", 256, 256),
    ("gqa_q", 256, 256), ("gqa_k", 128, 128), ("gqa_v", 128, 128),
    ("mla_cq", MLA_Q_RANK, 256), ("mla_ckv", MLA_KV_RANK, 128), ("mla_kr", MLA_ROPE, 128),
)
IN_OFFSETS = {}
_off = 0
for _name, _w, _pw in IN_PIECES:
    IN_OFFSETS[_name] = _off
    _off += _pw
IN_PACKED_WIDTH = _off


def _params(sem):
    return pltpu.CompilerParams(dimension_semantics=sem, vmem_limit_bytes=VMEM_LIMIT)


def _rms(x):
    return x * lax.rsqrt(jnp.mean(x * x, axis=-1, keepdims=True) + EPS)


def _split_bf16(a):
    hi = a.astype(BF16)
    lo = (a - hi.astype(F32)).astype(BF16)
    return hi, lo


def _dot_nt(a, b):
    return lax.dot_general(a, b, (((1,), (1,)), ((), ())), preferred_element_type=F32)


def _dot(a, b):
    return jnp.dot(a, b, preferred_element_type=F32)


def _sigmoid(a):
    return 1.0 / (1.0 + jnp.exp(-a))


def _pack_halves(a):
    w = a.shape[1] // 2
    bits = lax.bitcast_convert_type(a.astype(BF16).astype(F32), jnp.uint32)
    return bits[:, :w] | (bits[:, w:] >> 16)


def _unpack_halves(u):
    hi = lax.bitcast_convert_type(u & jnp.uint32(0xFFFF0000), F32)
    lo = lax.bitcast_convert_type(u << 16, F32)
    return jnp.concatenate([hi, lo], axis=1)


def _mod_spec(d, n_lat_tiles):
    return pl.BlockSpec((1, 1, 6, d), lambda i, t: (i, jnp.where(t < n_lat_tiles, 1, 0), 0, 0))


def _ada_kernel(c_ref, w_ref, b_ref, o_ref):
    s = c_ref[...]
    s = s * _sigmoid(s)
    s_hi, s_lo = _split_bf16(s)
    w_hi, w_lo = _split_bf16(w_ref[0])
    o_ref[0] = _dot(s_hi, w_hi) + _dot(s_hi, w_lo) + _dot(s_lo, w_hi) + b_ref[0]


def _ada_call(c_rows, ada_w, ada_b):
    depth, d, n6 = ada_w.shape
    rows = c_rows.shape[0]
    tn = 1536
    return pl.pallas_call(
        _ada_kernel,
        grid=(depth, n6 // tn),
        in_specs=[
            pl.BlockSpec((rows, d), lambda l, j: (0, 0)),
            pl.BlockSpec((1, d, tn), lambda l, j: (l, 0, j)),
            pl.BlockSpec((1, 1, tn), lambda l, j: (l, 0, j)),
        ],
        out_specs=pl.BlockSpec((1, rows, tn), lambda l, j: (l, 0, j)),
        out_shape=jax.ShapeDtypeStruct((depth, rows, n6), F32),
        compiler_params=_params(("parallel", "parallel")),
    )(c_rows, ada_w, ada_b.reshape(depth, 1, n6))


def _rope(x, cos, sin_signed, half):
    outs = []
    for c in range(x.shape[1] // LANES):
        sl = slice(c * LANES, (c + 1) * LANES)
        xc = x[:, sl]
        lane = lax.broadcasted_iota(jnp.int32, xc.shape, 1)
        first_half = (lane % (2 * half)) < half
        partner = jnp.where(first_half, pltpu.roll(xc, LANES - half, 1), pltpu.roll(xc, half, 1))
        outs.append(xc * cos[:, sl] + partner * sin_signed[:, sl])
    return outs[0] if len(outs) == 1 else jnp.concatenate(outs, axis=1)


def _group_mean_sq(x, gsize):
    w = x.shape[1]
    r = lax.broadcasted_iota(jnp.int32, (w, w), 0) // gsize
    c = lax.broadcasted_iota(jnp.int32, (w, w), 1) // gsize
    ones = jnp.where(r == c, 1.0, 0.0).astype(BF16)
    hi, lo = _split_bf16(x * x)
    return (_dot(hi, ones) + _dot(lo, ones)) * (1.0 / gsize)


def _in_proj_kernel(x_ref, mod_ref, g_ref, w_ref, cos64_ref, sin64_ref, cos32_ref, sin32_ref,
                    gqn_ref, gkn_ref, mqn_ref, mkvn_ref, wuq_ref, wukv_ref,
                    rq_ref, rk_ref, rv_ref, rg_ref, dq_ref, dk_ref, dv_ref, gq_ref, gk_ref, gv_ref,
                    mqn_o, mqr_o, mkn_o, mv_o, mkr_o):
    mod = mod_ref[0, 0]
    h = _rms(x_ref[0]) * g_ref[...] * (1.0 + mod[1:2]) + mod[0:1]
    hb = h.astype(BF16)
    cos64, sin64 = cos64_ref[...], sin64_ref[...]
    cos32, sin32 = cos32_ref[...], sin32_ref[...]

    def proj(name, width):
        o = IN_OFFSETS[name]
        return _dot(hb, w_ref[:, o:o + width])

    rq_ref[0] = _rope(proj("ret_q", 256), cos64, sin64, 32).astype(BF16)
    rk_ref[0] = (_rope(proj("ret_k", 256), cos64, sin64, 32) * (HEAD_DIM ** -0.5)).astype(BF16)
    rv_ref[0] = proj("ret_v", 256).astype(BF16)
    rg_ref[0] = proj("ret_g", 256)
    dq_ref[0] = (_rope(proj("dif_q", 256), cos32, sin32, 16) * (DIFF_D ** -0.5 * LOG2E)).astype(BF16)
    dk_ref[0] = _rope(proj("dif_k", 256), cos32, sin32, 16).astype(BF16)
    dv_ref[0] = proj("dif_v", 256).astype(BF16)
    gq = proj("gqa_q", 256)
    gq = gq * lax.rsqrt(_group_mean_sq(gq, HEAD_DIM) + EPS) * gqn_ref[...]
    gq_ref[0] = (_rope(gq, cos64, sin64, 32) * (HEAD_DIM ** -0.5 * LOG2E)).astype(BF16)
    gk = proj("gqa_k", 128)
    gk = gk * lax.rsqrt(_group_mean_sq(gk, HEAD_DIM) + EPS) * gkn_ref[...]
    gk_ref[0] = _rope(gk, cos64[:, :LANES], sin64[:, :LANES], 32).astype(BF16)
    gv_ref[0] = proj("gqa_v", 128).astype(BF16)
    cq = proj("mla_cq", 256)
    cq = cq * lax.rsqrt(jnp.sum(cq * cq, axis=-1, keepdims=True) * (1.0 / MLA_Q_RANK) + EPS) * mqn_ref[...]
    q_up = _dot(cq.astype(BF16), wuq_ref[...])
    mla_scale = (MLA_NOPE + MLA_ROPE) ** -0.5 * LOG2E
    mqn_o[0] = (q_up[:, :256] * mla_scale).astype(BF16)
    mqr_o[0] = (_rope(q_up[:, 256:], cos32[:, :LANES], sin32[:, :LANES], 16) * mla_scale).astype(BF16)
    ckv = proj("mla_ckv", 128)
    ckv = ckv * lax.rsqrt(jnp.mean(ckv * ckv, axis=-1, keepdims=True) + EPS) * mkvn_ref[...]
    kv_up = _dot(ckv.astype(BF16), wukv_ref[...])
    mkn_o[0] = kv_up[:, :256].astype(BF16)
    mv_o[0] = kv_up[:, 256:].astype(BF16)
    mkr_o[0] = _rope(proj("mla_kr", 128), cos32[:, :LANES], sin32[:, :LANES], 16).astype(BF16)


def _in_proj_call(xc, modl, g0, w_in_p, tables, gqn, gkn, mqn, mkvn, wuq_p, wukv_p, n_lat):
    b, n, d = xc.shape
    tm = TOKEN_TILE
    tok = lambda w: pl.BlockSpec((1, tm, w), lambda i, t: (i, t, 0))
    const2 = lambda a: pl.BlockSpec(a.shape, lambda i, t: (0, 0))
    tab = pl.BlockSpec((tm, 256), lambda i, t: (t, 0))
    out_widths = (256, 256, 256, 256, 256, 256, 256, 256, 128, 128, 256, 128, 256, 256, 128)
    out_dtypes = (BF16, BF16, BF16, F32, BF16, BF16, BF16, BF16, BF16, BF16, BF16, BF16, BF16, BF16, BF16)
    return pl.pallas_call(
        _in_proj_kernel,
        grid=(b, n // tm),
        in_specs=[
            tok(d), _mod_spec(d, n_lat // tm),
            const2(g0), const2(w_in_p), tab, tab, tab, tab,
            const2(gqn), const2(gkn), const2(mqn), const2(mkvn), const2(wuq_p), const2(wukv_p),
        ],
        out_specs=[tok(w) for w in out_widths],
        out_shape=[jax.ShapeDtypeStruct((b, n, w), dt) for w, dt in zip(out_widths, out_dtypes)],
        compiler_params=_params(("parallel", "parallel")),
    )(xc, modl, g0, w_in_p, *tables, gqn, gkn, mqn, mkvn, wuq_p, wukv_p)


def _ret_kernel(*refs, readout):
    if readout:
        (q_ref, k_ref, v_ref, dec_ref, xi_ref, zeta_ref, gc_ref, of_ref, g_ref, gnw_ref, gnb_ref,
         o_ref, state_ref) = refs
    else:
        q_ref, k_ref, v_ref, dec_ref, xi_ref, zeta_ref, gc_ref, o_ref, state_ref = refs

    @pl.when(pl.program_id(1) == 0)
    def _():
        state_ref[...] = jnp.zeros_like(state_ref)

    q, k, v = q_ref[0], k_ref[0], v_ref[0]
    outs = []
    for h in range(N_HEADS):
        sl = slice(h * HEAD_DIM, (h + 1) * HEAD_DIM)
        qh, kh, vh = q[:, sl], k[:, sl], v[:, sl]
        st = state_ref[h]
        inner = _dot_nt(qh, kh) * dec_ref[h]
        o = _dot(inner.astype(BF16), vh) + _dot(qh, st.astype(BF16)) * xi_ref[h]
        kz = (kh.astype(F32) * zeta_ref[h]).astype(BF16)
        kv = lax.dot_general(kz, vh, (((0,), (0,)), ((), ())), preferred_element_type=F32)
        state_ref[h] = gc_ref[h] * st + kv
        if readout:
            o = o + of_ref[0][:, sl]
            mu = jnp.mean(o, axis=-1, keepdims=True)
            var = jnp.mean(jnp.square(o - mu), axis=-1, keepdims=True)
            o = (o - mu) * lax.rsqrt(var + EPS)
        outs.append(o)
    o = jnp.concatenate(outs, axis=1)
    if readout:
        g = g_ref[0]
        o_ref[0] = ((o * gnw_ref[...] + gnb_ref[...]) * (g * _sigmoid(g))).astype(o_ref.dtype)
    else:
        o_ref[0] = o


def _ret_call(q, k, v, tabs, n_lat, backward, readout_args=None):
    b, n, w = q.shape
    c = RET_CHUNK
    nch = n // c
    nlc = n_lat // c
    if backward:
        chunk = lambda t: nch - 1 - t
    else:
        chunk = lambda t: jnp.where(t < nch - nlc, nlc + t, t - (nch - nlc))
    tok = lambda dt_w: pl.BlockSpec((1, c, dt_w), lambda i, t: (i, chunk(t), 0))
    const = lambda a: pl.BlockSpec(a.shape, lambda i, t: (0,) * a.ndim)
    in_specs = [tok(w), tok(w), tok(w)] + [const(a) for a in tabs]
    args = [q, k, v, *tabs]
    if readout_args is not None:
        o_f, g, gnw, gnb = readout_args
        in_specs += [tok(w), tok(w), const(gnw), const(gnb)]
        args += [o_f, g, gnw, gnb]
    out_dtype = BF16 if readout_args is not None else F32
    return pl.pallas_call(
        functools.partial(_ret_kernel, readout=readout_args is not None),
        grid=(b, nch),
        in_specs=in_specs,
        out_specs=tok(w),
        out_shape=jax.ShapeDtypeStruct((b, n, w), out_dtype),
        scratch_shapes=[pltpu.VMEM((N_HEADS, HEAD_DIM, HEAD_DIM), F32)],
        compiler_params=_params(("parallel", "arbitrary")),
    )(*args)


def _ret_tables(log_g, backward):
    c = RET_CHUNK
    pos = jnp.arange(c, dtype=F32)
    dist = (pos[None, :] - pos[:, None]) if backward else (pos[:, None] - pos[None, :])
    lg = log_g[:, None, None]
    decay = jnp.where(dist >= 0, jnp.exp(lg * jnp.maximum(dist, 0.0)), 0.0)
    to_state = (c - pos) if backward else (pos + 1.0)
    to_end = pos if backward else (c - 1.0 - pos)
    xi = jnp.exp(log_g[:, None] * to_state)[:, :, None]
    zeta = jnp.exp(log_g[:, None] * to_end)[:, :, None]
    ones = jnp.ones((1, 1, HEAD_DIM), F32)
    gc = jnp.exp(log_g * c)[:, None, None] * jnp.ones((1, HEAD_DIM, HEAD_DIM), F32)
    return decay, xi * ones, zeta * ones, gc


def _attn_kernel(*refs, n_maps, readout_scale):
    if n_maps == 2:
        lam_ref, q_ref, kt_ref, v_ref, sub_ref, o_ref, first_ref = refs
    else:
        q_ref, kt_ref, v_ref, o_ref = refs
    hm = pl.program_id(2)
    q = q_ref[0, 0]
    tq = q.shape[0]
    n_keys = kt_ref.shape[3]
    dv = v_ref.shape[3] // 2
    tk = min(KV_TILE, n_keys)
    n_full, tail = divmod(n_keys, tk)

    def step(off, size, carry):
        m, acc = carry
        s = _dot(q, kt_ref[0, 0, :, pl.ds(off, size)])
        m_new = jnp.maximum(m, jnp.max(s, axis=-1, keepdims=True))
        p = jnp.exp2(s - m_new)
        acc = jnp.exp2(m - m_new) * acc + _dot(p.astype(BF16), v_ref[0, 0, pl.ds(off, size), :])
        return m_new, acc

    carry = (jnp.full((tq, 1), -1e30, F32), jnp.zeros((tq, 2 * dv), F32))
    for j in range(n_full):
        carry = step(j * tk, tk, carry)
    if tail:
        carry = step(n_full * tk, tail, carry)
    acc = carry[1]
    o = acc[:, :dv] / acc[:, dv:]

    def store(head_of_step, val):
        for hh in range(N_HEADS):
            @pl.when(head_of_step == hh)
            def _(hh=hh):
                o_ref[0, :, hh * dv:(hh + 1) * dv] = val.astype(o_ref.dtype)

    if n_maps == 1:
        store(hm, o)
    else:
        @pl.when(hm % 2 == 0)
        def _():
            first_ref[...] = o

        @pl.when(hm % 2 == 1)
        def _():
            od = first_ref[...] - lam_ref[0] * o
            od = od * lax.rsqrt(jnp.mean(od * od, axis=-1, keepdims=True) + EPS) * sub_ref[...]
            store(hm // 2, od * readout_scale)


def _attn_call(q, kt, v, rows, keys, tq, diff=None):
    b, hq, n, d = q.shape
    hk, hv, dv = kt.shape[1], v.shape[1], v.shape[3] // 2
    (q0, qn), (k0, kn) = rows, keys
    assert q0 % tq == 0 and qn % tq == 0 and k0 % kn == 0
    n_maps = 2 if diff is not None else 1
    in_specs = [
        pl.BlockSpec((1, 1, tq, d), lambda i, t, h: (i, h, q0 // tq + t, 0)),
        pl.BlockSpec((1, 1, d, kn), lambda i, t, h: (i, h // (hq // hk), 0, k0 // kn)),
        pl.BlockSpec((1, 1, kn, 2 * dv), lambda i, t, h: (i, h // (hq // hv), k0 // kn, 0)),
    ]
    args = [q, kt, v]
    scratch = []
    scale = 1.0
    if diff is not None:
        lam, subln, scale = diff
        in_specs = [pl.BlockSpec(memory_space=pltpu.SMEM)] + in_specs + [pl.BlockSpec((1, dv), lambda i, t, h: (0, 0))]
        args = [lam.reshape(1)] + args + [subln.reshape(1, dv)]
        scratch = [pltpu.VMEM((tq, dv), F32)]
    return pl.pallas_call(
        functools.partial(_attn_kernel, n_maps=n_maps, readout_scale=scale),
        grid=(b, qn // tq, hq),
        in_specs=in_specs,
        out_specs=pl.BlockSpec((1, tq, (hq // n_maps) * dv), lambda i, t, h: (i, t, 0)),
        out_shape=jax.ShapeDtypeStruct((b, qn, (hq // n_maps) * dv), BF16),
        scratch_shapes=scratch,
        compiler_params=_params(("parallel", "parallel", "arbitrary")),
    )(*args)


def _out_proj_kernel(y_ref, w_ref, x_ref, mod_ref, g_ref, wr_ref, br_ref,
                     xo_ref, hs_ref, gate_ref, pos_ref, cnt_ref):
    mod = mod_ref[0, 0]
    g = g_ref[...]
    y = _dot(y_ref[0], w_ref[...])
    xn = x_ref[0] + mod[2:3] * (_rms(y) * g[1:2])
    xo_ref[0] = xn
    h = _rms(xn) * g[2:3] * (1.0 + mod[4:5]) + mod[3:4]
    w_hi, w_lo = _split_bf16(wr_ref[...])
    h_hi, h_lo = _split_bf16(h)
    logits = _dot_nt(w_hi, h_hi) + _dot_nt(w_hi, h_lo) + _dot_nt(w_lo, h_hi) + br_ref[...]
    row = lax.broadcasted_iota(jnp.int32, logits.shape, 0)
    vals = logits
    tops, idxs = [], []
    for _ in range(TOP_K):
        m = jnp.max(vals, axis=0, keepdims=True)
        idx = jnp.min(jnp.where(vals == m, row, N_EXPERTS), axis=0, keepdims=True)
        tops.append(m)
        idxs.append(idx)
        vals = jnp.where(row == idx, -jnp.inf, vals)
    ex = [jnp.exp(tv - tops[0]) for tv in tops]
    den = ex[0] + ex[1] + ex[2] + ex[3]
    gate_ref[0] = jnp.concatenate([e / den for e in ex], axis=0)
    tm = logits.shape[1]
    before = jnp.where(lax.broadcasted_iota(jnp.int32, (tm, tm), 0) < lax.broadcasted_iota(jnp.int32, (tm, tm), 1),
                       1.0, 0.0).astype(BF16)
    counts = jnp.zeros((N_EXPERTS, 1), F32)
    onehots, ranks = [], []
    for idx in idxs:
        onehot = jnp.where(row == idx, 1.0, 0.0)
        ranks.append(_dot(onehot.astype(BF16), before) + counts)
        onehots.append(onehot)
        counts = counts + jnp.sum(onehot, axis=1, keepdims=True)
    counts_b = jnp.broadcast_to(counts, (N_EXPERTS, LANES))
    earlier = jnp.where(lax.broadcasted_iota(jnp.int32, (N_EXPERTS, N_EXPERTS), 0)
                        > lax.broadcasted_iota(jnp.int32, (N_EXPERTS, N_EXPERTS), 1), 1.0, 0.0).astype(BF16)
    first = _dot(earlier, counts_b.astype(BF16))[:, 0:1]
    pos = [jnp.sum(oh * (rk + first), axis=0, keepdims=True).astype(jnp.int32) for oh, rk in zip(onehots, ranks)]
    pos_ref[0] = jnp.concatenate(pos, axis=0)
    cnt_ref[0, 0] = counts_b.astype(jnp.int32)
    slot = lax.broadcasted_iota(jnp.int32, (TOP_K * tm, tm), 0)
    pick = jnp.where(slot == pos[0], 1.0, 0.0)
    for p in pos[1:]:
        pick = jnp.where(slot == p, 1.0, pick)
    hs_ref[0, 0] = _pack_halves(_dot(pick.astype(BF16), h.astype(BF16)))


def _out_proj_call(ymix, w_out, xc, modl, g, wr_t, br, n_lat, n_rows):
    b, n, d = xc.shape
    tm = TOKEN_TILE
    tok = lambda w: pl.BlockSpec((1, tm, w), lambda i, t: (i, t, 0))
    const2 = lambda a: pl.BlockSpec(a.shape, lambda i, t: (0, 0))
    sel = pl.BlockSpec((1, TOP_K, tm), lambda i, t: (i, 0, t))
    return pl.pallas_call(
        _out_proj_kernel,
        grid=(b, n_rows // tm),
        in_specs=[
            tok(ymix.shape[2]), const2(w_out), tok(d), _mod_spec(d, n_lat // tm),
            const2(g), const2(wr_t), const2(br),
        ],
        out_specs=[tok(d),
                   pl.BlockSpec((1, 1, TOP_K * tm, d // 2), lambda i, t: (i, t, 0, 0)),
                   sel, sel,
                   pl.BlockSpec((1, 1, N_EXPERTS, LANES), lambda i, t: (i, t, 0, 0))],
        out_shape=[
            jax.ShapeDtypeStruct((b, n_rows, d), F32),
            jax.ShapeDtypeStruct((b, n_rows // tm, TOP_K * tm, d // 2), jnp.uint32),
            jax.ShapeDtypeStruct((b, TOP_K, n_rows), F32),
            jax.ShapeDtypeStruct((b, TOP_K, n_rows), jnp.int32),
            jax.ShapeDtypeStruct((b, n_rows // tm, N_EXPERTS, LANES), jnp.int32),
        ],
        compiler_params=_params(("parallel", "parallel")),
    )(ymix, w_out, xc, modl, g, wr_t, br)


def _w1_prep_kernel(w_ref, g_ref, l_ref):
    r = lax.broadcasted_iota(jnp.int32, (MXU_DIM, MXU_DIM), 0)
    c = lax.broadcasted_iota(jnp.int32, (MXU_DIM, MXU_DIM), 1)
    src = jnp.where(c < LANES, 2 * c, 2 * (c - LANES) + 1)
    perm = jnp.where(r == src, 1.0, 0.0).astype(BF16)
    for j in range(w_ref.shape[3] // MXU_DIM):
        out = _dot(w_ref[0, 0, :, j * MXU_DIM:(j + 1) * MXU_DIM].astype(BF16), perm)
        g_ref[0, :, j * LANES:(j + 1) * LANES] = out[:, :LANES].astype(BF16)
        l_ref[0, :, j * LANES:(j + 1) * LANES] = out[:, LANES:].astype(BF16)


def _w1_prep_call(w1_all, layer):
    _, e, d, f2 = w1_all.shape
    f = f2 // 2
    rows = 256
    out_spec = pl.BlockSpec((1, rows, f), lambda i, j: (i, j, 0))
    return pl.pallas_call(
        _w1_prep_kernel,
        grid=(e, d // rows),
        in_specs=[pl.BlockSpec((1, 1, rows, f2), lambda i, j: (layer, i, j, 0))],
        out_specs=[out_spec, out_spec],
        out_shape=[jax.ShapeDtypeStruct((e, d, f), BF16)] * 2,
        compiler_params=_params(("parallel", "parallel")),
    )(w1_all)


def _expert_kernel(be_ref, nu_ref, x_ref, w1g_ref, w1l_ref, b1g_ref, b1l_ref, w2_ref, b2_ref, o_ref):
    del be_ref
    i = pl.program_id(0)

    @pl.when(i < nu_ref[0])
    def _():
        x = _unpack_halves(x_ref[...]).astype(BF16)
        glu = jnp.minimum(_dot(x, w1g_ref[0]) + b1g_ref[0], SWIGLU_LIMIT)
        lin = jnp.clip(_dot(x, w1l_ref[0]) + b1l_ref[0], -SWIGLU_LIMIT, SWIGLU_LIMIT)
        act = glu * _sigmoid(SWIGLU_ALPHA * glu) * (lin + 1.0)
        o_ref[...] = _pack_halves(_dot(act.astype(BF16), w2_ref[0, 0]) + b2_ref[0])

    @pl.when(i >= nu_ref[0])
    def _():
        o_ref[...] = jnp.zeros_like(o_ref)


def _expert_call(block_expert, n_used, x_rows, w1g, w1l, b1g, b1l, w2_all, layer, b2):
    r = x_rows.shape[0]
    d, f = w1g.shape[1:]
    tm = MOE_BLOCK
    wspec = lambda s: pl.BlockSpec((1,) + s, lambda i, be, nu: (be[i], 0, 0))
    return pl.pallas_call(
        _expert_kernel,
        grid_spec=pltpu.PrefetchScalarGridSpec(
            num_scalar_prefetch=2,
            grid=(r // tm,),
            in_specs=[
                pl.BlockSpec((tm, d // 2), lambda i, be, nu: (i, 0)),
                wspec((d, f)), wspec((d, f)), wspec((1, f)), wspec((1, f)),
                pl.BlockSpec((1, 1, f, d), lambda i, be, nu: (layer, be[i], 0, 0)),
                wspec((1, d)),
            ],
            out_specs=pl.BlockSpec((tm, d // 2), lambda i, be, nu: (i, 0)),
        ),
        out_shape=jax.ShapeDtypeStruct((r, d // 2), jnp.uint32),
        compiler_params=_params(("arbitrary",)),
    )(block_expert, n_used, x_rows, w1g, w1l, b1g, b1l, w2_all, b2)


def _combine_kernel(x_ref, y_ref, gate_ref, pos_ref, mod_ref, g_ref, o_ref):
    gates, pos = gate_ref[0], pos_ref[0]
    rows = _unpack_halves(y_ref[0, 0]).astype(BF16)
    lane = lax.broadcasted_iota(jnp.int32, (gates.shape[0], rows.shape[0]), 1)
    mix = jnp.zeros(lane.shape, F32)
    for k in range(TOP_K):
        mix = jnp.where(lane == pos[:, k:k + 1], gates[:, k:k + 1], mix)
    mix_hi, mix_lo = _split_bf16(mix)
    y = _dot(mix_hi, rows) + _dot(mix_lo, rows)
    mod = mod_ref[0, 0]
    o_ref[0] = x_ref[0] + mod[5:6] * (_rms(y) * g_ref[...][3:4])


def _combine_call(xc, picked, gates, pos, modl, g, n_lat):
    b, n, d = xc.shape
    tm = TOKEN_TILE
    tok = pl.BlockSpec((1, tm, d), lambda i, t: (i, t, 0))
    return pl.pallas_call(
        _combine_kernel,
        grid=(b, n // tm),
        in_specs=[
            tok,
            pl.BlockSpec((1, 1, TOP_K * tm, d // 2), lambda i, t: (i, t, 0, 0)),
            pl.BlockSpec((1, tm, TOP_K), lambda i, t: (i, t, 0)),
            pl.BlockSpec((1, tm, TOP_K), lambda i, t: (i, t, 0)),
            _mod_spec(d, n_lat // tm),
            pl.BlockSpec(g.shape, lambda i, t: (0, 0)),
        ],
        out_specs=tok,
        out_shape=jax.ShapeDtypeStruct((b, n, d), F32),
        compiler_params=_params(("parallel", "parallel")),
    )(xc, picked, gates, pos, modl, g)


def _rope_tables(rows, n_ctx, rot_dim):
    row = jnp.repeat(jnp.arange(rows), GRID_W)
    col = jnp.tile(jnp.arange(GRID_W), rows)
    n_freq = rot_dim // 4
    inv = ROPE_THETA ** (-jnp.arange(n_freq, dtype=F32) / n_freq)
    ang = jnp.concatenate([row[:, None] * inv, col[:, None] * inv], axis=-1)
    cos, sin = jnp.cos(ang), jnp.sin(ang)
    reps = 256 // rot_dim
    cos_t = jnp.tile(jnp.concatenate([cos, cos], axis=-1), (1, reps))
    sin_t = jnp.tile(jnp.concatenate([-sin, sin], axis=-1), (1, reps))
    cos_t = jnp.concatenate([cos_t, jnp.ones((n_ctx, 256), F32)], axis=0)
    sin_t = jnp.concatenate([sin_t, jnp.zeros((n_ctx, 256), F32)], axis=0)
    return cos_t, sin_t


def _pack_w_in(w):
    parts, src = [], 0
    for _, width, padded in IN_PIECES:
        parts.append(w[:, src:src + width])
        if padded > width:
            parts.append(jnp.zeros((w.shape[0], padded - width), w.dtype))
        src += width
    return jnp.concatenate(parts, axis=1).astype(BF16)


def _pack_w_uq(w):
    wh = w.reshape(MLA_Q_RANK, N_HEADS, MLA_NOPE + MLA_ROPE)
    nope = wh[:, :, :MLA_NOPE].reshape(MLA_Q_RANK, N_HEADS * MLA_NOPE)
    rope = wh[:, :, MLA_NOPE:].reshape(MLA_Q_RANK, N_HEADS * MLA_ROPE)
    packed = jnp.concatenate([nope, rope], axis=1)
    return jnp.pad(packed, ((0, 256 - MLA_Q_RANK), (0, 0))).astype(BF16)


def _pack_w_ukv(w):
    wh = w.reshape(MLA_KV_RANK, N_HEADS, MLA_NOPE + HEAD_DIM)
    kn = wh[:, :, :MLA_NOPE].reshape(MLA_KV_RANK, N_HEADS * MLA_NOPE)
    vv = wh[:, :, MLA_NOPE:].reshape(MLA_KV_RANK, N_HEADS * HEAD_DIM)
    return jnp.concatenate([kn, vv], axis=1).astype(BF16)


def _heads(a, n_heads):
    b, n, w = a.shape
    return a.reshape(b, n, n_heads, w // n_heads).transpose(0, 2, 1, 3)


def _heads_t(a, n_heads):
    b, n, w = a.shape
    return a.reshape(b, n, n_heads, w // n_heads).transpose(0, 2, 3, 1)


def _attention(q, kt, v, n_lat, n_ctx, ctx_out, diff=None):
    n = n_lat + n_ctx
    y = _attn_call(q, kt, v, (0, n_lat), (0, n), min(Q_TILE, n_lat), diff)
    if ctx_out:
        y_c = _attn_call(q, kt, v, (n_lat, n_ctx), (n_lat, n_ctx), n_ctx, diff)
    else:
        y_c = jnp.zeros((y.shape[0], n_ctx, y.shape[2]), y.dtype)
    return jnp.concatenate([y, y_c], axis=1)


def _heads_v(a, n_heads):
    h = _heads(a, n_heads)
    return jnp.concatenate([h, jnp.ones_like(h)], axis=-1)


def _moe(x_new, hs, gates, pos, counts, modl, g, n_lat, w1_all, b1, w2_all, b2, layer):
    b, n, d = x_new.shape
    tile_rows = TOP_K * TOKEN_TILE
    n_blocks = -(-(b * n * TOP_K) // MOE_BLOCK) + N_EXPERTS
    tile_counts = counts.reshape(-1, N_EXPERTS)
    n_tiles = tile_counts.shape[0]
    tile_base = jnp.cumsum(tile_counts, axis=0) - tile_counts
    first = jnp.cumsum(tile_counts, axis=1) - tile_counts
    totals = jnp.sum(tile_counts, axis=0)
    padded = (totals + MOE_BLOCK - 1) // MOE_BLOCK * MOE_BLOCK
    pad_ends = jnp.cumsum(padded)
    pad_start = pad_ends - padded
    block_start = jnp.arange(n_blocks, dtype=jnp.int32) * MOE_BLOCK
    block_expert = jnp.minimum(jnp.sum((pad_ends[None, :] <= block_start[:, None]).astype(jnp.int32), axis=1),
                               N_EXPERTS - 1)
    n_used = (pad_ends[-1:] // MOE_BLOCK).astype(jnp.int32)

    def steps(x, bounds, offsets):
        deltas = offsets - jnp.concatenate([jnp.zeros_like(offsets[..., :1]), offsets[..., :-1]], axis=-1)
        return x + jnp.sum(jnp.where(x[..., None] >= bounds[..., None, :], deltas[..., None, :], 0), axis=-1)

    j = block_start[:, None] + jnp.arange(MOE_BLOCK, dtype=jnp.int32)[None, :] - pad_start[block_expert][:, None]
    tile_offset = jnp.arange(n_tiles, dtype=jnp.int32)[:, None] * tile_rows + first - tile_base
    src = steps(j, tile_base.T[block_expert], tile_offset.T[block_expert])
    src = jnp.clip(src, 0, n_tiles * tile_rows - 1)
    x_rows = hs.reshape(n_tiles * tile_rows, d // 2).at[src.reshape(-1)].get(mode="promise_in_bounds")
    f = w2_all.shape[2]
    w1g, w1l = _w1_prep_call(w1_all, layer)
    y_rows = _expert_call(
        block_expert, n_used, x_rows, w1g, w1l,
        b1[:, 0::2].reshape(N_EXPERTS, 1, f), b1[:, 1::2].reshape(N_EXPERTS, 1, f),
        w2_all, layer, b2.reshape(N_EXPERTS, 1, d))
    slot = jnp.broadcast_to(jnp.arange(tile_rows, dtype=jnp.int32)[None, :], (n_tiles, tile_rows))
    back = steps(slot, first, tile_base + pad_start[None, :] - first)
    picked = y_rows.at[back.reshape(-1)].get(mode="promise_in_bounds")
    return _combine_call(x_new, picked.reshape(b, n // TOKEN_TILE, tile_rows, d // 2),
                         gates.transpose(0, 2, 1), pos.transpose(0, 2, 1), modl, g, n_lat)


def kernel(x, c, ctx, c_ctx, ada_w, ada_b, norm_g, w_in, w_out, ret_log_decay, ret_gn_w, ret_gn_b,
           diff_lambda, diff_subln, gqa_qk_norm, mla_q_norm, mla_kv_norm, mla_w_uq, mla_w_ukv,
           router_w, router_b, exp_w1, exp_b1, exp_w2, exp_b2):
    b, s, d = x.shape
    n_ctx = ctx.shape[1]
    n = s + n_ctx
    depth = ada_w.shape[0]
    assert n_ctx % TOKEN_TILE == 0 and s % TOKEN_TILE == 0 and s % GRID_W == 0
    assert n_ctx % RET_CHUNK == 0 and s % min(Q_TILE, s) == 0 and s % n_ctx == 0

    tables = _rope_tables(s // GRID_W, n_ctx, HEAD_DIM) + _rope_tables(s // GRID_W, n_ctx, DIFF_D)
    c_rows = jnp.zeros((16, d), F32).at[:b].set(c).at[b].set(c_ctx)
    mods = _ada_call(c_rows, ada_w, ada_b)
    w2_all = exp_w2.astype(BF16)
    xc = jnp.concatenate([x, ctx], axis=1)

    for l in range(depth):
        last = l == depth - 1
        lam_init = 0.8 - 0.6 * math.exp(-0.3 * l)
        mod_lat = mods[l, :b].reshape(b, 1, 6, d)
        mod_ctx = jnp.broadcast_to(mods[l, b].reshape(1, 1, 6, d), (b, 1, 6, d))
        modl = jnp.concatenate([mod_ctx, mod_lat], axis=1)

        (rq, rk, rv, rg, dq, dk, dv, gq, gk, gv, mqn, mqr, mkn, mv, mkr) = _in_proj_call(
            xc, modl, norm_g[l, 0:1], _pack_w_in(w_in[l]), tables,
            jnp.tile(gqa_qk_norm[l, 0], N_HEADS)[None, :], jnp.tile(gqa_qk_norm[l, 1], GQA_KV_HEADS)[None, :],
            jnp.pad(mla_q_norm[l], (0, 256 - MLA_Q_RANK))[None, :], mla_kv_norm[l][None, :],
            _pack_w_uq(mla_w_uq[l]), _pack_w_ukv(mla_w_ukv[l]), s)

        log_g = -jnp.exp(ret_log_decay[l].astype(F32))
        o_f = _ret_call(rq, rk, rv, _ret_tables(log_g[0], False), s, False)
        ret_y = _ret_call(rq, rk, rv, _ret_tables(log_g[1], True), s, True,
                          (o_f, rg, ret_gn_w[l][None, :], ret_gn_b[l][None, :]))

        lp = diff_lambda[l].astype(F32)
        lam = jnp.exp(jnp.sum(lp[0] * lp[1])) - jnp.exp(jnp.sum(lp[2] * lp[3])) + lam_init
        dif_y = _attention(_heads(dq, 2 * N_HEADS), _heads_t(dk, 2 * N_HEADS), _heads_v(dv, N_HEADS), s, n_ctx,
                           not last, (lam, diff_subln[l], 1.0 - lam_init))

        gqa_y = _attention(_heads(gq, N_HEADS), _heads_t(gk, GQA_KV_HEADS), _heads_v(gv, GQA_KV_HEADS), s, n_ctx,
                           not last)

        q_m = jnp.concatenate([_heads(mqn, N_HEADS), _heads(mqr, N_HEADS)], axis=-1)
        kr_t = jnp.broadcast_to(mkr[:, None, :, :MLA_ROPE].transpose(0, 1, 3, 2), (b, N_HEADS, MLA_ROPE, n))
        kt_m = jnp.concatenate([_heads_t(mkn, N_HEADS), kr_t], axis=2)
        mla_y = _attention(q_m, kt_m, _heads_v(mv, N_HEADS), s, n_ctx, not last)

        ymix = jnp.concatenate([ret_y, dif_y, gqa_y, mla_y], axis=-1)
        n_rows = s if last else n
        x_new, hs, gates, pos, counts = _out_proj_call(
            ymix, w_out[l].astype(BF16), xc, modl, norm_g[l], router_w[l].T, router_b[l][:, None], s, n_rows)
        xc = _moe(x_new, hs, gates, pos, counts[..., 0], modl, norm_g[l], s,
                  exp_w1, exp_b1[l], w2_all, exp_b2[l], l)

    return xc[:, :s]
```

```python
import functools
import math

import jax
import jax.numpy as jnp
from jax import lax
from jax.experimental import pallas as pl
from jax.experimental.pallas import tpu as pltpu

F32 = jnp.float32
BF16 = jnp.bfloat16

GRID_W = 64
ROPE_THETA = 10000.0
EPS = 1e-6
GROUP_WIDTH = 256
HEAD_DIM = 64
N_HEADS = 4
RET_CHUNK = 256
DIFF_D = 32
GQA_KV_HEADS = 2
MLA_Q_RANK = 192
MLA_KV_RANK = 128
MLA_NOPE = 64
MLA_ROPE = 32
N_EXPERTS = 32
TOP_K = 4
SWIGLU_LIMIT = 7.0
SWIGLU_ALPHA = 1.702
MOE_BLOCK = 512

LANES = 128
MXU_DIM = 256
TOKEN_TILE = 256
Q_TILE = 1024
KV_TILE = 512
VMEM_LIMIT = 48 * 1024 * 1024
LOG2E = math.log2(math.e)

IN_PIECES = (
    ("ret_q", 256, 256), ("ret_k", 256, 256), ("ret_v", 256, 256), ("ret_g", 256, 256),
    ("dif_q", 256, 256), ("dif_k", 256, 256), ("dif_v", 256, 256),
    ("gqa_q", 256, 256), ("gqa_k", 128, 128), ("gqa_v", 128, 128),
    ("mla_cq", MLA_Q_RANK, 256), ("mla_ckv", MLA_KV_RANK, 128), ("mla_kr", MLA_ROPE, 128),
)
IN_OFFSETS = {}
_off = 0
for _name, _w, _pw in IN_PIECES:
    IN_OFFSETS[_name] = _off
    _off += _pw
IN_PACKED_WIDTH = _off


def _params(sem):
    return pltpu.CompilerParams(dimension_semantics=sem, vmem_limit_bytes=VMEM_LIMIT)


def _rms(x):
    return x * lax.rsqrt(jnp.mean(x * x, axis=-1, keepdims=True) + EPS)


def _split_bf16(a):
    hi = a.astype(BF16)
    lo = (a - hi.astype(F32)).astype(BF16)
    return hi, lo


def _dot_nt(a, b):
    return lax.dot_general(a, b, (((1,), (1,)), ((), ())), preferred_element_type=F32)


def _dot(a, b):
    return jnp.dot(a, b, preferred_element_type=F32)


def _sigmoid(a):
    return 1.0 / (1.0 + jnp.exp(-a))


def _pack_halves(a):
    w = a.shape[1] // 2
    bits = lax.bitcast_convert_type(a.astype(BF16).astype(F32), jnp.uint32)
    return bits[:, :w] | (bits[:, w:] >> 16)


def _unpack_halves(u):
    hi = lax.bitcast_convert_type(u & jnp.uint32(0xFFFF0000), F32)
    lo = lax.bitcast_convert_type(u << 16, F32)
    return jnp.concatenate([hi, lo], axis=1)


def _mod_spec(d, n_lat_tiles):
    return pl.BlockSpec((1, 1, 6, d), lambda i, t: (i, jnp.where(t < n_lat_tiles, 1, 0), 0, 0))


def _ada_kernel(c_ref, w_ref, b_ref, o_ref):
    s = c_ref[...]
    s = s * _sigmoid(s)
    s_hi, s_lo = _split_bf16(s)
    w_hi, w_lo = _split_bf16(w_ref[0])
    o_ref[0] = _dot(s_hi, w_hi) + _dot(s_hi, w_lo) + _dot(s_lo, w_hi) + b_ref[0]


def _ada_call(c_rows, ada_w, ada_b):
    depth, d, n6 = ada_w.shape
    rows = c_rows.shape[0]
    tn = 1536
    return pl.pallas_call(
        _ada_kernel,
        grid=(depth, n6 // tn),
        in_specs=[
            pl.BlockSpec((rows, d), lambda l, j: (0, 0)),
            pl.BlockSpec((1, d, tn), lambda l, j: (l, 0, j)),
            pl.BlockSpec((1, 1, tn), lambda l, j: (l, 0, j)),
        ],
        out_specs=pl.BlockSpec((1, rows, tn), lambda l, j: (l, 0, j)),
        out_shape=jax.ShapeDtypeStruct((depth, rows, n6), F32),
        compiler_params=_params(("parallel", "parallel")),
    )(c_rows, ada_w, ada_b.reshape(depth, 1, n6))


def _rope(x, cos, sin_signed, half):
    outs = []
    for c in range(x.shape[1] // LANES):
        sl = slice(c * LANES, (c + 1) * LANES)
        xc = x[:, sl]
        lane = lax.broadcasted_iota(jnp.int32, xc.shape, 1)
        first_half = (lane % (2 * half)) < half
        partner = jnp.where(first_half, pltpu.roll(xc, LANES - half, 1), pltpu.roll(xc, half, 1))
        outs.append(xc * cos[:, sl] + partner * sin_signed[:, sl])
    return outs[0] if len(outs) == 1 else jnp.concatenate(outs, axis=1)


def _group_mean_sq(x, gsize):
    w = x.shape[1]
    r = lax.broadcasted_iota(jnp.int32, (w, w), 0) // gsize
    c = lax.broadcasted_iota(jnp.int32, (w, w), 1) // gsize
    ones = jnp.where(r == c, 1.0, 0.0).astype(BF16)
    hi, lo = _split_bf16(x * x)
    return (_dot(hi, ones) + _dot(lo, ones)) * (1.0 / gsize)


def _in_proj_kernel(x_ref, mod_ref, g_ref, w_ref, cos64_ref, sin64_ref, cos32_ref, sin32_ref,
                    gqn_ref, gkn_ref, mqn_ref, mkvn_ref, wuq_ref, wukv_ref,
                    rq_ref, rk_ref, rv_ref, rg_ref, dq_ref, dk_ref, dv_ref, gq_ref, gk_ref, gv_ref,
                    mqn_o, mqr_o, mkn_o, mv_o, mkr_o):
    mod = mod_ref[0, 0]
    h = _rms(x_ref[0]) * g_ref[...] * (1.0 + mod[1:2]) + mod[0:1]
    hb = h.astype(BF16)
    cos64, sin64 = cos64_ref[...], sin64_ref[...]
    cos32, sin32 = cos32_ref[...], sin32_ref[...]

    def proj(name, width):
        o = IN_OFFSETS[name]
        return _dot(hb, w_ref[:, o:o + width])

    rq_ref[0] = _rope(proj("ret_q", 256), cos64, sin64, 32).astype(BF16)
    rk_ref[0] = (_rope(proj("ret_k", 256), cos64, sin64, 32) * (HEAD_DIM ** -0.5)).astype(BF16)
    rv_ref[0] = proj("ret_v", 256).astype(BF16)
    rg_ref[0] = proj("ret_g", 256)
    dq_ref[0] = (_rope(proj("dif_q", 256), cos32, sin32, 16) * (DIFF_D ** -0.5 * LOG2E)).astype(BF16)
    dk_ref[0] = _rope(proj("dif_k", 256), cos32, sin32, 16).astype(BF16)
    dv_ref[0] = proj("dif_v", 256).astype(BF16)
    gq = proj("gqa_q", 256)
    gq = gq * lax.rsqrt(_group_mean_sq(gq, HEAD_DIM) + EPS) * gqn_ref[...]
    gq_ref[0] = (_rope(gq, cos64, sin64, 32) * (HEAD_DIM ** -0.5 * LOG2E)).astype(BF16)
    gk = proj("gqa_k", 128)
    gk = gk * lax.rsqrt(_group_mean_sq(gk, HEAD_DIM) + EPS) * gkn_ref[...]
    gk_ref[0] = _rope(gk, cos64[:, :LANES], sin64[:, :LANES], 32).astype(BF16)
    gv_ref[0] = proj("gqa_v", 128).astype(BF16)
    cq = proj("mla_cq", 256)
    cq = cq * lax.rsqrt(jnp.sum(cq * cq, axis=-1, keepdims=True) * (1.0 / MLA_Q_RANK) + EPS) * mqn_ref[...]
    q_up = _dot(cq.astype(BF16), wuq_ref[...])
    mla_scale = (MLA_NOPE + MLA_ROPE) ** -0.5 * LOG2E
    mqn_o[0] = (q_up[:, :256] * mla_scale).astype(BF16)
    mqr_o[0] = (_rope(q_up[:, 256:], cos32[:, :LANES], sin32[:, :LANES], 16) * mla_scale).astype(BF16)
    ckv = proj("mla_ckv", 128)
    ckv = ckv * lax.rsqrt(jnp.mean(ckv * ckv, axis=-1, keepdims=True) + EPS) * mkvn_ref[...]
    kv_up = _dot(ckv.astype(BF16), wukv_ref[...])
    mkn_o[0] = kv_up[:, :256].astype(BF16)
    mv_o[0] = kv_up[:, 256:].astype(BF16)
    mkr_o[0] = _rope(proj("mla_kr", 128), cos32[:, :LANES], sin32[:, :LANES], 16).astype(BF16)


def _in_proj_call(xc, modl, g0, w_in_p, tables, gqn, gkn, mqn, mkvn, wuq_p, wukv_p, n_lat):
    b, n, d = xc.shape
    tm = TOKEN_TILE
    tok = lambda w: pl.BlockSpec((1, tm, w), lambda i, t: (i, t, 0))
    const2 = lambda a: pl.BlockSpec(a.shape, lambda i, t: (0, 0))
    tab = pl.BlockSpec((tm, 256), lambda i, t: (t, 0))
    out_widths = (256, 256, 256, 256, 256, 256, 256, 256, 128, 128, 256, 128, 256, 256, 128)
    out_dtypes = (BF16, BF16, BF16, F32, BF16, BF16, BF16, BF16, BF16, BF16, BF16, BF16, BF16, BF16, BF16)
    return pl.pallas_call(
        _in_proj_kernel,
        grid=(b, n // tm),
        in_specs=[
            tok(d), _mod_spec(d, n_lat // tm),
            const2(g0), const2(w_in_p), tab, tab, tab, tab,
            const2(gqn), const2(gkn), const2(mqn), const2(mkvn), const2(wuq_p), const2(wukv_p),
        ],
        out_specs=[tok(w) for w in out_widths],
        out_shape=[jax.ShapeDtypeStruct((b, n, w), dt) for w, dt in zip(out_widths, out_dtypes)],
        compiler_params=_params(("parallel", "parallel")),
    )(xc, modl, g0, w_in_p, *tables, gqn, gkn, mqn, mkvn, wuq_p, wukv_p)


def _ret_kernel(*refs, readout):
    if readout:
        (q_ref, k_ref, v_ref, dec_ref, xi_ref, zeta_ref, gc_ref, of_ref, g_ref, gnw_ref, gnb_ref,
         o_ref, state_ref) = refs
    else:
        q_ref, k_ref, v_ref, dec_ref, xi_ref, zeta_ref, gc_ref, o_ref, state_ref = refs

    @pl.when(pl.program_id(1) == 0)
    def _():
        state_ref[...] = jnp.zeros_like(state_ref)

    q, k, v = q_ref[0], k_ref[0], v_ref[0]
    outs = []
    for h in range(N_HEADS):
        sl = slice(h * HEAD_DIM, (h + 1) * HEAD_DIM)
        qh, kh, vh = q[:, sl], k[:, sl], v[:, sl]
        st = state_ref[h]
        inner = _dot_nt(qh, kh) * dec_ref[h]
        o = _dot(inner.astype(BF16), vh) + _dot(qh, st.astype(BF16)) * xi_ref[h]
        kz = (kh.astype(F32) * zeta_ref[h]).astype(BF16)
        kv = lax.dot_general(kz, vh, (((0,), (0,)), ((), ())), preferred_element_type=F32)
        state_ref[h] = gc_ref[h] * st + kv
        if readout:
            o = o + of_ref[0][:, sl]
            mu = jnp.mean(o, axis=-1, keepdims=True)
            var = jnp.mean(jnp.square(o - mu), axis=-1, keepdims=True)
            o = (o - mu) * lax.rsqrt(var + EPS)
        outs.append(o)
    o = jnp.concatenate(outs, axis=1)
    if readout:
        g = g_ref[0]
        o_ref[0] = ((o * gnw_ref[...] + gnb_ref[...]) * (g * _sigmoid(g))).astype(o_ref.dtype)
    else:
        o_ref[0] = o


def _ret_call(q, k, v, tabs, n_lat, backward, readout_args=None):
    b, n, w = q.shape
    c = RET_CHUNK
    nch = n // c
    nlc = n_lat // c
    if backward:
        chunk = lambda t: nch - 1 - t
    else:
        chunk = lambda t: jnp.where(t < nch - nlc, nlc + t, t - (nch - nlc))
    tok = lambda dt_w: pl.BlockSpec((1, c, dt_w), lambda i, t: (i, chunk(t), 0))
    const = lambda a: pl.BlockSpec(a.shape, lambda i, t: (0,) * a.ndim)
    in_specs = [tok(w), tok(w), tok(w)] + [const(a) for a in tabs]
    args = [q, k, v, *tabs]
    if readout_args is not None:
        o_f, g, gnw, gnb = readout_args
        in_specs += [tok(w), tok(w), const(gnw), const(gnb)]
        args += [o_f, g, gnw, gnb]
    out_dtype = BF16 if readout_args is not None else F32
    return pl.pallas_call(
        functools.partial(_ret_kernel, readout=readout_args is not None),
        grid=(b, nch),
        in_specs=in_specs,
        out_specs=tok(w),
        out_shape=jax.ShapeDtypeStruct((b, n, w), out_dtype),
        scratch_shapes=[pltpu.VMEM((N_HEADS, HEAD_DIM, HEAD_DIM), F32)],
        compiler_params=_params(("parallel", "arbitrary")),
    )(*args)


def _ret_tables(log_g, backward):
    c = RET_CHUNK
    pos = jnp.arange(c, dtype=F32)
    dist = (pos[None, :] - pos[:, None]) if backward else (pos[:, None] - pos[None, :])
    lg = log_g[:, None, None]
    decay = jnp.where(dist >= 0, jnp.exp(lg * jnp.maximum(dist, 0.0)), 0.0)
    to_state = (c - pos) if backward else (pos + 1.0)
    to_end = pos if backward else (c - 1.0 - pos)
    xi = jnp.exp(log_g[:, None] * to_state)[:, :, None]
    zeta = jnp.exp(log_g[:, None] * to_end)[:, :, None]
    ones = jnp.ones((1, 1, HEAD_DIM), F32)
    gc = jnp.exp(log_g * c)[:, None, None] * jnp.ones((1, HEAD_DIM, HEAD_DIM), F32)
    return decay, xi * ones, zeta * ones, gc


def _attn_kernel(*refs, n_maps, readout_scale):
    if n_maps == 2:
        lam_ref, q_ref, kt_ref, v_ref, sub_ref, o_ref, first_ref = refs
    else:
        q_ref, kt_ref, v_ref, o_ref = refs
    hm = pl.program_id(2)
    q = q_ref[0, 0]
    tq = q.shape[0]
    n_keys = kt_ref.shape[3]
    dv = v_ref.shape[3] // 2
    tk = min(KV_TILE, n_keys)
    n_full, tail = divmod(n_keys, tk)

    def step(off, size, carry):
        m, acc = carry
        s = _dot(q, kt_ref[0, 0, :, pl.ds(off, size)])
        m_new = jnp.maximum(m, jnp.max(s, axis=-1, keepdims=True))
        p = jnp.exp2(s - m_new)
        acc = jnp.exp2(m - m_new) * acc + _dot(p.astype(BF16), v_ref[0, 0, pl.ds(off, size), :])
        return m_new, acc

    carry = (jnp.full((tq, 1), -1e30, F32), jnp.zeros((tq, 2 * dv), F32))
    for j in range(n_full):
        carry = step(j * tk, tk, carry)
    if tail:
        carry = step(n_full * tk, tail, carry)
    acc = carry[1]
    o = acc[:, :dv] / acc[:, dv:]

    def store(head_of_step, val):
        for hh in range(N_HEADS):
            @pl.when(head_of_step == hh)
            def _(hh=hh):
                o_ref[0, :, hh * dv:(hh + 1) * dv] = val.astype(o_ref.dtype)

    if n_maps == 1:
        store(hm, o)
    else:
        @pl.when(hm % 2 == 0)
        def _():
            first_ref[...] = o

        @pl.when(hm % 2 == 1)
        def _():
            od = first_ref[...] - lam_ref[0] * o
            od = od * lax.rsqrt(jnp.mean(od * od, axis=-1, keepdims=True) + EPS) * sub_ref[...]
            store(hm // 2, od * readout_scale)


def _attn_call(q, kt, v, rows, keys, tq, diff=None):
    b, hq, n, d = q.shape
    hk, hv, dv = kt.shape[1], v.shape[1], v.shape[3] // 2
    (q0, qn), (k0, kn) = rows, keys
    assert q0 % tq == 0 and qn % tq == 0 and k0 % kn == 0
    n_maps = 2 if diff is not None else 1
    in_specs = [
        pl.BlockSpec((1, 1, tq, d), lambda i, t, h: (i, h, q0 // tq + t, 0)),
        pl.BlockSpec((1, 1, d, kn), lambda i, t, h: (i, h // (hq // hk), 0, k0 // kn)),
        pl.BlockSpec((1, 1, kn, 2 * dv), lambda i, t, h: (i, h // (hq // hv), k0 // kn, 0)),
    ]
    args = [q, kt, v]
    scratch = []
    scale = 1.0
    if diff is not None:
        lam, subln, scale = diff
        in_specs = [pl.BlockSpec(memory_space=pltpu.SMEM)] + in_specs + [pl.BlockSpec((1, dv), lambda i, t, h: (0, 0))]
        args = [lam.reshape(1)] + args + [subln.reshape(1, dv)]
        scratch = [pltpu.VMEM((tq, dv), F32)]
    return pl.pallas_call(
        functools.partial(_attn_kernel, n_maps=n_maps, readout_scale=scale),
        grid=(b, qn // tq, hq),
        in_specs=in_specs,
        out_specs=pl.BlockSpec((1, tq, (hq // n_maps) * dv), lambda i, t, h: (i, t, 0)),
        out_shape=jax.ShapeDtypeStruct((b, qn, (hq // n_maps) * dv), BF16),
        scratch_shapes=scratch,
        compiler_params=_params(("parallel", "parallel", "arbitrary")),
    )(*args)


def _out_proj_kernel(y_ref, w_ref, x_ref, mod_ref, g_ref, wr_ref, br_ref,
                     xo_ref, hs_ref, gate_ref, pos_ref, cnt_ref):
    mod = mod_ref[0, 0]
    g = g_ref[...]
    y = _dot(y_ref[0], w_ref[...])
    xn = x_ref[0] + mod[2:3] * (_rms(y) * g[1:2])
    xo_ref[0] = xn
    h = _rms(xn) * g[2:3] * (1.0 + mod[4:5]) + mod[3:4]
    w_hi, w_lo = _split_bf16(wr_ref[...])
    h_hi, h_lo = _split_bf16(h)
    logits = _dot_nt(w_hi, h_hi) + _dot_nt(w_hi, h_lo) + _dot_nt(w_lo, h_hi) + br_ref[...]
    row = lax.broadcasted_iota(jnp.int32, logits.shape, 0)
    vals = logits
    tops, idxs = [], []
    for _ in range(TOP_K):
        m = jnp.max(vals, axis=0, keepdims=True)
        idx = jnp.min(jnp.where(vals == m, row, N_EXPERTS), axis=0, keepdims=True)
        tops.append(m)
        idxs.append(idx)
        vals = jnp.where(row == idx, -jnp.inf, vals)
    ex = [jnp.exp(tv - tops[0]) for tv in tops]
    den = ex[0] + ex[1] + ex[2] + ex[3]
    gate_ref[0] = jnp.concatenate([e / den for e in ex], axis=0)
    tm = logits.shape[1]
    before = jnp.where(lax.broadcasted_iota(jnp.int32, (tm, tm), 0) < lax.broadcasted_iota(jnp.int32, (tm, tm), 1),
                       1.0, 0.0).astype(BF16)
    counts = jnp.zeros((N_EXPERTS, 1), F32)
    onehots, ranks = [], []
    for idx in idxs:
        onehot = jnp.where(row == idx, 1.0, 0.0)
        ranks.append(_dot(onehot.astype(BF16), before) + counts)
        onehots.append(onehot)
        counts = counts + jnp.sum(onehot, axis=1, keepdims=True)
    counts_b = jnp.broadcast_to(counts, (N_EXPERTS, LANES))
    earlier = jnp.where(lax.broadcasted_iota(jnp.int32, (N_EXPERTS, N_EXPERTS), 0)
                        > lax.broadcasted_iota(jnp.int32, (N_EXPERTS, N_EXPERTS), 1), 1.0, 0.0).astype(BF16)
    first = _dot(earlier, counts_b.astype(BF16))[:, 0:1]
    pos = [jnp.sum(oh * (rk + first), axis=0, keepdims=True).astype(jnp.int32) for oh, rk in zip(onehots, ranks)]
    pos_ref[0] = jnp.concatenate(pos, axis=0)
    cnt_ref[0, 0] = counts_b.astype(jnp.int32)
    slot = lax.broadcasted_iota(jnp.int32, (TOP_K * tm, tm), 0)
    pick = jnp.where(slot == pos[0], 1.0, 0.0)
    for p in pos[1:]:
        pick = jnp.where(slot == p, 1.0, pick)
    hs_ref[0, 0] = _pack_halves(_dot(pick.astype(BF16), h.astype(BF16)))


def _out_proj_call(ymix, w_out, xc, modl, g, wr_t, br, n_lat, n_rows):
    b, n, d = xc.shape
    tm = TOKEN_TILE
    tok = lambda w: pl.BlockSpec((1, tm, w), lambda i, t: (i, t, 0))
    const2 = lambda a: pl.BlockSpec(a.shape, lambda i, t: (0, 0))
    sel = pl.BlockSpec((1, TOP_K, tm), lambda i, t: (i, 0, t))
    return pl.pallas_call(
        _out_proj_kernel,
        grid=(b, n_rows // tm),
        in_specs=[
            tok(ymix.shape[2]), const2(w_out), tok(d), _mod_spec(d, n_lat // tm),
            const2(g), const2(wr_t), const2(br),
        ],
        out_specs=[tok(d),
                   pl.BlockSpec((1, 1, TOP_K * tm, d // 2), lambda i, t: (i, t, 0, 0)),
                   sel, sel,
                   pl.BlockSpec((1, 1, N_EXPERTS, LANES), lambda i, t: (i, t, 0, 0))],
        out_shape=[
            jax.ShapeDtypeStruct((b, n_rows, d), F32),
            jax.ShapeDtypeStruct((b, n_rows // tm, TOP_K * tm, d // 2), jnp.uint32),
            jax.ShapeDtypeStruct((b, TOP_K, n_rows), F32),
            jax.ShapeDtypeStruct((b, TOP_K, n_rows), jnp.int32),
            jax.ShapeDtypeStruct((b, n_rows // tm, N_EXPERTS, LANES), jnp.int32),
        ],
        compiler_params=_params(("parallel", "parallel")),
    )(ymix, w_out, xc, modl, g, wr_t, br)


def _w1_prep_kernel(w_ref, g_ref, l_ref):
    r = lax.broadcasted_iota(jnp.int32, (MXU_DIM, MXU_DIM), 0)
    c = lax.broadcasted_iota(jnp.int32, (MXU_DIM, MXU_DIM), 1)
    src = jnp.where(c < LANES, 2 * c, 2 * (c - LANES) + 1)
    perm = jnp.where(r == src, 1.0, 0.0).astype(BF16)
    for j in range(w_ref.shape[3] // MXU_DIM):
        out = _dot(w_ref[0, 0, :, j * MXU_DIM:(j + 1) * MXU_DIM].astype(BF16), perm)
        g_ref[0, :, j * LANES:(j + 1) * LANES] = out[:, :LANES].astype(BF16)
        l_ref[0, :, j * LANES:(j + 1) * LANES] = out[:, LANES:].astype(BF16)


def _w1_prep_call(w1_all, layer):
    _, e, d, f2 = w1_all.shape
    f = f2 // 2
    rows = 256
    out_spec = pl.BlockSpec((1, rows, f), lambda i, j: (i, j, 0))
    return pl.pallas_call(
        _w1_prep_kernel,
        grid=(e, d // rows),
        in_specs=[pl.BlockSpec((1, 1, rows, f2), lambda i, j: (layer, i, j, 0))],
        out_specs=[out_spec, out_spec],
        out_shape=[jax.ShapeDtypeStruct((e, d, f), BF16)] * 2,
        compiler_params=_params(("parallel", "parallel")),
    )(w1_all)


def _expert_kernel(be_ref, nu_ref, x_ref, w1g_ref, w1l_ref, b1g_ref, b1l_ref, w2_ref, b2_ref, o_ref):
    del be_ref
    i = pl.program_id(0)

    @pl.when(i < nu_ref[0])
    def _():
        x = _unpack_halves(x_ref[...]).astype(BF16)
        glu = jnp.minimum(_dot(x, w1g_ref[0]) + b1g_ref[0], SWIGLU_LIMIT)
        lin = jnp.clip(_dot(x, w1l_ref[0]) + b1l_ref[0], -SWIGLU_LIMIT, SWIGLU_LIMIT)
        act = glu * _sigmoid(SWIGLU_ALPHA * glu) * (lin + 1.0)
        o_ref[...] = _pack_halves(_dot(act.astype(BF16), w2_ref[0, 0]) + b2_ref[0])

    @pl.when(i >= nu_ref[0])
    def _():
        o_ref[...] = jnp.zeros_like(o_ref)


def _expert_call(block_expert, n_used, x_rows, w1g, w1l, b1g, b1l, w2_all, layer, b2):
    r = x_rows.shape[0]
    d, f = w1g.shape[1:]
    tm = MOE_BLOCK
    wspec = lambda s: pl.BlockSpec((1,) + s, lambda i, be, nu: (be[i], 0, 0))
    return pl.pallas_call(
        _expert_kernel,
        grid_spec=pltpu.PrefetchScalarGridSpec(
            num_scalar_prefetch=2,
            grid=(r // tm,),
            in_specs=[
                pl.BlockSpec((tm, d // 2), lambda i, be, nu: (i, 0)),
                wspec((d, f)), wspec((d, f)), wspec((1, f)), wspec((1, f)),
                pl.BlockSpec((1, 1, f, d), lambda i, be, nu: (layer, be[i], 0, 0)),
                wspec((1, d)),
            ],
            out_specs=pl.BlockSpec((tm, d // 2), lambda i, be, nu: (i, 0)),
        ),
        out_shape=jax.ShapeDtypeStruct((r, d // 2), jnp.uint32),
        compiler_params=_params(("arbitrary",)),
    )(block_expert, n_used, x_rows, w1g, w1l, b1g, b1l, w2_all, b2)


def _combine_kernel(*refs, n_parts):
    x_ref, y_refs = refs[0], refs[1:1 + n_parts]
    gate_ref, pos_ref, mod_ref, g_ref, o_ref = refs[1 + n_parts:]
    gates, pos = gate_ref[0], pos_ref[0]
    lane = lax.broadcasted_iota(jnp.int32, (gates.shape[0], y_refs[0].shape[2]), 1)
    mix = jnp.zeros(lane.shape, F32)
    for k in range(TOP_K):
        mix = jnp.where(lane == pos[:, k:k + 1], gates[:, k:k + 1], mix)
    mix_hi, mix_lo = _split_bf16(mix)
    mod = mod_ref[0, 0]
    part_batches = pl.num_programs(0) // n_parts
    for p, y_ref in enumerate(y_refs):
        @pl.when(pl.program_id(0) // part_batches == p)
        def _(y_ref=y_ref):
            rows = _unpack_halves(y_ref[0, 0]).astype(BF16)
            y = _dot(mix_hi, rows) + _dot(mix_lo, rows)
            o_ref[0] = x_ref[0] + mod[5:6] * (_rms(y) * g_ref[...][3:4])


def _combine_call(xc, picked_parts, gates, pos, modl, g, n_lat):
    b, n, d = xc.shape
    tm = TOKEN_TILE
    n_parts = len(picked_parts)
    bp = b // n_parts
    tok = pl.BlockSpec((1, tm, d), lambda i, t: (i, t, 0))

    def part_spec(p):
        def index(i, t):
            live = (i >= p * bp) & (i < (p + 1) * bp)
            return jnp.clip(i - p * bp, 0, bp - 1), jnp.where(live, t, 0), 0, 0
        return pl.BlockSpec((1, 1, TOP_K * tm, d // 2), index)

    return pl.pallas_call(
        functools.partial(_combine_kernel, n_parts=n_parts),
        grid=(b, n // tm),
        in_specs=[tok] + [part_spec(p) for p in range(n_parts)] + [
            pl.BlockSpec((1, tm, TOP_K), lambda i, t: (i, t, 0)),
            pl.BlockSpec((1, tm, TOP_K), lambda i, t: (i, t, 0)),
            _mod_spec(d, n_lat // tm),
            pl.BlockSpec(g.shape, lambda i, t: (0, 0)),
        ],
        out_specs=tok,
        out_shape=jax.ShapeDtypeStruct((b, n, d), F32),
        compiler_params=_params(("parallel", "parallel")),
    )(xc, *picked_parts, gates, pos, modl, g)


def _rope_tables(rows, n_ctx, rot_dim):
    row = jnp.repeat(jnp.arange(rows), GRID_W)
    col = jnp.tile(jnp.arange(GRID_W), rows)
    n_freq = rot_dim // 4
    inv = ROPE_THETA ** (-jnp.arange(n_freq, dtype=F32) / n_freq)
    ang = jnp.concatenate([row[:, None] * inv, col[:, None] * inv], axis=-1)
    cos, sin = jnp.cos(ang), jnp.sin(ang)
    reps = 256 // rot_dim
    cos_t = jnp.tile(jnp.concatenate([cos, cos], axis=-1), (1, reps))
    sin_t = jnp.tile(jnp.concatenate([-sin, sin], axis=-1), (1, reps))
    cos_t = jnp.concatenate([cos_t, jnp.ones((n_ctx, 256), F32)], axis=0)
    sin_t = jnp.concatenate([sin_t, jnp.zeros((n_ctx, 256), F32)], axis=0)
    return cos_t, sin_t


def _pack_w_in(w):
    parts, src = [], 0
    for _, width, padded in IN_PIECES:
        parts.append(w[:, src:src + width])
        if padded > width:
            parts.append(jnp.zeros((w.shape[0], padded - width), w.dtype))
        src += width
    return jnp.concatenate(parts, axis=1).astype(BF16)


def _pack_w_uq(w):
    wh = w.reshape(MLA_Q_RANK, N_HEADS, MLA_NOPE + MLA_ROPE)
    nope = wh[:, :, :MLA_NOPE].reshape(MLA_Q_RANK, N_HEADS * MLA_NOPE)
    rope = wh[:, :, MLA_NOPE:].reshape(MLA_Q_RANK, N_HEADS * MLA_ROPE)
    packed = jnp.concatenate([nope, rope], axis=1)
    return jnp.pad(packed, ((0, 256 - MLA_Q_RANK), (0, 0))).astype(BF16)


def _pack_w_ukv(w):
    wh = w.reshape(MLA_KV_RANK, N_HEADS, MLA_NOPE + HEAD_DIM)
    kn = wh[:, :, :MLA_NOPE].reshape(MLA_KV_RANK, N_HEADS * MLA_NOPE)
    vv = wh[:, :, MLA_NOPE:].reshape(MLA_KV_RANK, N_HEADS * HEAD_DIM)
    return jnp.concatenate([kn, vv], axis=1).astype(BF16)


def _heads(a, n_heads):
    b, n, w = a.shape
    return a.reshape(b, n, n_heads, w // n_heads).transpose(0, 2, 1, 3)


def _heads_t(a, n_heads):
    b, n, w = a.shape
    return a.reshape(b, n, n_heads, w // n_heads).transpose(0, 2, 3, 1)


def _attention(q, kt, v, n_lat, n_ctx, ctx_out, diff=None):
    n = n_lat + n_ctx
    y = _attn_call(q, kt, v, (0, n_lat), (0, n), min(Q_TILE, n_lat), diff)
    if ctx_out:
        y_c = _attn_call(q, kt, v, (n_lat, n_ctx), (n_lat, n_ctx), n_ctx, diff)
    else:
        y_c = jnp.zeros((y.shape[0], n_ctx, y.shape[2]), y.dtype)
    return jnp.concatenate([y, y_c], axis=1)


def _heads_v(a, n_heads):
    h = _heads(a, n_heads)
    return jnp.concatenate([h, jnp.ones_like(h)], axis=-1)


def _steps(x, bounds, offsets):
    deltas = offsets - jnp.concatenate([jnp.zeros_like(offsets[..., :1]), offsets[..., :-1]], axis=-1)
    return x + jnp.sum(jnp.where(x[..., None] >= bounds[..., None, :], deltas[..., None, :], 0), axis=-1)


def _moe(x_new, hs, gates, pos, counts, modl, g, n_lat, w1_all, b1, w2_all, b2, layer):
    b, n, d = x_new.shape
    f = w2_all.shape[2]
    tile_rows = TOP_K * TOKEN_TILE
    n_parts = 2 if b % 2 == 0 else 1
    tiles = (b // n_parts) * (n // TOKEN_TILE)
    n_blocks = -(-(tiles * tile_rows) // MOE_BLOCK) + N_EXPERTS
    hs_rows = hs.reshape(b * (n // TOKEN_TILE) * tile_rows, d // 2)
    w1g, w1l = _w1_prep_call(w1_all, layer)
    b1g, b1l = b1[:, 0::2].reshape(N_EXPERTS, 1, f), b1[:, 1::2].reshape(N_EXPERTS, 1, f)
    picked_parts = []
    for p in range(n_parts):
        tile_counts = counts.reshape(-1, N_EXPERTS)[p * tiles:(p + 1) * tiles]
        tile_base = jnp.cumsum(tile_counts, axis=0) - tile_counts
        first = jnp.cumsum(tile_counts, axis=1) - tile_counts
        padded = (jnp.sum(tile_counts, axis=0) + MOE_BLOCK - 1) // MOE_BLOCK * MOE_BLOCK
        pad_ends = jnp.cumsum(padded)
        pad_start = pad_ends - padded
        block_start = jnp.arange(n_blocks, dtype=jnp.int32) * MOE_BLOCK
        block_expert = jnp.minimum(jnp.sum((pad_ends[None, :] <= block_start[:, None]).astype(jnp.int32), axis=1),
                                   N_EXPERTS - 1)
        n_used = (pad_ends[-1:] // MOE_BLOCK).astype(jnp.int32)
        j = block_start[:, None] + jnp.arange(MOE_BLOCK, dtype=jnp.int32)[None, :] - pad_start[block_expert][:, None]
        tile_offset = (jnp.arange(tiles, dtype=jnp.int32)[:, None] + p * tiles) * tile_rows + first - tile_base
        src = _steps(j, tile_base.T[block_expert], tile_offset.T[block_expert])
        src = jnp.clip(src, 0, hs_rows.shape[0] - 1)
        x_rows = hs_rows.at[src.reshape(-1)].get(mode="promise_in_bounds")
        y_rows = _expert_call(block_expert, n_used, x_rows, w1g, w1l, b1g, b1l, w2_all, layer,
                              b2.reshape(N_EXPERTS, 1, d))
        slot = jnp.broadcast_to(jnp.arange(tile_rows, dtype=jnp.int32)[None, :], (tiles, tile_rows))
        back = _steps(slot, first, tile_base + pad_start[None, :] - first)
        picked = y_rows.at[back.reshape(-1)].get(mode="promise_in_bounds")
        picked_parts.append(picked.reshape(b // n_parts, n // TOKEN_TILE, tile_rows, d // 2))
    return _combine_call(x_new, picked_parts, gates.transpose(0, 2, 1), pos.transpose(0, 2, 1), modl, g, n_lat)


def kernel(x, c, ctx, c_ctx, ada_w, ada_b, norm_g, w_in, w_out, ret_log_decay, ret_gn_w, ret_gn_b,
           diff_lambda, diff_subln, gqa_qk_norm, mla_q_norm, mla_kv_norm, mla_w_uq, mla_w_ukv,
           router_w, router_b, exp_w1, exp_b1, exp_w2, exp_b2):
    b, s, d = x.shape
    n_ctx = ctx.shape[1]
    n = s + n_ctx
    depth = ada_w.shape[0]
    assert n_ctx % TOKEN_TILE == 0 and s % TOKEN_TILE == 0 and s % GRID_W == 0
    assert n_ctx % RET_CHUNK == 0 and s % min(Q_TILE, s) == 0 and s % n_ctx == 0

    tables = _rope_tables(s // GRID_W, n_ctx, HEAD_DIM) + _rope_tables(s // GRID_W, n_ctx, DIFF_D)
    c_rows = jnp.zeros((16, d), F32).at[:b].set(c).at[b].set(c_ctx)
    mods = _ada_call(c_rows, ada_w, ada_b)
    w2_all = exp_w2.astype(BF16)
    xc = jnp.concatenate([x, ctx], axis=1)

    for l in range(depth):
        last = l == depth - 1
        lam_init = 0.8 - 0.6 * math.exp(-0.3 * l)
        mod_lat = mods[l, :b].reshape(b, 1, 6, d)
        mod_ctx = jnp.broadcast_to(mods[l, b].reshape(1, 1, 6, d), (b, 1, 6, d))
        modl = jnp.concatenate([mod_ctx, mod_lat], axis=1)

        (rq, rk, rv, rg, dq, dk, dv, gq, gk, gv, mqn, mqr, mkn, mv, mkr) = _in_proj_call(
            xc, modl, norm_g[l, 0:1], _pack_w_in(w_in[l]), tables,
            jnp.tile(gqa_qk_norm[l, 0], N_HEADS)[None, :], jnp.tile(gqa_qk_norm[l, 1], GQA_KV_HEADS)[None, :],
            jnp.pad(mla_q_norm[l], (0, 256 - MLA_Q_RANK))[None, :], mla_kv_norm[l][None, :],
            _pack_w_uq(mla_w_uq[l]), _pack_w_ukv(mla_w_ukv[l]), s)

        log_g = -jnp.exp(ret_log_decay[l].astype(F32))
        o_f = _ret_call(rq, rk, rv, _ret_tables(log_g[0], False), s, False)
        ret_y = _ret_call(rq, rk, rv, _ret_tables(log_g[1], True), s, True,
                          (o_f, rg, ret_gn_w[l][None, :], ret_gn_b[l][None, :]))

        lp = diff_lambda[l].astype(F32)
        lam = jnp.exp(jnp.sum(lp[0] * lp[1])) - jnp.exp(jnp.sum(lp[2] * lp[3])) + lam_init
        dif_y = _attention(_heads(dq, 2 * N_HEADS), _heads_t(dk, 2 * N_HEADS), _heads_v(dv, N_HEADS), s, n_ctx,
                           not last, (lam, diff_subln[l], 1.0 - lam_init))

        gqa_y = _attention(_heads(gq, N_HEADS), _heads_t(gk, GQA_KV_HEADS), _heads_v(gv, GQA_KV_HEADS), s, n_ctx,
                           not last)

        q_m = jnp.concatenate([_heads(mqn, N_HEADS), _heads(mqr, N_HEADS)], axis=-1)
        kr_t = jnp.broadcast_to(mkr[:, None, :, :MLA_ROPE].transpose(0, 1, 3, 2), (b, N_HEADS, MLA_ROPE, n))
        kt_m = jnp.concatenate([_heads_t(mkn, N_HEADS), kr_t], axis=2)
        mla_y = _attention(q_m, kt_m, _heads_v(mv, N_HEADS), s, n_ctx, not last)

        ymix = jnp.concatenate([ret_y, dif_y, gqa_y, mla_y], axis=-1)
        n_rows = s if last else n
        x_new, hs, gates, pos, counts = _out_proj_call(
            ymix, w_out[l].astype(BF16), xc, modl, norm_g[l], router_w[l].T, router_b[l][:, None], s, n_rows)
        xc = _moe(x_new, hs, gates, pos, counts[..., 0], modl, norm_g[l], s,
                  exp_w1, exp_b1[l], w2_all, exp_b2[l], l)

    return xc[:, :s]
```

```python
import functools
import math

import jax
import jax.numpy as jnp
from jax import lax
from jax.experimental import pallas as pl
from jax.experimental.pallas import tpu as pltpu

F32 = jnp.float32
BF16 = jnp.bfloat16

GRID_W = 64
ROPE_THETA = 10000.0
EPS = 1e-6
GROUP_WIDTH = 256
HEAD_DIM = 64
N_HEADS = 4
RET_CHUNK = 256
DIFF_D = 32
GQA_KV_HEADS = 2
MLA_Q_RANK = 192
MLA_KV_RANK = 128
MLA_NOPE = 64
MLA_ROPE = 32
N_EXPERTS = 32
TOP_K = 4
SWIGLU_LIMIT = 7.0
SWIGLU_ALPHA = 1.702
MOE_BLOCK = 512

LANES = 128
MXU_DIM = 256
TOKEN_TILE = 256
Q_TILE = 1024
KV_TILE = 512
VMEM_LIMIT = 48 * 1024 * 1024
LOG2E = math.log2(math.e)

IN_PIECES = (
    ("ret_q", 256, 256), ("ret_k", 256, 256), ("ret_v", 256, 256), ("ret_g", 256, 256),
    ("dif_q", 256, 256), ("dif_k", 256, 256), ("dif_v", 256, 256),
    ("gqa_q", 256, 256), ("gqa_k", 128, 128), ("gqa_v", 128, 128),
    ("mla_cq", MLA_Q_RANK, 256), ("mla_ckv", MLA_KV_RANK, 128), ("mla_kr", MLA_ROPE, 128),
)
IN_OFFSETS = {}
_off = 0
for _name, _w, _pw in IN_PIECES:
    IN_OFFSETS[_name] = _off
    _off += _pw
IN_PACKED_WIDTH = _off


def _params(sem):
    return pltpu.CompilerParams(dimension_semantics=sem, vmem_limit_bytes=VMEM_LIMIT)


def _rms(x):
    return x * lax.rsqrt(jnp.mean(x * x, axis=-1, keepdims=True) + EPS)


def _split_bf16(a):
    hi = a.astype(BF16)
    lo = (a - hi.astype(F32)).astype(BF16)
    return hi, lo


def _dot_nt(a, b):
    return lax.dot_general(a, b, (((1,), (1,)), ((), ())), preferred_element_type=F32)


def _dot(a, b):
    return jnp.dot(a, b, preferred_element_type=F32)


def _sigmoid(a):
    return 1.0 / (1.0 + jnp.exp(-a))


def _pack_halves(a):
    w = a.shape[1] // 2
    bits = lax.bitcast_convert_type(a.astype(BF16).astype(F32), jnp.uint32)
    return bits[:, :w] | (bits[:, w:] >> 16)


def _unpack_halves(u):
    hi = lax.bitcast_convert_type(u & jnp.uint32(0xFFFF0000), F32)
    lo = lax.bitcast_convert_type(u << 16, F32)
    return jnp.concatenate([hi, lo], axis=1)


def _mod_spec(d, n_lat_tiles):
    return pl.BlockSpec((1, 1, 6, d), lambda i, t: (i, jnp.where(t < n_lat_tiles, 1, 0), 0, 0))


def _ada_kernel(c_ref, w_ref, b_ref, o_ref):
    s = c_ref[...]
    s = s * _sigmoid(s)
    s_hi, s_lo = _split_bf16(s)
    w_hi, w_lo = _split_bf16(w_ref[0])
    o_ref[0] = _dot(s_hi, w_hi) + _dot(s_hi, w_lo) + _dot(s_lo, w_hi) + b_ref[0]


def _ada_call(c_rows, ada_w, ada_b):
    depth, d, n6 = ada_w.shape
    rows = c_rows.shape[0]
    tn = 1536
    return pl.pallas_call(
        _ada_kernel,
        grid=(depth, n6 // tn),
        in_specs=[
            pl.BlockSpec((rows, d), lambda l, j: (0, 0)),
            pl.BlockSpec((1, d, tn), lambda l, j: (l, 0, j)),
            pl.BlockSpec((1, 1, tn), lambda l, j: (l, 0, j)),
        ],
        out_specs=pl.BlockSpec((1, rows, tn), lambda l, j: (l, 0, j)),
        out_shape=jax.ShapeDtypeStruct((depth, rows, n6), F32),
        compiler_params=_params(("parallel", "parallel")),
    )(c_rows, ada_w, ada_b.reshape(depth, 1, n6))


def _rope(x, cos, sin_signed, half):
    outs = []
    for c in range(x.shape[1] // LANES):
        sl = slice(c * LANES, (c + 1) * LANES)
        xc = x[:, sl]
        lane = lax.broadcasted_iota(jnp.int32, xc.shape, 1)
        first_half = (lane % (2 * half)) < half
        partner = jnp.where(first_half, pltpu.roll(xc, LANES - half, 1), pltpu.roll(xc, half, 1))
        outs.append(xc * cos[:, sl] + partner * sin_signed[:, sl])
    return outs[0] if len(outs) == 1 else jnp.concatenate(outs, axis=1)


def _group_mean(x, gsize):
    w = x.shape[1]
    r = lax.broadcasted_iota(jnp.int32, (w, w), 0) // gsize
    c = lax.broadcasted_iota(jnp.int32, (w, w), 1) // gsize
    ones = jnp.where(r == c, 1.0, 0.0).astype(BF16)
    hi, lo = _split_bf16(x)
    return (_dot(hi, ones) + _dot(lo, ones)) * (1.0 / gsize)


def _group_mean_sq(x, gsize):
    return _group_mean(x * x, gsize)


def _in_proj_kernel(x_ref, mod_ref, g_ref, w_ref, cos64_ref, sin64_ref, cos32_ref, sin32_ref,
                    gqn_ref, gkn_ref, mqn_ref, mkvn_ref, wuq_ref, wukv_ref,
                    rq_ref, rk_ref, rv_ref, rg_ref, dq_ref, dk_ref, dv_ref, gq_ref, gk_ref, gv_ref,
                    mqn_o, mqr_o, mkn_o, mv_o, mkr_o):
    mod = mod_ref[0, 0]
    h = _rms(x_ref[0]) * g_ref[...] * (1.0 + mod[1:2]) + mod[0:1]
    hb = h.astype(BF16)
    cos64, sin64 = cos64_ref[...], sin64_ref[...]
    cos32, sin32 = cos32_ref[...], sin32_ref[...]

    def proj(name, width):
        o = IN_OFFSETS[name]
        return _dot(hb, w_ref[:, o:o + width])

    rq_ref[0] = _rope(proj("ret_q", 256), cos64, sin64, 32).astype(BF16)
    rk_ref[0] = (_rope(proj("ret_k", 256), cos64, sin64, 32) * (HEAD_DIM ** -0.5)).astype(BF16)
    rv_ref[0] = proj("ret_v", 256).astype(BF16)
    rg_ref[0] = proj("ret_g", 256)
    dq_ref[0] = (_rope(proj("dif_q", 256), cos32, sin32, 16) * (DIFF_D ** -0.5 * LOG2E)).astype(BF16)
    dk_ref[0] = _rope(proj("dif_k", 256), cos32, sin32, 16).astype(BF16)
    dv_ref[0] = proj("dif_v", 256).astype(BF16)
    gq = proj("gqa_q", 256)
    gq = gq * lax.rsqrt(_group_mean_sq(gq, HEAD_DIM) + EPS) * gqn_ref[...]
    gq_ref[0] = (_rope(gq, cos64, sin64, 32) * (HEAD_DIM ** -0.5 * LOG2E)).astype(BF16)
    gk = proj("gqa_k", 128)
    gk = gk * lax.rsqrt(_group_mean_sq(gk, HEAD_DIM) + EPS) * gkn_ref[...]
    gk_ref[0] = _rope(gk, cos64[:, :LANES], sin64[:, :LANES], 32).astype(BF16)
    gv_ref[0] = proj("gqa_v", 128).astype(BF16)
    cq = proj("mla_cq", 256)
    cq = cq * lax.rsqrt(jnp.sum(cq * cq, axis=-1, keepdims=True) * (1.0 / MLA_Q_RANK) + EPS) * mqn_ref[...]
    q_up = _dot(cq.astype(BF16), wuq_ref[...])
    mla_scale = (MLA_NOPE + MLA_ROPE) ** -0.5 * LOG2E
    mqn_o[0] = (q_up[:, :256] * mla_scale).astype(BF16)
    mqr_o[0] = (_rope(q_up[:, 256:], cos32[:, :LANES], sin32[:, :LANES], 16) * mla_scale).astype(BF16)
    ckv = proj("mla_ckv", 128)
    ckv = ckv * lax.rsqrt(jnp.mean(ckv * ckv, axis=-1, keepdims=True) + EPS) * mkvn_ref[...]
    kv_up = _dot(ckv.astype(BF16), wukv_ref[...])
    mkn_o[0] = kv_up[:, :256].astype(BF16)
    mv_o[0] = kv_up[:, 256:].astype(BF16)
    mkr_o[0] = _rope(proj("mla_kr", 128), cos32[:, :LANES], sin32[:, :LANES], 16).astype(BF16)


def _in_proj_call(xc, modl, g0, w_in_p, tables, gqn, gkn, mqn, mkvn, wuq_p, wukv_p, n_lat):
    b, n, d = xc.shape
    tm = TOKEN_TILE
    tok = lambda w: pl.BlockSpec((1, tm, w), lambda i, t: (i, t, 0))
    const2 = lambda a: pl.BlockSpec(a.shape, lambda i, t: (0, 0))
    tab = pl.BlockSpec((tm, 256), lambda i, t: (t, 0))
    out_widths = (256, 256, 256, 256, 256, 256, 256, 256, 128, 128, 256, 128, 256, 256, 128)
    out_dtypes = (BF16, BF16, BF16, F32, BF16, BF16, BF16, BF16, BF16, BF16, BF16, BF16, BF16, BF16, BF16)
    return pl.pallas_call(
        _in_proj_kernel,
        grid=(b, n // tm),
        in_specs=[
            tok(d), _mod_spec(d, n_lat // tm),
            const2(g0), const2(w_in_p), tab, tab, tab, tab,
            const2(gqn), const2(gkn), const2(mqn), const2(mkvn), const2(wuq_p), const2(wukv_p),
        ],
        out_specs=[tok(w) for w in out_widths],
        out_shape=[jax.ShapeDtypeStruct((b, n, w), dt) for w, dt in zip(out_widths, out_dtypes)],
        compiler_params=_params(("parallel", "parallel")),
    )(xc, modl, g0, w_in_p, *tables, gqn, gkn, mqn, mkvn, wuq_p, wukv_p)


def _ret_kernel(*refs, readout):
    if readout:
        (q_ref, k_ref, v_ref, dec_ref, xi_ref, zeta_ref, gc_ref, of_ref, g_ref, gnw_ref, gnb_ref,
         o_ref, state_ref) = refs
    else:
        q_ref, k_ref, v_ref, dec_ref, xi_ref, zeta_ref, gc_ref, o_ref, state_ref = refs

    @pl.when(pl.program_id(1) == 0)
    def _():
        state_ref[...] = jnp.zeros_like(state_ref)

    q, k, v = q_ref[0], k_ref[0], v_ref[0]
    outs = []
    for h in range(N_HEADS):
        sl = slice(h * HEAD_DIM, (h + 1) * HEAD_DIM)
        qh, kh, vh = q[:, sl], k[:, sl], v[:, sl]
        st = state_ref[h]
        inner = _dot_nt(qh, kh) * dec_ref[h]
        o = _dot(inner.astype(BF16), vh) + _dot(qh, st.astype(BF16)) * xi_ref[h]
        kz = (kh.astype(F32) * zeta_ref[h]).astype(BF16)
        kv = lax.dot_general(kz, vh, (((0,), (0,)), ((), ())), preferred_element_type=F32)
        state_ref[h] = gc_ref[h] * st + kv
        outs.append(o)
    o = jnp.concatenate(outs, axis=1)
    if readout:
        o = o + of_ref[0]
        o = o - _group_mean(o, HEAD_DIM)
        o = o * lax.rsqrt(_group_mean_sq(o, HEAD_DIM) + EPS)
        g = g_ref[0]
        o_ref[0] = ((o * gnw_ref[...] + gnb_ref[...]) * (g * _sigmoid(g))).astype(o_ref.dtype)
    else:
        o_ref[0] = o


def _ret_call(q, k, v, tabs, n_lat, backward, readout_args=None):
    b, n, w = q.shape
    c = RET_CHUNK
    nch = n // c
    nlc = n_lat // c
    if backward:
        chunk = lambda t: nch - 1 - t
    else:
        chunk = lambda t: jnp.where(t < nch - nlc, nlc + t, t - (nch - nlc))
    tok = lambda dt_w: pl.BlockSpec((1, c, dt_w), lambda i, t: (i, chunk(t), 0))
    const = lambda a: pl.BlockSpec(a.shape, lambda i, t: (0,) * a.ndim)
    in_specs = [tok(w), tok(w), tok(w)] + [const(a) for a in tabs]
    args = [q, k, v, *tabs]
    if readout_args is not None:
        o_f, g, gnw, gnb = readout_args
        in_specs += [tok(w), tok(w), const(gnw), const(gnb)]
        args += [o_f, g, gnw, gnb]
    out_dtype = BF16 if readout_args is not None else F32
    return pl.pallas_call(
        functools.partial(_ret_kernel, readout=readout_args is not None),
        grid=(b, nch),
        in_specs=in_specs,
        out_specs=tok(w),
        out_shape=jax.ShapeDtypeStruct((b, n, w), out_dtype),
        scratch_shapes=[pltpu.VMEM((N_HEADS, HEAD_DIM, HEAD_DIM), F32)],
        compiler_params=_params(("parallel", "arbitrary")),
    )(*args)


def _ret_tables(log_g, backward):
    c = RET_CHUNK
    pos = jnp.arange(c, dtype=F32)
    dist = (pos[None, :] - pos[:, None]) if backward else (pos[:, None] - pos[None, :])
    lg = log_g[:, None, None]
    decay = jnp.where(dist >= 0, jnp.exp(lg * jnp.maximum(dist, 0.0)), 0.0)
    to_state = (c - pos) if backward else (pos + 1.0)
    to_end = pos if backward else (c - 1.0 - pos)
    xi = jnp.exp(log_g[:, None] * to_state)[:, :, None]
    zeta = jnp.exp(log_g[:, None] * to_end)[:, :, None]
    ones = jnp.ones((1, 1, HEAD_DIM), F32)
    gc = jnp.exp(log_g * c)[:, None, None] * jnp.ones((1, HEAD_DIM, HEAD_DIM), F32)
    return decay, xi * ones, zeta * ones, gc


def _attn_kernel(*refs, n_maps, readout_scale):
    if n_maps == 2:
        lam_ref, q_ref, kt_ref, v_ref, sub_ref, o_ref, first_ref = refs
    else:
        q_ref, kt_ref, v_ref, o_ref = refs
    hm = pl.program_id(2)
    q = q_ref[0, 0]
    tq = q.shape[0]
    n_keys = kt_ref.shape[3]
    dv = v_ref.shape[3] // 2
    tk = min(KV_TILE, n_keys)
    n_full, tail = divmod(n_keys, tk)

    def step(off, size, carry):
        m, acc = carry
        s = _dot(q, kt_ref[0, 0, :, pl.ds(off, size)])
        m_new = jnp.maximum(m, jnp.max(s, axis=-1, keepdims=True))
        p = jnp.exp2(s - m_new)
        acc = jnp.exp2(m - m_new) * acc + _dot(p.astype(BF16), v_ref[0, 0, pl.ds(off, size), :])
        return m_new, acc

    carry = (jnp.full((tq, 1), -1e30, F32), jnp.zeros((tq, 2 * dv), F32))
    for j in range(n_full):
        carry = step(j * tk, tk, carry)
    if tail:
        carry = step(n_full * tk, tail, carry)
    acc = carry[1]
    o = acc[:, :dv] / acc[:, dv:]

    def store(head_of_step, val):
        for hh in range(N_HEADS):
            @pl.when(head_of_step == hh)
            def _(hh=hh):
                o_ref[0, :, hh * dv:(hh + 1) * dv] = val.astype(o_ref.dtype)

    if n_maps == 1:
        store(hm, o)
    else:
        @pl.when(hm % 2 == 0)
        def _():
            first_ref[...] = o

        @pl.when(hm % 2 == 1)
        def _():
            od = first_ref[...] - lam_ref[0] * o
            od = od * lax.rsqrt(jnp.mean(od * od, axis=-1, keepdims=True) + EPS) * sub_ref[...]
            store(hm // 2, od * readout_scale)


def _attn_call(q, kt, v, rows, keys, tq, diff=None):
    b, hq, n, d = q.shape
    hk, hv, dv = kt.shape[1], v.shape[1], v.shape[3] // 2
    (q0, qn), (k0, kn) = rows, keys
    assert q0 % tq == 0 and qn % tq == 0 and k0 % kn == 0
    n_maps = 2 if diff is not None else 1
    in_specs = [
        pl.BlockSpec((1, 1, tq, d), lambda i, t, h: (i, h, q0 // tq + t, 0)),
        pl.BlockSpec((1, 1, d, kn), lambda i, t, h: (i, h // (hq // hk), 0, k0 // kn)),
        pl.BlockSpec((1, 1, kn, 2 * dv), lambda i, t, h: (i, h // (hq // hv), k0 // kn, 0)),
    ]
    args = [q, kt, v]
    scratch = []
    scale = 1.0
    if diff is not None:
        lam, subln, scale = diff
        in_specs = [pl.BlockSpec(memory_space=pltpu.SMEM)] + in_specs + [pl.BlockSpec((1, dv), lambda i, t, h: (0, 0))]
        args = [lam.reshape(1)] + args + [subln.reshape(1, dv)]
        scratch = [pltpu.VMEM((tq, dv), F32)]
    return pl.pallas_call(
        functools.partial(_attn_kernel, n_maps=n_maps, readout_scale=scale),
        grid=(b, qn // tq, hq),
        in_specs=in_specs,
        out_specs=pl.BlockSpec((1, tq, (hq // n_maps) * dv), lambda i, t, h: (i, t, 0)),
        out_shape=jax.ShapeDtypeStruct((b, qn, (hq // n_maps) * dv), BF16),
        scratch_shapes=scratch,
        compiler_params=_params(("parallel", "parallel", "arbitrary")),
    )(*args)


def _out_proj_kernel(y_ref, w_ref, x_ref, mod_ref, g_ref, wr_ref, br_ref,
                     xo_ref, hs_ref, gate_ref, pos_ref, cnt_ref):
    mod = mod_ref[0, 0]
    g = g_ref[...]
    y = _dot(y_ref[0], w_ref[...])
    xn = x_ref[0] + mod[2:3] * (_rms(y) * g[1:2])
    xo_ref[0] = xn
    h = _rms(xn) * g[2:3] * (1.0 + mod[4:5]) + mod[3:4]
    w_hi, w_lo = _split_bf16(wr_ref[...])
    h_hi, h_lo = _split_bf16(h)
    logits = _dot_nt(w_hi, h_hi) + _dot_nt(w_hi, h_lo) + _dot_nt(w_lo, h_hi) + br_ref[...]
    row = lax.broadcasted_iota(jnp.int32, logits.shape, 0)
    vals = logits
    tops, idxs = [], []
    for _ in range(TOP_K):
        m = jnp.max(vals, axis=0, keepdims=True)
        idx = jnp.min(jnp.where(vals == m, row, N_EXPERTS), axis=0, keepdims=True)
        tops.append(m)
        idxs.append(idx)
        vals = jnp.where(row == idx, -jnp.inf, vals)
    ex = [jnp.exp(tv - tops[0]) for tv in tops]
    den = ex[0] + ex[1] + ex[2] + ex[3]
    gate_ref[0] = jnp.concatenate([e / den for e in ex], axis=0)
    tm = logits.shape[1]
    before = jnp.where(lax.broadcasted_iota(jnp.int32, (tm, tm), 0) < lax.broadcasted_iota(jnp.int32, (tm, tm), 1),
                       1.0, 0.0).astype(BF16)
    counts = jnp.zeros((N_EXPERTS, 1), F32)
    onehots, ranks = [], []
    for idx in idxs:
        onehot = jnp.where(row == idx, 1.0, 0.0)
        ranks.append(_dot(onehot.astype(BF16), before) + counts)
        onehots.append(onehot)
        counts = counts + jnp.sum(onehot, axis=1, keepdims=True)
    counts_b = jnp.broadcast_to(counts, (N_EXPERTS, LANES))
    earlier = jnp.where(lax.broadcasted_iota(jnp.int32, (N_EXPERTS, N_EXPERTS), 0)
                        > lax.broadcasted_iota(jnp.int32, (N_EXPERTS, N_EXPERTS), 1), 1.0, 0.0).astype(BF16)
    first = _dot(earlier, counts_b.astype(BF16))[:, 0:1]
    pos = [jnp.sum(oh * (rk + first), axis=0, keepdims=True).astype(jnp.int32) for oh, rk in zip(onehots, ranks)]
    pos_ref[0] = jnp.concatenate(pos, axis=0)
    cnt_ref[0, 0] = counts_b.astype(jnp.int32)
    slot = lax.broadcasted_iota(jnp.int32, (TOP_K * tm, tm), 0)
    pick = jnp.where(slot == pos[0], 1.0, 0.0)
    for p in pos[1:]:
        pick = jnp.where(slot == p, 1.0, pick)
    hs_ref[0, 0] = _pack_halves(_dot(pick.astype(BF16), h.astype(BF16)))


def _out_proj_call(ymix, w_out, xc, modl, g, wr_t, br, n_lat, n_rows):
    b, n, d = xc.shape
    tm = TOKEN_TILE
    tok = lambda w: pl.BlockSpec((1, tm, w), lambda i, t: (i, t, 0))
    const2 = lambda a: pl.BlockSpec(a.shape, lambda i, t: (0, 0))
    sel = pl.BlockSpec((1, TOP_K, tm), lambda i, t: (i, 0, t))
    return pl.pallas_call(
        _out_proj_kernel,
        grid=(b, n_rows // tm),
        in_specs=[
            tok(ymix.shape[2]), const2(w_out), tok(d), _mod_spec(d, n_lat // tm),
            const2(g), const2(wr_t), const2(br),
        ],
        out_specs=[tok(d),
                   pl.BlockSpec((1, 1, TOP_K * tm, d // 2), lambda i, t: (i, t, 0, 0)),
                   sel, sel,
                   pl.BlockSpec((1, 1, N_EXPERTS, LANES), lambda i, t: (i, t, 0, 0))],
        out_shape=[
            jax.ShapeDtypeStruct((b, n_rows, d), F32),
            jax.ShapeDtypeStruct((b, n_rows // tm, TOP_K * tm, d // 2), jnp.uint32),
            jax.ShapeDtypeStruct((b, TOP_K, n_rows), F32),
            jax.ShapeDtypeStruct((b, TOP_K, n_rows), jnp.int32),
            jax.ShapeDtypeStruct((b, n_rows // tm, N_EXPERTS, LANES), jnp.int32),
        ],
        compiler_params=_params(("parallel", "parallel")),
    )(ymix, w_out, xc, modl, g, wr_t, br)


def _w1_prep_kernel(w_ref, g_ref, l_ref):
    r = lax.broadcasted_iota(jnp.int32, (MXU_DIM, MXU_DIM), 0)
    c = lax.broadcasted_iota(jnp.int32, (MXU_DIM, MXU_DIM), 1)
    src = jnp.where(c < LANES, 2 * c, 2 * (c - LANES) + 1)
    perm = jnp.where(r == src, 1.0, 0.0).astype(BF16)
    for j in range(w_ref.shape[3] // MXU_DIM):
        out = _dot(w_ref[0, 0, :, j * MXU_DIM:(j + 1) * MXU_DIM].astype(BF16), perm)
        g_ref[0, :, j * LANES:(j + 1) * LANES] = out[:, :LANES].astype(BF16)
        l_ref[0, :, j * LANES:(j + 1) * LANES] = out[:, LANES:].astype(BF16)


def _w1_prep_call(w1_all, layer):
    _, e, d, f2 = w1_all.shape
    f = f2 // 2
    rows = 256
    out_spec = pl.BlockSpec((1, rows, f), lambda i, j: (i, j, 0))
    return pl.pallas_call(
        _w1_prep_kernel,
        grid=(e, d // rows),
        in_specs=[pl.BlockSpec((1, 1, rows, f2), lambda i, j: (layer, i, j, 0))],
        out_specs=[out_spec, out_spec],
        out_shape=[jax.ShapeDtypeStruct((e, d, f), BF16)] * 2,
        compiler_params=_params(("parallel", "parallel")),
    )(w1_all)


def _expert_kernel(be_ref, nu_ref, x_ref, w1g_ref, w1l_ref, b1g_ref, b1l_ref, w2_ref, b2_ref, o_ref, w2b_ref):
    i = pl.program_id(0)

    @pl.when(jnp.logical_or(i == 0, be_ref[i] != be_ref[jnp.maximum(i - 1, 0)]))
    def _():
        w2b_ref[...] = w2_ref[0, 0].astype(BF16)

    @pl.when(i < nu_ref[0])
    def _():
        x = _unpack_halves(x_ref[...]).astype(BF16)
        glu = jnp.minimum(_dot(x, w1g_ref[0]) + b1g_ref[0], SWIGLU_LIMIT)
        lin = jnp.clip(_dot(x, w1l_ref[0]) + b1l_ref[0], -SWIGLU_LIMIT, SWIGLU_LIMIT)
        act = glu * _sigmoid(SWIGLU_ALPHA * glu) * (lin + 1.0)
        o_ref[...] = _pack_halves(_dot(act.astype(BF16), w2b_ref[...]) + b2_ref[0])

    @pl.when(i >= nu_ref[0])
    def _():
        o_ref[...] = jnp.zeros_like(o_ref)


def _expert_call(block_expert, n_used, x_rows, w1g, w1l, b1g, b1l, w2_all, layer, b2):
    r = x_rows.shape[0]
    d, f = w1g.shape[1:]
    tm = MOE_BLOCK
    wspec = lambda s: pl.BlockSpec((1,) + s, lambda i, be, nu: (be[i], 0, 0))
    return pl.pallas_call(
        _expert_kernel,
        grid_spec=pltpu.PrefetchScalarGridSpec(
            num_scalar_prefetch=2,
            grid=(r // tm,),
            in_specs=[
                pl.BlockSpec((tm, d // 2), lambda i, be, nu: (i, 0)),
                wspec((d, f)), wspec((d, f)), wspec((1, f)), wspec((1, f)),
                pl.BlockSpec((1, 1, f, d), lambda i, be, nu: (layer, be[i], 0, 0)),
                wspec((1, d)),
            ],
            out_specs=pl.BlockSpec((tm, d // 2), lambda i, be, nu: (i, 0)),
            scratch_shapes=[pltpu.VMEM((f, d), BF16)],
        ),
        out_shape=jax.ShapeDtypeStruct((r, d // 2), jnp.uint32),
        compiler_params=_params(("arbitrary",)),
    )(block_expert, n_used, x_rows, w1g, w1l, b1g, b1l, w2_all, b2)


def _combine_kernel(*refs, n_parts):
    x_ref, y_refs = refs[0], refs[1:1 + n_parts]
    gate_ref, pos_ref, mod_ref, g_ref, o_ref = refs[1 + n_parts:]
    gates, pos = gate_ref[0], pos_ref[0]
    lane = lax.broadcasted_iota(jnp.int32, (gates.shape[0], y_refs[0].shape[2]), 1)
    mix = jnp.zeros(lane.shape, F32)
    for k in range(TOP_K):
        mix = jnp.where(lane == pos[:, k:k + 1], gates[:, k:k + 1], mix)
    mix_hi, mix_lo = _split_bf16(mix)
    mod = mod_ref[0, 0]
    part_batches = pl.num_programs(0) // n_parts
    for p, y_ref in enumerate(y_refs):
        @pl.when(pl.program_id(0) // part_batches == p)
        def _(y_ref=y_ref):
            rows = _unpack_halves(y_ref[0, 0]).astype(BF16)
            y = _dot(mix_hi, rows) + _dot(mix_lo, rows)
            o_ref[0] = x_ref[0] + mod[5:6] * (_rms(y) * g_ref[...][3:4])


def _combine_call(xc, picked_parts, gates, pos, modl, g, n_lat):
    b, n, d = xc.shape
    tm = TOKEN_TILE
    n_parts = len(picked_parts)
    bp = b // n_parts
    tok = pl.BlockSpec((1, tm, d), lambda i, t: (i, t, 0))

    def part_spec(p):
        def index(i, t):
            live = (i >= p * bp) & (i < (p + 1) * bp)
            return jnp.clip(i - p * bp, 0, bp - 1), jnp.where(live, t, 0), 0, 0
        return pl.BlockSpec((1, 1, TOP_K * tm, d // 2), index)

    return pl.pallas_call(
        functools.partial(_combine_kernel, n_parts=n_parts),
        grid=(b, n // tm),
        in_specs=[tok] + [part_spec(p) for p in range(n_parts)] + [
            pl.BlockSpec((1, tm, TOP_K), lambda i, t: (i, t, 0)),
            pl.BlockSpec((1, tm, TOP_K), lambda i, t: (i, t, 0)),
            _mod_spec(d, n_lat // tm),
            pl.BlockSpec(g.shape, lambda i, t: (0, 0)),
        ],
        out_specs=tok,
        out_shape=jax.ShapeDtypeStruct((b, n, d), F32),
        compiler_params=_params(("parallel", "parallel")),
    )(xc, *picked_parts, gates, pos, modl, g)


def _rope_tables(rows, n_ctx, rot_dim):
    row = jnp.repeat(jnp.arange(rows), GRID_W)
    col = jnp.tile(jnp.arange(GRID_W), rows)
    n_freq = rot_dim // 4
    inv = ROPE_THETA ** (-jnp.arange(n_freq, dtype=F32) / n_freq)
    ang = jnp.concatenate([row[:, None] * inv, col[:, None] * inv], axis=-1)
    cos, sin = jnp.cos(ang), jnp.sin(ang)
    reps = 256 // rot_dim
    cos_t = jnp.tile(jnp.concatenate([cos, cos], axis=-1), (1, reps))
    sin_t = jnp.tile(jnp.concatenate([-sin, sin], axis=-1), (1, reps))
    cos_t = jnp.concatenate([cos_t, jnp.ones((n_ctx, 256), F32)], axis=0)
    sin_t = jnp.concatenate([sin_t, jnp.zeros((n_ctx, 256), F32)], axis=0)
    return cos_t, sin_t


def _pack_w_in(w):
    parts, src = [], 0
    for _, width, padded in IN_PIECES:
        parts.append(w[:, src:src + width])
        if padded > width:
            parts.append(jnp.zeros((w.shape[0], padded - width), w.dtype))
        src += width
    return jnp.concatenate(parts, axis=1).astype(BF16)


def _pack_w_uq(w):
    wh = w.reshape(MLA_Q_RANK, N_HEADS, MLA_NOPE + MLA_ROPE)
    nope = wh[:, :, :MLA_NOPE].reshape(MLA_Q_RANK, N_HEADS * MLA_NOPE)
    rope = wh[:, :, MLA_NOPE:].reshape(MLA_Q_RANK, N_HEADS * MLA_ROPE)
    packed = jnp.concatenate([nope, rope], axis=1)
    return jnp.pad(packed, ((0, 256 - MLA_Q_RANK), (0, 0))).astype(BF16)


def _pack_w_ukv(w):
    wh = w.reshape(MLA_KV_RANK, N_HEADS, MLA_NOPE + HEAD_DIM)
    kn = wh[:, :, :MLA_NOPE].reshape(MLA_KV_RANK, N_HEADS * MLA_NOPE)
    vv = wh[:, :, MLA_NOPE:].reshape(MLA_KV_RANK, N_HEADS * HEAD_DIM)
    return jnp.concatenate([kn, vv], axis=1).astype(BF16)


def _heads(a, n_heads):
    b, n, w = a.shape
    return a.reshape(b, n, n_heads, w // n_heads).transpose(0, 2, 1, 3)


def _heads_t(a, n_heads):
    b, n, w = a.shape
    return a.reshape(b, n, n_heads, w // n_heads).transpose(0, 2, 3, 1)


def _attention(q, kt, v, n_lat, n_ctx, ctx_out, diff=None):
    n = n_lat + n_ctx
    y = _attn_call(q, kt, v, (0, n_lat), (0, n), min(Q_TILE, n_lat), diff)
    if ctx_out:
        y_c = _attn_call(q, kt, v, (n_lat, n_ctx), (n_lat, n_ctx), n_ctx, diff)
    else:
        y_c = jnp.zeros((y.shape[0], n_ctx, y.shape[2]), y.dtype)
    return jnp.concatenate([y, y_c], axis=1)


def _heads_v(a, n_heads):
    h = _heads(a, n_heads)
    return jnp.concatenate([h, jnp.ones_like(h)], axis=-1)


def _steps(x, bounds, offsets):
    deltas = offsets - jnp.concatenate([jnp.zeros_like(offsets[..., :1]), offsets[..., :-1]], axis=-1)
    return x + jnp.sum(jnp.where(x[..., None] >= bounds[..., None, :], deltas[..., None, :], 0), axis=-1)


def _moe(x_new, hs, gates, pos, counts, modl, g, n_lat, w1_all, b1, w2_all, b2, layer):
    b, n, d = x_new.shape
    f = w2_all.shape[2]
    tile_rows = TOP_K * TOKEN_TILE
    n_parts = 2 if b % 2 == 0 else 1
    tiles = (b // n_parts) * (n // TOKEN_TILE)
    n_blocks = -(-(tiles * tile_rows) // MOE_BLOCK) + N_EXPERTS
    hs_rows = hs.reshape(b * (n // TOKEN_TILE) * tile_rows, d // 2)
    w1g, w1l = _w1_prep_call(w1_all, layer)
    b1g, b1l = b1[:, 0::2].reshape(N_EXPERTS, 1, f), b1[:, 1::2].reshape(N_EXPERTS, 1, f)
    picked_parts = []
    for p in range(n_parts):
        tile_counts = counts.reshape(-1, N_EXPERTS)[p * tiles:(p + 1) * tiles]
        tile_base = jnp.cumsum(tile_counts, axis=0) - tile_counts
        first = jnp.cumsum(tile_counts, axis=1) - tile_counts
        padded = (jnp.sum(tile_counts, axis=0) + MOE_BLOCK - 1) // MOE_BLOCK * MOE_BLOCK
        pad_ends = jnp.cumsum(padded)
        pad_start = pad_ends - padded
        block_start = jnp.arange(n_blocks, dtype=jnp.int32) * MOE_BLOCK
        block_expert = jnp.minimum(jnp.sum((pad_ends[None, :] <= block_start[:, None]).astype(jnp.int32), axis=1),
                                   N_EXPERTS - 1)
        n_used = (pad_ends[-1:] // MOE_BLOCK).astype(jnp.int32)
        j = block_start[:, None] + jnp.arange(MOE_BLOCK, dtype=jnp.int32)[None, :] - pad_start[block_expert][:, None]
        tile_offset = (jnp.arange(tiles, dtype=jnp.int32)[:, None] + p * tiles) * tile_rows + first - tile_base
        src = _steps(j, tile_base.T[block_expert], tile_offset.T[block_expert])
        src = jnp.clip(src, 0, hs_rows.shape[0] - 1)
        x_rows = hs_rows.at[src.reshape(-1)].get(mode="promise_in_bounds")
        y_rows = _expert_call(block_expert, n_used, x_rows, w1g, w1l, b1g, b1l, w2_all, layer,
                              b2.reshape(N_EXPERTS, 1, d))
        slot = jnp.broadcast_to(jnp.arange(tile_rows, dtype=jnp.int32)[None, :], (tiles, tile_rows))
        back = _steps(slot, first, tile_base + pad_start[None, :] - first)
        picked = y_rows.at[back.reshape(-1)].get(mode="promise_in_bounds")
        picked_parts.append(picked.reshape(b // n_parts, n // TOKEN_TILE, tile_rows, d // 2))
    return _combine_call(x_new, picked_parts, gates.transpose(0, 2, 1), pos.transpose(0, 2, 1), modl, g, n_lat)


def kernel(x, c, ctx, c_ctx, ada_w, ada_b, norm_g, w_in, w_out, ret_log_decay, ret_gn_w, ret_gn_b,
           diff_lambda, diff_subln, gqa_qk_norm, mla_q_norm, mla_kv_norm, mla_w_uq, mla_w_ukv,
           router_w, router_b, exp_w1, exp_b1, exp_w2, exp_b2):
    b, s, d = x.shape
    n_ctx = ctx.shape[1]
    n = s + n_ctx
    depth = ada_w.shape[0]
    assert n_ctx % TOKEN_TILE == 0 and s % TOKEN_TILE == 0 and s % GRID_W == 0
    assert n_ctx % RET_CHUNK == 0 and s % min(Q_TILE, s) == 0 and s % n_ctx == 0

    tables = _rope_tables(s // GRID_W, n_ctx, HEAD_DIM) + _rope_tables(s // GRID_W, n_ctx, DIFF_D)
    c_rows = jnp.zeros((16, d), F32).at[:b].set(c).at[b].set(c_ctx)
    mods = _ada_call(c_rows, ada_w, ada_b)
    xc = jnp.concatenate([x, ctx], axis=1)

    for l in range(depth):
        last = l == depth - 1
        lam_init = 0.8 - 0.6 * math.exp(-0.3 * l)
        mod_lat = mods[l, :b].reshape(b, 1, 6, d)
        mod_ctx = jnp.broadcast_to(mods[l, b].reshape(1, 1, 6, d), (b, 1, 6, d))
        modl = jnp.concatenate([mod_ctx, mod_lat], axis=1)

        (rq, rk, rv, rg, dq, dk, dv, gq, gk, gv, mqn, mqr, mkn, mv, mkr) = _in_proj_call(
            xc, modl, norm_g[l, 0:1], _pack_w_in(w_in[l]), tables,
            jnp.tile(gqa_qk_norm[l, 0], N_HEADS)[None, :], jnp.tile(gqa_qk_norm[l, 1], GQA_KV_HEADS)[None, :],
            jnp.pad(mla_q_norm[l], (0, 256 - MLA_Q_RANK))[None, :], mla_kv_norm[l][None, :],
            _pack_w_uq(mla_w_uq[l]), _pack_w_ukv(mla_w_ukv[l]), s)

        log_g = -jnp.exp(ret_log_decay[l].astype(F32))
        o_f = _ret_call(rq, rk, rv, _ret_tables(log_g[0], False), s, False)
        ret_y = _ret_call(rq, rk, rv, _ret_tables(log_g[1], True), s, True,
                          (o_f, rg, ret_gn_w[l][None, :], ret_gn_b[l][None, :]))

        lp = diff_lambda[l].astype(F32)
        lam = jnp.exp(jnp.sum(lp[0] * lp[1])) - jnp.exp(jnp.sum(lp[2] * lp[3])) + lam_init
        dif_y = _attention(_heads(dq, 2 * N_HEADS), _heads_t(dk, 2 * N_HEADS), _heads_v(dv, N_HEADS), s, n_ctx,
                           not last, (lam, diff_subln[l], 1.0 - lam_init))

        gqa_y = _attention(_heads(gq, N_HEADS), _heads_t(gk, GQA_KV_HEADS), _heads_v(gv, GQA_KV_HEADS), s, n_ctx,
                           not last)

        q_m = jnp.concatenate([_heads(mqn, N_HEADS), _heads(mqr, N_HEADS)], axis=-1)
        kr_t = jnp.broadcast_to(mkr[:, None, :, :MLA_ROPE].transpose(0, 1, 3, 2), (b, N_HEADS, MLA_ROPE, n))
        kt_m = jnp.concatenate([_heads_t(mkn, N_HEADS), kr_t], axis=2)
        mla_y = _attention(q_m, kt_m, _heads_v(mv, N_HEADS), s, n_ctx, not last)

        ymix = jnp.concatenate([ret_y, dif_y, gqa_y, mla_y], axis=-1)
        n_rows = s if last else n
        x_new, hs, gates, pos, counts = _out_proj_call(
            ymix, w_out[l].astype(BF16), xc, modl, norm_g[l], router_w[l].T, router_b[l][:, None], s, n_rows)
        xc = _moe(x_new, hs, gates, pos, counts[..., 0], modl, norm_g[l], s,
                  exp_w1, exp_b1[l], exp_w2, exp_b2[l], l)

    return xc[:, :s]
```

```python
import functools
import math

import jax
import jax.numpy as jnp
from jax import lax
from jax.experimental import pallas as pl
from jax.experimental.pallas import tpu as pltpu

F32 = jnp.float32
BF16 = jnp.bfloat16

GRID_W = 64
ROPE_THETA = 10000.0
EPS = 1e-6
GROUP_WIDTH = 256
HEAD_DIM = 64
N_HEADS = 4
RET_CHUNK = 256
DIFF_D = 32
GQA_KV_HEADS = 2
MLA_Q_RANK = 192
MLA_KV_RANK = 128
MLA_NOPE = 64
MLA_ROPE = 32
N_EXPERTS = 32
TOP_K = 4
SWIGLU_LIMIT = 7.0
SWIGLU_ALPHA = 1.702
MOE_BLOCK = 512

LANES = 128
MXU_DIM = 256
TOKEN_TILE = 256
Q_TILE = 1024
KV_TILE = 512
VMEM_LIMIT = 48 * 1024 * 1024
DMA_SPLIT = 4
LOG2E = math.log2(math.e)

IN_PIECES = (
    ("ret_q", 256, 256), ("ret_k", 256, 256), ("ret_v", 256, 256), ("ret_g", 256, 256),
    ("dif_q", 256, 256), ("dif_k", 256, 256), ("dif_v", 256, 256),
    ("gqa_q", 256, 256), ("gqa_k", 128, 128), ("gqa_v", 128, 128),
    ("mla_cq", MLA_Q_RANK, 256), ("mla_ckv", MLA_KV_RANK, 128), ("mla_kr", MLA_ROPE, 128),
)
IN_OFFSETS = {}
_off = 0
for _name, _w, _pw in IN_PIECES:
    IN_OFFSETS[_name] = _off
    _off += _pw
IN_PACKED_WIDTH = _off


def _params(sem):
    return pltpu.CompilerParams(dimension_semantics=sem, vmem_limit_bytes=VMEM_LIMIT)


def _rms(x):
    return x * lax.rsqrt(jnp.mean(x * x, axis=-1, keepdims=True) + EPS)


def _split_bf16(a):
    hi = a.astype(BF16)
    lo = (a - hi.astype(F32)).astype(BF16)
    return hi, lo


def _dot_nt(a, b):
    return lax.dot_general(a, b, (((1,), (1,)), ((), ())), preferred_element_type=F32)


def _dot(a, b):
    return jnp.dot(a, b, preferred_element_type=F32)


def _sigmoid(a):
    return 1.0 / (1.0 + jnp.exp(-a))


def _pack_halves(a):
    w = a.shape[1] // 2
    bits = lax.bitcast_convert_type(a.astype(BF16).astype(F32), jnp.uint32)
    return bits[:, :w] | (bits[:, w:] >> 16)


def _unpack_halves(u):
    hi = lax.bitcast_convert_type(u & jnp.uint32(0xFFFF0000), F32)
    lo = lax.bitcast_convert_type(u << 16, F32)
    return jnp.concatenate([hi, lo], axis=1)


def _mod_spec(d, n_lat_tiles):
    return pl.BlockSpec((1, 1, 6, d), lambda i, t: (i, jnp.where(t < n_lat_tiles, 1, 0), 0, 0))


def _ada_kernel(c_ref, w_ref, b_ref, o_ref):
    s = c_ref[...]
    s = s * _sigmoid(s)
    s_hi, s_lo = _split_bf16(s)
    w_hi, w_lo = _split_bf16(w_ref[0])
    o_ref[0] = _dot(s_hi, w_hi) + _dot(s_hi, w_lo) + _dot(s_lo, w_hi) + b_ref[0]


def _ada_call(c_rows, ada_w, ada_b):
    depth, d, n6 = ada_w.shape
    rows = c_rows.shape[0]
    tn = 1536
    return pl.pallas_call(
        _ada_kernel,
        grid=(depth, n6 // tn),
        in_specs=[
            pl.BlockSpec((rows, d), lambda l, j: (0, 0)),
            pl.BlockSpec((1, d, tn), lambda l, j: (l, 0, j)),
            pl.BlockSpec((1, 1, tn), lambda l, j: (l, 0, j)),
        ],
        out_specs=pl.BlockSpec((1, rows, tn), lambda l, j: (l, 0, j)),
        out_shape=jax.ShapeDtypeStruct((depth, rows, n6), F32),
        compiler_params=_params(("parallel", "parallel")),
    )(c_rows, ada_w, ada_b.reshape(depth, 1, n6))


def _rope(x, cos, sin_signed, half):
    outs = []
    for c in range(x.shape[1] // LANES):
        sl = slice(c * LANES, (c + 1) * LANES)
        xc = x[:, sl]
        lane = lax.broadcasted_iota(jnp.int32, xc.shape, 1)
        first_half = (lane % (2 * half)) < half
        partner = jnp.where(first_half, pltpu.roll(xc, LANES - half, 1), pltpu.roll(xc, half, 1))
        outs.append(xc * cos[:, sl] + partner * sin_signed[:, sl])
    return outs[0] if len(outs) == 1 else jnp.concatenate(outs, axis=1)


def _group_mean(x, gsize):
    w = x.shape[1]
    r = lax.broadcasted_iota(jnp.int32, (w, w), 0) // gsize
    c = lax.broadcasted_iota(jnp.int32, (w, w), 1) // gsize
    ones = jnp.where(r == c, 1.0, 0.0).astype(BF16)
    hi, lo = _split_bf16(x)
    return (_dot(hi, ones) + _dot(lo, ones)) * (1.0 / gsize)


def _group_mean_sq(x, gsize):
    return _group_mean(x * x, gsize)


def _in_proj_kernel(x_ref, mod_ref, g_ref, w_ref, cos64_ref, sin64_ref, cos32_ref, sin32_ref,
                    gqn_ref, gkn_ref, mqn_ref, mkvn_ref, wuq_ref, wukv_ref,
                    rq_ref, rk_ref, rv_ref, rg_ref, dq_ref, dk_ref, dv_ref, gq_ref, gk_ref, gv_ref,
                    mqn_o, mqr_o, mkn_o, mv_o, mkr_o):
    mod = mod_ref[0, 0]
    h = _rms(x_ref[0]) * g_ref[...] * (1.0 + mod[1:2]) + mod[0:1]
    hb = h.astype(BF16)
    cos64, sin64 = cos64_ref[...], sin64_ref[...]
    cos32, sin32 = cos32_ref[...], sin32_ref[...]

    def proj(name, width):
        o = IN_OFFSETS[name]
        return _dot(hb, w_ref[:, o:o + width])

    rq_ref[0] = _rope(proj("ret_q", 256), cos64, sin64, 32).astype(BF16)
    rk_ref[0] = (_rope(proj("ret_k", 256), cos64, sin64, 32) * (HEAD_DIM ** -0.5)).astype(BF16)
    rv_ref[0] = proj("ret_v", 256).astype(BF16)
    rg_ref[0] = proj("ret_g", 256)
    dq_ref[0] = (_rope(proj("dif_q", 256), cos32, sin32, 16) * (DIFF_D ** -0.5 * LOG2E)).astype(BF16)
    dk_ref[0] = _rope(proj("dif_k", 256), cos32, sin32, 16).astype(BF16)
    dv_ref[0] = proj("dif_v", 256).astype(BF16)
    gq = proj("gqa_q", 256)
    gq = gq * lax.rsqrt(_group_mean_sq(gq, HEAD_DIM) + EPS) * gqn_ref[...]
    gq_ref[0] = (_rope(gq, cos64, sin64, 32) * (HEAD_DIM ** -0.5 * LOG2E)).astype(BF16)
    gk = proj("gqa_k", 128)
    gk = gk * lax.rsqrt(_group_mean_sq(gk, HEAD_DIM) + EPS) * gkn_ref[...]
    gk_ref[0] = _rope(gk, cos64[:, :LANES], sin64[:, :LANES], 32).astype(BF16)
    gv_ref[0] = proj("gqa_v", 128).astype(BF16)
    cq = proj("mla_cq", 256)
    cq = cq * lax.rsqrt(jnp.sum(cq * cq, axis=-1, keepdims=True) * (1.0 / MLA_Q_RANK) + EPS) * mqn_ref[...]
    q_up = _dot(cq.astype(BF16), wuq_ref[...])
    mla_scale = (MLA_NOPE + MLA_ROPE) ** -0.5 * LOG2E
    mqn_o[0] = (q_up[:, :256] * mla_scale).astype(BF16)
    mqr_o[0] = (_rope(q_up[:, 256:], cos32[:, :LANES], sin32[:, :LANES], 16) * mla_scale).astype(BF16)
    ckv = proj("mla_ckv", 128)
    ckv = ckv * lax.rsqrt(jnp.mean(ckv * ckv, axis=-1, keepdims=True) + EPS) * mkvn_ref[...]
    kv_up = _dot(ckv.astype(BF16), wukv_ref[...])
    mkn_o[0] = kv_up[:, :256].astype(BF16)
    mv_o[0] = kv_up[:, 256:].astype(BF16)
    mkr_o[0] = _rope(proj("mla_kr", 128), cos32[:, :LANES], sin32[:, :LANES], 16).astype(BF16)


def _in_proj_call(xc, modl, g0, w_in_p, tables, gqn, gkn, mqn, mkvn, wuq_p, wukv_p, n_lat):
    b, n, d = xc.shape
    tm = TOKEN_TILE
    tok = lambda w: pl.BlockSpec((1, tm, w), lambda i, t: (i, t, 0))
    const2 = lambda a: pl.BlockSpec(a.shape, lambda i, t: (0, 0))
    tab = pl.BlockSpec((tm, 256), lambda i, t: (t, 0))
    out_widths = (256, 256, 256, 256, 256, 256, 256, 256, 128, 128, 256, 128, 256, 256, 128)
    out_dtypes = (BF16, BF16, BF16, F32, BF16, BF16, BF16, BF16, BF16, BF16, BF16, BF16, BF16, BF16, BF16)
    return pl.pallas_call(
        _in_proj_kernel,
        grid=(b, n // tm),
        in_specs=[
            tok(d), _mod_spec(d, n_lat // tm),
            const2(g0), const2(w_in_p), tab, tab, tab, tab,
            const2(gqn), const2(gkn), const2(mqn), const2(mkvn), const2(wuq_p), const2(wukv_p),
        ],
        out_specs=[tok(w) for w in out_widths],
        out_shape=[jax.ShapeDtypeStruct((b, n, w), dt) for w, dt in zip(out_widths, out_dtypes)],
        compiler_params=_params(("parallel", "parallel")),
    )(xc, modl, g0, w_in_p, *tables, gqn, gkn, mqn, mkvn, wuq_p, wukv_p)


def _ret_kernel(*refs, readout):
    if readout:
        (q_ref, k_ref, v_ref, dec_ref, xi_ref, zeta_ref, gc_ref, of_ref, g_ref, gnw_ref, gnb_ref,
         o_ref, state_ref) = refs
    else:
        q_ref, k_ref, v_ref, dec_ref, xi_ref, zeta_ref, gc_ref, o_ref, state_ref = refs

    @pl.when(pl.program_id(1) == 0)
    def _():
        state_ref[...] = jnp.zeros_like(state_ref)

    q, k, v = q_ref[0], k_ref[0], v_ref[0]
    outs = []
    for h in range(N_HEADS):
        sl = slice(h * HEAD_DIM, (h + 1) * HEAD_DIM)
        qh, kh, vh = q[:, sl], k[:, sl], v[:, sl]
        st = state_ref[h]
        inner = _dot_nt(qh, kh) * dec_ref[h]
        o = _dot(inner.astype(BF16), vh) + _dot(qh, st.astype(BF16)) * xi_ref[h]
        kz = (kh.astype(F32) * zeta_ref[h]).astype(BF16)
        kv = lax.dot_general(kz, vh, (((0,), (0,)), ((), ())), preferred_element_type=F32)
        state_ref[h] = gc_ref[h] * st + kv
        outs.append(o)
    o = jnp.concatenate(outs, axis=1)
    if readout:
        o = o + of_ref[0]
        o = o - _group_mean(o, HEAD_DIM)
        o = o * lax.rsqrt(_group_mean_sq(o, HEAD_DIM) + EPS)
        g = g_ref[0]
        o_ref[0] = ((o * gnw_ref[...] + gnb_ref[...]) * (g * _sigmoid(g))).astype(o_ref.dtype)
    else:
        o_ref[0] = o


def _ret_call(q, k, v, tabs, n_lat, backward, readout_args=None):
    b, n, w = q.shape
    c = RET_CHUNK
    nch = n // c
    nlc = n_lat // c
    if backward:
        chunk = lambda t: nch - 1 - t
    else:
        chunk = lambda t: jnp.where(t < nch - nlc, nlc + t, t - (nch - nlc))
    tok = lambda dt_w: pl.BlockSpec((1, c, dt_w), lambda i, t: (i, chunk(t), 0))
    const = lambda a: pl.BlockSpec(a.shape, lambda i, t: (0,) * a.ndim)
    in_specs = [tok(w), tok(w), tok(w)] + [const(a) for a in tabs]
    args = [q, k, v, *tabs]
    if readout_args is not None:
        o_f, g, gnw, gnb = readout_args
        in_specs += [tok(w), tok(w), const(gnw), const(gnb)]
        args += [o_f, g, gnw, gnb]
    out_dtype = BF16 if readout_args is not None else F32
    return pl.pallas_call(
        functools.partial(_ret_kernel, readout=readout_args is not None),
        grid=(b, nch),
        in_specs=in_specs,
        out_specs=tok(w),
        out_shape=jax.ShapeDtypeStruct((b, n, w), out_dtype),
        scratch_shapes=[pltpu.VMEM((N_HEADS, HEAD_DIM, HEAD_DIM), F32)],
        compiler_params=_params(("parallel", "arbitrary")),
    )(*args)


def _ret_tables(log_g, backward):
    c = RET_CHUNK
    pos = jnp.arange(c, dtype=F32)
    dist = (pos[None, :] - pos[:, None]) if backward else (pos[:, None] - pos[None, :])
    lg = log_g[:, None, None]
    decay = jnp.where(dist >= 0, jnp.exp(lg * jnp.maximum(dist, 0.0)), 0.0)
    to_state = (c - pos) if backward else (pos + 1.0)
    to_end = pos if backward else (c - 1.0 - pos)
    xi = jnp.exp(log_g[:, None] * to_state)[:, :, None]
    zeta = jnp.exp(log_g[:, None] * to_end)[:, :, None]
    ones = jnp.ones((1, 1, HEAD_DIM), F32)
    gc = jnp.exp(log_g * c)[:, None, None] * jnp.ones((1, HEAD_DIM, HEAD_DIM), F32)
    return decay, xi * ones, zeta * ones, gc


def _attn_kernel(*refs, n_maps, readout_scale):
    if n_maps == 2:
        lam_ref, q_ref, kt_ref, v_ref, sub_ref, o_ref, first_ref = refs
    else:
        q_ref, kt_ref, v_ref, o_ref = refs
    hm = pl.program_id(2)
    q = q_ref[0, 0]
    tq = q.shape[0]
    n_keys = kt_ref.shape[3]
    dv = v_ref.shape[3] // 2
    tk = min(KV_TILE, n_keys)
    n_full, tail = divmod(n_keys, tk)

    def step(off, size, carry):
        m, acc = carry
        s = _dot(q, kt_ref[0, 0, :, pl.ds(off, size)])
        m_new = jnp.maximum(m, jnp.max(s, axis=-1, keepdims=True))
        p = jnp.exp2(s - m_new)
        acc = jnp.exp2(m - m_new) * acc + _dot(p.astype(BF16), v_ref[0, 0, pl.ds(off, size), :])
        return m_new, acc

    carry = (jnp.full((tq, 1), -1e30, F32), jnp.zeros((tq, 2 * dv), F32))
    for j in range(n_full):
        carry = step(j * tk, tk, carry)
    if tail:
        carry = step(n_full * tk, tail, carry)
    acc = carry[1]
    o = acc[:, :dv] / acc[:, dv:]

    def store(head_of_step, val):
        for hh in range(N_HEADS):
            @pl.when(head_of_step == hh)
            def _(hh=hh):
                o_ref[0, :, hh * dv:(hh + 1) * dv] = val.astype(o_ref.dtype)

    if n_maps == 1:
        store(hm, o)
    else:
        @pl.when(hm % 2 == 0)
        def _():
            first_ref[...] = o

        @pl.when(hm % 2 == 1)
        def _():
            od = first_ref[...] - lam_ref[0] * o
            od = od * lax.rsqrt(jnp.mean(od * od, axis=-1, keepdims=True) + EPS) * sub_ref[...]
            store(hm // 2, od * readout_scale)


def _attn_call(q, kt, v, rows, keys, tq, diff=None):
    b, hq, n, d = q.shape
    hk, hv, dv = kt.shape[1], v.shape[1], v.shape[3] // 2
    (q0, qn), (k0, kn) = rows, keys
    assert q0 % tq == 0 and qn % tq == 0 and k0 % kn == 0
    n_maps = 2 if diff is not None else 1
    in_specs = [
        pl.BlockSpec((1, 1, tq, d), lambda i, t, h: (i, h, q0 // tq + t, 0)),
        pl.BlockSpec((1, 1, d, kn), lambda i, t, h: (i, h // (hq // hk), 0, k0 // kn)),
        pl.BlockSpec((1, 1, kn, 2 * dv), lambda i, t, h: (i, h // (hq // hv), k0 // kn, 0)),
    ]
    args = [q, kt, v]
    scratch = []
    scale = 1.0
    if diff is not None:
        lam, subln, scale = diff
        in_specs = [pl.BlockSpec(memory_space=pltpu.SMEM)] + in_specs + [pl.BlockSpec((1, dv), lambda i, t, h: (0, 0))]
        args = [lam.reshape(1)] + args + [subln.reshape(1, dv)]
        scratch = [pltpu.VMEM((tq, dv), F32)]
    return pl.pallas_call(
        functools.partial(_attn_kernel, n_maps=n_maps, readout_scale=scale),
        grid=(b, qn // tq, hq),
        in_specs=in_specs,
        out_specs=pl.BlockSpec((1, tq, (hq // n_maps) * dv), lambda i, t, h: (i, t, 0)),
        out_shape=jax.ShapeDtypeStruct((b, qn, (hq // n_maps) * dv), BF16),
        scratch_shapes=scratch,
        compiler_params=_params(("parallel", "parallel", "arbitrary")),
    )(*args)


def _out_proj_kernel(y_ref, w_ref, x_ref, mod_ref, g_ref, wr_ref, br_ref,
                     xo_ref, hs_ref, gate_ref, pos_ref, cnt_ref):
    mod = mod_ref[0, 0]
    g = g_ref[...]
    y = _dot(y_ref[0], w_ref[...])
    xn = x_ref[0] + mod[2:3] * (_rms(y) * g[1:2])
    xo_ref[0] = xn
    h = _rms(xn) * g[2:3] * (1.0 + mod[4:5]) + mod[3:4]
    w_hi, w_lo = _split_bf16(wr_ref[...])
    h_hi, h_lo = _split_bf16(h)
    logits = _dot_nt(w_hi, h_hi) + _dot_nt(w_hi, h_lo) + _dot_nt(w_lo, h_hi) + br_ref[...]
    row = lax.broadcasted_iota(jnp.int32, logits.shape, 0)
    vals = logits
    tops, idxs = [], []
    for _ in range(TOP_K):
        m = jnp.max(vals, axis=0, keepdims=True)
        idx = jnp.min(jnp.where(vals == m, row, N_EXPERTS), axis=0, keepdims=True)
        tops.append(m)
        idxs.append(idx)
        vals = jnp.where(row == idx, -jnp.inf, vals)
    ex = [jnp.exp(tv - tops[0]) for tv in tops]
    den = ex[0] + ex[1] + ex[2] + ex[3]
    gate_ref[0] = jnp.concatenate([e / den for e in ex], axis=0)
    tm = logits.shape[1]
    before = jnp.where(lax.broadcasted_iota(jnp.int32, (tm, tm), 0) < lax.broadcasted_iota(jnp.int32, (tm, tm), 1),
                       1.0, 0.0).astype(BF16)
    counts = jnp.zeros((N_EXPERTS, 1), F32)
    onehots, ranks = [], []
    for idx in idxs:
        onehot = jnp.where(row == idx, 1.0, 0.0)
        ranks.append(_dot(onehot.astype(BF16), before) + counts)
        onehots.append(onehot)
        counts = counts + jnp.sum(onehot, axis=1, keepdims=True)
    counts_b = jnp.broadcast_to(counts, (N_EXPERTS, LANES))
    earlier = jnp.where(lax.broadcasted_iota(jnp.int32, (N_EXPERTS, N_EXPERTS), 0)
                        > lax.broadcasted_iota(jnp.int32, (N_EXPERTS, N_EXPERTS), 1), 1.0, 0.0).astype(BF16)
    first = _dot(earlier, counts_b.astype(BF16))[:, 0:1]
    pos = [jnp.sum(oh * (rk + first), axis=0, keepdims=True).astype(jnp.int32) for oh, rk in zip(onehots, ranks)]
    pos_ref[0] = jnp.concatenate(pos, axis=0)
    cnt_ref[0, 0] = counts_b.astype(jnp.int32)
    slot = lax.broadcasted_iota(jnp.int32, (TOP_K * tm, tm), 0)
    pick = jnp.where(slot == pos[0], 1.0, 0.0)
    for p in pos[1:]:
        pick = jnp.where(slot == p, 1.0, pick)
    hs_ref[0, 0] = _pack_halves(_dot(pick.astype(BF16), h.astype(BF16)))


def _out_proj_call(ymix, w_out, xc, modl, g, wr_t, br, n_lat, n_rows):
    b, n, d = xc.shape
    tm = TOKEN_TILE
    tok = lambda w: pl.BlockSpec((1, tm, w), lambda i, t: (i, t, 0))
    const2 = lambda a: pl.BlockSpec(a.shape, lambda i, t: (0, 0))
    sel = pl.BlockSpec((1, TOP_K, tm), lambda i, t: (i, 0, t))
    return pl.pallas_call(
        _out_proj_kernel,
        grid=(b, n_rows // tm),
        in_specs=[
            tok(ymix.shape[2]), const2(w_out), tok(d), _mod_spec(d, n_lat // tm),
            const2(g), const2(wr_t), const2(br),
        ],
        out_specs=[tok(d),
                   pl.BlockSpec((1, 1, TOP_K * tm, d // 2), lambda i, t: (i, t, 0, 0)),
                   sel, sel,
                   pl.BlockSpec((1, 1, N_EXPERTS, LANES), lambda i, t: (i, t, 0, 0))],
        out_shape=[
            jax.ShapeDtypeStruct((b, n_rows, d), F32),
            jax.ShapeDtypeStruct((b, n_rows // tm, TOP_K * tm, d // 2), jnp.uint32),
            jax.ShapeDtypeStruct((b, TOP_K, n_rows), F32),
            jax.ShapeDtypeStruct((b, TOP_K, n_rows), jnp.int32),
            jax.ShapeDtypeStruct((b, n_rows // tm, N_EXPERTS, LANES), jnp.int32),
        ],
        compiler_params=_params(("parallel", "parallel")),
    )(ymix, w_out, xc, modl, g, wr_t, br)


def _w1_prep_kernel(*refs):
    w_refs, (g_ref, l_ref) = refs[:DMA_SPLIT], refs[DMA_SPLIT:]
    r = lax.broadcasted_iota(jnp.int32, (MXU_DIM, MXU_DIM), 0)
    c = lax.broadcasted_iota(jnp.int32, (MXU_DIM, MXU_DIM), 1)
    src = jnp.where(c < LANES, 2 * c, 2 * (c - LANES) + 1)
    perm = jnp.where(r == src, 1.0, 0.0).astype(BF16)
    per_slab = w_refs[0].shape[3] // MXU_DIM
    for s, w_ref in enumerate(w_refs):
        for jj in range(per_slab):
            j = s * per_slab + jj
            out = _dot(w_ref[0, 0, :, jj * MXU_DIM:(jj + 1) * MXU_DIM].astype(BF16), perm)
            g_ref[0, :, j * LANES:(j + 1) * LANES] = out[:, :LANES].astype(BF16)
            l_ref[0, :, j * LANES:(j + 1) * LANES] = out[:, LANES:].astype(BF16)


def _w1_prep_call(w1_all, layer):
    _, e, d, f2 = w1_all.shape
    f = f2 // 2
    rows = 256
    out_spec = pl.BlockSpec((1, rows, f), lambda i, j: (i, j, 0))
    return pl.pallas_call(
        _w1_prep_kernel,
        grid=(e, d // rows),
        in_specs=[pl.BlockSpec((1, 1, rows, f2 // DMA_SPLIT), lambda i, j, s=s: (layer, i, j, s))
                  for s in range(DMA_SPLIT)],
        out_specs=[out_spec, out_spec],
        out_shape=[jax.ShapeDtypeStruct((e, d, f), BF16)] * 2,
        compiler_params=_params(("parallel", "parallel")),
    )(*([w1_all] * DMA_SPLIT))


def _expert_kernel(be_ref, nu_ref, x_ref, w1g_ref, w1l_ref, b1g_ref, b1l_ref, *refs):
    w2_refs, (b2_ref, o_ref, w2b_ref) = refs[:DMA_SPLIT], refs[DMA_SPLIT:]
    i = pl.program_id(0)

    @pl.when(jnp.logical_or(i == 0, be_ref[i] != be_ref[jnp.maximum(i - 1, 0)]))
    def _():
        rows = w2_refs[0].shape[2]
        for s, w2_ref in enumerate(w2_refs):
            w2b_ref[s * rows:(s + 1) * rows, :] = w2_ref[0, 0].astype(BF16)

    @pl.when(i < nu_ref[0])
    def _():
        x = _unpack_halves(x_ref[...]).astype(BF16)
        glu = jnp.minimum(_dot(x, w1g_ref[0]) + b1g_ref[0], SWIGLU_LIMIT)
        lin = jnp.clip(_dot(x, w1l_ref[0]) + b1l_ref[0], -SWIGLU_LIMIT, SWIGLU_LIMIT)
        act = glu * _sigmoid(SWIGLU_ALPHA * glu) * (lin + 1.0)
        o_ref[...] = _pack_halves(_dot(act.astype(BF16), w2b_ref[...]) + b2_ref[0])

    @pl.when(i >= nu_ref[0])
    def _():
        o_ref[...] = jnp.zeros_like(o_ref)


def _expert_call(block_expert, n_used, x_rows, w1g, w1l, b1g, b1l, w2_all, layer, b2):
    r = x_rows.shape[0]
    d, f = w1g.shape[1:]
    tm = MOE_BLOCK
    wspec = lambda s: pl.BlockSpec((1,) + s, lambda i, be, nu: (be[i], 0, 0))
    return pl.pallas_call(
        _expert_kernel,
        grid_spec=pltpu.PrefetchScalarGridSpec(
            num_scalar_prefetch=2,
            grid=(r // tm,),
            in_specs=[
                pl.BlockSpec((tm, d // 2), lambda i, be, nu: (i, 0)),
                wspec((d, f)), wspec((d, f)), wspec((1, f)), wspec((1, f)),
                *[pl.BlockSpec((1, 1, f // DMA_SPLIT, d), lambda i, be, nu, s=s: (layer, be[i], s, 0))
                  for s in range(DMA_SPLIT)],
                wspec((1, d)),
            ],
            out_specs=pl.BlockSpec((tm, d // 2), lambda i, be, nu: (i, 0)),
            scratch_shapes=[pltpu.VMEM((f, d), BF16)],
        ),
        out_shape=jax.ShapeDtypeStruct((r, d // 2), jnp.uint32),
        compiler_params=_params(("arbitrary",)),
    )(block_expert, n_used, x_rows, w1g, w1l, b1g, b1l, *([w2_all] * DMA_SPLIT), b2)


def _combine_kernel(*refs, n_parts):
    x_ref, y_refs = refs[0], refs[1:1 + n_parts]
    gate_ref, pos_ref, mod_ref, g_ref, o_ref = refs[1 + n_parts:]
    gates, pos = gate_ref[0], pos_ref[0]
    lane = lax.broadcasted_iota(jnp.int32, (gates.shape[0], y_refs[0].shape[2]), 1)
    mix = jnp.zeros(lane.shape, F32)
    for k in range(TOP_K):
        mix = jnp.where(lane == pos[:, k:k + 1], gates[:, k:k + 1], mix)
    mix_hi, mix_lo = _split_bf16(mix)
    mod = mod_ref[0, 0]
    part_batches = pl.num_programs(0) // n_parts
    for p, y_ref in enumerate(y_refs):
        @pl.when(pl.program_id(0) // part_batches == p)
        def _(y_ref=y_ref):
            rows = _unpack_halves(y_ref[0, 0]).astype(BF16)
            y = _dot(mix_hi, rows) + _dot(mix_lo, rows)
            o_ref[0] = x_ref[0] + mod[5:6] * (_rms(y) * g_ref[...][3:4])


def _combine_call(xc, picked_parts, gates, pos, modl, g, n_lat):
    b, n, d = xc.shape
    tm = TOKEN_TILE
    n_parts = len(picked_parts)
    bp = b // n_parts
    tok = pl.BlockSpec((1, tm, d), lambda i, t: (i, t, 0))

    def part_spec(p):
        def index(i, t):
            live = (i >= p * bp) & (i < (p + 1) * bp)
            return jnp.clip(i - p * bp, 0, bp - 1), jnp.where(live, t, 0), 0, 0
        return pl.BlockSpec((1, 1, TOP_K * tm, d // 2), index)

    return pl.pallas_call(
        functools.partial(_combine_kernel, n_parts=n_parts),
        grid=(b, n // tm),
        in_specs=[tok] + [part_spec(p) for p in range(n_parts)] + [
            pl.BlockSpec((1, tm, TOP_K), lambda i, t: (i, t, 0)),
            pl.BlockSpec((1, tm, TOP_K), lambda i, t: (i, t, 0)),
            _mod_spec(d, n_lat // tm),
            pl.BlockSpec(g.shape, lambda i, t: (0, 0)),
        ],
        out_specs=tok,
        out_shape=jax.ShapeDtypeStruct((b, n, d), F32),
        compiler_params=_params(("parallel", "parallel")),
    )(xc, *picked_parts, gates, pos, modl, g)


def _rope_tables(rows, n_ctx, rot_dim):
    row = jnp.repeat(jnp.arange(rows), GRID_W)
    col = jnp.tile(jnp.arange(GRID_W), rows)
    n_freq = rot_dim // 4
    inv = ROPE_THETA ** (-jnp.arange(n_freq, dtype=F32) / n_freq)
    ang = jnp.concatenate([row[:, None] * inv, col[:, None] * inv], axis=-1)
    cos, sin = jnp.cos(ang), jnp.sin(ang)
    reps = 256 // rot_dim
    cos_t = jnp.tile(jnp.concatenate([cos, cos], axis=-1), (1, reps))
    sin_t = jnp.tile(jnp.concatenate([-sin, sin], axis=-1), (1, reps))
    cos_t = jnp.concatenate([cos_t, jnp.ones((n_ctx, 256), F32)], axis=0)
    sin_t = jnp.concatenate([sin_t, jnp.zeros((n_ctx, 256), F32)], axis=0)
    return cos_t, sin_t


def _pack_w_in(w):
    parts, src = [], 0
    for _, width, padded in IN_PIECES:
        parts.append(w[:, src:src + width])
        if padded > width:
            parts.append(jnp.zeros((w.shape[0], padded - width), w.dtype))
        src += width
    return jnp.concatenate(parts, axis=1).astype(BF16)


def _pack_w_uq(w):
    wh = w.reshape(MLA_Q_RANK, N_HEADS, MLA_NOPE + MLA_ROPE)
    nope = wh[:, :, :MLA_NOPE].reshape(MLA_Q_RANK, N_HEADS * MLA_NOPE)
    rope = wh[:, :, MLA_NOPE:].reshape(MLA_Q_RANK, N_HEADS * MLA_ROPE)
    packed = jnp.concatenate([nope, rope], axis=1)
    return jnp.pad(packed, ((0, 256 - MLA_Q_RANK), (0, 0))).astype(BF16)


def _pack_w_ukv(w):
    wh = w.reshape(MLA_KV_RANK, N_HEADS, MLA_NOPE + HEAD_DIM)
    kn = wh[:, :, :MLA_NOPE].reshape(MLA_KV_RANK, N_HEADS * MLA_NOPE)
    vv = wh[:, :, MLA_NOPE:].reshape(MLA_KV_RANK, N_HEADS * HEAD_DIM)
    return jnp.concatenate([kn, vv], axis=1).astype(BF16)


def _heads(a, n_heads):
    b, n, w = a.shape
    return a.reshape(b, n, n_heads, w // n_heads).transpose(0, 2, 1, 3)


def _heads_t(a, n_heads):
    b, n, w = a.shape
    return a.reshape(b, n, n_heads, w // n_heads).transpose(0, 2, 3, 1)


def _attention(q, kt, v, n_lat, n_ctx, ctx_out, diff=None):
    n = n_lat + n_ctx
    y = _attn_call(q, kt, v, (0, n_lat), (0, n), min(Q_TILE, n_lat), diff)
    if ctx_out:
        y_c = _attn_call(q, kt, v, (n_lat, n_ctx), (n_lat, n_ctx), n_ctx, diff)
    else:
        y_c = jnp.zeros((y.shape[0], n_ctx, y.shape[2]), y.dtype)
    return jnp.concatenate([y, y_c], axis=1)


def _heads_v(a, n_heads):
    h = _heads(a, n_heads)
    return jnp.concatenate([h, jnp.ones_like(h)], axis=-1)


def _steps(x, bounds, offsets):
    deltas = offsets - jnp.concatenate([jnp.zeros_like(offsets[..., :1]), offsets[..., :-1]], axis=-1)
    return x + jnp.sum(jnp.where(x[..., None] >= bounds[..., None, :], deltas[..., None, :], 0), axis=-1)


def _moe(x_new, hs, gates, pos, counts, modl, g, n_lat, w1_all, b1, w2_all, b2, layer):
    b, n, d = x_new.shape
    f = w2_all.shape[2]
    tile_rows = TOP_K * TOKEN_TILE
    n_parts = 2 if b % 2 == 0 else 1
    tiles = (b // n_parts) * (n // TOKEN_TILE)
    n_blocks = -(-(tiles * tile_rows) // MOE_BLOCK) + N_EXPERTS
    hs_rows = hs.reshape(b * (n // TOKEN_TILE) * tile_rows, d // 2)
    w1g, w1l = _w1_prep_call(w1_all, layer)
    b1g, b1l = b1[:, 0::2].reshape(N_EXPERTS, 1, f), b1[:, 1::2].reshape(N_EXPERTS, 1, f)
    picked_parts = []
    for p in range(n_parts):
        tile_counts = counts.reshape(-1, N_EXPERTS)[p * tiles:(p + 1) * tiles]
        tile_base = jnp.cumsum(tile_counts, axis=0) - tile_counts
        first = jnp.cumsum(tile_counts, axis=1) - tile_counts
        padded = (jnp.sum(tile_counts, axis=0) + MOE_BLOCK - 1) // MOE_BLOCK * MOE_BLOCK
        pad_ends = jnp.cumsum(padded)
        pad_start = pad_ends - padded
        block_start = jnp.arange(n_blocks, dtype=jnp.int32) * MOE_BLOCK
        block_expert = jnp.minimum(jnp.sum((pad_ends[None, :] <= block_start[:, None]).astype(jnp.int32), axis=1),
                                   N_EXPERTS - 1)
        n_used = (pad_ends[-1:] // MOE_BLOCK).astype(jnp.int32)
        j = block_start[:, None] + jnp.arange(MOE_BLOCK, dtype=jnp.int32)[None, :] - pad_start[block_expert][:, None]
        tile_offset = (jnp.arange(tiles, dtype=jnp.int32)[:, None] + p * tiles) * tile_rows + first - tile_base
        src = _steps(j, tile_base.T[block_expert], tile_offset.T[block_expert])
        src = jnp.clip(src, 0, hs_rows.shape[0] - 1)
        x_rows = hs_rows.at[src.reshape(-1)].get(mode="promise_in_bounds")
        y_rows = _expert_call(block_expert, n_used, x_rows, w1g, w1l, b1g, b1l, w2_all, layer,
                              b2.reshape(N_EXPERTS, 1, d))
        slot = jnp.broadcast_to(jnp.arange(tile_rows, dtype=jnp.int32)[None, :], (tiles, tile_rows))
        back = _steps(slot, first, tile_base + pad_start[None, :] - first)
        picked = y_rows.at[back.reshape(-1)].get(mode="promise_in_bounds")
        picked_parts.append(picked.reshape(b // n_parts, n // TOKEN_TILE, tile_rows, d // 2))
    return _combine_call(x_new, picked_parts, gates.transpose(0, 2, 1), pos.transpose(0, 2, 1), modl, g, n_lat)


def kernel(x, c, ctx, c_ctx, ada_w, ada_b, norm_g, w_in, w_out, ret_log_decay, ret_gn_w, ret_gn_b,
           diff_lambda, diff_subln, gqa_qk_norm, mla_q_norm, mla_kv_norm, mla_w_uq, mla_w_ukv,
           router_w, router_b, exp_w1, exp_b1, exp_w2, exp_b2):
    b, s, d = x.shape
    n_ctx = ctx.shape[1]
    n = s + n_ctx
    depth = ada_w.shape[0]
    assert n_ctx % TOKEN_TILE == 0 and s % TOKEN_TILE == 0 and s % GRID_W == 0
    assert n_ctx % RET_CHUNK == 0 and s % min(Q_TILE, s) == 0 and s % n_ctx == 0

    tables = _rope_tables(s // GRID_W, n_ctx, HEAD_DIM) + _rope_tables(s // GRID_W, n_ctx, DIFF_D)
    c_rows = jnp.zeros((16, d), F32).at[:b].set(c).at[b].set(c_ctx)
    mods = _ada_call(c_rows, ada_w, ada_b)
    xc = jnp.concatenate([x, ctx], axis=1)

    for l in range(depth):
        last = l == depth - 1
        lam_init = 0.8 - 0.6 * math.exp(-0.3 * l)
        mod_lat = mods[l, :b].reshape(b, 1, 6, d)
        mod_ctx = jnp.broadcast_to(mods[l, b].reshape(1, 1, 6, d), (b, 1, 6, d))
        modl = jnp.concatenate([mod_ctx, mod_lat], axis=1)

        (rq, rk, rv, rg, dq, dk, dv, gq, gk, gv, mqn, mqr, mkn, mv, mkr) = _in_proj_call(
            xc, modl, norm_g[l, 0:1], _pack_w_in(w_in[l]), tables,
            jnp.tile(gqa_qk_norm[l, 0], N_HEADS)[None, :], jnp.tile(gqa_qk_norm[l, 1], GQA_KV_HEADS)[None, :],
            jnp.pad(mla_q_norm[l], (0, 256 - MLA_Q_RANK))[None, :], mla_kv_norm[l][None, :],
            _pack_w_uq(mla_w_uq[l]), _pack_w_ukv(mla_w_ukv[l]), s)

        log_g = -jnp.exp(ret_log_decay[l].astype(F32))
        o_f = _ret_call(rq, rk, rv, _ret_tables(log_g[0], False), s, False)
        ret_y = _ret_call(rq, rk, rv, _ret_tables(log_g[1], True), s, True,
                          (o_f, rg, ret_gn_w[l][None, :], ret_gn_b[l][None, :]))

        lp = diff_lambda[l].astype(F32)
        lam = jnp.exp(jnp.sum(lp[0] * lp[1])) - jnp.exp(jnp.sum(lp[2] * lp[3])) + lam_init
        dif_y = _attention(_heads(dq, 2 * N_HEADS), _heads_t(dk, 2 * N_HEADS), _heads_v(dv, N_HEADS), s, n_ctx,
                           not last, (lam, diff_subln[l], 1.0 - lam_init))

        gqa_y = _attention(_heads(gq, N_HEADS), _heads_t(gk, GQA_KV_HEADS), _heads_v(gv, GQA_KV_HEADS), s, n_ctx,
                           not last)

        q_m = jnp.concatenate([_heads(mqn, N_HEADS), _heads(mqr, N_HEADS)], axis=-1)
        kr_t = jnp.broadcast_to(mkr[:, None, :, :MLA_ROPE].transpose(0, 1, 3, 2), (b, N_HEADS, MLA_ROPE, n))
        kt_m = jnp.concatenate([_heads_t(mkn, N_HEADS), kr_t], axis=2)
        mla_y = _attention(q_m, kt_m, _heads_v(mv, N_HEADS), s, n_ctx, not last)

        ymix = jnp.concatenate([ret_y, dif_y, gqa_y, mla_y], axis=-1)
        n_rows = s if last else n
        x_new, hs, gates, pos, counts = _out_proj_call(
            ymix, w_out[l].astype(BF16), xc, modl, norm_g[l], router_w[l].T, router_b[l][:, None], s, n_rows)
        xc = _moe(x_new, hs, gates, pos, counts[..., 0], modl, norm_g[l], s,
                  exp_w1, exp_b1[l], exp_w2, exp_b2[l], l)

    return xc[:, :s]
```

```python
import functools
import math

import jax
import jax.numpy as jnp
from jax import lax
from jax.experimental import pallas as pl
from jax.experimental.pallas import tpu as pltpu

F32 = jnp.float32
BF16 = jnp.bfloat16

GRID_W = 64
ROPE_THETA = 10000.0
EPS = 1e-6
GROUP_WIDTH = 256
HEAD_DIM = 64
N_HEADS = 4
RET_CHUNK = 256
DIFF_D = 32
GQA_KV_HEADS = 2
MLA_Q_RANK = 192
MLA_KV_RANK = 128
MLA_NOPE = 64
MLA_ROPE = 32
N_EXPERTS = 32
TOP_K = 4
SWIGLU_LIMIT = 7.0
SWIGLU_ALPHA = 1.702
MOE_BLOCK = 512

LANES = 128
MXU_DIM = 256
TOKEN_TILE = 256
Q_TILE = 1024
KV_TILE = 512
VMEM_LIMIT = 48 * 1024 * 1024
DMA_SPLIT = 4
LOG2E = math.log2(math.e)

IN_PIECES = (
    ("ret_q", 256, 256), ("ret_k", 256, 256), ("ret_v", 256, 256), ("ret_g", 256, 256),
    ("dif_q", 256, 256), ("dif_k", 256, 256), ("dif_v", 256, 256),
    ("gqa_q", 256, 256), ("gqa_k", 128, 128), ("gqa_v", 128, 128),
    ("mla_cq", MLA_Q_RANK, 256), ("mla_ckv", MLA_KV_RANK, 128), ("mla_kr", MLA_ROPE, 128),
)
IN_OFFSETS = {}
_off = 0
for _name, _w, _pw in IN_PIECES:
    IN_OFFSETS[_name] = _off
    _off += _pw
IN_PACKED_WIDTH = _off


def _params(sem):
    return pltpu.CompilerParams(dimension_semantics=sem, vmem_limit_bytes=VMEM_LIMIT)


def _rms(x):
    return x * lax.rsqrt(jnp.mean(x * x, axis=-1, keepdims=True) + EPS)


def _split_bf16(a):
    hi = a.astype(BF16)
    lo = (a - hi.astype(F32)).astype(BF16)
    return hi, lo


def _dot_nt(a, b):
    return lax.dot_general(a, b, (((1,), (1,)), ((), ())), preferred_element_type=F32)


def _dot(a, b):
    return jnp.dot(a, b, preferred_element_type=F32)


def _sigmoid(a):
    return 1.0 / (1.0 + jnp.exp(-a))


def _pack_halves(a):
    w = a.shape[1] // 2
    bits = lax.bitcast_convert_type(a.astype(BF16).astype(F32), jnp.uint32)
    return bits[:, :w] | (bits[:, w:] >> 16)


def _unpack_halves(u):
    hi = lax.bitcast_convert_type(u & jnp.uint32(0xFFFF0000), F32)
    lo = lax.bitcast_convert_type(u << 16, F32)
    return jnp.concatenate([hi, lo], axis=1)


def _mod_spec(d, n_lat_tiles):
    return pl.BlockSpec((1, 1, 6, d), lambda i, t: (i, jnp.where(t < n_lat_tiles, 1, 0), 0, 0))


def _ada_kernel(c_ref, w_ref, b_ref, o_ref):
    s = c_ref[...]
    s = s * _sigmoid(s)
    s_hi, s_lo = _split_bf16(s)
    w_hi, w_lo = _split_bf16(w_ref[0])
    o_ref[0] = _dot(s_hi, w_hi) + _dot(s_hi, w_lo) + _dot(s_lo, w_hi) + b_ref[0]


def _ada_call(c_rows, ada_w, ada_b):
    depth, d, n6 = ada_w.shape
    rows = c_rows.shape[0]
    tn = 1536
    return pl.pallas_call(
        _ada_kernel,
        grid=(depth, n6 // tn),
        in_specs=[
            pl.BlockSpec((rows, d), lambda l, j: (0, 0)),
            pl.BlockSpec((1, d, tn), lambda l, j: (l, 0, j)),
            pl.BlockSpec((1, 1, tn), lambda l, j: (l, 0, j)),
        ],
        out_specs=pl.BlockSpec((1, rows, tn), lambda l, j: (l, 0, j)),
        out_shape=jax.ShapeDtypeStruct((depth, rows, n6), F32),
        compiler_params=_params(("parallel", "parallel")),
    )(c_rows, ada_w, ada_b.reshape(depth, 1, n6))


def _rope(x, cos, sin_signed, half):
    outs = []
    for c in range(x.shape[1] // LANES):
        sl = slice(c * LANES, (c + 1) * LANES)
        xc = x[:, sl]
        lane = lax.broadcasted_iota(jnp.int32, xc.shape, 1)
        first_half = (lane % (2 * half)) < half
        partner = jnp.where(first_half, pltpu.roll(xc, LANES - half, 1), pltpu.roll(xc, half, 1))
        outs.append(xc * cos[:, sl] + partner * sin_signed[:, sl])
    return outs[0] if len(outs) == 1 else jnp.concatenate(outs, axis=1)


def _group_mean(x, gsize):
    w = x.shape[1]
    r = lax.broadcasted_iota(jnp.int32, (w, w), 0) // gsize
    c = lax.broadcasted_iota(jnp.int32, (w, w), 1) // gsize
    ones = jnp.where(r == c, 1.0, 0.0).astype(BF16)
    hi, lo = _split_bf16(x)
    return (_dot(hi, ones) + _dot(lo, ones)) * (1.0 / gsize)


def _group_mean_sq(x, gsize):
    return _group_mean(x * x, gsize)


def _in_proj_kernel(x_ref, mod_ref, g_ref, w_ref, cos64_ref, sin64_ref, cos32_ref, sin32_ref,
                    gqn_ref, gkn_ref, mqn_ref, mkvn_ref, wuq_ref, wukv_ref,
                    rq_ref, rk_ref, rv_ref, rg_ref, dq_ref, dkt_ref, dv_ref, gq_ref, gkt_ref, gv_ref,
                    mqn_o, mqr_o, mkt_o, mv_o):
    mod = mod_ref[0, 0]
    h = _rms(x_ref[0]) * g_ref[...] * (1.0 + mod[1:2]) + mod[0:1]
    hb = h.astype(BF16)
    cos64, sin64 = cos64_ref[...], sin64_ref[...]
    cos32, sin32 = cos32_ref[...], sin32_ref[...]

    def proj(name, width):
        o = IN_OFFSETS[name]
        return _dot(hb, w_ref[:, o:o + width])

    def store_keys(o_ref, k, extra=None):
        kt = k.T
        d = kt.shape[0] // o_ref.shape[1]
        for hd in range(o_ref.shape[1]):
            rows = kt[hd * d:(hd + 1) * d]
            o_ref[0, hd] = (rows if extra is None else jnp.concatenate([rows, extra], axis=0)).astype(o_ref.dtype)

    def store_values(o_ref, v):
        dv = v.shape[1] // o_ref.shape[1]
        ones = jnp.ones((v.shape[0], dv), F32)
        for hd in range(o_ref.shape[1]):
            o_ref[0, hd] = jnp.concatenate([v[:, hd * dv:(hd + 1) * dv], ones], axis=1).astype(o_ref.dtype)

    rq_ref[0] = _rope(proj("ret_q", 256), cos64, sin64, 32).astype(BF16)
    rk_ref[0] = (_rope(proj("ret_k", 256), cos64, sin64, 32) * (HEAD_DIM ** -0.5)).astype(BF16)
    rv_ref[0] = proj("ret_v", 256).astype(BF16)
    rg_ref[0] = proj("ret_g", 256)
    dq_ref[0] = (_rope(proj("dif_q", 256), cos32, sin32, 16) * (DIFF_D ** -0.5 * LOG2E)).astype(BF16)
    store_keys(dkt_ref, _rope(proj("dif_k", 256), cos32, sin32, 16))
    store_values(dv_ref, proj("dif_v", 256))
    gq = proj("gqa_q", 256)
    gq = gq * lax.rsqrt(_group_mean_sq(gq, HEAD_DIM) + EPS) * gqn_ref[...]
    gq_ref[0] = (_rope(gq, cos64, sin64, 32) * (HEAD_DIM ** -0.5 * LOG2E)).astype(BF16)
    gk = proj("gqa_k", 128)
    gk = gk * lax.rsqrt(_group_mean_sq(gk, HEAD_DIM) + EPS) * gkn_ref[...]
    store_keys(gkt_ref, _rope(gk, cos64[:, :LANES], sin64[:, :LANES], 32))
    store_values(gv_ref, proj("gqa_v", 128))
    cq = proj("mla_cq", 256)
    cq = cq * lax.rsqrt(jnp.sum(cq * cq, axis=-1, keepdims=True) * (1.0 / MLA_Q_RANK) + EPS) * mqn_ref[...]
    q_up = _dot(cq.astype(BF16), wuq_ref[...])
    mla_scale = (MLA_NOPE + MLA_ROPE) ** -0.5 * LOG2E
    mqn_o[0] = (q_up[:, :256] * mla_scale).astype(BF16)
    mqr_o[0] = (_rope(q_up[:, 256:], cos32[:, :LANES], sin32[:, :LANES], 16) * mla_scale).astype(BF16)
    ckv = proj("mla_ckv", 128)
    ckv = ckv * lax.rsqrt(jnp.mean(ckv * ckv, axis=-1, keepdims=True) + EPS) * mkvn_ref[...]
    kv_up = _dot(ckv.astype(BF16), wukv_ref[...])
    kr = _rope(proj("mla_kr", 128), cos32[:, :LANES], sin32[:, :LANES], 16)
    store_keys(mkt_o, kv_up[:, :256], extra=kr.T[:MLA_ROPE])
    store_values(mv_o, kv_up[:, 256:])


def _in_proj_call(xc, modl, g0, w_in_p, tables, gqn, gkn, mqn, mkvn, wuq_p, wukv_p, n_lat):
    b, n, d = xc.shape
    tm = TOKEN_TILE
    tok = lambda w: pl.BlockSpec((1, tm, w), lambda i, t: (i, t, 0))
    const2 = lambda a: pl.BlockSpec(a.shape, lambda i, t: (0, 0))
    tab = pl.BlockSpec((tm, 256), lambda i, t: (t, 0))
    keys_t = lambda h, dk: pl.BlockSpec((1, h, dk, tm), lambda i, t: (i, 0, 0, t))
    vals = lambda h: pl.BlockSpec((1, h, tm, 2 * HEAD_DIM), lambda i, t: (i, 0, t, 0))
    sds = jax.ShapeDtypeStruct
    outs = [
        (tok(256), sds((b, n, 256), BF16)), (tok(256), sds((b, n, 256), BF16)),
        (tok(256), sds((b, n, 256), BF16)), (tok(256), sds((b, n, 256), F32)),
        (tok(256), sds((b, n, 256), BF16)),
        (keys_t(2 * N_HEADS, DIFF_D), sds((b, 2 * N_HEADS, DIFF_D, n), BF16)),
        (vals(N_HEADS), sds((b, N_HEADS, n, 2 * HEAD_DIM), BF16)),
        (tok(256), sds((b, n, 256), BF16)),
        (keys_t(GQA_KV_HEADS, HEAD_DIM), sds((b, GQA_KV_HEADS, HEAD_DIM, n), BF16)),
        (vals(GQA_KV_HEADS), sds((b, GQA_KV_HEADS, n, 2 * HEAD_DIM), BF16)),
        (tok(256), sds((b, n, 256), BF16)), (tok(128), sds((b, n, 128), BF16)),
        (keys_t(N_HEADS, MLA_NOPE + MLA_ROPE), sds((b, N_HEADS, MLA_NOPE + MLA_ROPE, n), BF16)),
        (vals(N_HEADS), sds((b, N_HEADS, n, 2 * HEAD_DIM), BF16)),
    ]
    return pl.pallas_call(
        _in_proj_kernel,
        grid=(b, n // tm),
        in_specs=[
            tok(d), _mod_spec(d, n_lat // tm),
            const2(g0), const2(w_in_p), tab, tab, tab, tab,
            const2(gqn), const2(gkn), const2(mqn), const2(mkvn), const2(wuq_p), const2(wukv_p),
        ],
        out_specs=[o[0] for o in outs],
        out_shape=[o[1] for o in outs],
        compiler_params=_params(("parallel", "parallel")),
    )(xc, modl, g0, w_in_p, *tables, gqn, gkn, mqn, mkvn, wuq_p, wukv_p)


def _ret_kernel(*refs, readout):
    if readout:
        (q_ref, k_ref, v_ref, dec_ref, xi_ref, zeta_ref, gc_ref, of_ref, g_ref, gnw_ref, gnb_ref,
         o_ref, state_ref) = refs
    else:
        q_ref, k_ref, v_ref, dec_ref, xi_ref, zeta_ref, gc_ref, o_ref, state_ref = refs

    @pl.when(pl.program_id(1) == 0)
    def _():
        state_ref[...] = jnp.zeros_like(state_ref)

    q, k, v = q_ref[0], k_ref[0], v_ref[0]
    outs = []
    for h in range(N_HEADS):
        sl = slice(h * HEAD_DIM, (h + 1) * HEAD_DIM)
        qh, kh, vh = q[:, sl], k[:, sl], v[:, sl]
        st = state_ref[h]
        inner = _dot_nt(qh, kh) * dec_ref[h]
        o = _dot(inner.astype(BF16), vh) + _dot(qh, st.astype(BF16)) * xi_ref[h]
        kz = (kh.astype(F32) * zeta_ref[h]).astype(BF16)
        kv = lax.dot_general(kz, vh, (((0,), (0,)), ((), ())), preferred_element_type=F32)
        state_ref[h] = gc_ref[h] * st + kv
        outs.append(o)
    o = jnp.concatenate(outs, axis=1)
    if readout:
        o = o + of_ref[0]
        o = o - _group_mean(o, HEAD_DIM)
        o = o * lax.rsqrt(_group_mean_sq(o, HEAD_DIM) + EPS)
        g = g_ref[0]
        o_ref[0] = ((o * gnw_ref[...] + gnb_ref[...]) * (g * _sigmoid(g))).astype(o_ref.dtype)
    else:
        o_ref[0] = o


def _ret_call(q, k, v, tabs, n_lat, backward, readout_args=None):
    b, n, w = q.shape
    c = RET_CHUNK
    nch = n // c
    nlc = n_lat // c
    if backward:
        chunk = lambda t: nch - 1 - t
    else:
        chunk = lambda t: jnp.where(t < nch - nlc, nlc + t, t - (nch - nlc))
    tok = lambda dt_w: pl.BlockSpec((1, c, dt_w), lambda i, t: (i, chunk(t), 0))
    const = lambda a: pl.BlockSpec(a.shape, lambda i, t: (0,) * a.ndim)
    in_specs = [tok(w), tok(w), tok(w)] + [const(a) for a in tabs]
    args = [q, k, v, *tabs]
    if readout_args is not None:
        o_f, g, gnw, gnb = readout_args
        in_specs += [tok(w), tok(w), const(gnw), const(gnb)]
        args += [o_f, g, gnw, gnb]
    out_dtype = BF16 if readout_args is not None else F32
    return pl.pallas_call(
        functools.partial(_ret_kernel, readout=readout_args is not None),
        grid=(b, nch),
        in_specs=in_specs,
        out_specs=tok(w),
        out_shape=jax.ShapeDtypeStruct((b, n, w), out_dtype),
        scratch_shapes=[pltpu.VMEM((N_HEADS, HEAD_DIM, HEAD_DIM), F32)],
        compiler_params=_params(("parallel", "arbitrary")),
    )(*args)


def _ret_tables(log_g, backward):
    c = RET_CHUNK
    pos = jnp.arange(c, dtype=F32)
    dist = (pos[None, :] - pos[:, None]) if backward else (pos[:, None] - pos[None, :])
    lg = log_g[:, None, None]
    decay = jnp.where(dist >= 0, jnp.exp(lg * jnp.maximum(dist, 0.0)), 0.0)
    to_state = (c - pos) if backward else (pos + 1.0)
    to_end = pos if backward else (c - 1.0 - pos)
    xi = jnp.exp(log_g[:, None] * to_state)[:, :, None]
    zeta = jnp.exp(log_g[:, None] * to_end)[:, :, None]
    ones = jnp.ones((1, 1, HEAD_DIM), F32)
    gc = jnp.exp(log_g * c)[:, None, None] * jnp.ones((1, HEAD_DIM, HEAD_DIM), F32)
    return decay, xi * ones, zeta * ones, gc


def _attn_kernel(*refs, n_maps, readout_scale, n_q, hq):
    refs = list(refs)
    lam_ref = refs.pop(0) if n_maps == 2 else None
    q_parts = [refs.pop(0) for _ in range(n_q)]
    kt_ref, v_ref = refs.pop(0), refs.pop(0)
    sub_ref = refs.pop(0) if n_maps == 2 else None
    o_ref, q_scr = refs.pop(0), refs.pop(0)
    first_ref = refs.pop(0) if n_maps == 2 else None
    hm = pl.program_id(2)
    for hh in range(hq):
        @pl.when(hm == hh)
        def _(hh=hh):
            pieces = [qp[0][:, hh * (qp.shape[2] // hq):(hh + 1) * (qp.shape[2] // hq)] for qp in q_parts]
            q_scr[...] = pieces[0] if n_q == 1 else jnp.concatenate(pieces, axis=1)
    q = q_scr[...]
    tq = q.shape[0]
    n_keys = kt_ref.shape[3]
    dv = v_ref.shape[3] // 2
    tk = min(KV_TILE, n_keys)
    n_full, tail = divmod(n_keys, tk)

    def step(off, size, carry):
        m, acc = carry
        s = _dot(q, kt_ref[0, 0, :, pl.ds(off, size)])
        m_new = jnp.maximum(m, jnp.max(s, axis=-1, keepdims=True))
        p = jnp.exp2(s - m_new)
        acc = jnp.exp2(m - m_new) * acc + _dot(p.astype(BF16), v_ref[0, 0, pl.ds(off, size), :])
        return m_new, acc

    carry = (jnp.full((tq, 1), -1e30, F32), jnp.zeros((tq, 2 * dv), F32))
    for j in range(n_full):
        carry = step(j * tk, tk, carry)
    if tail:
        carry = step(n_full * tk, tail, carry)
    acc = carry[1]
    o = acc[:, :dv] / acc[:, dv:]

    def store(head_of_step, val):
        for hh in range(N_HEADS):
            @pl.when(head_of_step == hh)
            def _(hh=hh):
                o_ref[0, :, hh * dv:(hh + 1) * dv] = val.astype(o_ref.dtype)

    if n_maps == 1:
        store(hm, o)
    else:
        @pl.when(hm % 2 == 0)
        def _():
            first_ref[...] = o

        @pl.when(hm % 2 == 1)
        def _():
            od = first_ref[...] - lam_ref[0] * o
            od = od * lax.rsqrt(jnp.mean(od * od, axis=-1, keepdims=True) + EPS) * sub_ref[...]
            store(hm // 2, od * readout_scale)


def _attn_call(q_parts, hq, kt, v, rows, keys, tq, diff=None):
    b, hk, d, n = kt.shape
    hv, dv = v.shape[1], v.shape[3] // 2
    (q0, qn), (k0, kn) = rows, keys
    assert q0 % tq == 0 and qn % tq == 0 and k0 % kn == 0
    assert sum(qp.shape[2] // hq for qp in q_parts) == d
    n_maps = 2 if diff is not None else 1
    in_specs = [pl.BlockSpec((1, tq, qp.shape[2]), lambda i, t, h: (i, q0 // tq + t, 0)) for qp in q_parts] + [
        pl.BlockSpec((1, 1, d, kn), lambda i, t, h: (i, h // (hq // hk), 0, k0 // kn)),
        pl.BlockSpec((1, 1, kn, 2 * dv), lambda i, t, h: (i, h // (hq // hv), k0 // kn, 0)),
    ]
    args = [*q_parts, kt, v]
    scratch = [pltpu.VMEM((tq, d), BF16)]
    scale = 1.0
    if diff is not None:
        lam, subln, scale = diff
        in_specs = [pl.BlockSpec(memory_space=pltpu.SMEM)] + in_specs + [pl.BlockSpec((1, dv), lambda i, t, h: (0, 0))]
        args = [lam.reshape(1)] + args + [subln.reshape(1, dv)]
        scratch.append(pltpu.VMEM((tq, dv), F32))
    return pl.pallas_call(
        functools.partial(_attn_kernel, n_maps=n_maps, readout_scale=scale, n_q=len(q_parts), hq=hq),
        grid=(b, qn // tq, hq),
        in_specs=in_specs,
        out_specs=pl.BlockSpec((1, tq, (hq // n_maps) * dv), lambda i, t, h: (i, t, 0)),
        out_shape=jax.ShapeDtypeStruct((b, qn, (hq // n_maps) * dv), BF16),
        scratch_shapes=scratch,
        compiler_params=_params(("parallel", "parallel", "arbitrary")),
    )(*args)


def _out_proj_kernel(y_ref, w_ref, x_ref, mod_ref, g_ref, wr_ref, br_ref,
                     xo_ref, hs_ref, gate_ref, pos_ref, cnt_ref):
    mod = mod_ref[0, 0]
    g = g_ref[...]
    y = _dot(y_ref[0], w_ref[...])
    xn = x_ref[0] + mod[2:3] * (_rms(y) * g[1:2])
    xo_ref[0] = xn
    h = _rms(xn) * g[2:3] * (1.0 + mod[4:5]) + mod[3:4]
    w_hi, w_lo = _split_bf16(wr_ref[...])
    h_hi, h_lo = _split_bf16(h)
    logits = _dot_nt(w_hi, h_hi) + _dot_nt(w_hi, h_lo) + _dot_nt(w_lo, h_hi) + br_ref[...]
    row = lax.broadcasted_iota(jnp.int32, logits.shape, 0)
    vals = logits
    tops, idxs = [], []
    for _ in range(TOP_K):
        m = jnp.max(vals, axis=0, keepdims=True)
        idx = jnp.min(jnp.where(vals == m, row, N_EXPERTS), axis=0, keepdims=True)
        tops.append(m)
        idxs.append(idx)
        vals = jnp.where(row == idx, -jnp.inf, vals)
    ex = [jnp.exp(tv - tops[0]) for tv in tops]
    den = ex[0] + ex[1] + ex[2] + ex[3]
    gate_ref[0] = jnp.concatenate([e / den for e in ex], axis=0)
    tm = logits.shape[1]
    before = jnp.where(lax.broadcasted_iota(jnp.int32, (tm, tm), 0) < lax.broadcasted_iota(jnp.int32, (tm, tm), 1),
                       1.0, 0.0).astype(BF16)
    counts = jnp.zeros((N_EXPERTS, 1), F32)
    onehots, ranks = [], []
    for idx in idxs:
        onehot = jnp.where(row == idx, 1.0, 0.0)
        ranks.append(_dot(onehot.astype(BF16), before) + counts)
        onehots.append(onehot)
        counts = counts + jnp.sum(onehot, axis=1, keepdims=True)
    counts_b = jnp.broadcast_to(counts, (N_EXPERTS, LANES))
    earlier = jnp.where(lax.broadcasted_iota(jnp.int32, (N_EXPERTS, N_EXPERTS), 0)
                        > lax.broadcasted_iota(jnp.int32, (N_EXPERTS, N_EXPERTS), 1), 1.0, 0.0).astype(BF16)
    first = _dot(earlier, counts_b.astype(BF16))[:, 0:1]
    pos = [jnp.sum(oh * (rk + first), axis=0, keepdims=True).astype(jnp.int32) for oh, rk in zip(onehots, ranks)]
    pos_ref[0] = jnp.concatenate(pos, axis=0)
    cnt_ref[0, 0] = counts_b.astype(jnp.int32)
    slot = lax.broadcasted_iota(jnp.int32, (TOP_K * tm, tm), 0)
    pick = jnp.where(slot == pos[0], 1.0, 0.0)
    for p in pos[1:]:
        pick = jnp.where(slot == p, 1.0, pick)
    hs_ref[0, 0] = _pack_halves(_dot(pick.astype(BF16), h.astype(BF16)))


def _out_proj_call(ymix, w_out, xc, modl, g, wr_t, br, n_lat, n_rows):
    b, n, d = xc.shape
    tm = TOKEN_TILE
    tok = lambda w: pl.BlockSpec((1, tm, w), lambda i, t: (i, t, 0))
    const2 = lambda a: pl.BlockSpec(a.shape, lambda i, t: (0, 0))
    sel = pl.BlockSpec((1, TOP_K, tm), lambda i, t: (i, 0, t))
    return pl.pallas_call(
        _out_proj_kernel,
        grid=(b, n_rows // tm),
        in_specs=[
            tok(ymix.shape[2]), const2(w_out), tok(d), _mod_spec(d, n_lat // tm),
            const2(g), const2(wr_t), const2(br),
        ],
        out_specs=[tok(d),
                   pl.BlockSpec((1, 1, TOP_K * tm, d // 2), lambda i, t: (i, t, 0, 0)),
                   sel, sel,
                   pl.BlockSpec((1, 1, N_EXPERTS, LANES), lambda i, t: (i, t, 0, 0))],
        out_shape=[
            jax.ShapeDtypeStruct((b, n_rows, d), F32),
            jax.ShapeDtypeStruct((b, n_rows // tm, TOP_K * tm, d // 2), jnp.uint32),
            jax.ShapeDtypeStruct((b, TOP_K, n_rows), F32),
            jax.ShapeDtypeStruct((b, TOP_K, n_rows), jnp.int32),
            jax.ShapeDtypeStruct((b, n_rows // tm, N_EXPERTS, LANES), jnp.int32),
        ],
        compiler_params=_params(("parallel", "parallel")),
    )(ymix, w_out, xc, modl, g, wr_t, br)


def _w1_prep_kernel(*refs):
    w_refs, (g_ref, l_ref) = refs[:DMA_SPLIT], refs[DMA_SPLIT:]
    r = lax.broadcasted_iota(jnp.int32, (MXU_DIM, MXU_DIM), 0)
    c = lax.broadcasted_iota(jnp.int32, (MXU_DIM, MXU_DIM), 1)
    src = jnp.where(c < LANES, 2 * c, 2 * (c - LANES) + 1)
    perm = jnp.where(r == src, 1.0, 0.0).astype(BF16)
    per_slab = w_refs[0].shape[3] // MXU_DIM
    for s, w_ref in enumerate(w_refs):
        for jj in range(per_slab):
            j = s * per_slab + jj
            out = _dot(w_ref[0, 0, :, jj * MXU_DIM:(jj + 1) * MXU_DIM].astype(BF16), perm)
            g_ref[0, :, j * LANES:(j + 1) * LANES] = out[:, :LANES].astype(BF16)
            l_ref[0, :, j * LANES:(j + 1) * LANES] = out[:, LANES:].astype(BF16)


def _w1_prep_call(w1_all, layer):
    _, e, d, f2 = w1_all.shape
    f = f2 // 2
    rows = 256
    out_spec = pl.BlockSpec((1, rows, f), lambda i, j: (i, j, 0))
    return pl.pallas_call(
        _w1_prep_kernel,
        grid=(e, d // rows),
        in_specs=[pl.BlockSpec((1, 1, rows, f2 // DMA_SPLIT), lambda i, j, s=s: (layer, i, j, s))
                  for s in range(DMA_SPLIT)],
        out_specs=[out_spec, out_spec],
        out_shape=[jax.ShapeDtypeStruct((e, d, f), BF16)] * 2,
        compiler_params=_params(("parallel", "parallel")),
    )(*([w1_all] * DMA_SPLIT))


def _expert_kernel(be_ref, nu_ref, x_ref, w1g_ref, w1l_ref, b1g_ref, b1l_ref, *refs):
    w2_refs, (b2_ref, o_ref, w2b_ref) = refs[:DMA_SPLIT], refs[DMA_SPLIT:]
    i = pl.program_id(0)

    @pl.when(jnp.logical_or(i == 0, be_ref[i] != be_ref[jnp.maximum(i - 1, 0)]))
    def _():
        rows = w2_refs[0].shape[2]
        for s, w2_ref in enumerate(w2_refs):
            w2b_ref[s * rows:(s + 1) * rows, :] = w2_ref[0, 0].astype(BF16)

    @pl.when(i < nu_ref[0])
    def _():
        x = _unpack_halves(x_ref[...]).astype(BF16)
        glu = jnp.minimum(_dot(x, w1g_ref[0]) + b1g_ref[0], SWIGLU_LIMIT)
        lin = jnp.clip(_dot(x, w1l_ref[0]) + b1l_ref[0], -SWIGLU_LIMIT, SWIGLU_LIMIT)
        act = glu * _sigmoid(SWIGLU_ALPHA * glu) * (lin + 1.0)
        o_ref[...] = _pack_halves(_dot(act.astype(BF16), w2b_ref[...]) + b2_ref[0])

    @pl.when(i >= nu_ref[0])
    def _():
        o_ref[...] = jnp.zeros_like(o_ref)


def _expert_call(block_expert, n_used, x_rows, w1g, w1l, b1g, b1l, w2_all, layer, b2):
    r = x_rows.shape[0]
    d, f = w1g.shape[1:]
    tm = MOE_BLOCK
    wspec = lambda s: pl.BlockSpec((1,) + s, lambda i, be, nu: (be[i], 0, 0))
    return pl.pallas_call(
        _expert_kernel,
        grid_spec=pltpu.PrefetchScalarGridSpec(
            num_scalar_prefetch=2,
            grid=(r // tm,),
            in_specs=[
                pl.BlockSpec((tm, d // 2), lambda i, be, nu: (i, 0)),
                wspec((d, f)), wspec((d, f)), wspec((1, f)), wspec((1, f)),
                *[pl.BlockSpec((1, 1, f // DMA_SPLIT, d), lambda i, be, nu, s=s: (layer, be[i], s, 0))
                  for s in range(DMA_SPLIT)],
                wspec((1, d)),
            ],
            out_specs=pl.BlockSpec((tm, d // 2), lambda i, be, nu: (i, 0)),
            scratch_shapes=[pltpu.VMEM((f, d), BF16)],
        ),
        out_shape=jax.ShapeDtypeStruct((r, d // 2), jnp.uint32),
        compiler_params=_params(("arbitrary",)),
    )(block_expert, n_used, x_rows, w1g, w1l, b1g, b1l, *([w2_all] * DMA_SPLIT), b2)


def _combine_kernel(*refs, n_parts):
    x_ref, y_refs = refs[0], refs[1:1 + n_parts]
    gate_ref, pos_ref, mod_ref, g_ref, o_ref = refs[1 + n_parts:]
    gates, pos = gate_ref[0], pos_ref[0]
    lane = lax.broadcasted_iota(jnp.int32, (gates.shape[0], y_refs[0].shape[2]), 1)
    mix = jnp.zeros(lane.shape, F32)
    for k in range(TOP_K):
        mix = jnp.where(lane == pos[:, k:k + 1], gates[:, k:k + 1], mix)
    mix_hi, mix_lo = _split_bf16(mix)
    mod = mod_ref[0, 0]
    part_batches = pl.num_programs(0) // n_parts
    for p, y_ref in enumerate(y_refs):
        @pl.when(pl.program_id(0) // part_batches == p)
        def _(y_ref=y_ref):
            rows = _unpack_halves(y_ref[0, 0]).astype(BF16)
            y = _dot(mix_hi, rows) + _dot(mix_lo, rows)
            o_ref[0] = x_ref[0] + mod[5:6] * (_rms(y) * g_ref[...][3:4])


def _combine_call(xc, picked_parts, gates, pos, modl, g, n_lat):
    b, n, d = xc.shape
    tm = TOKEN_TILE
    n_parts = len(picked_parts)
    bp = b // n_parts
    tok = pl.BlockSpec((1, tm, d), lambda i, t: (i, t, 0))

    def part_spec(p):
        def index(i, t):
            live = (i >= p * bp) & (i < (p + 1) * bp)
            return jnp.clip(i - p * bp, 0, bp - 1), jnp.where(live, t, 0), 0, 0
        return pl.BlockSpec((1, 1, TOP_K * tm, d // 2), index)

    return pl.pallas_call(
        functools.partial(_combine_kernel, n_parts=n_parts),
        grid=(b, n // tm),
        in_specs=[tok] + [part_spec(p) for p in range(n_parts)] + [
            pl.BlockSpec((1, tm, TOP_K), lambda i, t: (i, t, 0)),
            pl.BlockSpec((1, tm, TOP_K), lambda i, t: (i, t, 0)),
            _mod_spec(d, n_lat // tm),
            pl.BlockSpec(g.shape, lambda i, t: (0, 0)),
        ],
        out_specs=tok,
        out_shape=jax.ShapeDtypeStruct((b, n, d), F32),
        compiler_params=_params(("parallel", "parallel")),
    )(xc, *picked_parts, gates, pos, modl, g)


def _rope_tables(rows, n_ctx, rot_dim):
    row = jnp.repeat(jnp.arange(rows), GRID_W)
    col = jnp.tile(jnp.arange(GRID_W), rows)
    n_freq = rot_dim // 4
    inv = ROPE_THETA ** (-jnp.arange(n_freq, dtype=F32) / n_freq)
    ang = jnp.concatenate([row[:, None] * inv, col[:, None] * inv], axis=-1)
    cos, sin = jnp.cos(ang), jnp.sin(ang)
    reps = 256 // rot_dim
    cos_t = jnp.tile(jnp.concatenate([cos, cos], axis=-1), (1, reps))
    sin_t = jnp.tile(jnp.concatenate([-sin, sin], axis=-1), (1, reps))
    cos_t = jnp.concatenate([cos_t, jnp.ones((n_ctx, 256), F32)], axis=0)
    sin_t = jnp.concatenate([sin_t, jnp.zeros((n_ctx, 256), F32)], axis=0)
    return cos_t, sin_t


def _pack_w_in(w):
    parts, src = [], 0
    for _, width, padded in IN_PIECES:
        parts.append(w[:, src:src + width])
        if padded > width:
            parts.append(jnp.zeros((w.shape[0], padded - width), w.dtype))
        src += width
    return jnp.concatenate(parts, axis=1).astype(BF16)


def _pack_w_uq(w):
    wh = w.reshape(MLA_Q_RANK, N_HEADS, MLA_NOPE + MLA_ROPE)
    nope = wh[:, :, :MLA_NOPE].reshape(MLA_Q_RANK, N_HEADS * MLA_NOPE)
    rope = wh[:, :, MLA_NOPE:].reshape(MLA_Q_RANK, N_HEADS * MLA_ROPE)
    packed = jnp.concatenate([nope, rope], axis=1)
    return jnp.pad(packed, ((0, 256 - MLA_Q_RANK), (0, 0))).astype(BF16)


def _pack_w_ukv(w):
    wh = w.reshape(MLA_KV_RANK, N_HEADS, MLA_NOPE + HEAD_DIM)
    kn = wh[:, :, :MLA_NOPE].reshape(MLA_KV_RANK, N_HEADS * MLA_NOPE)
    vv = wh[:, :, MLA_NOPE:].reshape(MLA_KV_RANK, N_HEADS * HEAD_DIM)
    return jnp.concatenate([kn, vv], axis=1).astype(BF16)


def _attention(q_parts, hq, kt, v, n_lat, n_ctx, ctx_out, diff=None):
    n = n_lat + n_ctx
    y = _attn_call(q_parts, hq, kt, v, (0, n_lat), (0, n), min(Q_TILE, n_lat), diff)
    if ctx_out:
        y_c = _attn_call(q_parts, hq, kt, v, (n_lat, n_ctx), (n_lat, n_ctx), n_ctx, diff)
    else:
        y_c = jnp.zeros((y.shape[0], n_ctx, y.shape[2]), y.dtype)
    return jnp.concatenate([y, y_c], axis=1)


def _steps(x, bounds, offsets):
    deltas = offsets - jnp.concatenate([jnp.zeros_like(offsets[..., :1]), offsets[..., :-1]], axis=-1)
    return x + jnp.sum(jnp.where(x[..., None] >= bounds[..., None, :], deltas[..., None, :], 0), axis=-1)


def _moe(x_new, hs, gates, pos, counts, modl, g, n_lat, w1_all, b1, w2_all, b2, layer):
    b, n, d = x_new.shape
    f = w2_all.shape[2]
    tile_rows = TOP_K * TOKEN_TILE
    n_parts = 2 if b % 2 == 0 else 1
    tiles = (b // n_parts) * (n // TOKEN_TILE)
    n_blocks = -(-(tiles * tile_rows) // MOE_BLOCK) + N_EXPERTS
    hs_rows = hs.reshape(b * (n // TOKEN_TILE) * tile_rows, d // 2)
    w1g, w1l = _w1_prep_call(w1_all, layer)
    b1g, b1l = b1[:, 0::2].reshape(N_EXPERTS, 1, f), b1[:, 1::2].reshape(N_EXPERTS, 1, f)
    picked_parts = []
    for p in range(n_parts):
        tile_counts = counts.reshape(-1, N_EXPERTS)[p * tiles:(p + 1) * tiles]
        tile_base = jnp.cumsum(tile_counts, axis=0) - tile_counts
        first = jnp.cumsum(tile_counts, axis=1) - tile_counts
        padded = (jnp.sum(tile_counts, axis=0) + MOE_BLOCK - 1) // MOE_BLOCK * MOE_BLOCK
        pad_ends = jnp.cumsum(padded)
        pad_start = pad_ends - padded
        block_start = jnp.arange(n_blocks, dtype=jnp.int32) * MOE_BLOCK
        block_expert = jnp.minimum(jnp.sum((pad_ends[None, :] <= block_start[:, None]).astype(jnp.int32), axis=1),
                                   N_EXPERTS - 1)
        n_used = (pad_ends[-1:] // MOE_BLOCK).astype(jnp.int32)
        j = block_start[:, None] + jnp.arange(MOE_BLOCK, dtype=jnp.int32)[None, :] - pad_start[block_expert][:, None]
        tile_offset = (jnp.arange(tiles, dtype=jnp.int32)[:, None] + p * tiles) * tile_rows + first - tile_base
        src = _steps(j, tile_base.T[block_expert], tile_offset.T[block_expert])
        src = jnp.clip(src, 0, hs_rows.shape[0] - 1)
        x_rows = hs_rows.at[src.reshape(-1)].get(mode="promise_in_bounds")
        y_rows = _expert_call(block_expert, n_used, x_rows, w1g, w1l, b1g, b1l, w2_all, layer,
                              b2.reshape(N_EXPERTS, 1, d))
        slot = jnp.broadcast_to(jnp.arange(tile_rows, dtype=jnp.int32)[None, :], (tiles, tile_rows))
        back = _steps(slot, first, tile_base + pad_start[None, :] - first)
        picked = y_rows.at[back.reshape(-1)].get(mode="promise_in_bounds")
        picked_parts.append(picked.reshape(b // n_parts, n // TOKEN_TILE, tile_rows, d // 2))
    return _combine_call(x_new, picked_parts, gates.transpose(0, 2, 1), pos.transpose(0, 2, 1), modl, g, n_lat)


def kernel(x, c, ctx, c_ctx, ada_w, ada_b, norm_g, w_in, w_out, ret_log_decay, ret_gn_w, ret_gn_b,
           diff_lambda, diff_subln, gqa_qk_norm, mla_q_norm, mla_kv_norm, mla_w_uq, mla_w_ukv,
           router_w, router_b, exp_w1, exp_b1, exp_w2, exp_b2):
    b, s, d = x.shape
    n_ctx = ctx.shape[1]
    n = s + n_ctx
    depth = ada_w.shape[0]
    assert n_ctx % TOKEN_TILE == 0 and s % TOKEN_TILE == 0 and s % GRID_W == 0
    assert n_ctx % RET_CHUNK == 0 and s % min(Q_TILE, s) == 0 and s % n_ctx == 0

    tables = _rope_tables(s // GRID_W, n_ctx, HEAD_DIM) + _rope_tables(s // GRID_W, n_ctx, DIFF_D)
    c_rows = jnp.zeros((16, d), F32).at[:b].set(c).at[b].set(c_ctx)
    mods = _ada_call(c_rows, ada_w, ada_b)
    xc = jnp.concatenate([x, ctx], axis=1)

    for l in range(depth):
        last = l == depth - 1
        lam_init = 0.8 - 0.6 * math.exp(-0.3 * l)
        mod_lat = mods[l, :b].reshape(b, 1, 6, d)
        mod_ctx = jnp.broadcast_to(mods[l, b].reshape(1, 1, 6, d), (b, 1, 6, d))
        modl = jnp.concatenate([mod_ctx, mod_lat], axis=1)

        (rq, rk, rv, rg, dq, dkt, dv, gq, gkt, gv, mqn, mqr, mkt, mv) = _in_proj_call(
            xc, modl, norm_g[l, 0:1], _pack_w_in(w_in[l]), tables,
            jnp.tile(gqa_qk_norm[l, 0], N_HEADS)[None, :], jnp.tile(gqa_qk_norm[l, 1], GQA_KV_HEADS)[None, :],
            jnp.pad(mla_q_norm[l], (0, 256 - MLA_Q_RANK))[None, :], mla_kv_norm[l][None, :],
            _pack_w_uq(mla_w_uq[l]), _pack_w_ukv(mla_w_ukv[l]), s)

        log_g = -jnp.exp(ret_log_decay[l].astype(F32))
        o_f = _ret_call(rq, rk, rv, _ret_tables(log_g[0], False), s, False)
        ret_y = _ret_call(rq, rk, rv, _ret_tables(log_g[1], True), s, True,
                          (o_f, rg, ret_gn_w[l][None, :], ret_gn_b[l][None, :]))

        lp = diff_lambda[l].astype(F32)
        lam = jnp.exp(jnp.sum(lp[0] * lp[1])) - jnp.exp(jnp.sum(lp[2] * lp[3])) + lam_init
        dif_y = _attention([dq], 2 * N_HEADS, dkt, dv, s, n_ctx, not last, (lam, diff_subln[l], 1.0 - lam_init))

        gqa_y = _attention([gq], N_HEADS, gkt, gv, s, n_ctx, not last)

        mla_y = _attention([mqn, mqr], N_HEADS, mkt, mv, s, n_ctx, not last)

        ymix = jnp.concatenate([ret_y, dif_y, gqa_y, mla_y], axis=-1)
        n_rows = s if last else n
        x_new, hs, gates, pos, counts = _out_proj_call(
            ymix, w_out[l].astype(BF16), xc, modl, norm_g[l], router_w[l].T, router_b[l][:, None], s, n_rows)
        xc = _moe(x_new, hs, gates, pos, counts[..., 0], modl, norm_g[l], s,
                  exp_w1, exp_b1[l], exp_w2, exp_b2[l], l)

    return xc[:, :s]
```

```python
import functools
import math

import jax
import jax.numpy as jnp
from jax import lax
from jax.experimental import pallas as pl
from jax.experimental.pallas import tpu as pltpu

F32 = jnp.float32
BF16 = jnp.bfloat16

GRID_W = 64
ROPE_THETA = 10000.0
EPS = 1e-6
GROUP_WIDTH = 256
HEAD_DIM = 64
N_HEADS = 4
RET_CHUNK = 256
DIFF_D = 32
GQA_KV_HEADS = 2
MLA_Q_RANK = 192
MLA_KV_RANK = 128
MLA_NOPE = 64
MLA_ROPE = 32
N_EXPERTS = 32
TOP_K = 4
SWIGLU_LIMIT = 7.0
SWIGLU_ALPHA = 1.702
MOE_BLOCK = 512

LANES = 128
MXU_DIM = 256
TOKEN_TILE = 256
Q_TILE = 1024
KV_TILE = 512
VMEM_LIMIT = 48 * 1024 * 1024
DMA_SPLIT = 4
LOG2E = math.log2(math.e)

IN_PIECES = (
    ("ret_q", 256, 256), ("ret_k", 256, 256), ("ret_v", 256, 256), ("ret_g", 256, 256),
    ("dif_q", 256, 256), ("dif_k", 256, 256), ("dif_v", 256, 256),
    ("gqa_q", 256, 256), ("gqa_k", 128, 128), ("gqa_v", 128, 128),
    ("mla_cq", MLA_Q_RANK, 256), ("mla_ckv", MLA_KV_RANK, 128), ("mla_kr", MLA_ROPE, 128),
)
IN_OFFSETS = {}
_off = 0
for _name, _w, _pw in IN_PIECES:
    IN_OFFSETS[_name] = _off
    _off += _pw
IN_PACKED_WIDTH = _off


def _params(sem):
    return pltpu.CompilerParams(dimension_semantics=sem, vmem_limit_bytes=VMEM_LIMIT)


def _rms(x):
    return x * lax.rsqrt(jnp.mean(x * x, axis=-1, keepdims=True) + EPS)


def _split_bf16(a):
    hi = a.astype(BF16)
    lo = (a - hi.astype(F32)).astype(BF16)
    return hi, lo


def _dot_nt(a, b):
    return lax.dot_general(a, b, (((1,), (1,)), ((), ())), preferred_element_type=F32)


def _dot(a, b):
    return jnp.dot(a, b, preferred_element_type=F32)


def _sigmoid(a):
    return 1.0 / (1.0 + jnp.exp(-a))


def _pack_halves(a):
    w = a.shape[1] // 2
    bits = lax.bitcast_convert_type(a.astype(BF16).astype(F32), jnp.uint32)
    return bits[:, :w] | (bits[:, w:] >> 16)


def _unpack_halves(u):
    hi = lax.bitcast_convert_type(u & jnp.uint32(0xFFFF0000), F32)
    lo = lax.bitcast_convert_type(u << 16, F32)
    return jnp.concatenate([hi, lo], axis=1)


def _mod_spec(d, n_lat_tiles):
    return pl.BlockSpec((1, 1, 6, d), lambda i, t: (i, jnp.where(t < n_lat_tiles, 1, 0), 0, 0))


def _ada_kernel(c_ref, w_ref, b_ref, o_ref):
    s = c_ref[...]
    s = s * _sigmoid(s)
    s_hi, s_lo = _split_bf16(s)
    w_hi, w_lo = _split_bf16(w_ref[0])
    o_ref[0] = _dot(s_hi, w_hi) + _dot(s_hi, w_lo) + _dot(s_lo, w_hi) + b_ref[0]


def _ada_call(c_rows, ada_w, ada_b):
    depth, d, n6 = ada_w.shape
    rows = c_rows.shape[0]
    tn = 1536
    return pl.pallas_call(
        _ada_kernel,
        grid=(depth, n6 // tn),
        in_specs=[
            pl.BlockSpec((rows, d), lambda l, j: (0, 0)),
            pl.BlockSpec((1, d, tn), lambda l, j: (l, 0, j)),
            pl.BlockSpec((1, 1, tn), lambda l, j: (l, 0, j)),
        ],
        out_specs=pl.BlockSpec((1, rows, tn), lambda l, j: (l, 0, j)),
        out_shape=jax.ShapeDtypeStruct((depth, rows, n6), F32),
        compiler_params=_params(("parallel", "parallel")),
    )(c_rows, ada_w, ada_b.reshape(depth, 1, n6))


def _rope(x, cos, sin_signed, half):
    outs = []
    for c in range(x.shape[1] // LANES):
        sl = slice(c * LANES, (c + 1) * LANES)
        xc = x[:, sl]
        lane = lax.broadcasted_iota(jnp.int32, xc.shape, 1)
        first_half = (lane % (2 * half)) < half
        partner = jnp.where(first_half, pltpu.roll(xc, LANES - half, 1), pltpu.roll(xc, half, 1))
        outs.append(xc * cos[:, sl] + partner * sin_signed[:, sl])
    return outs[0] if len(outs) == 1 else jnp.concatenate(outs, axis=1)


def _group_mean(x, gsize):
    w = x.shape[1]
    r = lax.broadcasted_iota(jnp.int32, (w, w), 0) // gsize
    c = lax.broadcasted_iota(jnp.int32, (w, w), 1) // gsize
    ones = jnp.where(r == c, 1.0, 0.0).astype(BF16)
    hi, lo = _split_bf16(x)
    return (_dot(hi, ones) + _dot(lo, ones)) * (1.0 / gsize)


def _group_mean_sq(x, gsize):
    return _group_mean(x * x, gsize)


def _in_proj_kernel(x_ref, mod_ref, g_ref, w_ref, cos64_ref, sin64_ref, cos32_ref, sin32_ref,
                    gqn_ref, gkn_ref, mqn_ref, mkvn_ref, wuq_ref, wukv_ref,
                    rq_ref, rk_ref, rv_ref, rg_ref, dq_ref, dkt_ref, dv_ref, gq_ref, gkt_ref, gv_ref,
                    mqn_o, mqr_o, mkt_o, mv_o):
    mod = mod_ref[0, 0]
    h = _rms(x_ref[0]) * g_ref[...] * (1.0 + mod[1:2]) + mod[0:1]
    hb = h.astype(BF16)
    cos64, sin64 = cos64_ref[...], sin64_ref[...]
    cos32, sin32 = cos32_ref[...], sin32_ref[...]

    def proj(name, width):
        o = IN_OFFSETS[name]
        return _dot(hb, w_ref[:, o:o + width])

    def store_keys(o_ref, k, extra=None):
        kt = k.T
        d = kt.shape[0] // o_ref.shape[1]
        for hd in range(o_ref.shape[1]):
            rows = kt[hd * d:(hd + 1) * d]
            o_ref[0, hd] = (rows if extra is None else jnp.concatenate([rows, extra], axis=0)).astype(o_ref.dtype)

    def store_values(o_ref, v):
        dv = v.shape[1] // o_ref.shape[1]
        ones = jnp.ones((v.shape[0], dv), F32)
        for hd in range(o_ref.shape[1]):
            o_ref[0, hd] = jnp.concatenate([v[:, hd * dv:(hd + 1) * dv], ones], axis=1).astype(o_ref.dtype)

    rq_ref[0] = _rope(proj("ret_q", 256), cos64, sin64, 32).astype(BF16)
    rk_ref[0] = (_rope(proj("ret_k", 256), cos64, sin64, 32) * (HEAD_DIM ** -0.5)).astype(BF16)
    rv_ref[0] = proj("ret_v", 256).astype(BF16)
    rg_ref[0] = proj("ret_g", 256)
    dq_ref[0] = (_rope(proj("dif_q", 256), cos32, sin32, 16) * (DIFF_D ** -0.5 * LOG2E)).astype(BF16)
    store_keys(dkt_ref, _rope(proj("dif_k", 256), cos32, sin32, 16))
    store_values(dv_ref, proj("dif_v", 256))
    gq = proj("gqa_q", 256)
    gq = gq * lax.rsqrt(_group_mean_sq(gq, HEAD_DIM) + EPS) * gqn_ref[...]
    gq_ref[0] = (_rope(gq, cos64, sin64, 32) * (HEAD_DIM ** -0.5 * LOG2E)).astype(BF16)
    gk = proj("gqa_k", 128)
    gk = gk * lax.rsqrt(_group_mean_sq(gk, HEAD_DIM) + EPS) * gkn_ref[...]
    store_keys(gkt_ref, _rope(gk, cos64[:, :LANES], sin64[:, :LANES], 32))
    store_values(gv_ref, proj("gqa_v", 128))
    cq = proj("mla_cq", 256)
    cq = cq * lax.rsqrt(jnp.sum(cq * cq, axis=-1, keepdims=True) * (1.0 / MLA_Q_RANK) + EPS) * mqn_ref[...]
    q_up = _dot(cq.astype(BF16), wuq_ref[...])
    mla_scale = (MLA_NOPE + MLA_ROPE) ** -0.5 * LOG2E
    mqn_o[0] = (q_up[:, :256] * mla_scale).astype(BF16)
    mqr_o[0] = (_rope(q_up[:, 256:], cos32[:, :LANES], sin32[:, :LANES], 16) * mla_scale).astype(BF16)
    ckv = proj("mla_ckv", 128)
    ckv = ckv * lax.rsqrt(jnp.mean(ckv * ckv, axis=-1, keepdims=True) + EPS) * mkvn_ref[...]
    kv_up = _dot(ckv.astype(BF16), wukv_ref[...])
    kr = _rope(proj("mla_kr", 128), cos32[:, :LANES], sin32[:, :LANES], 16)
    store_keys(mkt_o, kv_up[:, :256], extra=kr.T[:MLA_ROPE])
    store_values(mv_o, kv_up[:, 256:])


def _in_proj_call(xc, modl, g0, w_in_p, tables, gqn, gkn, mqn, mkvn, wuq_p, wukv_p, n_lat):
    b, n, d = xc.shape
    tm = TOKEN_TILE
    tok = lambda w: pl.BlockSpec((1, tm, w), lambda i, t: (i, t, 0))
    const2 = lambda a: pl.BlockSpec(a.shape, lambda i, t: (0, 0))
    tab = pl.BlockSpec((tm, 256), lambda i, t: (t, 0))
    keys_t = lambda h, dk: pl.BlockSpec((1, h, dk, tm), lambda i, t: (i, 0, 0, t))
    vals = lambda h: pl.BlockSpec((1, h, tm, 2 * HEAD_DIM), lambda i, t: (i, 0, t, 0))
    sds = jax.ShapeDtypeStruct
    outs = [
        (tok(256), sds((b, n, 256), BF16)), (tok(256), sds((b, n, 256), BF16)),
        (tok(256), sds((b, n, 256), BF16)), (tok(256), sds((b, n, 256), F32)),
        (tok(256), sds((b, n, 256), BF16)),
        (keys_t(2 * N_HEADS, DIFF_D), sds((b, 2 * N_HEADS, DIFF_D, n), BF16)),
        (vals(N_HEADS), sds((b, N_HEADS, n, 2 * HEAD_DIM), BF16)),
        (tok(256), sds((b, n, 256), BF16)),
        (keys_t(GQA_KV_HEADS, HEAD_DIM), sds((b, GQA_KV_HEADS, HEAD_DIM, n), BF16)),
        (vals(GQA_KV_HEADS), sds((b, GQA_KV_HEADS, n, 2 * HEAD_DIM), BF16)),
        (tok(256), sds((b, n, 256), BF16)), (tok(128), sds((b, n, 128), BF16)),
        (keys_t(N_HEADS, MLA_NOPE + MLA_ROPE), sds((b, N_HEADS, MLA_NOPE + MLA_ROPE, n), BF16)),
        (vals(N_HEADS), sds((b, N_HEADS, n, 2 * HEAD_DIM), BF16)),
    ]
    return pl.pallas_call(
        _in_proj_kernel,
        grid=(b, n // tm),
        in_specs=[
            tok(d), _mod_spec(d, n_lat // tm),
            const2(g0), const2(w_in_p), tab, tab, tab, tab,
            const2(gqn), const2(gkn), const2(mqn), const2(mkvn), const2(wuq_p), const2(wukv_p),
        ],
        out_specs=[o[0] for o in outs],
        out_shape=[o[1] for o in outs],
        compiler_params=_params(("parallel", "parallel")),
    )(xc, modl, g0, w_in_p, *tables, gqn, gkn, mqn, mkvn, wuq_p, wukv_p)


def _ret_kernel(*refs, readout):
    if readout:
        (q_ref, k_ref, v_ref, dec_ref, xi_ref, zeta_ref, gc_ref, of_ref, g_ref, gnw_ref, gnb_ref,
         o_ref, state_ref) = refs
    else:
        q_ref, k_ref, v_ref, dec_ref, xi_ref, zeta_ref, gc_ref, o_ref, state_ref = refs

    @pl.when(pl.program_id(1) == 0)
    def _():
        state_ref[...] = jnp.zeros_like(state_ref)

    q, k, v = q_ref[0], k_ref[0], v_ref[0]
    outs = []
    for h in range(N_HEADS):
        sl = slice(h * HEAD_DIM, (h + 1) * HEAD_DIM)
        qh, kh, vh = q[:, sl], k[:, sl], v[:, sl]
        st = state_ref[h]
        inner = _dot_nt(qh, kh) * dec_ref[h]
        o = _dot(inner.astype(BF16), vh) + _dot(qh, st.astype(BF16)) * xi_ref[h]
        kz = (kh.astype(F32) * zeta_ref[h]).astype(BF16)
        kv = lax.dot_general(kz, vh, (((0,), (0,)), ((), ())), preferred_element_type=F32)
        state_ref[h] = gc_ref[h] * st + kv
        outs.append(o)
    o = jnp.concatenate(outs, axis=1)
    if readout:
        o = o + of_ref[0]
        o = o - _group_mean(o, HEAD_DIM)
        o = o * lax.rsqrt(_group_mean_sq(o, HEAD_DIM) + EPS)
        g = g_ref[0]
        o_ref[0] = ((o * gnw_ref[...] + gnb_ref[...]) * (g * _sigmoid(g))).astype(o_ref.dtype)
    else:
        o_ref[0] = o


def _ret_call(q, k, v, tabs, n_lat, backward, readout_args=None):
    b, n, w = q.shape
    c = RET_CHUNK
    nch = n // c
    nlc = n_lat // c
    if backward:
        chunk = lambda t: nch - 1 - t
    else:
        chunk = lambda t: jnp.where(t < nch - nlc, nlc + t, t - (nch - nlc))
    tok = lambda dt_w: pl.BlockSpec((1, c, dt_w), lambda i, t: (i, chunk(t), 0))
    const = lambda a: pl.BlockSpec(a.shape, lambda i, t: (0,) * a.ndim)
    in_specs = [tok(w), tok(w), tok(w)] + [const(a) for a in tabs]
    args = [q, k, v, *tabs]
    if readout_args is not None:
        o_f, g, gnw, gnb = readout_args
        in_specs += [tok(w), tok(w), const(gnw), const(gnb)]
        args += [o_f, g, gnw, gnb]
    out_dtype = BF16 if readout_args is not None else F32
    return pl.pallas_call(
        functools.partial(_ret_kernel, readout=readout_args is not None),
        grid=(b, nch),
        in_specs=in_specs,
        out_specs=tok(w),
        out_shape=jax.ShapeDtypeStruct((b, n, w), out_dtype),
        scratch_shapes=[pltpu.VMEM((N_HEADS, HEAD_DIM, HEAD_DIM), F32)],
        compiler_params=_params(("parallel", "arbitrary")),
    )(*args)


def _ret_tables(log_g, backward):
    c = RET_CHUNK
    pos = jnp.arange(c, dtype=F32)
    dist = (pos[None, :] - pos[:, None]) if backward else (pos[:, None] - pos[None, :])
    lg = log_g[:, None, None]
    decay = jnp.where(dist >= 0, jnp.exp(lg * jnp.maximum(dist, 0.0)), 0.0)
    to_state = (c - pos) if backward else (pos + 1.0)
    to_end = pos if backward else (c - 1.0 - pos)
    xi = jnp.exp(log_g[:, None] * to_state)[:, :, None]
    zeta = jnp.exp(log_g[:, None] * to_end)[:, :, None]
    ones = jnp.ones((1, 1, HEAD_DIM), F32)
    gc = jnp.exp(log_g * c)[:, None, None] * jnp.ones((1, HEAD_DIM, HEAD_DIM), F32)
    return decay, xi * ones, zeta * ones, gc


def _attn_kernel(*refs, n_maps, readout_scale, n_q, hq):
    refs = list(refs)
    lam_ref = refs.pop(0) if n_maps == 2 else None
    q_parts = [refs.pop(0) for _ in range(n_q)]
    kt_ref, v_ref = refs.pop(0), refs.pop(0)
    sub_ref = refs.pop(0) if n_maps == 2 else None
    o_ref, q_scr = refs.pop(0), refs.pop(0)
    first_ref = refs.pop(0) if n_maps == 2 else None
    hm = pl.program_id(2)
    for hh in range(hq):
        @pl.when(hm == hh)
        def _(hh=hh):
            pieces = [qp[0][:, hh * (qp.shape[2] // hq):(hh + 1) * (qp.shape[2] // hq)] for qp in q_parts]
            q_scr[...] = pieces[0] if n_q == 1 else jnp.concatenate(pieces, axis=1)
    q = q_scr[...]
    tq = q.shape[0]
    n_keys = kt_ref.shape[3]
    dv = v_ref.shape[3] // 2
    tk = min(KV_TILE, n_keys)
    n_full, tail = divmod(n_keys, tk)

    def step(off, size, carry):
        m, acc = carry
        s = _dot(q, kt_ref[0, 0, :, pl.ds(off, size)])
        m_new = jnp.maximum(m, jnp.max(s, axis=-1, keepdims=True))
        p = jnp.exp2(s - m_new)
        acc = jnp.exp2(m - m_new) * acc + _dot(p.astype(BF16), v_ref[0, 0, pl.ds(off, size), :])
        return m_new, acc

    carry = (jnp.full((tq, 1), -1e30, F32), jnp.zeros((tq, 2 * dv), F32))
    for j in range(n_full):
        carry = step(j * tk, tk, carry)
    if tail:
        carry = step(n_full * tk, tail, carry)
    acc = carry[1]
    o = acc[:, :dv] / acc[:, dv:]

    def store(head_of_step, val):
        for hh in range(N_HEADS):
            @pl.when(head_of_step == hh)
            def _(hh=hh):
                o_ref[0, :, hh * dv:(hh + 1) * dv] = val.astype(o_ref.dtype)

    if n_maps == 1:
        store(hm, o)
    else:
        @pl.when(hm % 2 == 0)
        def _():
            first_ref[...] = o

        @pl.when(hm % 2 == 1)
        def _():
            od = first_ref[...] - lam_ref[0] * o
            od = od * lax.rsqrt(jnp.mean(od * od, axis=-1, keepdims=True) + EPS) * sub_ref[...]
            store(hm // 2, od * readout_scale)


def _attn_call(q_parts, hq, kt, v, rows, keys, tq, diff=None):
    b, hk, d, n = kt.shape
    hv, dv = v.shape[1], v.shape[3] // 2
    (q0, qn), (k0, kn) = rows, keys
    assert q0 % tq == 0 and qn % tq == 0 and k0 % kn == 0
    assert sum(qp.shape[2] // hq for qp in q_parts) == d
    n_maps = 2 if diff is not None else 1
    in_specs = [pl.BlockSpec((1, tq, qp.shape[2]), lambda i, t, h: (i, q0 // tq + t, 0)) for qp in q_parts] + [
        pl.BlockSpec((1, 1, d, kn), lambda i, t, h: (i, h // (hq // hk), 0, k0 // kn)),
        pl.BlockSpec((1, 1, kn, 2 * dv), lambda i, t, h: (i, h // (hq // hv), k0 // kn, 0)),
    ]
    args = [*q_parts, kt, v]
    scratch = [pltpu.VMEM((tq, d), BF16)]
    scale = 1.0
    if diff is not None:
        lam, subln, scale = diff
        in_specs = [pl.BlockSpec(memory_space=pltpu.SMEM)] + in_specs + [pl.BlockSpec((1, dv), lambda i, t, h: (0, 0))]
        args = [lam.reshape(1)] + args + [subln.reshape(1, dv)]
        scratch.append(pltpu.VMEM((tq, dv), F32))
    return pl.pallas_call(
        functools.partial(_attn_kernel, n_maps=n_maps, readout_scale=scale, n_q=len(q_parts), hq=hq),
        grid=(b, qn // tq, hq),
        in_specs=in_specs,
        out_specs=pl.BlockSpec((1, tq, (hq // n_maps) * dv), lambda i, t, h: (i, t, 0)),
        out_shape=jax.ShapeDtypeStruct((b, qn, (hq // n_maps) * dv), BF16),
        scratch_shapes=scratch,
        compiler_params=_params(("parallel", "parallel", "arbitrary")),
    )(*args)


def _out_proj_kernel(y_ref, w_ref, x_ref, mod_ref, g_ref, wr_ref, br_ref,
                     xo_ref, hs_ref, gate_ref, pos_ref, cnt_ref):
    mod = mod_ref[0, 0]
    g = g_ref[...]
    y = _dot(y_ref[0], w_ref[...])
    xn = x_ref[0] + mod[2:3] * (_rms(y) * g[1:2])
    xo_ref[0] = xn
    h = _rms(xn) * g[2:3] * (1.0 + mod[4:5]) + mod[3:4]
    w_hi, w_lo = _split_bf16(wr_ref[...])
    h_hi, h_lo = _split_bf16(h)
    logits = _dot_nt(w_hi, h_hi) + _dot_nt(w_hi, h_lo) + _dot_nt(w_lo, h_hi) + br_ref[...]
    row = lax.broadcasted_iota(jnp.int32, logits.shape, 0)
    vals = logits
    tops, idxs = [], []
    for _ in range(TOP_K):
        m = jnp.max(vals, axis=0, keepdims=True)
        idx = jnp.min(jnp.where(vals == m, row, N_EXPERTS), axis=0, keepdims=True)
        tops.append(m)
        idxs.append(idx)
        vals = jnp.where(row == idx, -jnp.inf, vals)
    ex = [jnp.exp(tv - tops[0]) for tv in tops]
    den = ex[0] + ex[1] + ex[2] + ex[3]
    gate_ref[0] = jnp.concatenate([e / den for e in ex], axis=0)
    tm = logits.shape[1]
    before = jnp.where(lax.broadcasted_iota(jnp.int32, (tm, tm), 0) < lax.broadcasted_iota(jnp.int32, (tm, tm), 1),
                       1.0, 0.0).astype(BF16)
    counts = jnp.zeros((N_EXPERTS, 1), F32)
    onehots, ranks = [], []
    for idx in idxs:
        onehot = jnp.where(row == idx, 1.0, 0.0)
        ranks.append(_dot(onehot.astype(BF16), before) + counts)
        onehots.append(onehot)
        counts = counts + jnp.sum(onehot, axis=1, keepdims=True)
    counts_b = jnp.broadcast_to(counts, (N_EXPERTS, LANES))
    earlier = jnp.where(lax.broadcasted_iota(jnp.int32, (N_EXPERTS, N_EXPERTS), 0)
                        > lax.broadcasted_iota(jnp.int32, (N_EXPERTS, N_EXPERTS), 1), 1.0, 0.0).astype(BF16)
    first = _dot(earlier, counts_b.astype(BF16))[:, 0:1]
    pos = [jnp.sum(oh * (rk + first), axis=0, keepdims=True).astype(jnp.int32) for oh, rk in zip(onehots, ranks)]
    pos_ref[0] = jnp.concatenate(pos, axis=0)
    cnt_ref[0, 0] = counts_b.astype(jnp.int32)
    slot = lax.broadcasted_iota(jnp.int32, (TOP_K * tm, tm), 0)
    pick = jnp.where(slot == pos[0], 1.0, 0.0)
    for p in pos[1:]:
        pick = jnp.where(slot == p, 1.0, pick)
    hs_ref[0, 0] = _pack_halves(_dot(pick.astype(BF16), h.astype(BF16)))


def _out_proj_call(ymix, w_out, xc, modl, g, wr_t, br, n_lat, n_rows):
    b, n, d = xc.shape
    tm = TOKEN_TILE
    tok = lambda w: pl.BlockSpec((1, tm, w), lambda i, t: (i, t, 0))
    const2 = lambda a: pl.BlockSpec(a.shape, lambda i, t: (0, 0))
    sel = pl.BlockSpec((1, TOP_K, tm), lambda i, t: (i, 0, t))
    return pl.pallas_call(
        _out_proj_kernel,
        grid=(b, n_rows // tm),
        in_specs=[
            tok(ymix.shape[2]), const2(w_out), tok(d), _mod_spec(d, n_lat // tm),
            const2(g), const2(wr_t), const2(br),
        ],
        out_specs=[tok(d),
                   pl.BlockSpec((1, 1, TOP_K * tm, d // 2), lambda i, t: (i, t, 0, 0)),
                   sel, sel,
                   pl.BlockSpec((1, 1, N_EXPERTS, LANES), lambda i, t: (i, t, 0, 0))],
        out_shape=[
            jax.ShapeDtypeStruct((b, n_rows, d), F32),
            jax.ShapeDtypeStruct((b, n_rows // tm, TOP_K * tm, d // 2), jnp.uint32),
            jax.ShapeDtypeStruct((b, TOP_K, n_rows), F32),
            jax.ShapeDtypeStruct((b, TOP_K, n_rows), jnp.int32),
            jax.ShapeDtypeStruct((b, n_rows // tm, N_EXPERTS, LANES), jnp.int32),
        ],
        compiler_params=_params(("parallel", "parallel")),
    )(ymix, w_out, xc, modl, g, wr_t, br)


def _w1_prep_kernel(*refs):
    w_refs, (g_ref, l_ref) = refs[:DMA_SPLIT], refs[DMA_SPLIT:]
    r = lax.broadcasted_iota(jnp.int32, (MXU_DIM, MXU_DIM), 0)
    c = lax.broadcasted_iota(jnp.int32, (MXU_DIM, MXU_DIM), 1)
    src = jnp.where(c < LANES, 2 * c, 2 * (c - LANES) + 1)
    perm = jnp.where(r == src, 1.0, 0.0).astype(BF16)
    per_slab = w_refs[0].shape[3] // MXU_DIM
    for s, w_ref in enumerate(w_refs):
        for jj in range(per_slab):
            j = s * per_slab + jj
            out = _dot(w_ref[0, 0, :, jj * MXU_DIM:(jj + 1) * MXU_DIM].astype(BF16), perm)
            g_ref[0, :, j * LANES:(j + 1) * LANES] = out[:, :LANES].astype(BF16)
            l_ref[0, :, j * LANES:(j + 1) * LANES] = out[:, LANES:].astype(BF16)


def _w1_prep_call(w1_all, layer):
    _, e, d, f2 = w1_all.shape
    f = f2 // 2
    rows = 256
    out_spec = pl.BlockSpec((1, rows, f), lambda i, j: (i, j, 0))
    return pl.pallas_call(
        _w1_prep_kernel,
        grid=(e, d // rows),
        in_specs=[pl.BlockSpec((1, 1, rows, f2 // DMA_SPLIT), lambda i, j, s=s: (layer, i, j, s))
                  for s in range(DMA_SPLIT)],
        out_specs=[out_spec, out_spec],
        out_shape=[jax.ShapeDtypeStruct((e, d, f), BF16)] * 2,
        compiler_params=_params(("parallel", "parallel")),
    )(*([w1_all] * DMA_SPLIT))


def _expert_kernel(be_ref, nu_ref, sw_ref, nx_ref, sl_ref, x_ref, b1g_ref, b1l_ref, b2_ref,
                   w1g_hbm, w1l_hbm, w2_hbm, o_ref, wg_ref, wl_ref, w2f_ref, w2b_ref, sem_ref, *, layer):
    i = pl.program_id(0)

    def weight_copies(expert, slot):
        return (pltpu.make_async_copy(w1g_hbm.at[expert], wg_ref.at[slot], sem_ref.at[slot, 0]),
                pltpu.make_async_copy(w1l_hbm.at[expert], wl_ref.at[slot], sem_ref.at[slot, 1]),
                pltpu.make_async_copy(w2_hbm.at[layer, expert], w2f_ref.at[slot], sem_ref.at[slot, 2]))

    @pl.when(jnp.logical_and(i == 0, nu_ref[0] > 0))
    def _():
        for cp in weight_copies(be_ref[0], 0):
            cp.start()

    @pl.when(sw_ref[i] == 1)
    def _():
        slot = sl_ref[i]
        for cp in weight_copies(be_ref[i], slot):
            cp.wait()

        @pl.when(nx_ref[i] >= 0)
        def _():
            for cp in weight_copies(nx_ref[i], 1 - slot):
                cp.start()

        w2b_ref[...] = w2f_ref[slot].astype(BF16)

    @pl.when(i < nu_ref[0])
    def _():
        slot = sl_ref[i]
        x = _unpack_halves(x_ref[...]).astype(BF16)
        glu = jnp.minimum(_dot(x, wg_ref[slot]) + b1g_ref[0], SWIGLU_LIMIT)
        lin = jnp.clip(_dot(x, wl_ref[slot]) + b1l_ref[0], -SWIGLU_LIMIT, SWIGLU_LIMIT)
        act = glu * _sigmoid(SWIGLU_ALPHA * glu) * (lin + 1.0)
        o_ref[...] = _pack_halves(_dot(act.astype(BF16), w2b_ref[...]) + b2_ref[0])

    @pl.when(i >= nu_ref[0])
    def _():
        o_ref[...] = jnp.zeros_like(o_ref)


def _expert_call(block_expert, n_used, x_rows, w1g, w1l, b1g, b1l, w2_all, layer, b2):
    r = x_rows.shape[0]
    d, f = w1g.shape[1:]
    tm = MOE_BLOCK
    n_blocks = r // tm
    blk = jnp.arange(n_blocks, dtype=jnp.int32)
    prev = jnp.concatenate([block_expert[:1], block_expert[:-1]])
    switch = (blk < n_used[0]) & ((blk == 0) | (block_expert != prev))
    slot = (jnp.cumsum(switch.astype(jnp.int32)) - 1) % 2
    later = jnp.where(switch, blk, n_blocks)
    next_start = jnp.concatenate([lax.cummin(later, reverse=True)[1:], jnp.full((1,), n_blocks, jnp.int32)])
    next_expert = jnp.where(next_start < n_blocks, block_expert[jnp.minimum(next_start, n_blocks - 1)], -1)
    scalars = (block_expert, n_used, switch.astype(jnp.int32), next_expert.astype(jnp.int32), slot.astype(jnp.int32))
    bias = lambda w: pl.BlockSpec((1, 1, w), lambda i, be, *_: (be[i], 0, 0))
    hbm = pl.BlockSpec(memory_space=pl.ANY)
    return pl.pallas_call(
        functools.partial(_expert_kernel, layer=layer),
        grid_spec=pltpu.PrefetchScalarGridSpec(
            num_scalar_prefetch=len(scalars),
            grid=(n_blocks,),
            in_specs=[pl.BlockSpec((tm, d // 2), lambda i, *_: (i, 0)), bias(f), bias(f), bias(d), hbm, hbm, hbm],
            out_specs=pl.BlockSpec((tm, d // 2), lambda i, *_: (i, 0)),
            scratch_shapes=[pltpu.VMEM((2, d, f), BF16), pltpu.VMEM((2, d, f), BF16), pltpu.VMEM((2, f, d), F32),
                            pltpu.VMEM((f, d), BF16), pltpu.SemaphoreType.DMA((2, 3))],
        ),
        out_shape=jax.ShapeDtypeStruct((r, d // 2), jnp.uint32),
        compiler_params=_params(("arbitrary",)),
    )(*scalars, x_rows, b1g, b1l, b2, w1g, w1l, w2_all)


def _combine_kernel(*refs, n_parts):
    x_ref, y_refs = refs[0], refs[1:1 + n_parts]
    gate_ref, pos_ref, mod_ref, g_ref, o_ref = refs[1 + n_parts:]
    gates, pos = gate_ref[0], pos_ref[0]
    lane = lax.broadcasted_iota(jnp.int32, (gates.shape[0], y_refs[0].shape[2]), 1)
    mix = jnp.zeros(lane.shape, F32)
    for k in range(TOP_K):
        mix = jnp.where(lane == pos[:, k:k + 1], gates[:, k:k + 1], mix)
    mix_hi, mix_lo = _split_bf16(mix)
    mod = mod_ref[0, 0]
    part_batches = pl.num_programs(0) // n_parts
    for p, y_ref in enumerate(y_refs):
        @pl.when(pl.program_id(0) // part_batches == p)
        def _(y_ref=y_ref):
            rows = _unpack_halves(y_ref[0, 0]).astype(BF16)
            y = _dot(mix_hi, rows) + _dot(mix_lo, rows)
            o_ref[0] = x_ref[0] + mod[5:6] * (_rms(y) * g_ref[...][3:4])


def _combine_call(xc, picked_parts, gates, pos, modl, g, n_lat):
    b, n, d = xc.shape
    tm = TOKEN_TILE
    n_parts = len(picked_parts)
    bp = b // n_parts
    tok = pl.BlockSpec((1, tm, d), lambda i, t: (i, t, 0))

    def part_spec(p):
        def index(i, t):
            live = (i >= p * bp) & (i < (p + 1) * bp)
            return jnp.clip(i - p * bp, 0, bp - 1), jnp.where(live, t, 0), 0, 0
        return pl.BlockSpec((1, 1, TOP_K * tm, d // 2), index)

    return pl.pallas_call(
        functools.partial(_combine_kernel, n_parts=n_parts),
        grid=(b, n // tm),
        in_specs=[tok] + [part_spec(p) for p in range(n_parts)] + [
            pl.BlockSpec((1, tm, TOP_K), lambda i, t: (i, t, 0)),
            pl.BlockSpec((1, tm, TOP_K), lambda i, t: (i, t, 0)),
            _mod_spec(d, n_lat // tm),
            pl.BlockSpec(g.shape, lambda i, t: (0, 0)),
        ],
        out_specs=tok,
        out_shape=jax.ShapeDtypeStruct((b, n, d), F32),
        compiler_params=_params(("parallel", "parallel")),
    )(xc, *picked_parts, gates, pos, modl, g)


def _rope_tables(rows, n_ctx, rot_dim):
    row = jnp.repeat(jnp.arange(rows), GRID_W)
    col = jnp.tile(jnp.arange(GRID_W), rows)
    n_freq = rot_dim // 4
    inv = ROPE_THETA ** (-jnp.arange(n_freq, dtype=F32) / n_freq)
    ang = jnp.concatenate([row[:, None] * inv, col[:, None] * inv], axis=-1)
    cos, sin = jnp.cos(ang), jnp.sin(ang)
    reps = 256 // rot_dim
    cos_t = jnp.tile(jnp.concatenate([cos, cos], axis=-1), (1, reps))
    sin_t = jnp.tile(jnp.concatenate([-sin, sin], axis=-1), (1, reps))
    cos_t = jnp.concatenate([cos_t, jnp.ones((n_ctx, 256), F32)], axis=0)
    sin_t = jnp.concatenate([sin_t, jnp.zeros((n_ctx, 256), F32)], axis=0)
    return cos_t, sin_t


def _pack_w_in(w):
    parts, src = [], 0
    for _, width, padded in IN_PIECES:
        parts.append(w[:, src:src + width])
        if padded > width:
            parts.append(jnp.zeros((w.shape[0], padded - width), w.dtype))
        src += width
    return jnp.concatenate(parts, axis=1).astype(BF16)


def _pack_w_uq(w):
    wh = w.reshape(MLA_Q_RANK, N_HEADS, MLA_NOPE + MLA_ROPE)
    nope = wh[:, :, :MLA_NOPE].reshape(MLA_Q_RANK, N_HEADS * MLA_NOPE)
    rope = wh[:, :, MLA_NOPE:].reshape(MLA_Q_RANK, N_HEADS * MLA_ROPE)
    packed = jnp.concatenate([nope, rope], axis=1)
    return jnp.pad(packed, ((0, 256 - MLA_Q_RANK), (0, 0))).astype(BF16)


def _pack_w_ukv(w):
    wh = w.reshape(MLA_KV_RANK, N_HEADS, MLA_NOPE + HEAD_DIM)
    kn = wh[:, :, :MLA_NOPE].reshape(MLA_KV_RANK, N_HEADS * MLA_NOPE)
    vv = wh[:, :, MLA_NOPE:].reshape(MLA_KV_RANK, N_HEADS * HEAD_DIM)
    return jnp.concatenate([kn, vv], axis=1).astype(BF16)


def _attention(q_parts, hq, kt, v, n_lat, n_ctx, ctx_out, diff=None):
    n = n_lat + n_ctx
    y = _attn_call(q_parts, hq, kt, v, (0, n_lat), (0, n), min(Q_TILE, n_lat), diff)
    if ctx_out:
        y_c = _attn_call(q_parts, hq, kt, v, (n_lat, n_ctx), (n_lat, n_ctx), n_ctx, diff)
    else:
        y_c = jnp.zeros((y.shape[0], n_ctx, y.shape[2]), y.dtype)
    return jnp.concatenate([y, y_c], axis=1)


def _steps(x, bounds, offsets):
    deltas = offsets - jnp.concatenate([jnp.zeros_like(offsets[..., :1]), offsets[..., :-1]], axis=-1)
    return x + jnp.sum(jnp.where(x[..., None] >= bounds[..., None, :], deltas[..., None, :], 0), axis=-1)


def _moe(x_new, hs, gates, pos, counts, modl, g, n_lat, w1_all, b1, w2_all, b2, layer):
    b, n, d = x_new.shape
    f = w2_all.shape[2]
    tile_rows = TOP_K * TOKEN_TILE
    n_parts = 2 if b % 2 == 0 else 1
    tiles = (b // n_parts) * (n // TOKEN_TILE)
    n_blocks = -(-(tiles * tile_rows) // MOE_BLOCK) + N_EXPERTS
    hs_rows = hs.reshape(b * (n // TOKEN_TILE) * tile_rows, d // 2)
    w1g, w1l = _w1_prep_call(w1_all, layer)
    b1g, b1l = b1[:, 0::2].reshape(N_EXPERTS, 1, f), b1[:, 1::2].reshape(N_EXPERTS, 1, f)
    picked_parts = []
    for p in range(n_parts):
        tile_counts = counts.reshape(-1, N_EXPERTS)[p * tiles:(p + 1) * tiles]
        tile_base = jnp.cumsum(tile_counts, axis=0) - tile_counts
        first = jnp.cumsum(tile_counts, axis=1) - tile_counts
        padded = (jnp.sum(tile_counts, axis=0) + MOE_BLOCK - 1) // MOE_BLOCK * MOE_BLOCK
        pad_ends = jnp.cumsum(padded)
        pad_start = pad_ends - padded
        block_start = jnp.arange(n_blocks, dtype=jnp.int32) * MOE_BLOCK
        block_expert = jnp.minimum(jnp.sum((pad_ends[None, :] <= block_start[:, None]).astype(jnp.int32), axis=1),
                                   N_EXPERTS - 1)
        n_used = (pad_ends[-1:] // MOE_BLOCK).astype(jnp.int32)
        j = block_start[:, None] + jnp.arange(MOE_BLOCK, dtype=jnp.int32)[None, :] - pad_start[block_expert][:, None]
        tile_offset = (jnp.arange(tiles, dtype=jnp.int32)[:, None] + p * tiles) * tile_rows + first - tile_base
        src = _steps(j, tile_base.T[block_expert], tile_offset.T[block_expert])
        src = jnp.clip(src, 0, hs_rows.shape[0] - 1)
        x_rows = hs_rows.at[src.reshape(-1)].get(mode="promise_in_bounds")
        y_rows = _expert_call(block_expert, n_used, x_rows, w1g, w1l, b1g, b1l, w2_all, layer,
                              b2.reshape(N_EXPERTS, 1, d))
        slot = jnp.broadcast_to(jnp.arange(tile_rows, dtype=jnp.int32)[None, :], (tiles, tile_rows))
        back = _steps(slot, first, tile_base + pad_start[None, :] - first)
        picked = y_rows.at[back.reshape(-1)].get(mode="promise_in_bounds")
        picked_parts.append(picked.reshape(b // n_parts, n // TOKEN_TILE, tile_rows, d // 2))
    return _combine_call(x_new, picked_parts, gates.transpose(0, 2, 1), pos.transpose(0, 2, 1), modl, g, n_lat)


def kernel(x, c, ctx, c_ctx, ada_w, ada_b, norm_g, w_in, w_out, ret_log_decay, ret_gn_w, ret_gn_b,
           diff_lambda, diff_subln, gqa_qk_norm, mla_q_norm, mla_kv_norm, mla_w_uq, mla_w_ukv,
           router_w, router_b, exp_w1, exp_b1, exp_w2, exp_b2):
    b, s, d = x.shape
    n_ctx = ctx.shape[1]
    n = s + n_ctx
    depth = ada_w.shape[0]
    assert n_ctx % TOKEN_TILE == 0 and s % TOKEN_TILE == 0 and s % GRID_W == 0
    assert n_ctx % RET_CHUNK == 0 and s % min(Q_TILE, s) == 0 and s % n_ctx == 0

    tables = _rope_tables(s // GRID_W, n_ctx, HEAD_DIM) + _rope_tables(s // GRID_W, n_ctx, DIFF_D)
    c_rows = jnp.zeros((16, d), F32).at[:b].set(c).at[b].set(c_ctx)
    mods = _ada_call(c_rows, ada_w, ada_b)
    xc = jnp.concatenate([x, ctx], axis=1)

    for l in range(depth):
        last = l == depth - 1
        lam_init = 0.8 - 0.6 * math.exp(-0.3 * l)
        mod_lat = mods[l, :b].reshape(b, 1, 6, d)
        mod_ctx = jnp.broadcast_to(mods[l, b].reshape(1, 1, 6, d), (b, 1, 6, d))
        modl = jnp.concatenate([mod_ctx, mod_lat], axis=1)

        (rq, rk, rv, rg, dq, dkt, dv, gq, gkt, gv, mqn, mqr, mkt, mv) = _in_proj_call(
            xc, modl, norm_g[l, 0:1], _pack_w_in(w_in[l]), tables,
            jnp.tile(gqa_qk_norm[l, 0], N_HEADS)[None, :], jnp.tile(gqa_qk_norm[l, 1], GQA_KV_HEADS)[None, :],
            jnp.pad(mla_q_norm[l], (0, 256 - MLA_Q_RANK))[None, :], mla_kv_norm[l][None, :],
            _pack_w_uq(mla_w_uq[l]), _pack_w_ukv(mla_w_ukv[l]), s)

        log_g = -jnp.exp(ret_log_decay[l].astype(F32))
        o_f = _ret_call(rq, rk, rv, _ret_tables(log_g[0], False), s, False)
        ret_y = _ret_call(rq, rk, rv, _ret_tables(log_g[1], True), s, True,
                          (o_f, rg, ret_gn_w[l][None, :], ret_gn_b[l][None, :]))

        lp = diff_lambda[l].astype(F32)
        lam = jnp.exp(jnp.sum(lp[0] * lp[1])) - jnp.exp(jnp.sum(lp[2] * lp[3])) + lam_init
        dif_y = _attention([dq], 2 * N_HEADS, dkt, dv, s, n_ctx, not last, (lam, diff_subln[l], 1.0 - lam_init))

        gqa_y = _attention([gq], N_HEADS, gkt, gv, s, n_ctx, not last)

        mla_y = _attention([mqn, mqr], N_HEADS, mkt, mv, s, n_ctx, not last)

        ymix = jnp.concatenate([ret_y, dif_y, gqa_y, mla_y], axis=-1)
        n_rows = s if last else n
        x_new, hs, gates, pos, counts = _out_proj_call(
            ymix, w_out[l].astype(BF16), xc, modl, norm_g[l], router_w[l].T, router_b[l][:, None], s, n_rows)
        xc = _moe(x_new, hs, gates, pos, counts[..., 0], modl, norm_g[l], s,
                  exp_w1, exp_b1[l], exp_w2, exp_b2[l], l)

    return xc[:, :s]
```

```python
import functools
import math

import jax
import jax.numpy as jnp
from jax import lax
from jax.experimental import pallas as pl
from jax.experimental.pallas import tpu as pltpu

F32 = jnp.float32
BF16 = jnp.bfloat16

GRID_W = 64
ROPE_THETA = 10000.0
EPS = 1e-6
GROUP_WIDTH = 256
HEAD_DIM = 64
N_HEADS = 4
RET_CHUNK = 256
DIFF_D = 32
GQA_KV_HEADS = 2
MLA_Q_RANK = 192
MLA_KV_RANK = 128
MLA_NOPE = 64
MLA_ROPE = 32
N_EXPERTS = 32
TOP_K = 4
SWIGLU_LIMIT = 7.0
SWIGLU_ALPHA = 1.702
MOE_BLOCK = 512

LANES = 128
MXU_DIM = 256
TOKEN_TILE = 256
Q_TILE = 1024
KV_TILE = 512
VMEM_LIMIT = 48 * 1024 * 1024
LOG2E = math.log2(math.e)

IN_PIECES = (
    ("ret_q", 256, 256), ("ret_k", 256, 256), ("ret_v", 256, 256), ("ret_g", 256, 256),
    ("dif_q", 256, 256), ("dif_k", 256, 256), ("dif_v", 256, 256),
    ("gqa_q", 256, 256), ("gqa_k", 128, 128), ("gqa_v", 128, 128),
    ("mla_cq", MLA_Q_RANK, 256), ("mla_ckv", MLA_KV_RANK, 128), ("mla_kr", MLA_ROPE, 128),
)
IN_OFFSETS = {}
_off = 0
for _name, _w, _pw in IN_PIECES:
    IN_OFFSETS[_name] = _off
    _off += _pw
IN_PACKED_WIDTH = _off


def _params(sem):
    return pltpu.CompilerParams(dimension_semantics=sem, vmem_limit_bytes=VMEM_LIMIT)


def _rms(x):
    return x * lax.rsqrt(jnp.mean(x * x, axis=-1, keepdims=True) + EPS)


def _split_bf16(a):
    hi = a.astype(BF16)
    lo = (a - hi.astype(F32)).astype(BF16)
    return hi, lo


def _dot_nt(a, b):
    return lax.dot_general(a, b, (((1,), (1,)), ((), ())), preferred_element_type=F32)


def _dot(a, b):
    return jnp.dot(a, b, preferred_element_type=F32)


def _sigmoid(a):
    return 1.0 / (1.0 + jnp.exp(-a))


def _pack_halves(a):
    w = a.shape[1] // 2
    bits = lax.bitcast_convert_type(a.astype(BF16).astype(F32), jnp.uint32)
    return bits[:, :w] | (bits[:, w:] >> 16)


def _unpack_halves(u):
    hi = lax.bitcast_convert_type(u & jnp.uint32(0xFFFF0000), F32)
    lo = lax.bitcast_convert_type(u << 16, F32)
    return jnp.concatenate([hi, lo], axis=1)


def _mod_spec(d, n_lat_tiles):
    return pl.BlockSpec((1, 1, 6, d), lambda i, t: (i, jnp.where(t < n_lat_tiles, 1, 0), 0, 0))


def _ada_kernel(c_ref, w_ref, b_ref, o_ref):
    s = c_ref[...]
    s = s * _sigmoid(s)
    s_hi, s_lo = _split_bf16(s)
    w_hi, w_lo = _split_bf16(w_ref[0])
    o_ref[0] = _dot(s_hi, w_hi) + _dot(s_hi, w_lo) + _dot(s_lo, w_hi) + b_ref[0]


def _ada_call(c_rows, ada_w, ada_b):
    depth, d, n6 = ada_w.shape
    rows = c_rows.shape[0]
    tn = 1536
    return pl.pallas_call(
        _ada_kernel,
        grid=(depth, n6 // tn),
        in_specs=[
            pl.BlockSpec((rows, d), lambda l, j: (0, 0)),
            pl.BlockSpec((1, d, tn), lambda l, j: (l, 0, j)),
            pl.BlockSpec((1, 1, tn), lambda l, j: (l, 0, j)),
        ],
        out_specs=pl.BlockSpec((1, rows, tn), lambda l, j: (l, 0, j)),
        out_shape=jax.ShapeDtypeStruct((depth, rows, n6), F32),
        compiler_params=_params(("parallel", "parallel")),
    )(c_rows, ada_w, ada_b.reshape(depth, 1, n6))


def _rope(x, cos, sin_signed, half):
    outs = []
    for c in range(x.shape[1] // LANES):
        sl = slice(c * LANES, (c + 1) * LANES)
        xc = x[:, sl]
        lane = lax.broadcasted_iota(jnp.int32, xc.shape, 1)
        first_half = (lane % (2 * half)) < half
        partner = jnp.where(first_half, pltpu.roll(xc, LANES - half, 1), pltpu.roll(xc, half, 1))
        outs.append(xc * cos[:, sl] + partner * sin_signed[:, sl])
    return outs[0] if len(outs) == 1 else jnp.concatenate(outs, axis=1)


def _group_mean(x, gsize):
    w = x.shape[1]
    r = lax.broadcasted_iota(jnp.int32, (w, w), 0) // gsize
    c = lax.broadcasted_iota(jnp.int32, (w, w), 1) // gsize
    ones = jnp.where(r == c, 1.0, 0.0).astype(BF16)
    hi, lo = _split_bf16(x)
    return (_dot(hi, ones) + _dot(lo, ones)) * (1.0 / gsize)


def _group_mean_sq(x, gsize):
    return _group_mean(x * x, gsize)


def _in_proj_kernel(x_ref, mod_ref, g_ref, w_ref, cos64_ref, sin64_ref, cos32_ref, sin32_ref,
                    gqn_ref, gkn_ref, mqn_ref, mkvn_ref, wuq_ref, wukv_ref,
                    rq_ref, rk_ref, rv_ref, rg_ref, dq_ref, dkt_ref, dv_ref, gq_ref, gkt_ref, gv_ref,
                    mqn_o, mqr_o, mkt_o, mv_o):
    mod = mod_ref[0, 0]
    h = _rms(x_ref[0]) * g_ref[...] * (1.0 + mod[1:2]) + mod[0:1]
    hb = h.astype(BF16)
    cos64, sin64 = cos64_ref[...], sin64_ref[...]
    cos32, sin32 = cos32_ref[...], sin32_ref[...]

    def proj(name, width):
        o = IN_OFFSETS[name]
        return _dot(hb, w_ref[:, o:o + width])

    def store_keys(o_ref, k, extra=None):
        kt = k.T
        d = kt.shape[0] // o_ref.shape[1]
        for hd in range(o_ref.shape[1]):
            rows = kt[hd * d:(hd + 1) * d]
            o_ref[0, hd] = (rows if extra is None else jnp.concatenate([rows, extra], axis=0)).astype(o_ref.dtype)

    def store_values(o_ref, v):
        dv = v.shape[1] // o_ref.shape[1]
        ones = jnp.ones((v.shape[0], dv), F32)
        for hd in range(o_ref.shape[1]):
            o_ref[0, hd] = jnp.concatenate([v[:, hd * dv:(hd + 1) * dv], ones], axis=1).astype(o_ref.dtype)

    rq_ref[0] = _rope(proj("ret_q", 256), cos64, sin64, 32).astype(BF16)
    rk_ref[0] = (_rope(proj("ret_k", 256), cos64, sin64, 32) * (HEAD_DIM ** -0.5)).astype(BF16)
    rv_ref[0] = proj("ret_v", 256).astype(BF16)
    rg_ref[0] = proj("ret_g", 256)
    dq_ref[0] = (_rope(proj("dif_q", 256), cos32, sin32, 16) * (DIFF_D ** -0.5 * LOG2E)).astype(BF16)
    store_keys(dkt_ref, _rope(proj("dif_k", 256), cos32, sin32, 16))
    store_values(dv_ref, proj("dif_v", 256))
    gq = proj("gqa_q", 256)
    gq = gq * lax.rsqrt(_group_mean_sq(gq, HEAD_DIM) + EPS) * gqn_ref[...]
    gq_ref[0] = (_rope(gq, cos64, sin64, 32) * (HEAD_DIM ** -0.5 * LOG2E)).astype(BF16)
    gk = proj("gqa_k", 128)
    gk = gk * lax.rsqrt(_group_mean_sq(gk, HEAD_DIM) + EPS) * gkn_ref[...]
    store_keys(gkt_ref, _rope(gk, cos64[:, :LANES], sin64[:, :LANES], 32))
    store_values(gv_ref, proj("gqa_v", 128))
    cq = proj("mla_cq", 256)
    cq = cq * lax.rsqrt(jnp.sum(cq * cq, axis=-1, keepdims=True) * (1.0 / MLA_Q_RANK) + EPS) * mqn_ref[...]
    q_up = _dot(cq.astype(BF16), wuq_ref[...])
    mla_scale = (MLA_NOPE + MLA_ROPE) ** -0.5 * LOG2E
    mqn_o[0] = (q_up[:, :256] * mla_scale).astype(BF16)
    mqr_o[0] = (_rope(q_up[:, 256:], cos32[:, :LANES], sin32[:, :LANES], 16) * mla_scale).astype(BF16)
    ckv = proj("mla_ckv", 128)
    ckv = ckv * lax.rsqrt(jnp.mean(ckv * ckv, axis=-1, keepdims=True) + EPS) * mkvn_ref[...]
    kv_up = _dot(ckv.astype(BF16), wukv_ref[...])
    kr = _rope(proj("mla_kr", 128), cos32[:, :LANES], sin32[:, :LANES], 16)
    store_keys(mkt_o, kv_up[:, :256], extra=kr.T[:MLA_ROPE])
    store_values(mv_o, kv_up[:, 256:])


def _in_proj_call(xc, modl, g0, w_in_p, tables, gqn, gkn, mqn, mkvn, wuq_p, wukv_p, n_lat):
    b, n, d = xc.shape
    tm = TOKEN_TILE
    tok = lambda w: pl.BlockSpec((1, tm, w), lambda i, t: (i, t, 0))
    const2 = lambda a: pl.BlockSpec(a.shape, lambda i, t: (0, 0))
    tab = pl.BlockSpec((tm, 256), lambda i, t: (t, 0))
    keys_t = lambda h, dk: pl.BlockSpec((1, h, dk, tm), lambda i, t: (i, 0, 0, t))
    vals = lambda h: pl.BlockSpec((1, h, tm, 2 * HEAD_DIM), lambda i, t: (i, 0, t, 0))
    sds = jax.ShapeDtypeStruct
    outs = [
        (tok(256), sds((b, n, 256), BF16)), (tok(256), sds((b, n, 256), BF16)),
        (tok(256), sds((b, n, 256), BF16)), (tok(256), sds((b, n, 256), F32)),
        (tok(256), sds((b, n, 256), BF16)),
        (keys_t(2 * N_HEADS, DIFF_D), sds((b, 2 * N_HEADS, DIFF_D, n), BF16)),
        (vals(N_HEADS), sds((b, N_HEADS, n, 2 * HEAD_DIM), BF16)),
        (tok(256), sds((b, n, 256), BF16)),
        (keys_t(GQA_KV_HEADS, HEAD_DIM), sds((b, GQA_KV_HEADS, HEAD_DIM, n), BF16)),
        (vals(GQA_KV_HEADS), sds((b, GQA_KV_HEADS, n, 2 * HEAD_DIM), BF16)),
        (tok(256), sds((b, n, 256), BF16)), (tok(128), sds((b, n, 128), BF16)),
        (keys_t(N_HEADS, MLA_NOPE + MLA_ROPE), sds((b, N_HEADS, MLA_NOPE + MLA_ROPE, n), BF16)),
        (vals(N_HEADS), sds((b, N_HEADS, n, 2 * HEAD_DIM), BF16)),
    ]
    return pl.pallas_call(
        _in_proj_kernel,
        grid=(b, n // tm),
        in_specs=[
            tok(d), _mod_spec(d, n_lat // tm),
            const2(g0), const2(w_in_p), tab, tab, tab, tab,
            const2(gqn), const2(gkn), const2(mqn), const2(mkvn), const2(wuq_p), const2(wukv_p),
        ],
        out_specs=[o[0] for o in outs],
        out_shape=[o[1] for o in outs],
        compiler_params=_params(("parallel", "parallel")),
    )(xc, modl, g0, w_in_p, *tables, gqn, gkn, mqn, mkvn, wuq_p, wukv_p)


def _ret_kernel(*refs, readout):
    if readout:
        (q_ref, k_ref, v_ref, dec_ref, xi_ref, zeta_ref, gc_ref, of_ref, g_ref, gnw_ref, gnb_ref,
         o_ref, state_ref) = refs
    else:
        q_ref, k_ref, v_ref, dec_ref, xi_ref, zeta_ref, gc_ref, o_ref, state_ref = refs

    @pl.when(pl.program_id(1) == 0)
    def _():
        state_ref[...] = jnp.zeros_like(state_ref)

    q, k, v = q_ref[0], k_ref[0], v_ref[0]
    outs = []
    for h in range(N_HEADS):
        sl = slice(h * HEAD_DIM, (h + 1) * HEAD_DIM)
        qh, kh, vh = q[:, sl], k[:, sl], v[:, sl]
        st = state_ref[h]
        inner = _dot_nt(qh, kh) * dec_ref[h]
        o = _dot(inner.astype(BF16), vh) + _dot(qh, st.astype(BF16)) * xi_ref[h]
        kz = (kh.astype(F32) * zeta_ref[h]).astype(BF16)
        kv = lax.dot_general(kz, vh, (((0,), (0,)), ((), ())), preferred_element_type=F32)
        state_ref[h] = gc_ref[h] * st + kv
        outs.append(o)
    o = jnp.concatenate(outs, axis=1)
    if readout:
        o = o + of_ref[0]
        o = o - _group_mean(o, HEAD_DIM)
        o = o * lax.rsqrt(_group_mean_sq(o, HEAD_DIM) + EPS)
        g = g_ref[0]
        o_ref[0] = ((o * gnw_ref[...] + gnb_ref[...]) * (g * _sigmoid(g))).astype(o_ref.dtype)
    else:
        o_ref[0] = o


def _ret_call(q, k, v, tabs, n_lat, backward, readout_args=None):
    b, n, w = q.shape
    c = RET_CHUNK
    nch = n // c
    nlc = n_lat // c
    if backward:
        chunk = lambda t: nch - 1 - t
    else:
        chunk = lambda t: jnp.where(t < nch - nlc, nlc + t, t - (nch - nlc))
    tok = lambda dt_w: pl.BlockSpec((1, c, dt_w), lambda i, t: (i, chunk(t), 0))
    const = lambda a: pl.BlockSpec(a.shape, lambda i, t: (0,) * a.ndim)
    in_specs = [tok(w), tok(w), tok(w)] + [const(a) for a in tabs]
    args = [q, k, v, *tabs]
    if readout_args is not None:
        o_f, g, gnw, gnb = readout_args
        in_specs += [tok(w), tok(w), const(gnw), const(gnb)]
        args += [o_f, g, gnw, gnb]
    out_dtype = BF16 if readout_args is not None else F32
    return pl.pallas_call(
        functools.partial(_ret_kernel, readout=readout_args is not None),
        grid=(b, nch),
        in_specs=in_specs,
        out_specs=tok(w),
        out_shape=jax.ShapeDtypeStruct((b, n, w), out_dtype),
        scratch_shapes=[pltpu.VMEM((N_HEADS, HEAD_DIM, HEAD_DIM), F32)],
        compiler_params=_params(("parallel", "arbitrary")),
    )(*args)


def _ret_tables(log_g, backward):
    c = RET_CHUNK
    pos = jnp.arange(c, dtype=F32)
    dist = (pos[None, :] - pos[:, None]) if backward else (pos[:, None] - pos[None, :])
    lg = log_g[:, None, None]
    decay = jnp.where(dist >= 0, jnp.exp(lg * jnp.maximum(dist, 0.0)), 0.0)
    to_state = (c - pos) if backward else (pos + 1.0)
    to_end = pos if backward else (c - 1.0 - pos)
    xi = jnp.exp(log_g[:, None] * to_state)[:, :, None]
    zeta = jnp.exp(log_g[:, None] * to_end)[:, :, None]
    ones = jnp.ones((1, 1, HEAD_DIM), F32)
    gc = jnp.exp(log_g * c)[:, None, None] * jnp.ones((1, HEAD_DIM, HEAD_DIM), F32)
    return decay, xi * ones, zeta * ones, gc


def _attn_kernel(*refs, n_maps, readout_scale, n_q, hq):
    refs = list(refs)
    lam_ref = refs.pop(0) if n_maps == 2 else None
    q_parts = [refs.pop(0) for _ in range(n_q)]
    kt_ref, v_ref = refs.pop(0), refs.pop(0)
    sub_ref = refs.pop(0) if n_maps == 2 else None
    o_ref, q_scr = refs.pop(0), refs.pop(0)
    first_ref = refs.pop(0) if n_maps == 2 else None
    hm = pl.program_id(2)
    for hh in range(hq):
        @pl.when(hm == hh)
        def _(hh=hh):
            pieces = [qp[0][:, hh * (qp.shape[2] // hq):(hh + 1) * (qp.shape[2] // hq)] for qp in q_parts]
            q_scr[...] = pieces[0] if n_q == 1 else jnp.concatenate(pieces, axis=1)
    q = q_scr[...]
    tq = q.shape[0]
    n_keys = kt_ref.shape[3]
    dv = v_ref.shape[3] // 2
    tk = min(KV_TILE, n_keys)
    n_full, tail = divmod(n_keys, tk)

    def step(off, size, carry):
        m, acc = carry
        s = _dot(q, kt_ref[0, 0, :, pl.ds(off, size)])
        m_new = jnp.maximum(m, jnp.max(s, axis=-1, keepdims=True))
        p = jnp.exp2(s - m_new)
        acc = jnp.exp2(m - m_new) * acc + _dot(p.astype(BF16), v_ref[0, 0, pl.ds(off, size), :])
        return m_new, acc

    carry = (jnp.full((tq, 1), -1e30, F32), jnp.zeros((tq, 2 * dv), F32))
    for j in range(n_full):
        carry = step(j * tk, tk, carry)
    if tail:
        carry = step(n_full * tk, tail, carry)
    acc = carry[1]
    o = acc[:, :dv] / acc[:, dv:]

    def store(head_of_step, val):
        for hh in range(N_HEADS):
            @pl.when(head_of_step == hh)
            def _(hh=hh):
                o_ref[0, :, hh * dv:(hh + 1) * dv] = val.astype(o_ref.dtype)

    if n_maps == 1:
        store(hm, o)
    else:
        @pl.when(hm % 2 == 0)
        def _():
            first_ref[...] = o

        @pl.when(hm % 2 == 1)
        def _():
            od = first_ref[...] - lam_ref[0] * o
            od = od * lax.rsqrt(jnp.mean(od * od, axis=-1, keepdims=True) + EPS) * sub_ref[...]
            store(hm // 2, od * readout_scale)


def _attn_call(q_parts, hq, kt, v, rows, keys, tq, diff=None):
    b, hk, d, n = kt.shape
    hv, dv = v.shape[1], v.shape[3] // 2
    (q0, qn), (k0, kn) = rows, keys
    assert q0 % tq == 0 and qn % tq == 0 and k0 % kn == 0
    assert sum(qp.shape[2] // hq for qp in q_parts) == d
    n_maps = 2 if diff is not None else 1
    in_specs = [pl.BlockSpec((1, tq, qp.shape[2]), lambda i, t, h: (i, q0 // tq + t, 0)) for qp in q_parts] + [
        pl.BlockSpec((1, 1, d, kn), lambda i, t, h: (i, h // (hq // hk), 0, k0 // kn)),
        pl.BlockSpec((1, 1, kn, 2 * dv), lambda i, t, h: (i, h // (hq // hv), k0 // kn, 0)),
    ]
    args = [*q_parts, kt, v]
    scratch = [pltpu.VMEM((tq, d), BF16)]
    scale = 1.0
    if diff is not None:
        lam, subln, scale = diff
        in_specs = [pl.BlockSpec(memory_space=pltpu.SMEM)] + in_specs + [pl.BlockSpec((1, dv), lambda i, t, h: (0, 0))]
        args = [lam.reshape(1)] + args + [subln.reshape(1, dv)]
        scratch.append(pltpu.VMEM((tq, dv), F32))
    return pl.pallas_call(
        functools.partial(_attn_kernel, n_maps=n_maps, readout_scale=scale, n_q=len(q_parts), hq=hq),
        grid=(b, qn // tq, hq),
        in_specs=in_specs,
        out_specs=pl.BlockSpec((1, tq, (hq // n_maps) * dv), lambda i, t, h: (i, t, 0)),
        out_shape=jax.ShapeDtypeStruct((b, qn, (hq // n_maps) * dv), BF16),
        scratch_shapes=scratch,
        compiler_params=_params(("parallel", "parallel", "arbitrary")),
    )(*args)


def _out_proj_kernel(ret_ref, *refs, n_lat_tiles):
    (att_refs, (w_ref, x_ref, mod_ref, g_ref, wr_ref, br_ref, xo_ref, hs_ref, gate_ref, pos_ref, cnt_ref)) = (
        refs[:6], refs[6:])
    mod = mod_ref[0, 0]
    g = g_ref[...]
    is_lat = pl.program_id(1) < n_lat_tiles
    y = _dot(ret_ref[0], w_ref[:GROUP_WIDTH])
    for m in range(3):
        ym = jnp.where(is_lat, att_refs[2 * m][0], att_refs[2 * m + 1][0])
        y = y + _dot(ym, w_ref[(m + 1) * GROUP_WIDTH:(m + 2) * GROUP_WIDTH])
    xn = x_ref[0] + mod[2:3] * (_rms(y) * g[1:2])
    xo_ref[0] = xn
    h = _rms(xn) * g[2:3] * (1.0 + mod[4:5]) + mod[3:4]
    w_hi, w_lo = _split_bf16(wr_ref[...])
    h_hi, h_lo = _split_bf16(h)
    logits = _dot_nt(w_hi, h_hi) + _dot_nt(w_hi, h_lo) + _dot_nt(w_lo, h_hi) + br_ref[...]
    row = lax.broadcasted_iota(jnp.int32, logits.shape, 0)
    vals = logits
    tops, idxs = [], []
    for _ in range(TOP_K):
        m = jnp.max(vals, axis=0, keepdims=True)
        idx = jnp.min(jnp.where(vals == m, row, N_EXPERTS), axis=0, keepdims=True)
        tops.append(m)
        idxs.append(idx)
        vals = jnp.where(row == idx, -jnp.inf, vals)
    ex = [jnp.exp(tv - tops[0]) for tv in tops]
    den = ex[0] + ex[1] + ex[2] + ex[3]
    gate_ref[0] = jnp.concatenate([e / den for e in ex], axis=0)
    tm = logits.shape[1]
    before = jnp.where(lax.broadcasted_iota(jnp.int32, (tm, tm), 0) < lax.broadcasted_iota(jnp.int32, (tm, tm), 1),
                       1.0, 0.0).astype(BF16)
    counts = jnp.zeros((N_EXPERTS, 1), F32)
    onehots, ranks = [], []
    for idx in idxs:
        onehot = jnp.where(row == idx, 1.0, 0.0)
        ranks.append(_dot(onehot.astype(BF16), before) + counts)
        onehots.append(onehot)
        counts = counts + jnp.sum(onehot, axis=1, keepdims=True)
    counts_b = jnp.broadcast_to(counts, (N_EXPERTS, LANES))
    earlier = jnp.where(lax.broadcasted_iota(jnp.int32, (N_EXPERTS, N_EXPERTS), 0)
                        > lax.broadcasted_iota(jnp.int32, (N_EXPERTS, N_EXPERTS), 1), 1.0, 0.0).astype(BF16)
    first = _dot(earlier, counts_b.astype(BF16))[:, 0:1]
    pos = [jnp.sum(oh * (rk + first), axis=0, keepdims=True).astype(jnp.int32) for oh, rk in zip(onehots, ranks)]
    pos_ref[0] = jnp.concatenate(pos, axis=0)
    cnt_ref[0, 0] = counts_b.astype(jnp.int32)
    slot = lax.broadcasted_iota(jnp.int32, (TOP_K * tm, tm), 0)
    pick = jnp.where(slot == pos[0], 1.0, 0.0)
    for p in pos[1:]:
        pick = jnp.where(slot == p, 1.0, pick)
    hs_ref[0, 0] = _pack_halves(_dot(pick.astype(BF16), h.astype(BF16)))


def _out_proj_call(ret_y, att_ys, w_out, xc, modl, g, wr_t, br, n_lat, n_rows):
    b, n, d = xc.shape
    tm = TOKEN_TILE
    nlt = n_lat // tm
    tok = lambda w: pl.BlockSpec((1, tm, w), lambda i, t: (i, t, 0))
    lat = pl.BlockSpec((1, tm, GROUP_WIDTH), lambda i, t: (i, jnp.minimum(t, nlt - 1), 0))
    ctx = pl.BlockSpec((1, tm, GROUP_WIDTH), lambda i, t: (i, jnp.maximum(t - nlt, 0), 0))
    const2 = lambda a: pl.BlockSpec(a.shape, lambda i, t: (0, 0))
    sel = pl.BlockSpec((1, TOP_K, tm), lambda i, t: (i, 0, t))
    return pl.pallas_call(
        functools.partial(_out_proj_kernel, n_lat_tiles=nlt),
        grid=(b, n_rows // tm),
        in_specs=[
            tok(GROUP_WIDTH), lat, ctx, lat, ctx, lat, ctx, const2(w_out), tok(d), _mod_spec(d, nlt),
            const2(g), const2(wr_t), const2(br),
        ],
        out_specs=[tok(d),
                   pl.BlockSpec((1, 1, TOP_K * tm, d // 2), lambda i, t: (i, t, 0, 0)),
                   sel, sel,
                   pl.BlockSpec((1, 1, N_EXPERTS, LANES), lambda i, t: (i, t, 0, 0))],
        out_shape=[
            jax.ShapeDtypeStruct((b, n_rows, d), F32),
            jax.ShapeDtypeStruct((b, n_rows // tm, TOP_K * tm, d // 2), jnp.uint32),
            jax.ShapeDtypeStruct((b, TOP_K, n_rows), F32),
            jax.ShapeDtypeStruct((b, TOP_K, n_rows), jnp.int32),
            jax.ShapeDtypeStruct((b, n_rows // tm, N_EXPERTS, LANES), jnp.int32),
        ],
        compiler_params=_params(("parallel", "parallel")),
    )(ret_y, *[a for pair in att_ys for a in pair], w_out, xc, modl, g, wr_t, br)


def _w1_prep_kernel(w_ref, g_ref, l_ref):
    r = lax.broadcasted_iota(jnp.int32, (MXU_DIM, MXU_DIM), 0)
    c = lax.broadcasted_iota(jnp.int32, (MXU_DIM, MXU_DIM), 1)
    src = jnp.where(c < LANES, 2 * c, 2 * (c - LANES) + 1)
    perm = jnp.where(r == src, 1.0, 0.0).astype(BF16)
    for j in range(w_ref.shape[3] // MXU_DIM):
        out = _dot(w_ref[0, 0, :, j * MXU_DIM:(j + 1) * MXU_DIM].astype(BF16), perm)
        g_ref[0, :, j * LANES:(j + 1) * LANES] = out[:, :LANES].astype(BF16)
        l_ref[0, :, j * LANES:(j + 1) * LANES] = out[:, LANES:].astype(BF16)


def _w1_prep_call(w1_all, layer):
    _, e, d, f2 = w1_all.shape
    f = f2 // 2
    rows = 256
    out_spec = pl.BlockSpec((1, rows, f), lambda i, j: (i, j, 0))
    return pl.pallas_call(
        _w1_prep_kernel,
        grid=(e, d // rows),
        in_specs=[pl.BlockSpec((1, 1, rows, f2), lambda i, j: (layer, i, j, 0))],
        out_specs=[out_spec, out_spec],
        out_shape=[jax.ShapeDtypeStruct((e, d, f), BF16)] * 2,
        compiler_params=_params(("parallel", "parallel")),
    )(w1_all)


def _expert_kernel(be_ref, nu_ref, sw_ref, nx_ref, sl_ref, x_ref, b1g_ref, b1l_ref, b2_ref,
                   w1g_hbm, w1l_hbm, w2_hbm, o_ref, wg_ref, wl_ref, w2f_ref, w2b_ref, sem_ref, *, layer):
    i = pl.program_id(0)

    def weight_copies(expert, slot):
        return (pltpu.make_async_copy(w1g_hbm.at[expert], wg_ref.at[slot], sem_ref.at[slot, 0]),
                pltpu.make_async_copy(w1l_hbm.at[expert], wl_ref.at[slot], sem_ref.at[slot, 1]),
                pltpu.make_async_copy(w2_hbm.at[layer, expert], w2f_ref.at[slot], sem_ref.at[slot, 2]))

    @pl.when(jnp.logical_and(i == 0, nu_ref[0] > 0))
    def _():
        for cp in weight_copies(be_ref[0], 0):
            cp.start()

    @pl.when(sw_ref[i] == 1)
    def _():
        slot = sl_ref[i]
        for cp in weight_copies(be_ref[i], slot):
            cp.wait()

        @pl.when(nx_ref[i] >= 0)
        def _():
            for cp in weight_copies(nx_ref[i], 1 - slot):
                cp.start()

        w2b_ref[...] = w2f_ref[slot].astype(BF16)

    @pl.when(i < nu_ref[0])
    def _():
        slot = sl_ref[i]
        x = _unpack_halves(x_ref[...]).astype(BF16)
        glu = jnp.minimum(_dot(x, wg_ref[slot]) + b1g_ref[0], SWIGLU_LIMIT)
        lin = jnp.clip(_dot(x, wl_ref[slot]) + b1l_ref[0], -SWIGLU_LIMIT, SWIGLU_LIMIT)
        act = glu * _sigmoid(SWIGLU_ALPHA * glu) * (lin + 1.0)
        o_ref[...] = _pack_halves(_dot(act.astype(BF16), w2b_ref[...]) + b2_ref[0])

    @pl.when(i >= nu_ref[0])
    def _():
        o_ref[...] = jnp.zeros_like(o_ref)


def _expert_call(block_expert, n_used, x_rows, w1g, w1l, b1g, b1l, w2_all, layer, b2):
    r = x_rows.shape[0]
    d, f = w1g.shape[1:]
    tm = MOE_BLOCK
    n_blocks = r // tm
    blk = jnp.arange(n_blocks, dtype=jnp.int32)
    prev = jnp.concatenate([block_expert[:1], block_expert[:-1]])
    switch = (blk < n_used[0]) & ((blk == 0) | (block_expert != prev))
    slot = (jnp.cumsum(switch.astype(jnp.int32)) - 1) % 2
    later = jnp.where(switch, blk, n_blocks)
    next_start = jnp.concatenate([lax.cummin(later, reverse=True)[1:], jnp.full((1,), n_blocks, jnp.int32)])
    next_expert = jnp.where(next_start < n_blocks, block_expert[jnp.minimum(next_start, n_blocks - 1)], -1)
    scalars = (block_expert, n_used, switch.astype(jnp.int32), next_expert.astype(jnp.int32), slot.astype(jnp.int32))
    bias = lambda w: pl.BlockSpec((1, 1, w), lambda i, be, *_: (be[i], 0, 0))
    hbm = pl.BlockSpec(memory_space=pl.ANY)
    return pl.pallas_call(
        functools.partial(_expert_kernel, layer=layer),
        grid_spec=pltpu.PrefetchScalarGridSpec(
            num_scalar_prefetch=len(scalars),
            grid=(n_blocks,),
            in_specs=[pl.BlockSpec((tm, d // 2), lambda i, *_: (i, 0)), bias(f), bias(f), bias(d), hbm, hbm, hbm],
            out_specs=pl.BlockSpec((tm, d // 2), lambda i, *_: (i, 0)),
            scratch_shapes=[pltpu.VMEM((2, d, f), BF16), pltpu.VMEM((2, d, f), BF16), pltpu.VMEM((2, f, d), F32),
                            pltpu.VMEM((f, d), BF16), pltpu.SemaphoreType.DMA((2, 3))],
        ),
        out_shape=jax.ShapeDtypeStruct((r, d // 2), jnp.uint32),
        compiler_params=_params(("arbitrary",)),
    )(*scalars, x_rows, b1g, b1l, b2, w1g, w1l, w2_all)


def _combine_kernel(*refs, n_parts):
    x_ref, y_refs = refs[0], refs[1:1 + n_parts]
    gate_ref, pos_ref, mod_ref, g_ref, o_ref = refs[1 + n_parts:]
    gates, pos = gate_ref[0], pos_ref[0]
    lane = lax.broadcasted_iota(jnp.int32, (gates.shape[0], y_refs[0].shape[2]), 1)
    mix = jnp.zeros(lane.shape, F32)
    for k in range(TOP_K):
        mix = jnp.where(lane == pos[:, k:k + 1], gates[:, k:k + 1], mix)
    mix_hi, mix_lo = _split_bf16(mix)
    mod = mod_ref[0, 0]
    part_batches = pl.num_programs(0) // n_parts
    for p, y_ref in enumerate(y_refs):
        @pl.when(pl.program_id(0) // part_batches == p)
        def _(y_ref=y_ref):
            rows = _unpack_halves(y_ref[0, 0]).astype(BF16)
            y = _dot(mix_hi, rows) + _dot(mix_lo, rows)
            o_ref[0] = x_ref[0] + mod[5:6] * (_rms(y) * g_ref[...][3:4])


def _combine_call(xc, picked_parts, gates, pos, modl, g, n_lat):
    b, n, d = xc.shape
    tm = TOKEN_TILE
    n_parts = len(picked_parts)
    bp = b // n_parts
    tok = pl.BlockSpec((1, tm, d), lambda i, t: (i, t, 0))

    def part_spec(p):
        def index(i, t):
            live = (i >= p * bp) & (i < (p + 1) * bp)
            return jnp.clip(i - p * bp, 0, bp - 1), jnp.where(live, t, 0), 0, 0
        return pl.BlockSpec((1, 1, TOP_K * tm, d // 2), index)

    return pl.pallas_call(
        functools.partial(_combine_kernel, n_parts=n_parts),
        grid=(b, n // tm),
        in_specs=[tok] + [part_spec(p) for p in range(n_parts)] + [
            pl.BlockSpec((1, tm, TOP_K), lambda i, t: (i, t, 0)),
            pl.BlockSpec((1, tm, TOP_K), lambda i, t: (i, t, 0)),
            _mod_spec(d, n_lat // tm),
            pl.BlockSpec(g.shape, lambda i, t: (0, 0)),
        ],
        out_specs=tok,
        out_shape=jax.ShapeDtypeStruct((b, n, d), F32),
        compiler_params=_params(("parallel", "parallel")),
    )(xc, *picked_parts, gates, pos, modl, g)


def _rope_tables(rows, n_ctx, rot_dim):
    row = jnp.repeat(jnp.arange(rows), GRID_W)
    col = jnp.tile(jnp.arange(GRID_W), rows)
    n_freq = rot_dim // 4
    inv = ROPE_THETA ** (-jnp.arange(n_freq, dtype=F32) / n_freq)
    ang = jnp.concatenate([row[:, None] * inv, col[:, None] * inv], axis=-1)
    cos, sin = jnp.cos(ang), jnp.sin(ang)
    reps = 256 // rot_dim
    cos_t = jnp.tile(jnp.concatenate([cos, cos], axis=-1), (1, reps))
    sin_t = jnp.tile(jnp.concatenate([-sin, sin], axis=-1), (1, reps))
    cos_t = jnp.concatenate([cos_t, jnp.ones((n_ctx, 256), F32)], axis=0)
    sin_t = jnp.concatenate([sin_t, jnp.zeros((n_ctx, 256), F32)], axis=0)
    return cos_t, sin_t


def _pack_w_in(w):
    parts, src = [], 0
    for _, width, padded in IN_PIECES:
        parts.append(w[:, src:src + width])
        if padded > width:
            parts.append(jnp.zeros((w.shape[0], padded - width), w.dtype))
        src += width
    return jnp.concatenate(parts, axis=1).astype(BF16)


def _pack_w_uq(w):
    wh = w.reshape(MLA_Q_RANK, N_HEADS, MLA_NOPE + MLA_ROPE)
    nope = wh[:, :, :MLA_NOPE].reshape(MLA_Q_RANK, N_HEADS * MLA_NOPE)
    rope = wh[:, :, MLA_NOPE:].reshape(MLA_Q_RANK, N_HEADS * MLA_ROPE)
    packed = jnp.concatenate([nope, rope], axis=1)
    return jnp.pad(packed, ((0, 256 - MLA_Q_RANK), (0, 0))).astype(BF16)


def _pack_w_ukv(w):
    wh = w.reshape(MLA_KV_RANK, N_HEADS, MLA_NOPE + HEAD_DIM)
    kn = wh[:, :, :MLA_NOPE].reshape(MLA_KV_RANK, N_HEADS * MLA_NOPE)
    vv = wh[:, :, MLA_NOPE:].reshape(MLA_KV_RANK, N_HEADS * HEAD_DIM)
    return jnp.concatenate([kn, vv], axis=1).astype(BF16)


def _attention(q_parts, hq, kt, v, n_lat, n_ctx, ctx_out, diff=None):
    n = n_lat + n_ctx
    y = _attn_call(q_parts, hq, kt, v, (0, n_lat), (0, n), min(Q_TILE, n_lat), diff)
    if ctx_out:
        y_c = _attn_call(q_parts, hq, kt, v, (n_lat, n_ctx), (n_lat, n_ctx), n_ctx, diff)
    else:
        y_c = jnp.zeros((y.shape[0], n_ctx, y.shape[2]), y.dtype)
    return y, y_c


def _steps(x, bounds, offsets):
    deltas = offsets - jnp.concatenate([jnp.zeros_like(offsets[..., :1]), offsets[..., :-1]], axis=-1)
    return x + jnp.sum(jnp.where(x[..., None] >= bounds[..., None, :], deltas[..., None, :], 0), axis=-1)


def _moe(x_new, hs, gates, pos, counts, modl, g, n_lat, w1_all, b1, w2_all, b2, layer):
    b, n, d = x_new.shape
    f = w2_all.shape[2]
    tile_rows = TOP_K * TOKEN_TILE
    n_parts = 2 if b % 2 == 0 else 1
    tiles = (b // n_parts) * (n // TOKEN_TILE)
    n_blocks = -(-(tiles * tile_rows) // MOE_BLOCK) + N_EXPERTS
    hs_rows = hs.reshape(b * (n // TOKEN_TILE) * tile_rows, d // 2)
    w1g, w1l = _w1_prep_call(w1_all, layer)
    b1g, b1l = b1[:, 0::2].reshape(N_EXPERTS, 1, f), b1[:, 1::2].reshape(N_EXPERTS, 1, f)
    picked_parts = []
    for p in range(n_parts):
        tile_counts = counts.reshape(-1, N_EXPERTS)[p * tiles:(p + 1) * tiles]
        tile_base = jnp.cumsum(tile_counts, axis=0) - tile_counts
        first = jnp.cumsum(tile_counts, axis=1) - tile_counts
        padded = (jnp.sum(tile_counts, axis=0) + MOE_BLOCK - 1) // MOE_BLOCK * MOE_BLOCK
        pad_ends = jnp.cumsum(padded)
        pad_start = pad_ends - padded
        block_start = jnp.arange(n_blocks, dtype=jnp.int32) * MOE_BLOCK
        block_expert = jnp.minimum(jnp.sum((pad_ends[None, :] <= block_start[:, None]).astype(jnp.int32), axis=1),
                                   N_EXPERTS - 1)
        n_used = (pad_ends[-1:] // MOE_BLOCK).astype(jnp.int32)
        j = block_start[:, None] + jnp.arange(MOE_BLOCK, dtype=jnp.int32)[None, :] - pad_start[block_expert][:, None]
        tile_offset = (jnp.arange(tiles, dtype=jnp.int32)[:, None] + p * tiles) * tile_rows + first - tile_base
        src = _steps(j, tile_base.T[block_expert], tile_offset.T[block_expert])
        src = jnp.clip(src, 0, hs_rows.shape[0] - 1)
        x_rows = hs_rows.at[src.reshape(-1)].get(mode="promise_in_bounds")
        y_rows = _expert_call(block_expert, n_used, x_rows, w1g, w1l, b1g, b1l, w2_all, layer,
                              b2.reshape(N_EXPERTS, 1, d))
        slot = jnp.broadcast_to(jnp.arange(tile_rows, dtype=jnp.int32)[None, :], (tiles, tile_rows))
        back = _steps(slot, first, tile_base + pad_start[None, :] - first)
        picked = y_rows.at[back.reshape(-1)].get(mode="promise_in_bounds")
        picked_parts.append(picked.reshape(b // n_parts, n // TOKEN_TILE, tile_rows, d // 2))
    return _combine_call(x_new, picked_parts, gates.transpose(0, 2, 1), pos.transpose(0, 2, 1), modl, g, n_lat)


def kernel(x, c, ctx, c_ctx, ada_w, ada_b, norm_g, w_in, w_out, ret_log_decay, ret_gn_w, ret_gn_b,
           diff_lambda, diff_subln, gqa_qk_norm, mla_q_norm, mla_kv_norm, mla_w_uq, mla_w_ukv,
           router_w, router_b, exp_w1, exp_b1, exp_w2, exp_b2):
    b, s, d = x.shape
    n_ctx = ctx.shape[1]
    n = s + n_ctx
    depth = ada_w.shape[0]
    assert n_ctx % TOKEN_TILE == 0 and s % TOKEN_TILE == 0 and s % GRID_W == 0
    assert n_ctx % RET_CHUNK == 0 and s % min(Q_TILE, s) == 0 and s % n_ctx == 0

    tables = _rope_tables(s // GRID_W, n_ctx, HEAD_DIM) + _rope_tables(s // GRID_W, n_ctx, DIFF_D)
    c_rows = jnp.zeros((16, d), F32).at[:b].set(c).at[b].set(c_ctx)
    mods = _ada_call(c_rows, ada_w, ada_b)
    xc = jnp.concatenate([x, ctx], axis=1)

    for l in range(depth):
        last = l == depth - 1
        lam_init = 0.8 - 0.6 * math.exp(-0.3 * l)
        mod_lat = mods[l, :b].reshape(b, 1, 6, d)
        mod_ctx = jnp.broadcast_to(mods[l, b].reshape(1, 1, 6, d), (b, 1, 6, d))
        modl = jnp.concatenate([mod_ctx, mod_lat], axis=1)

        (rq, rk, rv, rg, dq, dkt, dv, gq, gkt, gv, mqn, mqr, mkt, mv) = _in_proj_call(
            xc, modl, norm_g[l, 0:1], _pack_w_in(w_in[l]), tables,
            jnp.tile(gqa_qk_norm[l, 0], N_HEADS)[None, :], jnp.tile(gqa_qk_norm[l, 1], GQA_KV_HEADS)[None, :],
            jnp.pad(mla_q_norm[l], (0, 256 - MLA_Q_RANK))[None, :], mla_kv_norm[l][None, :],
            _pack_w_uq(mla_w_uq[l]), _pack_w_ukv(mla_w_ukv[l]), s)

        log_g = -jnp.exp(ret_log_decay[l].astype(F32))
        o_f = _ret_call(rq, rk, rv, _ret_tables(log_g[0], False), s, False)
        ret_y = _ret_call(rq, rk, rv, _ret_tables(log_g[1], True), s, True,
                          (o_f, rg, ret_gn_w[l][None, :], ret_gn_b[l][None, :]))

        lp = diff_lambda[l].astype(F32)
        lam = jnp.exp(jnp.sum(lp[0] * lp[1])) - jnp.exp(jnp.sum(lp[2] * lp[3])) + lam_init
        dif_y = _attention([dq], 2 * N_HEADS, dkt, dv, s, n_ctx, not last, (lam, diff_subln[l], 1.0 - lam_init))

        gqa_y = _attention([gq], N_HEADS, gkt, gv, s, n_ctx, not last)

        mla_y = _attention([mqn, mqr], N_HEADS, mkt, mv, s, n_ctx, not last)

        n_rows = s if last else n
        x_new, hs, gates, pos, counts = _out_proj_call(
            ret_y, (dif_y, gqa_y, mla_y), w_out[l].astype(BF16), xc, modl, norm_g[l], router_w[l].T,
            router_b[l][:, None], s, n_rows)
        xc = _moe(x_new, hs, gates, pos, counts[..., 0], modl, norm_g[l], s,
                  exp_w1, exp_b1[l], exp_w2, exp_b2[l], l)

    return xc[:, :s]
```

```python
import functools
import math

import jax
import jax.numpy as jnp
from jax import lax
from jax.experimental import pallas as pl
from jax.experimental.pallas import tpu as pltpu

F32 = jnp.float32
BF16 = jnp.bfloat16

GRID_W = 64
ROPE_THETA = 10000.0
EPS = 1e-6
GROUP_WIDTH = 256
HEAD_DIM = 64
N_HEADS = 4
RET_CHUNK = 256
DIFF_D = 32
GQA_KV_HEADS = 2
MLA_Q_RANK = 192
MLA_KV_RANK = 128
MLA_NOPE = 64
MLA_ROPE = 32
N_EXPERTS = 32
TOP_K = 4
SWIGLU_LIMIT = 7.0
SWIGLU_ALPHA = 1.702
MOE_BLOCK = 512

LANES = 128
MXU_DIM = 256
TOKEN_TILE = 256
Q_TILE = 1024
KV_TILE = 512
VMEM_LIMIT = 48 * 1024 * 1024
LOG2E = math.log2(math.e)

IN_PIECES = (
    ("ret_q", 256, 256), ("ret_k", 256, 256), ("ret_v", 256, 256), ("ret_g", 256, 256),
    ("dif_q", 256, 256), ("dif_k", 256, 256), ("dif_v", 256, 256),
    ("gqa_q", 256, 256), ("gqa_k", 128, 128), ("gqa_v", 128, 128),
    ("mla_cq", MLA_Q_RANK, 256), ("mla_ckv", MLA_KV_RANK, 128), ("mla_kr", MLA_ROPE, 128),
)
IN_OFFSETS = {}
_off = 0
for _name, _w, _pw in IN_PIECES:
    IN_OFFSETS[_name] = _off
    _off += _pw
IN_PACKED_WIDTH = _off


def _params(sem):
    return pltpu.CompilerParams(dimension_semantics=sem, vmem_limit_bytes=VMEM_LIMIT)


def _rms(x):
    return x * lax.rsqrt(jnp.mean(x * x, axis=-1, keepdims=True) + EPS)


def _split_bf16(a):
    hi = a.astype(BF16)
    lo = (a - hi.astype(F32)).astype(BF16)
    return hi, lo


def _dot_nt(a, b):
    return lax.dot_general(a, b, (((1,), (1,)), ((), ())), preferred_element_type=F32)


def _dot(a, b):
    return jnp.dot(a, b, preferred_element_type=F32)


def _sigmoid(a):
    return 1.0 / (1.0 + jnp.exp(-a))


def _pack_halves(a):
    w = a.shape[1] // 2
    bits = lax.bitcast_convert_type(a.astype(BF16).astype(F32), jnp.uint32)
    return bits[:, :w] | (bits[:, w:] >> 16)


def _unpack_halves(u):
    hi = lax.bitcast_convert_type(u & jnp.uint32(0xFFFF0000), F32)
    lo = lax.bitcast_convert_type(u << 16, F32)
    return jnp.concatenate([hi, lo], axis=1)


def _mod_spec(d, n_lat_tiles):
    return pl.BlockSpec((1, 1, 6, d), lambda i, t: (i, jnp.where(t < n_lat_tiles, 1, 0), 0, 0))


def _ada_kernel(c_ref, w_ref, b_ref, o_ref):
    s = c_ref[...]
    s = s * _sigmoid(s)
    s_hi, s_lo = _split_bf16(s)
    w_hi, w_lo = _split_bf16(w_ref[0])
    o_ref[0] = _dot(s_hi, w_hi) + _dot(s_hi, w_lo) + _dot(s_lo, w_hi) + b_ref[0]


def _ada_call(c_rows, ada_w, ada_b):
    depth, d, n6 = ada_w.shape
    rows = c_rows.shape[0]
    tn = 1536
    return pl.pallas_call(
        _ada_kernel,
        grid=(depth, n6 // tn),
        in_specs=[
            pl.BlockSpec((rows, d), lambda l, j: (0, 0)),
            pl.BlockSpec((1, d, tn), lambda l, j: (l, 0, j)),
            pl.BlockSpec((1, 1, tn), lambda l, j: (l, 0, j)),
        ],
        out_specs=pl.BlockSpec((1, rows, tn), lambda l, j: (l, 0, j)),
        out_shape=jax.ShapeDtypeStruct((depth, rows, n6), F32),
        compiler_params=_params(("parallel", "parallel")),
    )(c_rows, ada_w, ada_b.reshape(depth, 1, n6))


def _rope(x, cos, sin_signed, half):
    outs = []
    for c in range(x.shape[1] // LANES):
        sl = slice(c * LANES, (c + 1) * LANES)
        xc = x[:, sl]
        lane = lax.broadcasted_iota(jnp.int32, xc.shape, 1)
        first_half = (lane % (2 * half)) < half
        partner = jnp.where(first_half, pltpu.roll(xc, LANES - half, 1), pltpu.roll(xc, half, 1))
        outs.append(xc * cos[:, sl] + partner * sin_signed[:, sl])
    return outs[0] if len(outs) == 1 else jnp.concatenate(outs, axis=1)


def _group_mean(x, gsize):
    w = x.shape[1]
    r = lax.broadcasted_iota(jnp.int32, (w, w), 0) // gsize
    c = lax.broadcasted_iota(jnp.int32, (w, w), 1) // gsize
    ones = jnp.where(r == c, 1.0, 0.0).astype(BF16)
    hi, lo = _split_bf16(x)
    return (_dot(hi, ones) + _dot(lo, ones)) * (1.0 / gsize)


def _group_mean_sq(x, gsize):
    return _group_mean(x * x, gsize)


def _in_proj_kernel(x_ref, mod_ref, g_ref, w_ref, cos64_ref, sin64_ref, cos32_ref, sin32_ref,
                    gqn_ref, gkn_ref, mqn_ref, mkvn_ref, wuq_ref, wukv_ref,
                    rq_ref, rk_ref, rv_ref, rg_ref, dq_ref, dkt_ref, dv_ref, gq_ref, gkt_ref, gv_ref,
                    mqn_o, mqr_o, mkt_o, mv_o):
    mod = mod_ref[0, 0]
    h = _rms(x_ref[0]) * g_ref[...] * (1.0 + mod[1:2]) + mod[0:1]
    hb = h.astype(BF16)
    cos64, sin64 = cos64_ref[...], sin64_ref[...]
    cos32, sin32 = cos32_ref[...], sin32_ref[...]

    def proj(name, width):
        o = IN_OFFSETS[name]
        return _dot(hb, w_ref[:, o:o + width])

    def store_keys(o_ref, k, extra=None):
        kt = k.T
        d = kt.shape[0] // o_ref.shape[1]
        for hd in range(o_ref.shape[1]):
            rows = kt[hd * d:(hd + 1) * d]
            o_ref[0, hd] = (rows if extra is None else jnp.concatenate([rows, extra], axis=0)).astype(o_ref.dtype)

    def store_values(o_ref, v):
        dv = v.shape[1] // o_ref.shape[1]
        ones = jnp.ones((v.shape[0], dv), F32)
        for hd in range(o_ref.shape[1]):
            o_ref[0, hd] = jnp.concatenate([v[:, hd * dv:(hd + 1) * dv], ones], axis=1).astype(o_ref.dtype)

    rq_ref[0] = _rope(proj("ret_q", 256), cos64, sin64, 32).astype(BF16)
    rk_ref[0] = (_rope(proj("ret_k", 256), cos64, sin64, 32) * (HEAD_DIM ** -0.5)).astype(BF16)
    rv_ref[0] = proj("ret_v", 256).astype(BF16)
    rg_ref[0] = proj("ret_g", 256)
    dq_ref[0] = (_rope(proj("dif_q", 256), cos32, sin32, 16) * (DIFF_D ** -0.5 * LOG2E)).astype(BF16)
    store_keys(dkt_ref, _rope(proj("dif_k", 256), cos32, sin32, 16))
    store_values(dv_ref, proj("dif_v", 256))
    gq = proj("gqa_q", 256)
    gq = gq * lax.rsqrt(_group_mean_sq(gq, HEAD_DIM) + EPS) * gqn_ref[...]
    gq_ref[0] = (_rope(gq, cos64, sin64, 32) * (HEAD_DIM ** -0.5 * LOG2E)).astype(BF16)
    gk = proj("gqa_k", 128)
    gk = gk * lax.rsqrt(_group_mean_sq(gk, HEAD_DIM) + EPS) * gkn_ref[...]
    store_keys(gkt_ref, _rope(gk, cos64[:, :LANES], sin64[:, :LANES], 32))
    store_values(gv_ref, proj("gqa_v", 128))
    cq = proj("mla_cq", 256)
    cq = cq * lax.rsqrt(jnp.sum(cq * cq, axis=-1, keepdims=True) * (1.0 / MLA_Q_RANK) + EPS) * mqn_ref[...]
    q_up = _dot(cq.astype(BF16), wuq_ref[...])
    mla_scale = (MLA_NOPE + MLA_ROPE) ** -0.5 * LOG2E
    mqn_o[0] = (q_up[:, :256] * mla_scale).astype(BF16)
    mqr_o[0] = (_rope(q_up[:, 256:], cos32[:, :LANES], sin32[:, :LANES], 16) * mla_scale).astype(BF16)
    ckv = proj("mla_ckv", 128)
    ckv = ckv * lax.rsqrt(jnp.mean(ckv * ckv, axis=-1, keepdims=True) + EPS) * mkvn_ref[...]
    kv_up = _dot(ckv.astype(BF16), wukv_ref[...])
    kr = _rope(proj("mla_kr", 128), cos32[:, :LANES], sin32[:, :LANES], 16)
    store_keys(mkt_o, kv_up[:, :256], extra=kr.T[:MLA_ROPE])
    store_values(mv_o, kv_up[:, 256:])


def _in_proj_call(xc, modl, g0, w_in_p, tables, gqn, gkn, mqn, mkvn, wuq_p, wukv_p, n_lat):
    b, n, d = xc.shape
    tm = TOKEN_TILE
    tok = lambda w: pl.BlockSpec((1, tm, w), lambda i, t: (i, t, 0))
    const2 = lambda a: pl.BlockSpec(a.shape, lambda i, t: (0, 0))
    tab = pl.BlockSpec((tm, 256), lambda i, t: (t, 0))
    keys_t = lambda h, dk: pl.BlockSpec((1, h, dk, tm), lambda i, t: (i, 0, 0, t))
    vals = lambda h: pl.BlockSpec((1, h, tm, 2 * HEAD_DIM), lambda i, t: (i, 0, t, 0))
    sds = jax.ShapeDtypeStruct
    outs = [
        (tok(256), sds((b, n, 256), BF16)), (tok(256), sds((b, n, 256), BF16)),
        (tok(256), sds((b, n, 256), BF16)), (tok(256), sds((b, n, 256), F32)),
        (tok(256), sds((b, n, 256), BF16)),
        (keys_t(2 * N_HEADS, DIFF_D), sds((b, 2 * N_HEADS, DIFF_D, n), BF16)),
        (vals(N_HEADS), sds((b, N_HEADS, n, 2 * HEAD_DIM), BF16)),
        (tok(256), sds((b, n, 256), BF16)),
        (keys_t(GQA_KV_HEADS, HEAD_DIM), sds((b, GQA_KV_HEADS, HEAD_DIM, n), BF16)),
        (vals(GQA_KV_HEADS), sds((b, GQA_KV_HEADS, n, 2 * HEAD_DIM), BF16)),
        (tok(256), sds((b, n, 256), BF16)), (tok(128), sds((b, n, 128), BF16)),
        (keys_t(N_HEADS, MLA_NOPE + MLA_ROPE), sds((b, N_HEADS, MLA_NOPE + MLA_ROPE, n), BF16)),
        (vals(N_HEADS), sds((b, N_HEADS, n, 2 * HEAD_DIM), BF16)),
    ]
    return pl.pallas_call(
        _in_proj_kernel,
        grid=(b, n // tm),
        in_specs=[
            tok(d), _mod_spec(d, n_lat // tm),
            const2(g0), const2(w_in_p), tab, tab, tab, tab,
            const2(gqn), const2(gkn), const2(mqn), const2(mkvn), const2(wuq_p), const2(wukv_p),
        ],
        out_specs=[o[0] for o in outs],
        out_shape=[o[1] for o in outs],
        compiler_params=_params(("parallel", "parallel")),
    )(xc, modl, g0, w_in_p, *tables, gqn, gkn, mqn, mkvn, wuq_p, wukv_p)


def _ret_kernel(*refs, readout):
    if readout:
        (q_ref, k_ref, v_ref, dec_ref, xi_ref, zeta_ref, gc_ref, of_ref, g_ref, gnw_ref, gnb_ref,
         o_ref, state_ref) = refs
    else:
        q_ref, k_ref, v_ref, dec_ref, xi_ref, zeta_ref, gc_ref, o_ref, state_ref = refs

    @pl.when(pl.program_id(1) == 0)
    def _():
        state_ref[...] = jnp.zeros_like(state_ref)

    q, k, v = q_ref[0], k_ref[0], v_ref[0]
    outs = []
    for h in range(N_HEADS):
        sl = slice(h * HEAD_DIM, (h + 1) * HEAD_DIM)
        qh, kh, vh = q[:, sl], k[:, sl], v[:, sl]
        st = state_ref[h]
        inner = _dot_nt(qh, kh) * dec_ref[h]
        o = _dot(inner.astype(BF16), vh) + _dot(qh, st.astype(BF16)) * xi_ref[h]
        kz = (kh.astype(F32) * zeta_ref[h]).astype(BF16)
        kv = lax.dot_general(kz, vh, (((0,), (0,)), ((), ())), preferred_element_type=F32)
        state_ref[h] = gc_ref[h] * st + kv
        outs.append(o)
    o = jnp.concatenate(outs, axis=1)
    if readout:
        o = o + of_ref[0]
        o = o - _group_mean(o, HEAD_DIM)
        o = o * lax.rsqrt(_group_mean_sq(o, HEAD_DIM) + EPS)
        g = g_ref[0]
        o_ref[0] = ((o * gnw_ref[...] + gnb_ref[...]) * (g * _sigmoid(g))).astype(o_ref.dtype)
    else:
        o_ref[0] = o


def _ret_call(q, k, v, tabs, n_lat, backward, readout_args=None):
    b, n, w = q.shape
    c = RET_CHUNK
    nch = n // c
    nlc = n_lat // c
    if backward:
        chunk = lambda t: nch - 1 - t
    else:
        chunk = lambda t: jnp.where(t < nch - nlc, nlc + t, t - (nch - nlc))
    tok = lambda dt_w: pl.BlockSpec((1, c, dt_w), lambda i, t: (i, chunk(t), 0))
    const = lambda a: pl.BlockSpec(a.shape, lambda i, t: (0,) * a.ndim)
    in_specs = [tok(w), tok(w), tok(w)] + [const(a) for a in tabs]
    args = [q, k, v, *tabs]
    if readout_args is not None:
        o_f, g, gnw, gnb = readout_args
        in_specs += [tok(w), tok(w), const(gnw), const(gnb)]
        args += [o_f, g, gnw, gnb]
    out_dtype = BF16 if readout_args is not None else F32
    return pl.pallas_call(
        functools.partial(_ret_kernel, readout=readout_args is not None),
        grid=(b, nch),
        in_specs=in_specs,
        out_specs=tok(w),
        out_shape=jax.ShapeDtypeStruct((b, n, w), out_dtype),
        scratch_shapes=[pltpu.VMEM((N_HEADS, HEAD_DIM, HEAD_DIM), F32)],
        compiler_params=_params(("parallel", "arbitrary")),
    )(*args)


def _ret_tables(log_g, backward):
    c = RET_CHUNK
    pos = jnp.arange(c, dtype=F32)
    dist = (pos[None, :] - pos[:, None]) if backward else (pos[:, None] - pos[None, :])
    lg = log_g[:, None, None]
    decay = jnp.where(dist >= 0, jnp.exp(lg * jnp.maximum(dist, 0.0)), 0.0)
    to_state = (c - pos) if backward else (pos + 1.0)
    to_end = pos if backward else (c - 1.0 - pos)
    xi = jnp.exp(log_g[:, None] * to_state)[:, :, None]
    zeta = jnp.exp(log_g[:, None] * to_end)[:, :, None]
    ones = jnp.ones((1, 1, HEAD_DIM), F32)
    gc = jnp.exp(log_g * c)[:, None, None] * jnp.ones((1, HEAD_DIM, HEAD_DIM), F32)
    return decay, xi * ones, zeta * ones, gc


def _attn_kernel(*refs, n_maps, readout_scale, n_q, hq):
    refs = list(refs)
    lam_ref = refs.pop(0) if n_maps == 2 else None
    q_parts = [refs.pop(0) for _ in range(n_q)]
    kt_ref, v_ref = refs.pop(0), refs.pop(0)
    sub_ref = refs.pop(0) if n_maps == 2 else None
    o_ref, q_scr = refs.pop(0), refs.pop(0)
    first_ref = refs.pop(0) if n_maps == 2 else None
    hm = pl.program_id(2)
    for hh in range(hq):
        @pl.when(hm == hh)
        def _(hh=hh):
            pieces = [qp[0][:, hh * (qp.shape[2] // hq):(hh + 1) * (qp.shape[2] // hq)] for qp in q_parts]
            q_scr[...] = pieces[0] if n_q == 1 else jnp.concatenate(pieces, axis=1)
    q = q_scr[...]
    tq = q.shape[0]
    n_keys = kt_ref.shape[3]
    dv = v_ref.shape[3] // 2
    tk = min(KV_TILE, n_keys)
    n_full, tail = divmod(n_keys, tk)

    def step(off, size, carry):
        m, acc = carry
        s = _dot(q, kt_ref[0, 0, :, pl.ds(off, size)])
        m_new = jnp.maximum(m, jnp.max(s, axis=-1, keepdims=True))
        p = jnp.exp2(s - m_new)
        acc = jnp.exp2(m - m_new) * acc + _dot(p.astype(BF16), v_ref[0, 0, pl.ds(off, size), :])
        return m_new, acc

    carry = (jnp.full((tq, 1), -1e30, F32), jnp.zeros((tq, 2 * dv), F32))
    for j in range(n_full):
        carry = step(j * tk, tk, carry)
    if tail:
        carry = step(n_full * tk, tail, carry)
    acc = carry[1]
    o = acc[:, :dv] / acc[:, dv:]

    def store(head_of_step, val):
        for hh in range(N_HEADS):
            @pl.when(head_of_step == hh)
            def _(hh=hh):
                o_ref[0, :, hh * dv:(hh + 1) * dv] = val.astype(o_ref.dtype)

    if n_maps == 1:
        store(hm, o)
    else:
        @pl.when(hm % 2 == 0)
        def _():
            first_ref[...] = o

        @pl.when(hm % 2 == 1)
        def _():
            od = first_ref[...] - lam_ref[0] * o
            od = od * lax.rsqrt(jnp.mean(od * od, axis=-1, keepdims=True) + EPS) * sub_ref[...]
            store(hm // 2, od * readout_scale)


def _attn_call(q_parts, hq, kt, v, rows, keys, tq, diff=None):
    b, hk, d, n = kt.shape
    hv, dv = v.shape[1], v.shape[3] // 2
    (q0, qn), (k0, kn) = rows, keys
    assert q0 % tq == 0 and qn % tq == 0 and k0 % kn == 0
    assert sum(qp.shape[2] // hq for qp in q_parts) == d
    n_maps = 2 if diff is not None else 1
    in_specs = [pl.BlockSpec((1, tq, qp.shape[2]), lambda i, t, h: (i, q0 // tq + t, 0)) for qp in q_parts] + [
        pl.BlockSpec((1, 1, d, kn), lambda i, t, h: (i, h // (hq // hk), 0, k0 // kn)),
        pl.BlockSpec((1, 1, kn, 2 * dv), lambda i, t, h: (i, h // (hq // hv), k0 // kn, 0)),
    ]
    args = [*q_parts, kt, v]
    scratch = [pltpu.VMEM((tq, d), BF16)]
    scale = 1.0
    if diff is not None:
        lam, subln, scale = diff
        in_specs = [pl.BlockSpec(memory_space=pltpu.SMEM)] + in_specs + [pl.BlockSpec((1, dv), lambda i, t, h: (0, 0))]
        args = [lam.reshape(1)] + args + [subln.reshape(1, dv)]
        scratch.append(pltpu.VMEM((tq, dv), F32))
    return pl.pallas_call(
        functools.partial(_attn_kernel, n_maps=n_maps, readout_scale=scale, n_q=len(q_parts), hq=hq),
        grid=(b, qn // tq, hq),
        in_specs=in_specs,
        out_specs=pl.BlockSpec((1, tq, (hq // n_maps) * dv), lambda i, t, h: (i, t, 0)),
        out_shape=jax.ShapeDtypeStruct((b, qn, (hq // n_maps) * dv), BF16),
        scratch_shapes=scratch,
        compiler_params=_params(("parallel", "parallel", "arbitrary")),
    )(*args)


def _out_proj_kernel(ret_ref, *refs, n_lat_tiles):
    (att_refs, (w_ref, x_ref, mod_ref, g_ref, wr_ref, br_ref, xo_ref, hs_ref, gate_ref, pos_ref, cnt_ref)) = (
        refs[:6], refs[6:])
    mod = mod_ref[0, 0]
    g = g_ref[...]
    is_lat = pl.program_id(1) < n_lat_tiles
    y = _dot(ret_ref[0], w_ref[:GROUP_WIDTH])
    for m in range(3):
        ym = jnp.where(is_lat, att_refs[2 * m][0], att_refs[2 * m + 1][0])
        y = y + _dot(ym, w_ref[(m + 1) * GROUP_WIDTH:(m + 2) * GROUP_WIDTH])
    xn = x_ref[0] + mod[2:3] * (_rms(y) * g[1:2])
    xo_ref[0] = xn
    h = _rms(xn) * g[2:3] * (1.0 + mod[4:5]) + mod[3:4]
    w_hi, w_lo = _split_bf16(wr_ref[...])
    h_hi, h_lo = _split_bf16(h)
    logits = _dot_nt(w_hi, h_hi) + _dot_nt(w_hi, h_lo) + _dot_nt(w_lo, h_hi) + br_ref[...]
    row = lax.broadcasted_iota(jnp.int32, logits.shape, 0)
    vals = logits
    tops, idxs = [], []
    for _ in range(TOP_K):
        m = jnp.max(vals, axis=0, keepdims=True)
        idx = jnp.min(jnp.where(vals == m, row, N_EXPERTS), axis=0, keepdims=True)
        tops.append(m)
        idxs.append(idx)
        vals = jnp.where(row == idx, -jnp.inf, vals)
    ex = [jnp.exp(tv - tops[0]) for tv in tops]
    den = ex[0] + ex[1] + ex[2] + ex[3]
    gate_ref[0] = jnp.concatenate([e / den for e in ex], axis=0)
    tm = logits.shape[1]
    before = jnp.where(lax.broadcasted_iota(jnp.int32, (tm, tm), 0) < lax.broadcasted_iota(jnp.int32, (tm, tm), 1),
                       1.0, 0.0).astype(BF16)
    counts = jnp.zeros((N_EXPERTS, 1), F32)
    onehots, ranks = [], []
    for idx in idxs:
        onehot = jnp.where(row == idx, 1.0, 0.0)
        ranks.append(_dot(onehot.astype(BF16), before) + counts)
        onehots.append(onehot)
        counts = counts + jnp.sum(onehot, axis=1, keepdims=True)
    counts_b = jnp.broadcast_to(counts, (N_EXPERTS, LANES))
    earlier = jnp.where(lax.broadcasted_iota(jnp.int32, (N_EXPERTS, N_EXPERTS), 0)
                        > lax.broadcasted_iota(jnp.int32, (N_EXPERTS, N_EXPERTS), 1), 1.0, 0.0).astype(BF16)
    first = _dot(earlier, counts_b.astype(BF16))[:, 0:1]
    pos = [jnp.sum(oh * (rk + first), axis=0, keepdims=True).astype(jnp.int32) for oh, rk in zip(onehots, ranks)]
    pos_ref[0] = jnp.concatenate(pos, axis=0)
    cnt_ref[0, 0] = counts_b.astype(jnp.int32)
    slot = lax.broadcasted_iota(jnp.int32, (TOP_K * tm, tm), 0)
    pick = jnp.where(slot == pos[0], 1.0, 0.0)
    for p in pos[1:]:
        pick = jnp.where(slot == p, 1.0, pick)
    hs_ref[0, 0] = _pack_halves(_dot(pick.astype(BF16), h.astype(BF16)))


def _out_proj_call(ret_y, att_ys, w_out, xc, modl, g, wr_t, br, n_lat, n_rows):
    b, n, d = xc.shape
    tm = TOKEN_TILE
    nlt = n_lat // tm
    tok = lambda w: pl.BlockSpec((1, tm, w), lambda i, t: (i, t, 0))
    lat = pl.BlockSpec((1, tm, GROUP_WIDTH), lambda i, t: (i, jnp.minimum(t, nlt - 1), 0))
    ctx = pl.BlockSpec((1, tm, GROUP_WIDTH), lambda i, t: (i, jnp.maximum(t - nlt, 0), 0))
    const2 = lambda a: pl.BlockSpec(a.shape, lambda i, t: (0, 0))
    sel = pl.BlockSpec((1, TOP_K, tm), lambda i, t: (i, 0, t))
    return pl.pallas_call(
        functools.partial(_out_proj_kernel, n_lat_tiles=nlt),
        grid=(b, n_rows // tm),
        in_specs=[
            tok(GROUP_WIDTH), lat, ctx, lat, ctx, lat, ctx, const2(w_out), tok(d), _mod_spec(d, nlt),
            const2(g), const2(wr_t), const2(br),
        ],
        out_specs=[tok(d),
                   pl.BlockSpec((1, 1, TOP_K * tm, d // 2), lambda i, t: (i, t, 0, 0)),
                   sel, sel,
                   pl.BlockSpec((1, 1, N_EXPERTS, LANES), lambda i, t: (i, t, 0, 0))],
        out_shape=[
            jax.ShapeDtypeStruct((b, n_rows, d), F32),
            jax.ShapeDtypeStruct((b, n_rows // tm, TOP_K * tm, d // 2), jnp.uint32),
            jax.ShapeDtypeStruct((b, TOP_K, n_rows), F32),
            jax.ShapeDtypeStruct((b, TOP_K, n_rows), jnp.int32),
            jax.ShapeDtypeStruct((b, n_rows // tm, N_EXPERTS, LANES), jnp.int32),
        ],
        compiler_params=_params(("parallel", "parallel")),
    )(ret_y, *[a for pair in att_ys for a in pair], w_out, xc, modl, g, wr_t, br)


def _w1_prep_kernel(w_ref, g_ref, l_ref):
    r = lax.broadcasted_iota(jnp.int32, (MXU_DIM, MXU_DIM), 0)
    c = lax.broadcasted_iota(jnp.int32, (MXU_DIM, MXU_DIM), 1)
    src = jnp.where(c < LANES, 2 * c, 2 * (c - LANES) + 1)
    perm = jnp.where(r == src, 1.0, 0.0).astype(BF16)
    for j in range(w_ref.shape[3] // MXU_DIM):
        out = _dot(w_ref[0, 0, :, j * MXU_DIM:(j + 1) * MXU_DIM].astype(BF16), perm)
        g_ref[0, :, j * LANES:(j + 1) * LANES] = out[:, :LANES].astype(BF16)
        l_ref[0, :, j * LANES:(j + 1) * LANES] = out[:, LANES:].astype(BF16)


def _w1_prep_call(w1_all, layer):
    _, e, d, f2 = w1_all.shape
    f = f2 // 2
    rows = 256
    out_spec = pl.BlockSpec((1, rows, f), lambda i, j: (i, j, 0))
    return pl.pallas_call(
        _w1_prep_kernel,
        grid=(e, d // rows),
        in_specs=[pl.BlockSpec((1, 1, rows, f2), lambda i, j: (layer, i, j, 0))],
        out_specs=[out_spec, out_spec],
        out_shape=[jax.ShapeDtypeStruct((e, d, f), BF16)] * 2,
        compiler_params=_params(("parallel", "parallel")),
    )(w1_all)


def _expert_kernel(be_ref, nu_ref, sw_ref, nx_ref, sl_ref, x_ref, b1g_ref, b1l_ref, b2_ref,
                   w1g_hbm, w1l_hbm, w2_hbm, o_ref, wg_ref, wl_ref, w2f_ref, w2b_ref, sem_ref, *, layer):
    i = pl.program_id(0)

    def weight_copies(expert, slot):
        return (pltpu.make_async_copy(w1g_hbm.at[expert], wg_ref.at[slot], sem_ref.at[slot, 0]),
                pltpu.make_async_copy(w1l_hbm.at[expert], wl_ref.at[slot], sem_ref.at[slot, 1]),
                pltpu.make_async_copy(w2_hbm.at[layer, expert], w2f_ref.at[slot], sem_ref.at[slot, 2]))

    @pl.when(jnp.logical_and(i == 0, nu_ref[0] > 0))
    def _():
        for cp in weight_copies(be_ref[0], 0):
            cp.start()

    @pl.when(sw_ref[i] == 1)
    def _():
        slot = sl_ref[i]
        for cp in weight_copies(be_ref[i], slot):
            cp.wait()

        @pl.when(nx_ref[i] >= 0)
        def _():
            for cp in weight_copies(nx_ref[i], 1 - slot):
                cp.start()

        w2b_ref[...] = w2f_ref[slot].astype(BF16)

    @pl.when(i < nu_ref[0])
    def _():
        slot = sl_ref[i]
        x = _unpack_halves(x_ref[...]).astype(BF16)
        glu = jnp.minimum(_dot(x, wg_ref[slot]) + b1g_ref[0], SWIGLU_LIMIT)
        lin = jnp.clip(_dot(x, wl_ref[slot]) + b1l_ref[0], -SWIGLU_LIMIT, SWIGLU_LIMIT)
        act = glu * _sigmoid(SWIGLU_ALPHA * glu) * (lin + 1.0)
        o_ref[...] = _pack_halves(_dot(act.astype(BF16), w2b_ref[...]) + b2_ref[0])

    @pl.when(i >= nu_ref[0])
    def _():
        o_ref[...] = jnp.zeros_like(o_ref)


def _expert_call(block_expert, n_used, x_rows, w1g, w1l, b1g, b1l, w2_all, layer, b2):
    r = x_rows.shape[0]
    d, f = w1g.shape[1:]
    tm = MOE_BLOCK
    n_blocks = r // tm
    blk = jnp.arange(n_blocks, dtype=jnp.int32)
    prev = jnp.concatenate([block_expert[:1], block_expert[:-1]])
    switch = (blk < n_used[0]) & ((blk == 0) | (block_expert != prev))
    slot = (jnp.cumsum(switch.astype(jnp.int32)) - 1) % 2
    later = jnp.where(switch, blk, n_blocks)
    next_start = jnp.concatenate([lax.cummin(later, reverse=True)[1:], jnp.full((1,), n_blocks, jnp.int32)])
    next_expert = jnp.where(next_start < n_blocks, block_expert[jnp.minimum(next_start, n_blocks - 1)], -1)
    scalars = (block_expert, n_used, switch.astype(jnp.int32), next_expert.astype(jnp.int32), slot.astype(jnp.int32))
    bias = lambda w: pl.BlockSpec((1, 1, w), lambda i, be, *_: (be[i], 0, 0))
    hbm = pl.BlockSpec(memory_space=pl.ANY)
    return pl.pallas_call(
        functools.partial(_expert_kernel, layer=layer),
        grid_spec=pltpu.PrefetchScalarGridSpec(
            num_scalar_prefetch=len(scalars),
            grid=(n_blocks,),
            in_specs=[pl.BlockSpec((tm, d // 2), lambda i, be, nu, *_: (jnp.minimum(i, jnp.maximum(nu[0] - 1, 0)), 0)),
                      bias(f), bias(f), bias(d), hbm, hbm, hbm],
            out_specs=pl.BlockSpec((tm, d // 2), lambda i, *_: (i, 0)),
            scratch_shapes=[pltpu.VMEM((2, d, f), BF16), pltpu.VMEM((2, d, f), BF16), pltpu.VMEM((2, f, d), F32),
                            pltpu.VMEM((f, d), BF16), pltpu.SemaphoreType.DMA((2, 3))],
        ),
        out_shape=jax.ShapeDtypeStruct((r, d // 2), jnp.uint32),
        compiler_params=_params(("arbitrary",)),
    )(*scalars, x_rows, b1g, b1l, b2, w1g, w1l, w2_all)


def _combine_kernel(*refs, n_parts):
    x_ref, y_refs = refs[0], refs[1:1 + n_parts]
    gate_ref, pos_ref, mod_ref, g_ref, o_ref = refs[1 + n_parts:]
    gates, pos = gate_ref[0], pos_ref[0]
    lane = lax.broadcasted_iota(jnp.int32, (gates.shape[0], y_refs[0].shape[2]), 1)
    mix = jnp.zeros(lane.shape, F32)
    for k in range(TOP_K):
        mix = jnp.where(lane == pos[:, k:k + 1], gates[:, k:k + 1], mix)
    mix_hi, mix_lo = _split_bf16(mix)
    mod = mod_ref[0, 0]
    part_batches = pl.num_programs(0) // n_parts
    for p, y_ref in enumerate(y_refs):
        @pl.when(pl.program_id(0) // part_batches == p)
        def _(y_ref=y_ref):
            rows = _unpack_halves(y_ref[0, 0]).astype(BF16)
            y = _dot(mix_hi, rows) + _dot(mix_lo, rows)
            o_ref[0] = x_ref[0] + mod[5:6] * (_rms(y) * g_ref[...][3:4])


def _combine_call(xc, picked_parts, gates, pos, modl, g, n_lat):
    b, n, d = xc.shape
    tm = TOKEN_TILE
    n_parts = len(picked_parts)
    bp = b // n_parts
    tok = pl.BlockSpec((1, tm, d), lambda i, t: (i, t, 0))

    def part_spec(p):
        def index(i, t):
            live = (i >= p * bp) & (i < (p + 1) * bp)
            return jnp.clip(i - p * bp, 0, bp - 1), jnp.where(live, t, 0), 0, 0
        return pl.BlockSpec((1, 1, TOP_K * tm, d // 2), index)

    return pl.pallas_call(
        functools.partial(_combine_kernel, n_parts=n_parts),
        grid=(b, n // tm),
        in_specs=[tok] + [part_spec(p) for p in range(n_parts)] + [
            pl.BlockSpec((1, tm, TOP_K), lambda i, t: (i, t, 0)),
            pl.BlockSpec((1, tm, TOP_K), lambda i, t: (i, t, 0)),
            _mod_spec(d, n_lat // tm),
            pl.BlockSpec(g.shape, lambda i, t: (0, 0)),
        ],
        out_specs=tok,
        out_shape=jax.ShapeDtypeStruct((b, n, d), F32),
        compiler_params=_params(("parallel", "parallel")),
    )(xc, *picked_parts, gates, pos, modl, g)


def _rope_tables(rows, n_ctx, rot_dim):
    row = jnp.repeat(jnp.arange(rows), GRID_W)
    col = jnp.tile(jnp.arange(GRID_W), rows)
    n_freq = rot_dim // 4
    inv = ROPE_THETA ** (-jnp.arange(n_freq, dtype=F32) / n_freq)
    ang = jnp.concatenate([row[:, None] * inv, col[:, None] * inv], axis=-1)
    cos, sin = jnp.cos(ang), jnp.sin(ang)
    reps = 256 // rot_dim
    cos_t = jnp.tile(jnp.concatenate([cos, cos], axis=-1), (1, reps))
    sin_t = jnp.tile(jnp.concatenate([-sin, sin], axis=-1), (1, reps))
    cos_t = jnp.concatenate([cos_t, jnp.ones((n_ctx, 256), F32)], axis=0)
    sin_t = jnp.concatenate([sin_t, jnp.zeros((n_ctx, 256), F32)], axis=0)
    return cos_t, sin_t


def _pack_w_in(w):
    parts, src = [], 0
    for _, width, padded in IN_PIECES:
        parts.append(w[:, src:src + width])
        if padded > width:
            parts.append(jnp.zeros((w.shape[0], padded - width), w.dtype))
        src += width
    return jnp.concatenate(parts, axis=1).astype(BF16)


def _pack_w_uq(w):
    wh = w.reshape(MLA_Q_RANK, N_HEADS, MLA_NOPE + MLA_ROPE)
    nope = wh[:, :, :MLA_NOPE].reshape(MLA_Q_RANK, N_HEADS * MLA_NOPE)
    rope = wh[:, :, MLA_NOPE:].reshape(MLA_Q_RANK, N_HEADS * MLA_ROPE)
    packed = jnp.concatenate([nope, rope], axis=1)
    return jnp.pad(packed, ((0, 256 - MLA_Q_RANK), (0, 0))).astype(BF16)


def _pack_w_ukv(w):
    wh = w.reshape(MLA_KV_RANK, N_HEADS, MLA_NOPE + HEAD_DIM)
    kn = wh[:, :, :MLA_NOPE].reshape(MLA_KV_RANK, N_HEADS * MLA_NOPE)
    vv = wh[:, :, MLA_NOPE:].reshape(MLA_KV_RANK, N_HEADS * HEAD_DIM)
    return jnp.concatenate([kn, vv], axis=1).astype(BF16)


def _attention(q_parts, hq, kt, v, n_lat, n_ctx, ctx_out, diff=None):
    n = n_lat + n_ctx
    y = _attn_call(q_parts, hq, kt, v, (0, n_lat), (0, n), min(Q_TILE, n_lat), diff)
    if ctx_out:
        y_c = _attn_call(q_parts, hq, kt, v, (n_lat, n_ctx), (n_lat, n_ctx), n_ctx, diff)
    else:
        y_c = jnp.zeros((y.shape[0], n_ctx, y.shape[2]), y.dtype)
    return y, y_c


def _steps(x, bounds, offsets):
    deltas = offsets - jnp.concatenate([jnp.zeros_like(offsets[..., :1]), offsets[..., :-1]], axis=-1)
    return x + jnp.sum(jnp.where(x[..., None] >= bounds[..., None, :], deltas[..., None, :], 0), axis=-1)


def _moe(x_new, hs, gates, pos, counts, modl, g, n_lat, w1_all, b1, w2_all, b2, layer):
    b, n, d = x_new.shape
    f = w2_all.shape[2]
    tile_rows = TOP_K * TOKEN_TILE
    n_parts = 2 if b % 2 == 0 else 1
    tiles = (b // n_parts) * (n // TOKEN_TILE)
    n_blocks = -(-(tiles * tile_rows) // MOE_BLOCK) + N_EXPERTS
    hs_rows = hs.reshape(b * (n // TOKEN_TILE) * tile_rows, d // 2)
    w1g, w1l = _w1_prep_call(w1_all, layer)
    b1g, b1l = b1[:, 0::2].reshape(N_EXPERTS, 1, f), b1[:, 1::2].reshape(N_EXPERTS, 1, f)
    picked_parts = []
    for p in range(n_parts):
        tile_counts = counts.reshape(-1, N_EXPERTS)[p * tiles:(p + 1) * tiles]
        tile_base = jnp.cumsum(tile_counts, axis=0) - tile_counts
        first = jnp.cumsum(tile_counts, axis=1) - tile_counts
        padded = (jnp.sum(tile_counts, axis=0) + MOE_BLOCK - 1) // MOE_BLOCK * MOE_BLOCK
        pad_ends = jnp.cumsum(padded)
        pad_start = pad_ends - padded
        block_start = jnp.arange(n_blocks, dtype=jnp.int32) * MOE_BLOCK
        block_expert = jnp.minimum(jnp.sum((pad_ends[None, :] <= block_start[:, None]).astype(jnp.int32), axis=1),
                                   N_EXPERTS - 1)
        n_used = (pad_ends[-1:] // MOE_BLOCK).astype(jnp.int32)
        j = block_start[:, None] + jnp.arange(MOE_BLOCK, dtype=jnp.int32)[None, :] - pad_start[block_expert][:, None]
        tile_offset = (jnp.arange(tiles, dtype=jnp.int32)[:, None] + p * tiles) * tile_rows + first - tile_base
        src = _steps(j, tile_base.T[block_expert], tile_offset.T[block_expert])
        src = jnp.clip(src, 0, hs_rows.shape[0] - 1)
        x_rows = hs_rows.at[src.reshape(-1)].get(mode="promise_in_bounds")
        y_rows = _expert_call(block_expert, n_used, x_rows, w1g, w1l, b1g, b1l, w2_all, layer,
                              b2.reshape(N_EXPERTS, 1, d))
        slot = jnp.broadcast_to(jnp.arange(tile_rows, dtype=jnp.int32)[None, :], (tiles, tile_rows))
        back = _steps(slot, first, tile_base + pad_start[None, :] - first)
        picked = y_rows.at[back.reshape(-1)].get(mode="promise_in_bounds")
        picked_parts.append(picked.reshape(b // n_parts, n // TOKEN_TILE, tile_rows, d // 2))
    return _combine_call(x_new, picked_parts, gates.transpose(0, 2, 1), pos.transpose(0, 2, 1), modl, g, n_lat)


def kernel(x, c, ctx, c_ctx, ada_w, ada_b, norm_g, w_in, w_out, ret_log_decay, ret_gn_w, ret_gn_b,
           diff_lambda, diff_subln, gqa_qk_norm, mla_q_norm, mla_kv_norm, mla_w_uq, mla_w_ukv,
           router_w, router_b, exp_w1, exp_b1, exp_w2, exp_b2):
    b, s, d = x.shape
    n_ctx = ctx.shape[1]
    n = s + n_ctx
    depth = ada_w.shape[0]
    assert n_ctx % TOKEN_TILE == 0 and s % TOKEN_TILE == 0 and s % GRID_W == 0
    assert n_ctx % RET_CHUNK == 0 and s % min(Q_TILE, s) == 0 and s % n_ctx == 0

    tables = _rope_tables(s // GRID_W, n_ctx, HEAD_DIM) + _rope_tables(s // GRID_W, n_ctx, DIFF_D)
    c_rows = jnp.zeros((16, d), F32).at[:b].set(c).at[b].set(c_ctx)
    mods = _ada_call(c_rows, ada_w, ada_b)
    xc = jnp.concatenate([x, ctx], axis=1)

    for l in range(depth):
        last = l == depth - 1
        lam_init = 0.8 - 0.6 * math.exp(-0.3 * l)
        mod_lat = mods[l, :b].reshape(b, 1, 6, d)
        mod_ctx = jnp.broadcast_to(mods[l, b].reshape(1, 1, 6, d), (b, 1, 6, d))
        modl = jnp.concatenate([mod_ctx, mod_lat], axis=1)

        (rq, rk, rv, rg, dq, dkt, dv, gq, gkt, gv, mqn, mqr, mkt, mv) = _in_proj_call(
            xc, modl, norm_g[l, 0:1], _pack_w_in(w_in[l]), tables,
            jnp.tile(gqa_qk_norm[l, 0], N_HEADS)[None, :], jnp.tile(gqa_qk_norm[l, 1], GQA_KV_HEADS)[None, :],
            jnp.pad(mla_q_norm[l], (0, 256 - MLA_Q_RANK))[None, :], mla_kv_norm[l][None, :],
            _pack_w_uq(mla_w_uq[l]), _pack_w_ukv(mla_w_ukv[l]), s)

        log_g = -jnp.exp(ret_log_decay[l].astype(F32))
        o_f = _ret_call(rq, rk, rv, _ret_tables(log_g[0], False), s, False)
        ret_y = _ret_call(rq, rk, rv, _ret_tables(log_g[1], True), s, True,
                          (o_f, rg, ret_gn_w[l][None, :], ret_gn_b[l][None, :]))

        lp = diff_lambda[l].astype(F32)
        lam = jnp.exp(jnp.sum(lp[0] * lp[1])) - jnp.exp(jnp.sum(lp[2] * lp[3])) + lam_init
        dif_y = _attention([dq], 2 * N_HEADS, dkt, dv, s, n_ctx, not last, (lam, diff_subln[l], 1.0 - lam_init))

        gqa_y = _attention([gq], N_HEADS, gkt, gv, s, n_ctx, not last)

        mla_y = _attention([mqn, mqr], N_HEADS, mkt, mv, s, n_ctx, not last)

        n_rows = s if last else n
        x_new, hs, gates, pos, counts = _out_proj_call(
            ret_y, (dif_y, gqa_y, mla_y), w_out[l].astype(BF16), xc, modl, norm_g[l], router_w[l].T,
            router_b[l][:, None], s, n_rows)
        xc = _moe(x_new, hs, gates, pos, counts[..., 0], modl, norm_g[l], s,
                  exp_w1, exp_b1[l], exp_w2, exp_b2[l], l)

    return xc[:, :s]
```

```python
import functools
import math

import jax
import jax.numpy as jnp
from jax import lax
from jax.experimental import pallas as pl
from jax.experimental.pallas import tpu as pltpu

F32 = jnp.float32
BF16 = jnp.bfloat16

GRID_W = 64
ROPE_THETA = 10000.0
EPS = 1e-6
GROUP_WIDTH = 256
HEAD_DIM = 64
N_HEADS = 4
RET_CHUNK = 256
DIFF_D = 32
GQA_KV_HEADS = 2
MLA_Q_RANK = 192
MLA_KV_RANK = 128
MLA_NOPE = 64
MLA_ROPE = 32
N_EXPERTS = 32
TOP_K = 4
SWIGLU_LIMIT = 7.0
SWIGLU_ALPHA = 1.702
MOE_BLOCK = 512

LANES = 128
MXU_DIM = 256
TOKEN_TILE = 256
Q_TILE = 1024
KV_TILE = 512
VMEM_LIMIT = 48 * 1024 * 1024
LOG2E = math.log2(math.e)

IN_PIECES = (
    ("ret_q", 256, 256), ("ret_k", 256, 256), ("ret_v", 256, 256), ("ret_g", 256, 256),
    ("dif_q", 256, 256), ("dif_k", 256, 256), ("dif_v", 256, 256),
    ("gqa_q", 256, 256), ("gqa_k", 128, 128), ("gqa_v", 128, 128),
    ("mla_cq", MLA_Q_RANK, 256), ("mla_ckv", MLA_KV_RANK, 128), ("mla_kr", MLA_ROPE, 128),
)
IN_OFFSETS = {}
_off = 0
for _name, _w, _pw in IN_PIECES:
    IN_OFFSETS[_name] = _off
    _off += _pw
IN_PACKED_WIDTH = _off


def _params(sem):
    return pltpu.CompilerParams(dimension_semantics=sem, vmem_limit_bytes=VMEM_LIMIT)


def _rms(x):
    return x * lax.rsqrt(jnp.mean(x * x, axis=-1, keepdims=True) + EPS)


def _split_bf16(a):
    hi = a.astype(BF16)
    lo = (a - hi.astype(F32)).astype(BF16)
    return hi, lo


def _dot_nt(a, b):
    return lax.dot_general(a, b, (((1,), (1,)), ((), ())), preferred_element_type=F32)


def _dot(a, b):
    return jnp.dot(a, b, preferred_element_type=F32)


def _sigmoid(a):
    return 1.0 / (1.0 + jnp.exp(-a))


def _pack_halves(a):
    w = a.shape[1] // 2
    bits = lax.bitcast_convert_type(a.astype(BF16).astype(F32), jnp.uint32)
    return bits[:, :w] | (bits[:, w:] >> 16)


def _unpack_halves(u):
    hi = lax.bitcast_convert_type(u & jnp.uint32(0xFFFF0000), F32)
    lo = lax.bitcast_convert_type(u << 16, F32)
    return jnp.concatenate([hi, lo], axis=1)


def _mod_spec(d, n_lat_tiles):
    return pl.BlockSpec((1, 1, 6, d), lambda i, t: (i, jnp.where(t < n_lat_tiles, 1, 0), 0, 0))


def _ada_kernel(c_ref, w_ref, b_ref, o_ref):
    s = c_ref[...]
    s = s * _sigmoid(s)
    s_hi, s_lo = _split_bf16(s)
    w_hi, w_lo = _split_bf16(w_ref[0])
    o_ref[0] = _dot(s_hi, w_hi) + _dot(s_hi, w_lo) + _dot(s_lo, w_hi) + b_ref[0]


def _ada_call(c_rows, ada_w, ada_b):
    depth, d, n6 = ada_w.shape
    rows = c_rows.shape[0]
    tn = 1536
    return pl.pallas_call(
        _ada_kernel,
        grid=(depth, n6 // tn),
        in_specs=[
            pl.BlockSpec((rows, d), lambda l, j: (0, 0)),
            pl.BlockSpec((1, d, tn), lambda l, j: (l, 0, j)),
            pl.BlockSpec((1, 1, tn), lambda l, j: (l, 0, j)),
        ],
        out_specs=pl.BlockSpec((1, rows, tn), lambda l, j: (l, 0, j)),
        out_shape=jax.ShapeDtypeStruct((depth, rows, n6), F32),
        compiler_params=_params(("parallel", "parallel")),
    )(c_rows, ada_w, ada_b.reshape(depth, 1, n6))


def _rope(x, cos, sin_signed, half):
    outs = []
    for c in range(x.shape[1] // LANES):
        sl = slice(c * LANES, (c + 1) * LANES)
        xc = x[:, sl]
        lane = lax.broadcasted_iota(jnp.int32, xc.shape, 1)
        first_half = (lane % (2 * half)) < half
        partner = jnp.where(first_half, pltpu.roll(xc, LANES - half, 1), pltpu.roll(xc, half, 1))
        outs.append(xc * cos[:, sl] + partner * sin_signed[:, sl])
    return outs[0] if len(outs) == 1 else jnp.concatenate(outs, axis=1)


def _group_mean(x, gsize):
    w = x.shape[1]
    r = lax.broadcasted_iota(jnp.int32, (w, w), 0) // gsize
    c = lax.broadcasted_iota(jnp.int32, (w, w), 1) // gsize
    ones = jnp.where(r == c, 1.0, 0.0).astype(BF16)
    hi, lo = _split_bf16(x)
    return (_dot(hi, ones) + _dot(lo, ones)) * (1.0 / gsize)


def _group_mean_sq(x, gsize):
    return _group_mean(x * x, gsize)


def _in_proj_kernel(x_ref, mod_ref, g_ref, w_ref, cos64_ref, sin64_ref, cos32_ref, sin32_ref,
                    gqn_ref, gkn_ref, mqn_ref, mkvn_ref, wuq_ref, wukv_ref,
                    rq_ref, rk_ref, rv_ref, rg_ref, dq_ref, dkt_ref, dv_ref, gq_ref, gkt_ref, gv_ref,
                    mqn_o, mqr_o, mkt_o, mv_o):
    mod = mod_ref[0, 0]
    h = _rms(x_ref[0]) * g_ref[...] * (1.0 + mod[1:2]) + mod[0:1]
    hb = h.astype(BF16)
    cos64, sin64 = cos64_ref[...], sin64_ref[...]
    cos32, sin32 = cos32_ref[...], sin32_ref[...]

    def proj(name, width):
        o = IN_OFFSETS[name]
        return _dot(hb, w_ref[:, o:o + width])

    def store_keys(o_ref, k, extra=None):
        kt = k.T
        d = kt.shape[0] // o_ref.shape[1]
        for hd in range(o_ref.shape[1]):
            rows = kt[hd * d:(hd + 1) * d]
            o_ref[0, hd] = (rows if extra is None else jnp.concatenate([rows, extra], axis=0)).astype(o_ref.dtype)

    def store_values(o_ref, v):
        dv = v.shape[1] // o_ref.shape[1]
        ones = jnp.ones((v.shape[0], dv), F32)
        for hd in range(o_ref.shape[1]):
            o_ref[0, hd] = jnp.concatenate([v[:, hd * dv:(hd + 1) * dv], ones], axis=1).astype(o_ref.dtype)

    rq_ref[0] = _rope(proj("ret_q", 256), cos64, sin64, 32).astype(BF16)
    rk_ref[0] = (_rope(proj("ret_k", 256), cos64, sin64, 32) * (HEAD_DIM ** -0.5)).astype(BF16)
    rv_ref[0] = proj("ret_v", 256).astype(BF16)
    rg_ref[0] = proj("ret_g", 256)
    dq_ref[0] = (_rope(proj("dif_q", 256), cos32, sin32, 16) * (DIFF_D ** -0.5 * LOG2E)).astype(BF16)
    store_keys(dkt_ref, _rope(proj("dif_k", 256), cos32, sin32, 16))
    store_values(dv_ref, proj("dif_v", 256))
    gq = proj("gqa_q", 256)
    gq = gq * lax.rsqrt(_group_mean_sq(gq, HEAD_DIM) + EPS) * gqn_ref[...]
    gq_ref[0] = (_rope(gq, cos64, sin64, 32) * (HEAD_DIM ** -0.5 * LOG2E)).astype(BF16)
    gk = proj("gqa_k", 128)
    gk = gk * lax.rsqrt(_group_mean_sq(gk, HEAD_DIM) + EPS) * gkn_ref[...]
    store_keys(gkt_ref, _rope(gk, cos64[:, :LANES], sin64[:, :LANES], 32))
    store_values(gv_ref, proj("gqa_v", 128))
    cq = proj("mla_cq", 256)
    cq = cq * lax.rsqrt(jnp.sum(cq * cq, axis=-1, keepdims=True) * (1.0 / MLA_Q_RANK) + EPS) * mqn_ref[...]
    q_up = _dot(cq.astype(BF16), wuq_ref[...])
    mla_scale = (MLA_NOPE + MLA_ROPE) ** -0.5 * LOG2E
    mqn_o[0] = (q_up[:, :256] * mla_scale).astype(BF16)
    mqr_o[0] = (_rope(q_up[:, 256:], cos32[:, :LANES], sin32[:, :LANES], 16) * mla_scale).astype(BF16)
    ckv = proj("mla_ckv", 128)
    ckv = ckv * lax.rsqrt(jnp.mean(ckv * ckv, axis=-1, keepdims=True) + EPS) * mkvn_ref[...]
    kv_up = _dot(ckv.astype(BF16), wukv_ref[...])
    kr = _rope(proj("mla_kr", 128), cos32[:, :LANES], sin32[:, :LANES], 16)
    store_keys(mkt_o, kv_up[:, :256], extra=kr.T[:MLA_ROPE])
    store_values(mv_o, kv_up[:, 256:])


def _in_proj_call(xc, modl, g0, w_in_p, tables, gqn, gkn, mqn, mkvn, wuq_p, wukv_p, n_lat):
    b, n, d = xc.shape
    tm = TOKEN_TILE
    tok = lambda w: pl.BlockSpec((1, tm, w), lambda i, t: (i, t, 0))
    const2 = lambda a: pl.BlockSpec(a.shape, lambda i, t: (0, 0))
    tab = pl.BlockSpec((tm, 256), lambda i, t: (t, 0))
    keys_t = lambda h, dk: pl.BlockSpec((1, h, dk, tm), lambda i, t: (i, 0, 0, t))
    vals = lambda h: pl.BlockSpec((1, h, tm, 2 * HEAD_DIM), lambda i, t: (i, 0, t, 0))
    sds = jax.ShapeDtypeStruct
    outs = [
        (tok(256), sds((b, n, 256), BF16)), (tok(256), sds((b, n, 256), BF16)),
        (tok(256), sds((b, n, 256), BF16)), (tok(256), sds((b, n, 256), F32)),
        (tok(256), sds((b, n, 256), BF16)),
        (keys_t(2 * N_HEADS, DIFF_D), sds((b, 2 * N_HEADS, DIFF_D, n), BF16)),
        (vals(N_HEADS), sds((b, N_HEADS, n, 2 * HEAD_DIM), BF16)),
        (tok(256), sds((b, n, 256), BF16)),
        (keys_t(GQA_KV_HEADS, HEAD_DIM), sds((b, GQA_KV_HEADS, HEAD_DIM, n), BF16)),
        (vals(GQA_KV_HEADS), sds((b, GQA_KV_HEADS, n, 2 * HEAD_DIM), BF16)),
        (tok(256), sds((b, n, 256), BF16)), (tok(128), sds((b, n, 128), BF16)),
        (keys_t(N_HEADS, MLA_NOPE + MLA_ROPE), sds((b, N_HEADS, MLA_NOPE + MLA_ROPE, n), BF16)),
        (vals(N_HEADS), sds((b, N_HEADS, n, 2 * HEAD_DIM), BF16)),
    ]
    return pl.pallas_call(
        _in_proj_kernel,
        grid=(b, n // tm),
        in_specs=[
            tok(d), _mod_spec(d, n_lat // tm),
            const2(g0), const2(w_in_p), tab, tab, tab, tab,
            const2(gqn), const2(gkn), const2(mqn), const2(mkvn), const2(wuq_p), const2(wukv_p),
        ],
        out_specs=[o[0] for o in outs],
        out_shape=[o[1] for o in outs],
        compiler_params=_params(("parallel", "parallel")),
    )(xc, modl, g0, w_in_p, *tables, gqn, gkn, mqn, mkvn, wuq_p, wukv_p)


def _ret_kernel(*refs, readout):
    if readout:
        (q_ref, k_ref, v_ref, dec_ref, xi_ref, zeta_ref, gc_ref, of_ref, g_ref, gnw_ref, gnb_ref,
         o_ref, state_ref) = refs
    else:
        q_ref, k_ref, v_ref, dec_ref, xi_ref, zeta_ref, gc_ref, o_ref, state_ref = refs

    @pl.when(pl.program_id(1) == 0)
    def _():
        state_ref[...] = jnp.zeros_like(state_ref)

    q, k, v = q_ref[0], k_ref[0], v_ref[0]
    outs = []
    for h in range(N_HEADS):
        sl = slice(h * HEAD_DIM, (h + 1) * HEAD_DIM)
        qh, kh, vh = q[:, sl], k[:, sl], v[:, sl]
        st = state_ref[h]
        inner = _dot_nt(qh, kh) * dec_ref[h]
        o = _dot(inner.astype(BF16), vh) + _dot(qh, st.astype(BF16)) * xi_ref[h]
        kz = (kh.astype(F32) * zeta_ref[h]).astype(BF16)
        kv = lax.dot_general(kz, vh, (((0,), (0,)), ((), ())), preferred_element_type=F32)
        state_ref[h] = gc_ref[h] * st + kv
        outs.append(o)
    o = jnp.concatenate(outs, axis=1)
    if readout:
        o = o + of_ref[0]
        o = o - _group_mean(o, HEAD_DIM)
        o = o * lax.rsqrt(_group_mean_sq(o, HEAD_DIM) + EPS)
        g = g_ref[0]
        o_ref[0] = ((o * gnw_ref[...] + gnb_ref[...]) * (g * _sigmoid(g))).astype(o_ref.dtype)
    else:
        o_ref[0] = o


def _ret_call(q, k, v, tabs, n_lat, backward, readout_args=None):
    b, n, w = q.shape
    c = RET_CHUNK
    nch = n // c
    nlc = n_lat // c
    if backward:
        chunk = lambda t: nch - 1 - t
    else:
        chunk = lambda t: jnp.where(t < nch - nlc, nlc + t, t - (nch - nlc))
    tok = lambda dt_w: pl.BlockSpec((1, c, dt_w), lambda i, t: (i, chunk(t), 0))
    const = lambda a: pl.BlockSpec(a.shape, lambda i, t: (0,) * a.ndim)
    in_specs = [tok(w), tok(w), tok(w)] + [const(a) for a in tabs]
    args = [q, k, v, *tabs]
    if readout_args is not None:
        o_f, g, gnw, gnb = readout_args
        in_specs += [tok(w), tok(w), const(gnw), const(gnb)]
        args += [o_f, g, gnw, gnb]
    out_dtype = BF16 if readout_args is not None else F32
    return pl.pallas_call(
        functools.partial(_ret_kernel, readout=readout_args is not None),
        grid=(b, nch),
        in_specs=in_specs,
        out_specs=tok(w),
        out_shape=jax.ShapeDtypeStruct((b, n, w), out_dtype),
        scratch_shapes=[pltpu.VMEM((N_HEADS, HEAD_DIM, HEAD_DIM), F32)],
        compiler_params=_params(("parallel", "arbitrary")),
    )(*args)


def _ret_tables(log_g, backward):
    c = RET_CHUNK
    pos = jnp.arange(c, dtype=F32)
    dist = (pos[None, :] - pos[:, None]) if backward else (pos[:, None] - pos[None, :])
    lg = log_g[:, None, None]
    decay = jnp.where(dist >= 0, jnp.exp(lg * jnp.maximum(dist, 0.0)), 0.0)
    to_state = (c - pos) if backward else (pos + 1.0)
    to_end = pos if backward else (c - 1.0 - pos)
    xi = jnp.exp(log_g[:, None] * to_state)[:, :, None]
    zeta = jnp.exp(log_g[:, None] * to_end)[:, :, None]
    ones = jnp.ones((1, 1, HEAD_DIM), F32)
    gc = jnp.exp(log_g * c)[:, None, None] * jnp.ones((1, HEAD_DIM, HEAD_DIM), F32)
    return decay, xi * ones, zeta * ones, gc


def _attn_kernel(*refs, n_maps, readout_scale, n_q, hq):
    refs = list(refs)
    lam_ref = refs.pop(0) if n_maps == 2 else None
    q_parts = [refs.pop(0) for _ in range(n_q)]
    kt_ref, v_ref = refs.pop(0), refs.pop(0)
    sub_ref = refs.pop(0) if n_maps == 2 else None
    o_ref, q_scr = refs.pop(0), refs.pop(0)
    first_ref = refs.pop(0) if n_maps == 2 else None
    hm = pl.program_id(2)
    for hh in range(hq):
        @pl.when(hm == hh)
        def _(hh=hh):
            pieces = [qp[0][:, hh * (qp.shape[2] // hq):(hh + 1) * (qp.shape[2] // hq)] for qp in q_parts]
            q_scr[...] = pieces[0] if n_q == 1 else jnp.concatenate(pieces, axis=1)
    q = q_scr[...]
    tq = q.shape[0]
    n_keys = kt_ref.shape[3]
    dv = v_ref.shape[3] // 2
    tk = min(KV_TILE, n_keys)
    n_full, tail = divmod(n_keys, tk)

    def step(off, size, carry):
        m, acc = carry
        s = _dot(q, kt_ref[0, 0, :, pl.ds(off, size)])
        m_new = jnp.maximum(m, jnp.max(s, axis=-1, keepdims=True))
        p = jnp.exp2(s - m_new)
        acc = jnp.exp2(m - m_new) * acc + _dot(p.astype(BF16), v_ref[0, 0, pl.ds(off, size), :])
        return m_new, acc

    carry = (jnp.full((tq, 1), -1e30, F32), jnp.zeros((tq, 2 * dv), F32))
    for j in range(n_full):
        carry = step(j * tk, tk, carry)
    if tail:
        carry = step(n_full * tk, tail, carry)
    acc = carry[1]
    o = acc[:, :dv] / acc[:, dv:]

    def store(head_of_step, val):
        for hh in range(N_HEADS):
            @pl.when(head_of_step == hh)
            def _(hh=hh):
                o_ref[0, :, hh * dv:(hh + 1) * dv] = val.astype(o_ref.dtype)

    if n_maps == 1:
        store(hm, o)
    else:
        @pl.when(hm % 2 == 0)
        def _():
            first_ref[...] = o

        @pl.when(hm % 2 == 1)
        def _():
            od = first_ref[...] - lam_ref[0] * o
            od = od * lax.rsqrt(jnp.mean(od * od, axis=-1, keepdims=True) + EPS) * sub_ref[...]
            store(hm // 2, od * readout_scale)


def _attn_call(q_parts, hq, kt, v, rows, keys, tq, diff=None):
    b, hk, d, n = kt.shape
    hv, dv = v.shape[1], v.shape[3] // 2
    (q0, qn), (k0, kn) = rows, keys
    assert q0 % tq == 0 and qn % tq == 0 and k0 % kn == 0
    assert sum(qp.shape[2] // hq for qp in q_parts) == d
    n_maps = 2 if diff is not None else 1
    in_specs = [pl.BlockSpec((1, tq, qp.shape[2]), lambda i, t, h: (i, q0 // tq + t, 0)) for qp in q_parts] + [
        pl.BlockSpec((1, 1, d, kn), lambda i, t, h: (i, h // (hq // hk), 0, k0 // kn)),
        pl.BlockSpec((1, 1, kn, 2 * dv), lambda i, t, h: (i, h // (hq // hv), k0 // kn, 0)),
    ]
    args = [*q_parts, kt, v]
    scratch = [pltpu.VMEM((tq, d), BF16)]
    scale = 1.0
    if diff is not None:
        lam, subln, scale = diff
        in_specs = [pl.BlockSpec(memory_space=pltpu.SMEM)] + in_specs + [pl.BlockSpec((1, dv), lambda i, t, h: (0, 0))]
        args = [lam.reshape(1)] + args + [subln.reshape(1, dv)]
        scratch.append(pltpu.VMEM((tq, dv), F32))
    return pl.pallas_call(
        functools.partial(_attn_kernel, n_maps=n_maps, readout_scale=scale, n_q=len(q_parts), hq=hq),
        grid=(b, qn // tq, hq),
        in_specs=in_specs,
        out_specs=pl.BlockSpec((1, tq, (hq // n_maps) * dv), lambda i, t, h: (i, t, 0)),
        out_shape=jax.ShapeDtypeStruct((b, qn, (hq // n_maps) * dv), BF16),
        scratch_shapes=scratch,
        compiler_params=_params(("parallel", "parallel", "arbitrary")),
    )(*args)


def _out_proj_kernel(ret_ref, *refs, n_lat_tiles):
    (att_refs, (w_ref, x_ref, mod_ref, g_ref, wr_ref, br_ref, xo_ref, hs_ref, gate_ref, pos_ref, cnt_ref)) = (
        refs[:6], refs[6:])
    mod = mod_ref[0, 0]
    g = g_ref[...]
    is_lat = pl.program_id(1) < n_lat_tiles
    y = _dot(ret_ref[0], w_ref[:GROUP_WIDTH])
    for m in range(3):
        ym = jnp.where(is_lat, att_refs[2 * m][0], att_refs[2 * m + 1][0])
        y = y + _dot(ym, w_ref[(m + 1) * GROUP_WIDTH:(m + 2) * GROUP_WIDTH])
    xn = x_ref[0] + mod[2:3] * (_rms(y) * g[1:2])
    xo_ref[0] = xn
    h = _rms(xn) * g[2:3] * (1.0 + mod[4:5]) + mod[3:4]
    w_hi, w_lo = _split_bf16(wr_ref[...])
    h_hi, h_lo = _split_bf16(h)
    logits = _dot_nt(w_hi, h_hi) + _dot_nt(w_hi, h_lo) + _dot_nt(w_lo, h_hi) + br_ref[...]
    row = lax.broadcasted_iota(jnp.int32, logits.shape, 0)
    vals = logits
    tops, idxs = [], []
    for _ in range(TOP_K):
        m = jnp.max(vals, axis=0, keepdims=True)
        idx = jnp.min(jnp.where(vals == m, row, N_EXPERTS), axis=0, keepdims=True)
        tops.append(m)
        idxs.append(idx)
        vals = jnp.where(row == idx, -jnp.inf, vals)
    ex = [jnp.exp(tv - tops[0]) for tv in tops]
    den = ex[0] + ex[1] + ex[2] + ex[3]
    gate_ref[0] = jnp.concatenate([e / den for e in ex], axis=0)
    tm = logits.shape[1]
    before = jnp.where(lax.broadcasted_iota(jnp.int32, (tm, tm), 0) < lax.broadcasted_iota(jnp.int32, (tm, tm), 1),
                       1.0, 0.0).astype(BF16)
    counts = jnp.zeros((N_EXPERTS, 1), F32)
    onehots, ranks = [], []
    for idx in idxs:
        onehot = jnp.where(row == idx, 1.0, 0.0)
        ranks.append(_dot(onehot.astype(BF16), before) + counts)
        onehots.append(onehot)
        counts = counts + jnp.sum(onehot, axis=1, keepdims=True)
    counts_b = jnp.broadcast_to(counts, (N_EXPERTS, LANES))
    earlier = jnp.where(lax.broadcasted_iota(jnp.int32, (N_EXPERTS, N_EXPERTS), 0)
                        > lax.broadcasted_iota(jnp.int32, (N_EXPERTS, N_EXPERTS), 1), 1.0, 0.0).astype(BF16)
    first = _dot(earlier, counts_b.astype(BF16))[:, 0:1]
    pos = [jnp.sum(oh * (rk + first), axis=0, keepdims=True).astype(jnp.int32) for oh, rk in zip(onehots, ranks)]
    pos_ref[0] = jnp.concatenate(pos, axis=0)
    cnt_ref[0, 0] = counts_b.astype(jnp.int32)
    slot = lax.broadcasted_iota(jnp.int32, (TOP_K * tm, tm), 0)
    pick = jnp.where(slot == pos[0], 1.0, 0.0)
    for p in pos[1:]:
        pick = jnp.where(slot == p, 1.0, pick)
    hs_ref[0, 0] = _pack_halves(_dot(pick.astype(BF16), h.astype(BF16)))


def _out_proj_call(ret_y, att_ys, w_out, xc, modl, g, wr_t, br, n_lat, n_rows):
    b, n, d = xc.shape
    tm = TOKEN_TILE
    nlt = n_lat // tm
    tok = lambda w: pl.BlockSpec((1, tm, w), lambda i, t: (i, t, 0))
    lat = pl.BlockSpec((1, tm, GROUP_WIDTH), lambda i, t: (i, jnp.minimum(t, nlt - 1), 0))
    ctx = pl.BlockSpec((1, tm, GROUP_WIDTH), lambda i, t: (i, jnp.maximum(t - nlt, 0), 0))
    const2 = lambda a: pl.BlockSpec(a.shape, lambda i, t: (0, 0))
    sel = pl.BlockSpec((1, TOP_K, tm), lambda i, t: (i, 0, t))
    return pl.pallas_call(
        functools.partial(_out_proj_kernel, n_lat_tiles=nlt),
        grid=(b, n_rows // tm),
        in_specs=[
            tok(GROUP_WIDTH), lat, ctx, lat, ctx, lat, ctx, const2(w_out), tok(d), _mod_spec(d, nlt),
            const2(g), const2(wr_t), const2(br),
        ],
        out_specs=[tok(d),
                   pl.BlockSpec((1, 1, TOP_K * tm, d // 2), lambda i, t: (i, t, 0, 0)),
                   sel, sel,
                   pl.BlockSpec((1, 1, N_EXPERTS, LANES), lambda i, t: (i, t, 0, 0))],
        out_shape=[
            jax.ShapeDtypeStruct((b, n_rows, d), F32),
            jax.ShapeDtypeStruct((b, n_rows // tm, TOP_K * tm, d // 2), jnp.uint32),
            jax.ShapeDtypeStruct((b, TOP_K, n_rows), F32),
            jax.ShapeDtypeStruct((b, TOP_K, n_rows), jnp.int32),
            jax.ShapeDtypeStruct((b, n_rows // tm, N_EXPERTS, LANES), jnp.int32),
        ],
        compiler_params=_params(("parallel", "parallel")),
    )(ret_y, *[a for pair in att_ys for a in pair], w_out, xc, modl, g, wr_t, br)


def _w1_prep_kernel(w_ref, g_ref, l_ref):
    r = lax.broadcasted_iota(jnp.int32, (MXU_DIM, MXU_DIM), 0)
    c = lax.broadcasted_iota(jnp.int32, (MXU_DIM, MXU_DIM), 1)
    src = jnp.where(c < LANES, 2 * c, 2 * (c - LANES) + 1)
    perm = jnp.where(r == src, 1.0, 0.0).astype(BF16)
    for j in range(w_ref.shape[3] // MXU_DIM):
        out = _dot(w_ref[0, 0, :, j * MXU_DIM:(j + 1) * MXU_DIM].astype(BF16), perm)
        g_ref[0, :, j * LANES:(j + 1) * LANES] = out[:, :LANES].astype(BF16)
        l_ref[0, :, j * LANES:(j + 1) * LANES] = out[:, LANES:].astype(BF16)


def _w1_prep_call(w1_all, layer):
    _, e, d, f2 = w1_all.shape
    f = f2 // 2
    rows = 256
    out_spec = pl.BlockSpec((1, rows, f), lambda i, j: (i, j, 0))
    return pl.pallas_call(
        _w1_prep_kernel,
        grid=(e, d // rows),
        in_specs=[pl.BlockSpec((1, 1, rows, f2), lambda i, j: (layer, i, j, 0))],
        out_specs=[out_spec, out_spec],
        out_shape=[jax.ShapeDtypeStruct((e, d, f), BF16)] * 2,
        compiler_params=_params(("parallel", "parallel")),
    )(w1_all)


def _expert_kernel(be_ref, nu_ref, sw_ref, nx_ref, sl_ref, x_ref, b1g_ref, b1l_ref, b2_ref,
                   w1g_hbm, w1l_hbm, w2_hbm, o_ref, wg_ref, wl_ref, w2f_ref, w2b_ref, sem_ref, *, layer):
    i = pl.program_id(0)

    def weight_copies(expert, slot):
        return (pltpu.make_async_copy(w1g_hbm.at[expert], wg_ref.at[slot], sem_ref.at[slot, 0]),
                pltpu.make_async_copy(w1l_hbm.at[expert], wl_ref.at[slot], sem_ref.at[slot, 1]),
                pltpu.make_async_copy(w2_hbm.at[layer, expert], w2f_ref.at[slot], sem_ref.at[slot, 2]))

    @pl.when(jnp.logical_and(i == 0, nu_ref[0] > 0))
    def _():
        for cp in weight_copies(be_ref[0], 0):
            cp.start()

    @pl.when(sw_ref[i] == 1)
    def _():
        slot = sl_ref[i]
        for cp in weight_copies(be_ref[i], slot):
            cp.wait()

        @pl.when(nx_ref[i] >= 0)
        def _():
            for cp in weight_copies(nx_ref[i], 1 - slot):
                cp.start(priority=1)

        w2b_ref[...] = w2f_ref[slot].astype(BF16)

    @pl.when(i < nu_ref[0])
    def _():
        slot = sl_ref[i]
        x = _unpack_halves(x_ref[...]).astype(BF16)
        glu = jnp.minimum(_dot(x, wg_ref[slot]) + b1g_ref[0], SWIGLU_LIMIT)
        lin = jnp.clip(_dot(x, wl_ref[slot]) + b1l_ref[0], -SWIGLU_LIMIT, SWIGLU_LIMIT)
        act = glu * _sigmoid(SWIGLU_ALPHA * glu) * (lin + 1.0)
        o_ref[...] = _pack_halves(_dot(act.astype(BF16), w2b_ref[...]) + b2_ref[0])

    @pl.when(i >= nu_ref[0])
    def _():
        o_ref[...] = jnp.zeros_like(o_ref)


def _expert_call(block_expert, n_used, x_rows, w1g, w1l, b1g, b1l, w2_all, layer, b2):
    r = x_rows.shape[0]
    d, f = w1g.shape[1:]
    tm = MOE_BLOCK
    n_blocks = r // tm
    blk = jnp.arange(n_blocks, dtype=jnp.int32)
    prev = jnp.concatenate([block_expert[:1], block_expert[:-1]])
    switch = (blk < n_used[0]) & ((blk == 0) | (block_expert != prev))
    slot = (jnp.cumsum(switch.astype(jnp.int32)) - 1) % 2
    later = jnp.where(switch, blk, n_blocks)
    next_start = jnp.concatenate([lax.cummin(later, reverse=True)[1:], jnp.full((1,), n_blocks, jnp.int32)])
    next_expert = jnp.where(next_start < n_blocks, block_expert[jnp.minimum(next_start, n_blocks - 1)], -1)
    scalars = (block_expert, n_used, switch.astype(jnp.int32), next_expert.astype(jnp.int32), slot.astype(jnp.int32))
    bias = lambda w: pl.BlockSpec((1, 1, w), lambda i, be, *_: (be[i], 0, 0))
    hbm = pl.BlockSpec(memory_space=pl.ANY)
    return pl.pallas_call(
        functools.partial(_expert_kernel, layer=layer),
        grid_spec=pltpu.PrefetchScalarGridSpec(
            num_scalar_prefetch=len(scalars),
            grid=(n_blocks,),
            in_specs=[pl.BlockSpec((tm, d // 2), lambda i, be, nu, *_: (jnp.minimum(i, jnp.maximum(nu[0] - 1, 0)), 0)),
                      bias(f), bias(f), bias(d), hbm, hbm, hbm],
            out_specs=pl.BlockSpec((tm, d // 2), lambda i, *_: (i, 0)),
            scratch_shapes=[pltpu.VMEM((2, d, f), BF16), pltpu.VMEM((2, d, f), BF16), pltpu.VMEM((2, f, d), F32),
                            pltpu.VMEM((f, d), BF16), pltpu.SemaphoreType.DMA((2, 3))],
        ),
        out_shape=jax.ShapeDtypeStruct((r, d // 2), jnp.uint32),
        compiler_params=_params(("arbitrary",)),
    )(*scalars, x_rows, b1g, b1l, b2, w1g, w1l, w2_all)


def _combine_kernel(*refs, n_parts):
    x_ref, y_refs = refs[0], refs[1:1 + n_parts]
    gate_ref, pos_ref, mod_ref, g_ref, o_ref = refs[1 + n_parts:]
    gates, pos = gate_ref[0], pos_ref[0]
    lane = lax.broadcasted_iota(jnp.int32, (gates.shape[0], y_refs[0].shape[2]), 1)
    mix = jnp.zeros(lane.shape, F32)
    for k in range(TOP_K):
        mix = jnp.where(lane == pos[:, k:k + 1], gates[:, k:k + 1], mix)
    mix_hi, mix_lo = _split_bf16(mix)
    mod = mod_ref[0, 0]
    part_batches = pl.num_programs(0) // n_parts
    for p, y_ref in enumerate(y_refs):
        @pl.when(pl.program_id(0) // part_batches == p)
        def _(y_ref=y_ref):
            rows = _unpack_halves(y_ref[0, 0]).astype(BF16)
            y = _dot(mix_hi, rows) + _dot(mix_lo, rows)
            o_ref[0] = x_ref[0] + mod[5:6] * (_rms(y) * g_ref[...][3:4])


def _combine_call(xc, picked_parts, gates, pos, modl, g, n_lat):
    b, n, d = xc.shape
    tm = TOKEN_TILE
    n_parts = len(picked_parts)
    bp = b // n_parts
    tok = pl.BlockSpec((1, tm, d), lambda i, t: (i, t, 0))

    def part_spec(p):
        def index(i, t):
            live = (i >= p * bp) & (i < (p + 1) * bp)
            return jnp.clip(i - p * bp, 0, bp - 1), jnp.where(live, t, 0), 0, 0
        return pl.BlockSpec((1, 1, TOP_K * tm, d // 2), index)

    return pl.pallas_call(
        functools.partial(_combine_kernel, n_parts=n_parts),
        grid=(b, n // tm),
        in_specs=[tok] + [part_spec(p) for p in range(n_parts)] + [
            pl.BlockSpec((1, tm, TOP_K), lambda i, t: (i, t, 0)),
            pl.BlockSpec((1, tm, TOP_K), lambda i, t: (i, t, 0)),
            _mod_spec(d, n_lat // tm),
            pl.BlockSpec(g.shape, lambda i, t: (0, 0)),
        ],
        out_specs=tok,
        out_shape=jax.ShapeDtypeStruct((b, n, d), F32),
        compiler_params=_params(("parallel", "parallel")),
    )(xc, *picked_parts, gates, pos, modl, g)


def _rope_tables(rows, n_ctx, rot_dim):
    row = jnp.repeat(jnp.arange(rows), GRID_W)
    col = jnp.tile(jnp.arange(GRID_W), rows)
    n_freq = rot_dim // 4
    inv = ROPE_THETA ** (-jnp.arange(n_freq, dtype=F32) / n_freq)
    ang = jnp.concatenate([row[:, None] * inv, col[:, None] * inv], axis=-1)
    cos, sin = jnp.cos(ang), jnp.sin(ang)
    reps = 256 // rot_dim
    cos_t = jnp.tile(jnp.concatenate([cos, cos], axis=-1), (1, reps))
    sin_t = jnp.tile(jnp.concatenate([-sin, sin], axis=-1), (1, reps))
    cos_t = jnp.concatenate([cos_t, jnp.ones((n_ctx, 256), F32)], axis=0)
    sin_t = jnp.concatenate([sin_t, jnp.zeros((n_ctx, 256), F32)], axis=0)
    return cos_t, sin_t


def _pack_w_in(w):
    parts, src = [], 0
    for _, width, padded in IN_PIECES:
        parts.append(w[:, src:src + width])
        if padded > width:
            parts.append(jnp.zeros((w.shape[0], padded - width), w.dtype))
        src += width
    return jnp.concatenate(parts, axis=1).astype(BF16)


def _pack_w_uq(w):
    wh = w.reshape(MLA_Q_RANK, N_HEADS, MLA_NOPE + MLA_ROPE)
    nope = wh[:, :, :MLA_NOPE].reshape(MLA_Q_RANK, N_HEADS * MLA_NOPE)
    rope = wh[:, :, MLA_NOPE:].reshape(MLA_Q_RANK, N_HEADS * MLA_ROPE)
    packed = jnp.concatenate([nope, rope], axis=1)
    return jnp.pad(packed, ((0, 256 - MLA_Q_RANK), (0, 0))).astype(BF16)


def _pack_w_ukv(w):
    wh = w.reshape(MLA_KV_RANK, N_HEADS, MLA_NOPE + HEAD_DIM)
    kn = wh[:, :, :MLA_NOPE].reshape(MLA_KV_RANK, N_HEADS * MLA_NOPE)
    vv = wh[:, :, MLA_NOPE:].reshape(MLA_KV_RANK, N_HEADS * HEAD_DIM)
    return jnp.concatenate([kn, vv], axis=1).astype(BF16)


def _attention(q_parts, hq, kt, v, n_lat, n_ctx, ctx_out, diff=None):
    n = n_lat + n_ctx
    y = _attn_call(q_parts, hq, kt, v, (0, n_lat), (0, n), min(Q_TILE, n_lat), diff)
    if ctx_out:
        y_c = _attn_call(q_parts, hq, kt, v, (n_lat, n_ctx), (n_lat, n_ctx), n_ctx, diff)
    else:
        y_c = jnp.zeros((y.shape[0], n_ctx, y.shape[2]), y.dtype)
    return y, y_c


def _steps(x, bounds, offsets):
    deltas = offsets - jnp.concatenate([jnp.zeros_like(offsets[..., :1]), offsets[..., :-1]], axis=-1)
    return x + jnp.sum(jnp.where(x[..., None] >= bounds[..., None, :], deltas[..., None, :], 0), axis=-1)


def _moe(x_new, hs, gates, pos, counts, modl, g, n_lat, w1_all, b1, w2_all, b2, layer):
    b, n, d = x_new.shape
    f = w2_all.shape[2]
    tile_rows = TOP_K * TOKEN_TILE
    n_parts = 2 if b % 2 == 0 else 1
    tiles = (b // n_parts) * (n // TOKEN_TILE)
    n_blocks = -(-(tiles * tile_rows) // MOE_BLOCK) + N_EXPERTS
    hs_rows = hs.reshape(b * (n // TOKEN_TILE) * tile_rows, d // 2)
    w1g, w1l = _w1_prep_call(w1_all, layer)
    b1g, b1l = b1[:, 0::2].reshape(N_EXPERTS, 1, f), b1[:, 1::2].reshape(N_EXPERTS, 1, f)
    picked_parts = []
    for p in range(n_parts):
        tile_counts = counts.reshape(-1, N_EXPERTS)[p * tiles:(p + 1) * tiles]
        tile_base = jnp.cumsum(tile_counts, axis=0) - tile_counts
        first = jnp.cumsum(tile_counts, axis=1) - tile_counts
        padded = (jnp.sum(tile_counts, axis=0) + MOE_BLOCK - 1) // MOE_BLOCK * MOE_BLOCK
        pad_ends = jnp.cumsum(padded)
        pad_start = pad_ends - padded
        block_start = jnp.arange(n_blocks, dtype=jnp.int32) * MOE_BLOCK
        block_expert = jnp.minimum(jnp.sum((pad_ends[None, :] <= block_start[:, None]).astype(jnp.int32), axis=1),
                                   N_EXPERTS - 1)
        n_used = (pad_ends[-1:] // MOE_BLOCK).astype(jnp.int32)
        j = block_start[:, None] + jnp.arange(MOE_BLOCK, dtype=jnp.int32)[None, :] - pad_start[block_expert][:, None]
        tile_offset = (jnp.arange(tiles, dtype=jnp.int32)[:, None] + p * tiles) * tile_rows + first - tile_base
        src = _steps(j, tile_base.T[block_expert], tile_offset.T[block_expert])
        src = jnp.clip(src, 0, hs_rows.shape[0] - 1)
        x_rows = hs_rows.at[src.reshape(-1)].get(mode="promise_in_bounds")
        y_rows = _expert_call(block_expert, n_used, x_rows, w1g, w1l, b1g, b1l, w2_all, layer,
                              b2.reshape(N_EXPERTS, 1, d))
        slot = jnp.broadcast_to(jnp.arange(tile_rows, dtype=jnp.int32)[None, :], (tiles, tile_rows))
        back = _steps(slot, first, tile_base + pad_start[None, :] - first)
        picked = y_rows.at[back.reshape(-1)].get(mode="promise_in_bounds")
        picked_parts.append(picked.reshape(b // n_parts, n // TOKEN_TILE, tile_rows, d // 2))
    return _combine_call(x_new, picked_parts, gates.transpose(0, 2, 1), pos.transpose(0, 2, 1), modl, g, n_lat)


def kernel(x, c, ctx, c_ctx, ada_w, ada_b, norm_g, w_in, w_out, ret_log_decay, ret_gn_w, ret_gn_b,
           diff_lambda, diff_subln, gqa_qk_norm, mla_q_norm, mla_kv_norm, mla_w_uq, mla_w_ukv,
           router_w, router_b, exp_w1, exp_b1, exp_w2, exp_b2):
    b, s, d = x.shape
    n_ctx = ctx.shape[1]
    n = s + n_ctx
    depth = ada_w.shape[0]
    assert n_ctx % TOKEN_TILE == 0 and s % TOKEN_TILE == 0 and s % GRID_W == 0
    assert n_ctx % RET_CHUNK == 0 and s % min(Q_TILE, s) == 0 and s % n_ctx == 0

    tables = _rope_tables(s // GRID_W, n_ctx, HEAD_DIM) + _rope_tables(s // GRID_W, n_ctx, DIFF_D)
    c_rows = jnp.zeros((16, d), F32).at[:b].set(c).at[b].set(c_ctx)
    mods = _ada_call(c_rows, ada_w, ada_b)
    xc = jnp.concatenate([x, ctx], axis=1)

    for l in range(depth):
        last = l == depth - 1
        lam_init = 0.8 - 0.6 * math.exp(-0.3 * l)
        mod_lat = mods[l, :b].reshape(b, 1, 6, d)
        mod_ctx = jnp.broadcast_to(mods[l, b].reshape(1, 1, 6, d), (b, 1, 6, d))
        modl = jnp.concatenate([mod_ctx, mod_lat], axis=1)

        (rq, rk, rv, rg, dq, dkt, dv, gq, gkt, gv, mqn, mqr, mkt, mv) = _in_proj_call(
            xc, modl, norm_g[l, 0:1], _pack_w_in(w_in[l]), tables,
            jnp.tile(gqa_qk_norm[l, 0], N_HEADS)[None, :], jnp.tile(gqa_qk_norm[l, 1], GQA_KV_HEADS)[None, :],
            jnp.pad(mla_q_norm[l], (0, 256 - MLA_Q_RANK))[None, :], mla_kv_norm[l][None, :],
            _pack_w_uq(mla_w_uq[l]), _pack_w_ukv(mla_w_ukv[l]), s)

        log_g = -jnp.exp(ret_log_decay[l].astype(F32))
        o_f = _ret_call(rq, rk, rv, _ret_tables(log_g[0], False), s, False)
        ret_y = _ret_call(rq, rk, rv, _ret_tables(log_g[1], True), s, True,
                          (o_f, rg, ret_gn_w[l][None, :], ret_gn_b[l][None, :]))

        lp = diff_lambda[l].astype(F32)
        lam = jnp.exp(jnp.sum(lp[0] * lp[1])) - jnp.exp(jnp.sum(lp[2] * lp[3])) + lam_init
        dif_y = _attention([dq], 2 * N_HEADS, dkt, dv, s, n_ctx, not last, (lam, diff_subln[l], 1.0 - lam_init))

        gqa_y = _attention([gq], N_HEADS, gkt, gv, s, n_ctx, not last)

        mla_y = _attention([mqn, mqr], N_HEADS, mkt, mv, s, n_ctx, not last)

        n_rows = s if last else n
        x_new, hs, gates, pos, counts = _out_proj_call(
            ret_y, (dif_y, gqa_y, mla_y), w_out[l].astype(BF16), xc, modl, norm_g[l], router_w[l].T,
            router_b[l][:, None], s, n_rows)
        xc = _moe(x_new, hs, gates, pos, counts[..., 0], modl, norm_g[l], s,
                  exp_w1, exp_b1[l], exp_w2, exp_b2[l], l)

    return xc[:, :s]
```
